```python
import math
import jax
import jax.numpy as jnp
from jax import lax
import numpy as np

D_MODEL = 1024
BATCH = 2
SEQ = 8192
DEPTH = 4
DEC_BATCH = 16
DEC_SEQ = 32
PAST_LEN = 1024

CHUNK = 64
N_META = 16
EPS = 1e-6
NEG_INF = -1e30
N_HEADS_A = 4
HEAD_DIM_A = 64
NUM_BUCKETS = 32
MAX_DISTANCE = 128
Q_BLOCK = 128
N_HEADS_B = 4
HEAD_DIM_B = 128
CONV_W = 4
S5_GROUP = 16
S5_GROUPS = 32
S5_STATE = 64
PEER_HEADS = 8
PEER_QDIM = 256
N_KEYS = 128
N_EXPERTS = N_KEYS * N_KEYS
PEER_TOPK = 16
PEER_BLOCK = 256

WIDTH_A = N_HEADS_A * 2 * HEAD_DIM_A
WIDTH_B = N_HEADS_B * HEAD_DIM_B
WIDTH_C = S5_GROUPS * S5_GROUP
OFF_QA = 0
OFF_KA = OFF_QA + WIDTH_A
OFF_VA = OFF_KA + WIDTH_A
OFF_QKVB = OFF_VA + WIDTH_A
OFF_GB = OFF_QKVB + 3 * WIDTH_B
OFF_BETA = OFF_GB + WIDTH_B
OFF_ALPHA = OFF_BETA + N_HEADS_B
OFF_UC = OFF_ALPHA + N_HEADS_B
OFF_GATE = OFF_UC + WIDTH_C
D_IN = OFF_GATE + 3 * D_MODEL

kernel_name = 'hybrid_stream_diffattn_gdn_s5_peer'


def rmsnorm(x, g):
    xf = x.astype(jnp.float32)
    y = xf * lax.rsqrt(jnp.mean(xf * xf, axis=-1, keepdims=True) + EPS)
    return (y * g.astype(jnp.float32)).astype(x.dtype)


def l2norm(x):
    xf = x.astype(jnp.float32)
    return xf * lax.rsqrt(jnp.sum(xf * xf, axis=-1, keepdims=True) + EPS)


def chunk_id(pos):
    return jnp.where(pos < N_META, 0, 1 + (pos - N_META) // CHUNK)


def t5_bias(q_pos, k_pos, table):
    rel = k_pos[None, :] - q_pos[:, None]
    half = NUM_BUCKETS // 2
    exact = half // 2
    n = jnp.abs(rel)
    nf = jnp.maximum(n, 1).astype(jnp.float32)
    far = exact + (jnp.log(nf / exact) / math.log(MAX_DISTANCE / exact) * (half - exact)).astype(jnp.int32)
    bucket = jnp.where(rel > 0, half, 0) + jnp.where(n < exact, n, jnp.minimum(far, half - 1))
    return jnp.moveaxis(table[bucket].astype(jnp.float32), -1, 0)


def diff_attn_core(q, k, v, q_pos, k_pos, lam, rel_table, mask):
    logits = jnp.einsum('bqhcd,bkhcd->bhcqk', q.astype(jnp.float32), k.astype(jnp.float32)) * (HEAD_DIM_A ** -0.5)
    logits = logits + t5_bias(q_pos, k_pos, rel_table)[None, :, None]
    if mask is not None:
        logits = jnp.where(mask, logits, NEG_INF)
    p = jax.nn.softmax(logits, axis=-1)
    w = p[:, :, 0] - lam * p[:, :, 1]
    return jnp.einsum('bhqk,bkhe->bqhe', w.astype(v.dtype), v)


def diff_attn_prompt(q, k, v, lam, rel_table):
    B, L = q.shape[0], q.shape[1]
    n_blk = -(-L // Q_BLOCK)
    qb = jnp.pad(q, ((0, 0), (0, n_blk * Q_BLOCK - L), (0, 0), (0, 0), (0, 0)))
    qb = jnp.moveaxis(qb.reshape(B, n_blk, Q_BLOCK, N_HEADS_A, 2, HEAD_DIM_A), 1, 0)
    starts = jnp.arange(n_blk, dtype=jnp.int32) * Q_BLOCK
    k_pos = jnp.arange(L, dtype=jnp.int32)
    k_chunk = chunk_id(k_pos)

    def block(args):
        q_i, s = args
        q_pos = s + jnp.arange(Q_BLOCK, dtype=jnp.int32)
        mask = k_chunk[None, :] <= chunk_id(q_pos)[:, None]
        return diff_attn_core(q_i, k, v, q_pos, k_pos, lam, rel_table, mask)

    o = lax.map(block, (qb, starts))
    o = jnp.moveaxis(o, 0, 1).reshape(B, n_blk * Q_BLOCK, N_HEADS_A, 2 * HEAD_DIM_A)
    return o[:, :L]


def causal_conv(hist, w):
    L = hist.shape[1] - (CONV_W - 1)
    out = hist[:, 0:L] * w[0]
    for i in range(1, CONV_W):
        out = out + hist[:, i:i + L] * w[i]
    return out


def gdn_chunk_scan(q, k, v, g, beta, S0, C):
    B, L, H = g.shape
    n = L // C

    def to_chunks(t):
        t = t.reshape((B, n, C) + t.shape[2:])
        return jnp.moveaxis(jnp.moveaxis(t, 1, 0), 3, 2)

    tri = jnp.tril(jnp.ones((C, C), dtype=bool))
    strict = jnp.tril(jnp.ones((C, C), dtype=bool), -1)
    eye = jnp.eye(C, dtype=jnp.float32)

    def step(S, inp):
        q_c, k_c, v_c, g_c, b_c = inp
        G = jnp.cumsum(g_c, axis=-1)
        decay = jnp.exp(jnp.where(tri, G[..., :, None] - G[..., None, :], -jnp.inf))
        kb = k_c * b_c[..., None]
        A = jnp.where(strict, jnp.einsum('bhid,bhjd->bhij', kb, k_c) * decay, 0.0)
        rhs = jnp.concatenate([v_c * b_c[..., None], kb * jnp.exp(G)[..., None]], axis=-1)
        sol = lax.linalg.triangular_solve(eye + A, rhs, left_side=True, lower=True, unit_diagonal=True)
        u, w = sol[..., :HEAD_DIM_B], sol[..., HEAD_DIM_B:]
        v_new = u - jnp.einsum('bhck,bhkv->bhcv', w, S)
        attn = jnp.einsum('bhid,bhjd->bhij', q_c, k_c) * decay
        o = jnp.einsum('bhck,bhkv->bhcv', q_c * jnp.exp(G)[..., None], S) + jnp.einsum('bhij,bhjv->bhiv', attn, v_new)
        G_last = G[..., -1:]
        S = S * jnp.exp(G_last)[..., None] + jnp.einsum('bhck,bhcv->bhkv', k_c * jnp.exp(G_last - G)[..., None], v_new)
        return S, o

    S, o = lax.scan(step, S0, (to_chunks(q), to_chunks(k), to_chunks(v), to_chunks(g), to_chunks(beta)))
    o = jnp.moveaxis(jnp.moveaxis(o, 2, 3), 0, 1).reshape(B, L, H, HEAD_DIM_B)
    return o, S


def gdn_segments(q, k, v, g, beta, S, segments):
    outs = []
    for (a, b, c) in segments:
        o, S = gdn_chunk_scan(q[:, a:b], k[:, a:b], v[:, a:b], g[:, a:b], beta[:, a:b], S, c)
        outs.append(o)
    return jnp.concatenate(outs, axis=1), S


def _ssm_combine(e1, e2):
    a1, b1 = e1
    a2, b2 = e2
    return a1 * a2, a2 * b1 + b2


def s5_scan(u, x0_re, x0_im, a_re, a_im, b_re, b_im, c_re, c_im, d, log_dt):
    f32 = jnp.float32
    uf = u.astype(f32)
    lam = lax.complex(a_re.astype(f32), a_im.astype(f32))
    lam_bar = jnp.exp(lam * jnp.exp(log_dt.astype(f32))[:, None])
    b_bar = ((lam_bar - 1.0) / lam)[..., None] * lax.complex(b_re.astype(f32), b_im.astype(f32))
    c = lax.complex(c_re.astype(f32), c_im.astype(f32))
    bu = jnp.einsum('blgi,gpi->blgp', uf.astype(jnp.complex64), b_bar)
    x0 = lax.complex(x0_re.astype(f32), x0_im.astype(f32))
    bu = bu.at[:, 0].add(lam_bar * x0)
    a = jnp.broadcast_to(lam_bar, bu.shape)
    _, xs = lax.associative_scan(_ssm_combine, (a, bu), axis=1)
    y = jnp.einsum('blgp,gip->blgi', xs, c).real + d.astype(f32) * uf
    return y, xs[:, -1]


def peer(h, wq, subkeys, pu, pv):
    T = h.shape[0]
    nb = -(-T // PEER_BLOCK)
    hb = jnp.pad(h, ((0, nb * PEER_BLOCK - T), (0, 0))).reshape(nb, PEER_BLOCK, D_MODEL)
    sk = subkeys.astype(jnp.float32)

    def block(hx):
        q = (hx @ wq).astype(jnp.float32).reshape(PEER_BLOCK, PEER_HEADS, 2, PEER_QDIM // 2)
        s = jnp.einsum('thcd,hcnd->thcn', q, sk)
        s1, i1 = lax.top_k(s[:, :, 0], PEER_TOPK)
        s2, i2 = lax.top_k(s[:, :, 1], PEER_TOPK)
        cand = (s1[..., :, None] + s2[..., None, :]).reshape(PEER_BLOCK, PEER_HEADS, PEER_TOPK * PEER_TOPK)
        best, ci = lax.top_k(cand, PEER_TOPK)
        e = (jnp.take_along_axis(i1, ci // PEER_TOPK, axis=-1) * N_KEYS
             + jnp.take_along_axis(i2, ci % PEER_TOPK, axis=-1))
        gw = jax.nn.softmax(best, axis=-1)
        act = jax.nn.gelu(jnp.einsum('td,thkd->thk', hx, pu[e]).astype(jnp.float32))
        return jnp.einsum('thk,thkd->td', (gw * act).astype(pv.dtype), pv[e])

    return lax.map(block, hb).reshape(nb * PEER_BLOCK, D_MODEL)[:T]


def trunk_layer(x, lp, layer_idx, rel_table, prompt, k_past, v_past, conv_hist, delta0, ssm0_re, ssm0_im):
    f32 = jnp.float32
    B, L = x.shape[0], x.shape[1]
    h = rmsnorm(x, lp['norm1'])
    z = h @ lp['w_in']

    qA = rmsnorm(z[..., OFF_QA:OFF_KA].reshape(B, L, N_HEADS_A, 2, HEAD_DIM_A), lp['q_norm'])
    kA = rmsnorm(z[..., OFF_KA:OFF_VA].reshape(B, L, N_HEADS_A, 2, HEAD_DIM_A), lp['k_norm'])
    vA = z[..., OFF_VA:OFF_QKVB].reshape(B, L, N_HEADS_A, 2 * HEAD_DIM_A)
    lam_init = 0.8 - 0.6 * math.exp(-0.3 * layer_idx)
    lqk = lp['lambda_qk'].astype(f32)
    lam = jnp.exp(jnp.sum(lqk[0] * lqk[1])) - jnp.exp(jnp.sum(lqk[2] * lqk[3])) + lam_init
    if prompt:
        oA = diff_attn_prompt(qA, kA, vA, lam, rel_table)
    else:
        P = k_past.shape[1]
        k_all = jnp.concatenate([k_past.reshape(B, P, N_HEADS_A, 2, HEAD_DIM_A), kA], axis=1)
        v_all = jnp.concatenate([v_past, vA], axis=1)
        oA = diff_attn_core(qA, k_all, v_all, P + jnp.arange(L, dtype=jnp.int32),
                            jnp.arange(P + L, dtype=jnp.int32), lam, rel_table, None)
    oA = (rmsnorm(oA, lp['subln']) * (1.0 - lam_init)).reshape(B, L, WIDTH_A)

    qkv_in = z[..., OFF_QKVB:OFF_GB]
    hist = jnp.concatenate([conv_hist.astype(qkv_in.dtype), qkv_in], axis=1)
    conv_state = hist[:, -(CONV_W - 1):]
    c = jax.nn.silu(causal_conv(hist, lp['conv_w']))
    qB = l2norm(c[..., :WIDTH_B].reshape(B, L, N_HEADS_B, HEAD_DIM_B)) * (HEAD_DIM_B ** -0.5)
    kB = l2norm(c[..., WIDTH_B:2 * WIDTH_B].reshape(B, L, N_HEADS_B, HEAD_DIM_B))
    vB = c[..., 2 * WIDTH_B:].reshape(B, L, N_HEADS_B, HEAD_DIM_B).astype(f32)
    beta = jax.nn.sigmoid(z[..., OFF_BETA:OFF_ALPHA].astype(f32))
    g = -jnp.exp(lp['a_log'].astype(f32)) * jax.nn.softplus(z[..., OFF_ALPHA:OFF_UC].astype(f32) + lp['dt_bias'].astype(f32))
    segments = ((0, N_META, N_META), (N_META, L, CHUNK)) if prompt else ((0, L, L),)
    oB, delta = gdn_segments(qB, kB, vB, g, beta, delta0.astype(f32), segments)
    gate_b = z[..., OFF_GB:OFF_BETA].reshape(B, L, N_HEADS_B, HEAD_DIM_B)
    oB = (rmsnorm(oB.astype(x.dtype), lp['gdn_norm']) * jax.nn.silu(gate_b)).reshape(B, L, WIDTH_B)

    u = z[..., OFF_UC:OFF_GATE].reshape(B, L, S5_GROUPS, S5_GROUP)
    yC, ssm = s5_scan(u, ssm0_re, ssm0_im, lp['s5_a_re'], lp['s5_a_im'], lp['s5_b_re'], lp['s5_b_im'],
                      lp['s5_c_re'], lp['s5_c_im'], lp['s5_d'], lp['s5_log_dt'])
    gl = jax.nn.gelu(yC.reshape(B, L, WIDTH_C).astype(x.dtype)) @ lp['w_glu']
    oC = gl[..., :WIDTH_C] * jax.nn.sigmoid(gl[..., WIDTH_C:])

    gates = jax.nn.sigmoid(z[..., OFF_GATE:].reshape(B, L, 3, D_MODEL))
    merged = (gates[:, :, 0] * (oA @ lp['wb_a']) + gates[:, :, 1] * (oB @ lp['wb_b'])
              + gates[:, :, 2] * (oC @ lp['wb_c']))
    x = x + merged @ lp['w_out']

    h2 = rmsnorm(x, lp['norm2'])
    x = x + peer(h2.reshape(B * L, D_MODEL), lp['peer_wq'], lp['peer_subkeys'], lp['peer_u'], lp['peer_v']).reshape(B, L, D_MODEL)
    return (x, kA.reshape(B, L, N_HEADS_A, 2 * HEAD_DIM_A), vA, conv_state, delta.astype(x.dtype),
            ssm.real.astype(x.dtype), ssm.imag.astype(x.dtype))


def setup_inputs(seed: int = 0) -> dict:
    key = jax.random.key(seed)
    ks = jax.random.split(key, 40)
    f32 = jnp.float32

    def nrm(k, shape, s):
        return jax.random.normal(k, shape, f32) * s

    dt = jnp.exp(jax.random.uniform(ks[14], (DEPTH, N_HEADS_B), f32, math.log(1e-3), math.log(1e-1)))
    return {
        'x_prompt': nrm(ks[0], (BATCH, SEQ, D_MODEL), 1.0),
        'x_sample': nrm(ks[1], (DEC_BATCH, DEC_SEQ, D_MODEL), 1.0),
        'cache_k': nrm(ks[2], (DEPTH, DEC_BATCH, PAST_LEN, N_HEADS_A, 2 * HEAD_DIM_A), 1.0),
        'cache_v': nrm(ks[3], (DEPTH, DEC_BATCH, PAST_LEN, N_HEADS_A, 2 * HEAD_DIM_A), 1.0),
        'state_conv': nrm(ks[4], (DEPTH, DEC_BATCH, CONV_W - 1, 3 * WIDTH_B), 1.0),
        'state_delta': nrm(ks[5], (DEPTH, DEC_BATCH, N_HEADS_B, HEAD_DIM_B, HEAD_DIM_B), 0.1),
        'state_ssm_re': nrm(ks[6], (DEPTH, DEC_BATCH, S5_GROUPS, S5_STATE), 0.1),
        'state_ssm_im': nrm(ks[7], (DEPTH, DEC_BATCH, S5_GROUPS, S5_STATE), 0.1),
        'meta_tokens': nrm(ks[8], (N_META, D_MODEL), 1.0),
        'rel_bias': nrm(ks[9], (NUM_BUCKETS, N_HEADS_A), 0.1),
        'norm1_g': 1.0 + nrm(ks[10], (DEPTH, D_MODEL), 0.01),
        'norm2_g': 1.0 + nrm(ks[11], (DEPTH, D_MODEL), 0.01),
        'final_norm_g': 1.0 + nrm(ks[12], (D_MODEL,), 0.01),
        'w_in': nrm(ks[13], (DEPTH, D_MODEL, D_IN), D_MODEL ** -0.5),
        'q_norm_g': 1.0 + nrm(ks[15], (DEPTH, HEAD_DIM_A), 0.01),
        'k_norm_g': 1.0 + nrm(ks[16], (DEPTH, HEAD_DIM_A), 0.01),
        'lambda_qk': nrm(ks[17], (DEPTH, 4, HEAD_DIM_A), 0.1),
        'subln_g': 1.0 + nrm(ks[18], (DEPTH, 2 * HEAD_DIM_A), 0.01),
        'conv_w': nrm(ks[19], (DEPTH, CONV_W, 3 * WIDTH_B), CONV_W ** -0.5),
        'gdn_a_log': jnp.log(jax.random.uniform(ks[20], (DEPTH, N_HEADS_B), f32, 1.0, 16.0)),
        'gdn_dt_bias': dt + jnp.log(-jnp.expm1(-dt)),
        'gdn_norm_g': 1.0 + nrm(ks[21], (DEPTH, HEAD_DIM_B), 0.01),
        's5_a_re': -0.5 + nrm(ks[22], (DEPTH, S5_GROUPS, S5_STATE), 0.01),
        's5_a_im': jnp.pi * jnp.arange(S5_STATE, dtype=f32) + nrm(ks[23], (DEPTH, S5_GROUPS, S5_STATE), 0.01),
        's5_b_re': nrm(ks[24], (DEPTH, S5_GROUPS, S5_STATE, S5_GROUP), (2 * S5_GROUP) ** -0.5),
        's5_b_im': nrm(ks[25], (DEPTH, S5_GROUPS, S5_STATE, S5_GROUP), (2 * S5_GROUP) ** -0.5),
        's5_c_re': nrm(ks[26], (DEPTH, S5_GROUPS, S5_GROUP, S5_STATE), (2 * S5_STATE) ** -0.5),
        's5_c_im': nrm(ks[27], (DEPTH, S5_GROUPS, S5_GROUP, S5_STATE), (2 * S5_STATE) ** -0.5),
        's5_d': nrm(ks[28], (DEPTH, S5_GROUPS, S5_GROUP), 1.0),
        's5_log_dt': jax.random.uniform(ks[29], (DEPTH, S5_GROUPS), f32, math.log(1e-3), math.log(1e-1)),
        'w_glu': nrm(ks[30], (DEPTH, WIDTH_C, 2 * WIDTH_C), WIDTH_C ** -0.5),
        'w_branch_a': nrm(ks[31], (DEPTH, WIDTH_A, D_MODEL), WIDTH_A ** -0.5),
        'w_branch_b': nrm(ks[32], (DEPTH, WIDTH_B, D_MODEL), WIDTH_B ** -0.5),
        'w_branch_c': nrm(ks[33], (DEPTH, WIDTH_C, D_MODEL), WIDTH_C ** -0.5),
        'w_out': nrm(ks[34], (DEPTH, D_MODEL, D_MODEL), D_MODEL ** -0.5),
        'peer_wq': nrm(ks[35], (DEPTH, D_MODEL, PEER_HEADS * PEER_QDIM), D_MODEL ** -0.5),
        'peer_subkeys': nrm(ks[36], (DEPTH, PEER_HEADS, 2, N_KEYS, PEER_QDIM // 2), (PEER_QDIM // 2) ** -0.5),
        'peer_u': nrm(ks[37], (DEPTH, N_EXPERTS, D_MODEL), D_MODEL ** -0.5),
        'peer_v': nrm(ks[38], (DEPTH, N_EXPERTS, D_MODEL), 0.1),
    }


def reference(x_prompt, x_sample, cache_k, cache_v, state_conv, state_delta, state_ssm_re, state_ssm_im,
              meta_tokens, rel_bias, norm1_g, norm2_g, final_norm_g, w_in, q_norm_g, k_norm_g, lambda_qk,
              subln_g, conv_w, gdn_a_log, gdn_dt_bias, gdn_norm_g, s5_a_re, s5_a_im, s5_b_re, s5_b_im,
              s5_c_re, s5_c_im, s5_d, s5_log_dt, w_glu, w_branch_a, w_branch_b, w_branch_c, w_out,
              peer_wq, peer_subkeys, peer_u, peer_v):
    params = [dict(norm1=norm1_g[l], norm2=norm2_g[l], w_in=w_in[l], q_norm=q_norm_g[l], k_norm=k_norm_g[l],
                   lambda_qk=lambda_qk[l], subln=subln_g[l], conv_w=conv_w[l], a_log=gdn_a_log[l],
                   dt_bias=gdn_dt_bias[l], gdn_norm=gdn_norm_g[l], s5_a_re=s5_a_re[l], s5_a_im=s5_a_im[l],
                   s5_b_re=s5_b_re[l], s5_b_im=s5_b_im[l], s5_c_re=s5_c_re[l], s5_c_im=s5_c_im[l],
                   s5_d=s5_d[l], s5_log_dt=s5_log_dt[l], w_glu=w_glu[l], wb_a=w_branch_a[l],
                   wb_b=w_branch_b[l], wb_c=w_branch_c[l], w_out=w_out[l], peer_wq=peer_wq[l],
                   peer_subkeys=peer_subkeys[l], peer_u=peer_u[l], peer_v=peer_v[l])
              for l in range(DEPTH)]
    f32 = jnp.float32

    B = x_prompt.shape[0]
    dt = x_prompt.dtype
    xp = jnp.concatenate([jnp.broadcast_to(meta_tokens.astype(dt)[None], (B, N_META, D_MODEL)), x_prompt], axis=1)
    kp, vp, cp, dp, srp, sip = [], [], [], [], [], []
    for l in range(DEPTH):
        xp, k_new, v_new, c_new, d_new, sr_new, si_new = trunk_layer(
            xp, params[l], l, rel_bias, True, None, None,
            jnp.zeros((B, CONV_W - 1, 3 * WIDTH_B), dt),
            jnp.zeros((B, N_HEADS_B, HEAD_DIM_B, HEAD_DIM_B), f32),
            jnp.zeros((B, S5_GROUPS, S5_STATE), f32), jnp.zeros((B, S5_GROUPS, S5_STATE), f32))
        kp.append(k_new); vp.append(v_new); cp.append(c_new); dp.append(d_new); srp.append(sr_new); sip.append(si_new)
    y_prompt = rmsnorm(xp, final_norm_g)[:, N_META:]

    xs = x_sample
    ks_, vs_, cs_, ds_, srs, sis = [], [], [], [], [], []
    for l in range(DEPTH):
        xs, k_new, v_new, c_new, d_new, sr_new, si_new = trunk_layer(
            xs, params[l], l, rel_bias, False, cache_k[l], cache_v[l], state_conv[l],
            state_delta[l], state_ssm_re[l], state_ssm_im[l])
        ks_.append(k_new); vs_.append(v_new); cs_.append(c_new); ds_.append(d_new); srs.append(sr_new); sis.append(si_new)
    y_sample = rmsnorm(xs, final_norm_g)

    k_prompt = jnp.stack(kp)
    v_prompt = jnp.stack(vp)
    k_sample = jnp.stack(ks_)
    v_sample = jnp.stack(vs_)
    conv_prompt = jnp.stack(cp)
    conv_sample = jnp.stack(cs_)
    delta_prompt = jnp.stack(dp)
    delta_sample = jnp.stack(ds_)
    ssm_re_prompt = jnp.stack(srp)
    ssm_im_prompt = jnp.stack(sip)
    ssm_re_sample = jnp.stack(srs)
    ssm_im_sample = jnp.stack(sis)
    return (y_prompt, y_sample, k_prompt, v_prompt, k_sample, v_sample, conv_prompt, conv_sample,
            delta_prompt, delta_sample, ssm_re_prompt, ssm_im_prompt, ssm_re_sample, ssm_im_sample)
```

```python
import functools
import math

import jax
import jax.numpy as jnp
from jax import lax
from jax.experimental import pallas as pl
from jax.experimental.pallas import tpu as pltpu

D_MODEL = 1024
DEPTH = 4
CHUNK = 64
N_META = 16
EPS = 1e-6
NEG_INF = -1e30
N_HEADS_A = 4
HEAD_DIM_A = 64
NUM_BUCKETS = 32
MAX_DISTANCE = 128
Q_BLOCK = 128
N_HEADS_B = 4
HEAD_DIM_B = 128
CONV_W = 4
S5_GROUP = 16
S5_GROUPS = 32
S5_STATE = 64
PEER_HEADS = 8
PEER_QDIM = 256
N_KEYS = 128
PEER_TOPK = 16
PEER_BLOCK = 256

WIDTH_A = N_HEADS_A * 2 * HEAD_DIM_A
WIDTH_B = N_HEADS_B * HEAD_DIM_B
WIDTH_C = S5_GROUPS * S5_GROUP
OFF_QA = 0
OFF_KA = OFF_QA + WIDTH_A
OFF_VA = OFF_KA + WIDTH_A
OFF_QKVB = OFF_VA + WIDTH_A
OFF_GB = OFF_QKVB + 3 * WIDTH_B
OFF_BETA = OFF_GB + WIDTH_B
OFF_ALPHA = OFF_BETA + N_HEADS_B
OFF_UC = OFF_ALPHA + N_HEADS_B
OFF_GATE = OFF_UC + WIDTH_C
D_IN = OFF_GATE + 3 * D_MODEL

LANE = 128


def _mm_kernel(x_ref, w_ref, o_ref):
    o_ref[...] = jnp.dot(x_ref[...].astype(jnp.bfloat16), w_ref[...],
                         preferred_element_type=jnp.float32)


def matmul(x, w, tm=512, tn=512):
    M, K = x.shape
    N = w.shape[1]
    n_pad = -(-N // LANE) * LANE
    wb = w.astype(jnp.bfloat16)
    if n_pad != N:
        wb = jnp.pad(wb, ((0, 0), (0, n_pad - N)))
    tn = math.gcd(n_pad, tn)
    tm = min(tm, M)
    out = pl.pallas_call(
        _mm_kernel,
        grid=(pl.cdiv(M, tm), n_pad // tn),
        in_specs=[pl.BlockSpec((tm, K), lambda i, j: (i, 0)),
                  pl.BlockSpec((K, tn), lambda i, j: (0, j))],
        out_specs=pl.BlockSpec((tm, tn), lambda i, j: (i, j)),
        out_shape=jax.ShapeDtypeStruct((M, n_pad), jnp.float32),
        compiler_params=pltpu.CompilerParams(dimension_semantics=("parallel", "parallel")),
        name="dense_proj",
    )(x, wb)
    return out[:, :N] if n_pad != N else out


def dense(x, w):
    lead = x.shape[:-1]
    return matmul(x.reshape(-1, x.shape[-1]), w).reshape(lead + (w.shape[1],))


def rmsnorm(x, g):
    xf = x.astype(jnp.float32)
    y = xf * lax.rsqrt(jnp.mean(xf * xf, axis=-1, keepdims=True) + EPS)
    return (y * g.astype(jnp.float32)).astype(x.dtype)


def l2norm(x):
    xf = x.astype(jnp.float32)
    return xf * lax.rsqrt(jnp.sum(xf * xf, axis=-1, keepdims=True) + EPS)


def chunk_id(pos):
    return jnp.where(pos < N_META, 0, 1 + (pos - N_META) // CHUNK)


def t5_bias(q_pos, k_pos, table):
    rel = k_pos[None, :] - q_pos[:, None]
    half = NUM_BUCKETS // 2
    exact = half // 2
    n = jnp.abs(rel)
    nf = jnp.maximum(n, 1).astype(jnp.float32)
    far = exact + (jnp.log(nf / exact) / math.log(MAX_DISTANCE / exact) * (half - exact)).astype(jnp.int32)
    bucket = jnp.where(rel > 0, half, 0) + jnp.where(n < exact, n, jnp.minimum(far, half - 1))
    return jnp.moveaxis(table[bucket].astype(jnp.float32), -1, 0)


def diff_attn_core(q, k, v, q_pos, k_pos, lam, rel_table, mask):
    logits = jnp.einsum('bqhcd,bkhcd->bhcqk', q, k) * (HEAD_DIM_A ** -0.5)
    logits = logits + t5_bias(q_pos, k_pos, rel_table)[None, :, None]
    if mask is not None:
        logits = jnp.where(mask, logits, NEG_INF)
    p = jax.nn.softmax(logits, axis=-1)
    w = p[:, :, 0] - lam * p[:, :, 1]
    return jnp.einsum('bhqk,bkhe->bqhe', w, v)


def diff_attn_prompt(q, k, v, lam, rel_table):
    B, L = q.shape[0], q.shape[1]
    n_blk = -(-L // Q_BLOCK)
    qb = jnp.pad(q, ((0, 0), (0, n_blk * Q_BLOCK - L), (0, 0), (0, 0), (0, 0)))
    qb = jnp.moveaxis(qb.reshape(B, n_blk, Q_BLOCK, N_HEADS_A, 2, HEAD_DIM_A), 1, 0)
    starts = jnp.arange(n_blk, dtype=jnp.int32) * Q_BLOCK
    k_pos = jnp.arange(L, dtype=jnp.int32)
    k_chunk = chunk_id(k_pos)

    def block(args):
        q_i, s = args
        q_pos = s + jnp.arange(Q_BLOCK, dtype=jnp.int32)
        mask = k_chunk[None, :] <= chunk_id(q_pos)[:, None]
        return diff_attn_core(q_i, k, v, q_pos, k_pos, lam, rel_table, mask)

    o = lax.map(block, (qb, starts))
    o = jnp.moveaxis(o, 0, 1).reshape(B, n_blk * Q_BLOCK, N_HEADS_A, 2 * HEAD_DIM_A)
    return o[:, :L]


def causal_conv(hist, w):
    L = hist.shape[1] - (CONV_W - 1)
    out = hist[:, 0:L] * w[0]
    for i in range(1, CONV_W):
        out = out + hist[:, i:i + L] * w[i]
    return out


def gdn_chunk_scan(q, k, v, g, beta, S0, C):
    B, L, H = g.shape
    n = L // C

    def to_chunks(t):
        t = t.reshape((B, n, C) + t.shape[2:])
        return jnp.moveaxis(jnp.moveaxis(t, 1, 0), 3, 2)

    tri = jnp.tril(jnp.ones((C, C), dtype=bool))
    strict = jnp.tril(jnp.ones((C, C), dtype=bool), -1)
    eye = jnp.eye(C, dtype=jnp.float32)

    def step(S, inp):
        q_c, k_c, v_c, g_c, b_c = inp
        G = jnp.cumsum(g_c, axis=-1)
        decay = jnp.exp(jnp.where(tri, G[..., :, None] - G[..., None, :], -jnp.inf))
        kb = k_c * b_c[..., None]
        A = jnp.where(strict, jnp.einsum('bhid,bhjd->bhij', kb, k_c) * decay, 0.0)
        rhs = jnp.concatenate([v_c * b_c[..., None], kb * jnp.exp(G)[..., None]], axis=-1)
        sol = lax.linalg.triangular_solve(eye + A, rhs, left_side=True, lower=True, unit_diagonal=True)
        u, w = sol[..., :HEAD_DIM_B], sol[..., HEAD_DIM_B:]
        v_new = u - jnp.einsum('bhck,bhkv->bhcv', w, S)
        attn = jnp.einsum('bhid,bhjd->bhij', q_c, k_c) * decay
        o = jnp.einsum('bhck,bhkv->bhcv', q_c * jnp.exp(G)[..., None], S) + jnp.einsum('bhij,bhjv->bhiv', attn, v_new)
        G_last = G[..., -1:]
        S = S * jnp.exp(G_last)[..., None] + jnp.einsum('bhck,bhcv->bhkv', k_c * jnp.exp(G_last - G)[..., None], v_new)
        return S, o

    S, o = lax.scan(step, S0, (to_chunks(q), to_chunks(k), to_chunks(v), to_chunks(g), to_chunks(beta)))
    o = jnp.moveaxis(jnp.moveaxis(o, 2, 3), 0, 1).reshape(B, L, H, HEAD_DIM_B)
    return o, S


def gdn_segments(q, k, v, g, beta, S, segments):
    outs = []
    for (a, b, c) in segments:
        o, S = gdn_chunk_scan(q[:, a:b], k[:, a:b], v[:, a:b], g[:, a:b], beta[:, a:b], S, c)
        outs.append(o)
    return jnp.concatenate(outs, axis=1), S


def _ssm_combine(e1, e2):
    a1, b1 = e1
    a2, b2 = e2
    return a1 * a2, a2 * b1 + b2


def s5_scan(u, x0_re, x0_im, a_re, a_im, b_re, b_im, c_re, c_im, d, log_dt):
    f32 = jnp.float32
    lam = lax.complex(a_re, a_im)
    lam_bar = jnp.exp(lam * jnp.exp(log_dt)[:, None])
    b_bar = ((lam_bar - 1.0) / lam)[..., None] * lax.complex(b_re, b_im)
    c = lax.complex(c_re, c_im)
    bu = jnp.einsum('blgi,gpi->blgp', u.astype(jnp.complex64), b_bar)
    x0 = lax.complex(x0_re.astype(f32), x0_im.astype(f32))
    bu = bu.at[:, 0].add(lam_bar * x0)
    a = jnp.broadcast_to(lam_bar, bu.shape)
    _, xs = lax.associative_scan(_ssm_combine, (a, bu), axis=1)
    y = jnp.einsum('blgp,gip->blgi', xs, c).real + d * u
    return y, xs[:, -1]


def peer(h, wq, subkeys, pu, pv):
    T = h.shape[0]
    nb = -(-T // PEER_BLOCK)
    hp = jnp.pad(h, ((0, nb * PEER_BLOCK - T), (0, 0)))
    qall = matmul(hp, wq).reshape(nb, PEER_BLOCK, PEER_HEADS, 2, PEER_QDIM // 2)
    hb = hp.reshape(nb, PEER_BLOCK, D_MODEL)

    def block(args):
        hx, q = args
        s = jnp.einsum('thcd,hcnd->thcn', q, subkeys)
        s1, i1 = lax.top_k(s[:, :, 0], PEER_TOPK)
        s2, i2 = lax.top_k(s[:, :, 1], PEER_TOPK)
        cand = (s1[..., :, None] + s2[..., None, :]).reshape(PEER_BLOCK, PEER_HEADS, PEER_TOPK * PEER_TOPK)
        best, ci = lax.top_k(cand, PEER_TOPK)
        e = (jnp.take_along_axis(i1, ci // PEER_TOPK, axis=-1) * N_KEYS
             + jnp.take_along_axis(i2, ci % PEER_TOPK, axis=-1))
        gw = jax.nn.softmax(best, axis=-1)
        act = jax.nn.gelu(jnp.einsum('td,thkd->thk', hx, pu[e]))
        return jnp.einsum('thk,thkd->td', gw * act, pv[e])

    return lax.map(block, (hb, qall)).reshape(nb * PEER_BLOCK, D_MODEL)[:T]


def trunk_layer(x, lp, layer_idx, rel_table, prompt, k_past, v_past, conv_hist, delta0, ssm0_re, ssm0_im):
    f32 = jnp.float32
    B, L = x.shape[0], x.shape[1]
    h = rmsnorm(x, lp['norm1'])
    z = dense(h, lp['w_in'])

    qA = rmsnorm(z[..., OFF_QA:OFF_KA].reshape(B, L, N_HEADS_A, 2, HEAD_DIM_A), lp['q_norm'])
    kA = rmsnorm(z[..., OFF_KA:OFF_VA].reshape(B, L, N_HEADS_A, 2, HEAD_DIM_A), lp['k_norm'])
    vA = z[..., OFF_VA:OFF_QKVB].reshape(B, L, N_HEADS_A, 2 * HEAD_DIM_A)
    lam_init = 0.8 - 0.6 * math.exp(-0.3 * layer_idx)
    lqk = lp['lambda_qk']
    lam = jnp.exp(jnp.sum(lqk[0] * lqk[1])) - jnp.exp(jnp.sum(lqk[2] * lqk[3])) + lam_init
    if prompt:
        oA = diff_attn_prompt(qA, kA, vA, lam, rel_table)
    else:
        P = k_past.shape[1]
        k_all = jnp.concatenate([k_past.reshape(B, P, N_HEADS_A, 2, HEAD_DIM_A), kA], axis=1)
        v_all = jnp.concatenate([v_past, vA], axis=1)
        oA = diff_attn_core(qA, k_all, v_all, P + jnp.arange(L, dtype=jnp.int32),
                            jnp.arange(P + L, dtype=jnp.int32), lam, rel_table, None)
    oA = (rmsnorm(oA, lp['subln']) * (1.0 - lam_init)).reshape(B, L, WIDTH_A)

    qkv_in = z[..., OFF_QKVB:OFF_GB]
    hist = jnp.concatenate([conv_hist, qkv_in], axis=1)
    conv_state = hist[:, -(CONV_W - 1):]
    c = jax.nn.silu(causal_conv(hist, lp['conv_w']))
    qB = l2norm(c[..., :WIDTH_B].reshape(B, L, N_HEADS_B, HEAD_DIM_B)) * (HEAD_DIM_B ** -0.5)
    kB = l2norm(c[..., WIDTH_B:2 * WIDTH_B].reshape(B, L, N_HEADS_B, HEAD_DIM_B))
    vB = c[..., 2 * WIDTH_B:].reshape(B, L, N_HEADS_B, HEAD_DIM_B)
    beta = jax.nn.sigmoid(z[..., OFF_BETA:OFF_ALPHA])
    g = -jnp.exp(lp['a_log']) * jax.nn.softplus(z[..., OFF_ALPHA:OFF_UC] + lp['dt_bias'])
    segments = ((0, N_META, N_META), (N_META, L, CHUNK)) if prompt else ((0, L, L),)
    oB, delta = gdn_segments(qB, kB, vB, g, beta, delta0.astype(f32), segments)
    gate_b = z[..., OFF_GB:OFF_BETA].reshape(B, L, N_HEADS_B, HEAD_DIM_B)
    oB = (rmsnorm(oB, lp['gdn_norm']) * jax.nn.silu(gate_b)).reshape(B, L, WIDTH_B)

    u = z[..., OFF_UC:OFF_GATE].reshape(B, L, S5_GROUPS, S5_GROUP)
    yC, ssm = s5_scan(u, ssm0_re, ssm0_im, lp['s5_a_re'], lp['s5_a_im'], lp['s5_b_re'], lp['s5_b_im'],
                      lp['s5_c_re'], lp['s5_c_im'], lp['s5_d'], lp['s5_log_dt'])
    gl = dense(jax.nn.gelu(yC.reshape(B, L, WIDTH_C)), lp['w_glu'])
    oC = gl[..., :WIDTH_C] * jax.nn.sigmoid(gl[..., WIDTH_C:])

    gates = jax.nn.sigmoid(z[..., OFF_GATE:].reshape(B, L, 3, D_MODEL))
    merged = (gates[:, :, 0] * dense(oA, lp['wb_a']) + gates[:, :, 1] * dense(oB, lp['wb_b'])
              + gates[:, :, 2] * dense(oC, lp['wb_c']))
    x = x + dense(merged, lp['w_out'])

    h2 = rmsnorm(x, lp['norm2'])
    x = x + peer(h2.reshape(B * L, D_MODEL), lp['peer_wq'], lp['peer_subkeys'], lp['peer_u'], lp['peer_v']).reshape(B, L, D_MODEL)
    return (x, kA.reshape(B, L, N_HEADS_A, 2 * HEAD_DIM_A), vA, conv_state, delta,
            ssm.real, ssm.imag)


def kernel(x_prompt, x_sample, cache_k, cache_v, state_conv, state_delta, state_ssm_re, state_ssm_im,
           meta_tokens, rel_bias, norm1_g, norm2_g, final_norm_g, w_in, q_norm_g, k_norm_g, lambda_qk,
           subln_g, conv_w, gdn_a_log, gdn_dt_bias, gdn_norm_g, s5_a_re, s5_a_im, s5_b_re, s5_b_im,
           s5_c_re, s5_c_im, s5_d, s5_log_dt, w_glu, w_branch_a, w_branch_b, w_branch_c, w_out,
           peer_wq, peer_subkeys, peer_u, peer_v):
    params = [dict(norm1=norm1_g[l], norm2=norm2_g[l], w_in=w_in[l], q_norm=q_norm_g[l], k_norm=k_norm_g[l],
                   lambda_qk=lambda_qk[l], subln=subln_g[l], conv_w=conv_w[l], a_log=gdn_a_log[l],
                   dt_bias=gdn_dt_bias[l], gdn_norm=gdn_norm_g[l], s5_a_re=s5_a_re[l], s5_a_im=s5_a_im[l],
                   s5_b_re=s5_b_re[l], s5_b_im=s5_b_im[l], s5_c_re=s5_c_re[l], s5_c_im=s5_c_im[l],
                   s5_d=s5_d[l], s5_log_dt=s5_log_dt[l], w_glu=w_glu[l], wb_a=w_branch_a[l],
                   wb_b=w_branch_b[l], wb_c=w_branch_c[l], w_out=w_out[l], peer_wq=peer_wq[l],
                   peer_subkeys=peer_subkeys[l], peer_u=peer_u[l], peer_v=peer_v[l])
              for l in range(DEPTH)]
    f32 = jnp.float32

    B = x_prompt.shape[0]
    xp = jnp.concatenate([jnp.broadcast_to(meta_tokens[None], (B, N_META, D_MODEL)), x_prompt], axis=1)
    outs_p = [[] for _ in range(6)]
    for l in range(DEPTH):
        res = trunk_layer(
            xp, params[l], l, rel_bias, True, None, None,
            jnp.zeros((B, CONV_W - 1, 3 * WIDTH_B), f32),
            jnp.zeros((B, N_HEADS_B, HEAD_DIM_B, HEAD_DIM_B), f32),
            jnp.zeros((B, S5_GROUPS, S5_STATE), f32), jnp.zeros((B, S5_GROUPS, S5_STATE), f32))
        xp = res[0]
        for acc, r in zip(outs_p, res[1:]):
            acc.append(r)
    y_prompt = rmsnorm(xp, final_norm_g)[:, N_META:]

    xs = x_sample
    outs_s = [[] for _ in range(6)]
    for l in range(DEPTH):
        res = trunk_layer(
            xs, params[l], l, rel_bias, False, cache_k[l], cache_v[l], state_conv[l],
            state_delta[l], state_ssm_re[l], state_ssm_im[l])
        xs = res[0]
        for acc, r in zip(outs_s, res[1:]):
            acc.append(r)
    y_sample = rmsnorm(xs, final_norm_g)

    kp, vp, cp, dp, srp, sip = [jnp.stack(a) for a in outs_p]
    ks_, vs_, cs_, ds_, srs, sis = [jnp.stack(a) for a in outs_s]
    return (y_prompt, y_sample, kp, vp, ks_, vs_, cp, cs_, dp, ds_, srp, sip, srs, sis)
```

```python
import math

import jax
import jax.numpy as jnp
from jax import lax
from jax.experimental import pallas as pl
from jax.experimental.pallas import tpu as pltpu

D_MODEL = 1024
DEPTH = 4
CHUNK = 64
N_META = 16
EPS = 1e-6
NEG_INF = -1e30
N_HEADS_A = 4
HEAD_DIM_A = 64
NUM_BUCKETS = 32
MAX_DISTANCE = 128
N_HEADS_B = 4
HEAD_DIM_B = 128
CONV_W = 4
S5_GROUP = 16
S5_GROUPS = 32
S5_STATE = 64
PEER_HEADS = 8
PEER_QDIM = 256
N_KEYS = 128
PEER_TOPK = 16

WIDTH_A = N_HEADS_A * 2 * HEAD_DIM_A
WIDTH_B = N_HEADS_B * HEAD_DIM_B
WIDTH_C = S5_GROUPS * S5_GROUP
OFF_QA = 0
OFF_KA = OFF_QA + WIDTH_A
OFF_VA = OFF_KA + WIDTH_A
OFF_QKVB = OFF_VA + WIDTH_A
OFF_GB = OFF_QKVB + 3 * WIDTH_B
OFF_BETA = OFF_GB + WIDTH_B
OFF_ALPHA = OFF_BETA + N_HEADS_B
OFF_UC = OFF_ALPHA + N_HEADS_B
OFF_GATE = OFF_UC + WIDTH_C
D_IN = OFF_GATE + 3 * D_MODEL

LANE = 128
VMEM_LIMIT = 56 * 1024 * 1024
HIGHEST = lax.Precision.HIGHEST


def _mm_kernel(x_ref, w_ref, o_ref):
    o_ref[...] = jnp.dot(x_ref[...].astype(jnp.bfloat16), w_ref[...],
                         preferred_element_type=jnp.float32)


def matmul(x, w, tm=512):
    M, K = x.shape
    N = w.shape[1]
    n_pad = -(-N // LANE) * LANE
    wb = w.astype(jnp.bfloat16)
    if n_pad != N:
        wb = jnp.pad(wb, ((0, 0), (0, n_pad - N)))
    n_lanes = n_pad // LANE
    tn = LANE * max(d for d in (4, 3, 2, 1) if n_lanes % d == 0)
    tm = min(tm, M)
    out = pl.pallas_call(
        _mm_kernel,
        grid=(pl.cdiv(M, tm), n_pad // tn),
        in_specs=[pl.BlockSpec((tm, K), lambda i, j: (i, 0)),
                  pl.BlockSpec((K, tn), lambda i, j: (0, j))],
        out_specs=pl.BlockSpec((tm, tn), lambda i, j: (i, j)),
        out_shape=jax.ShapeDtypeStruct((M, n_pad), jnp.float32),
        compiler_params=pltpu.CompilerParams(dimension_semantics=("parallel", "parallel")),
        name="dense_proj",
    )(x, wb)
    return out[:, :N] if n_pad != N else out


def dense(x, w):
    lead = x.shape[:-1]
    return matmul(x.reshape(-1, x.shape[-1]), w).reshape(lead + (w.shape[1],))


def rmsnorm(x, g):
    xf = x.astype(jnp.float32)
    y = xf * lax.rsqrt(jnp.mean(xf * xf, axis=-1, keepdims=True) + EPS)
    return (y * g.astype(jnp.float32)).astype(x.dtype)


def l2norm(x):
    xf = x.astype(jnp.float32)
    return xf * lax.rsqrt(jnp.sum(xf * xf, axis=-1, keepdims=True) + EPS)


def t5_bias(q_pos, k_pos, table):
    rel = k_pos[None, :] - q_pos[:, None]
    half = NUM_BUCKETS // 2
    exact = half // 2
    n = jnp.abs(rel)
    nf = jnp.maximum(n, 1).astype(jnp.float32)
    far = exact + (jnp.log(nf / exact) / math.log(MAX_DISTANCE / exact) * (half - exact)).astype(jnp.int32)
    bucket = jnp.where(rel > 0, half, 0) + jnp.where(n < exact, n, jnp.minimum(far, half - 1))
    return jnp.moveaxis(table[bucket].astype(jnp.float32), -1, 0)


def causal_conv(hist, w):
    L = hist.shape[1] - (CONV_W - 1)
    out = hist[:, 0:L] * w[0]
    for i in range(1, CONV_W):
        out = out + hist[:, i:i + L] * w[i]
    return out


def gdn_chunk_scan(q, k, v, g, beta, S0, C):
    B, L, H = g.shape
    n = L // C

    def to_chunks(t):
        t = t.reshape((B, n, C) + t.shape[2:])
        return jnp.moveaxis(jnp.moveaxis(t, 1, 0), 3, 2)

    tri = jnp.tril(jnp.ones((C, C), dtype=bool))
    strict = jnp.tril(jnp.ones((C, C), dtype=bool), -1)
    eye = jnp.eye(C, dtype=jnp.float32)

    def step(S, inp):
        q_c, k_c, v_c, g_c, b_c = inp
        G = jnp.cumsum(g_c, axis=-1)
        decay = jnp.exp(jnp.where(tri, G[..., :, None] - G[..., None, :], -jnp.inf))
        kb = k_c * b_c[..., None]
        A = jnp.where(strict, jnp.einsum('bhid,bhjd->bhij', kb, k_c) * decay, 0.0)
        rhs = jnp.concatenate([v_c * b_c[..., None], kb * jnp.exp(G)[..., None]], axis=-1)
        sol = lax.linalg.triangular_solve(eye + A, rhs, left_side=True, lower=True, unit_diagonal=True)
        u, w = sol[..., :HEAD_DIM_B], sol[..., HEAD_DIM_B:]
        v_new = u - jnp.einsum('bhck,bhkv->bhcv', w, S)
        attn = jnp.einsum('bhid,bhjd->bhij', q_c, k_c) * decay
        o = jnp.einsum('bhck,bhkv->bhcv', q_c * jnp.exp(G)[..., None], S) + jnp.einsum('bhij,bhjv->bhiv', attn, v_new)
        G_last = G[..., -1:]
        S = S * jnp.exp(G_last)[..., None] + jnp.einsum('bhck,bhcv->bhkv', k_c * jnp.exp(G_last - G)[..., None], v_new)
        return S, o

    S, o = lax.scan(step, S0, (to_chunks(q), to_chunks(k), to_chunks(v), to_chunks(g), to_chunks(beta)))
    o = jnp.moveaxis(jnp.moveaxis(o, 2, 3), 0, 1).reshape(B, L, H, HEAD_DIM_B)
    return o, S


def gdn_segments(q, k, v, g, beta, S, segments):
    outs = []
    for (a, b, c) in segments:
        o, S = gdn_chunk_scan(q[:, a:b], k[:, a:b], v[:, a:b], g[:, a:b], beta[:, a:b], S, c)
        outs.append(o)
    return jnp.concatenate(outs, axis=1), S


def _ssm_combine(e1, e2):
    a1, b1 = e1
    a2, b2 = e2
    return a1 * a2, a2 * b1 + b2


def s5_scan(u, x0_re, x0_im, a_re, a_im, b_re, b_im, c_re, c_im, d, log_dt):
    f32 = jnp.float32
    lam = lax.complex(a_re, a_im)
    lam_bar = jnp.exp(lam * jnp.exp(log_dt)[:, None])
    b_bar = ((lam_bar - 1.0) / lam)[..., None] * lax.complex(b_re, b_im)
    c = lax.complex(c_re, c_im)
    bu = jnp.einsum('blgi,gpi->blgp', u.astype(jnp.complex64), b_bar)
    x0 = lax.complex(x0_re.astype(f32), x0_im.astype(f32))
    bu = bu.at[:, 0].add(lam_bar * x0)
    a = jnp.broadcast_to(lam_bar, bu.shape)
    _, xs = lax.associative_scan(_ssm_combine, (a, bu), axis=1)
    y = jnp.einsum('blgp,gip->blgi', xs, c).real + d * u
    return y, xs[:, -1]


FRONT = 256
PAD = FRONT - N_META
ATT_TQ = 256
ATT_TK = 256


def _qk_prep_kernel(q_ref, k_ref, v_ref, gq_ref, gk_ref, seg_ref, qn_ref, kn_ref, knb_ref, vb_ref):
    f32 = jnp.float32
    seg = seg_ref[...]

    def norm(x, g):
        ms = jnp.dot(x * x, seg, precision=HIGHEST, preferred_element_type=f32)
        return x * lax.rsqrt(ms + EPS) * g

    qn_ref[...] = (norm(q_ref[...], gq_ref[...]) * (HEAD_DIM_A ** -0.5)).astype(jnp.bfloat16)
    kn = norm(k_ref[...], gk_ref[...])
    kn_ref[...] = kn
    knb_ref[...] = kn.astype(jnp.bfloat16)
    vb_ref[...] = v_ref[...].astype(jnp.bfloat16)


def qk_prep(z, gq, gk, tm):
    T = z.shape[0]
    seg = jnp.kron(jnp.eye(WIDTH_A // HEAD_DIM_A, dtype=jnp.float32),
                   jnp.full((HEAD_DIM_A, HEAD_DIM_A), 1.0 / HEAD_DIM_A, jnp.float32))
    row = lambda j: pl.BlockSpec((tm, WIDTH_A), lambda i, j=j: (i, j))
    const = lambda shape: pl.BlockSpec(shape, lambda i: (0,) * len(shape))
    out = lambda: pl.BlockSpec((tm, WIDTH_A), lambda i: (i, 0))
    bshape = jax.ShapeDtypeStruct((T, WIDTH_A), jnp.bfloat16)
    return pl.pallas_call(
        _qk_prep_kernel,
        grid=(T // tm,),
        in_specs=[row(0), row(1), row(2), const((1, WIDTH_A)), const((1, WIDTH_A)), const((WIDTH_A, WIDTH_A))],
        out_specs=[out(), out(), out(), out()],
        out_shape=[bshape, jax.ShapeDtypeStruct((T, WIDTH_A), jnp.float32), bshape, bshape],
        compiler_params=pltpu.CompilerParams(dimension_semantics=("parallel",)),
        name="qk_prep",
    )(z, z, z, jnp.tile(gq, WIDTH_A // HEAD_DIM_A).reshape(1, WIDTH_A),
      jnp.tile(gk, WIDTH_A // HEAD_DIM_A).reshape(1, WIDTH_A), seg)


def _split_maps(q):
    lane = lax.broadcasted_iota(jnp.int32, q.shape, 1)
    zero = jnp.zeros_like(q)
    return jnp.where(lane < HEAD_DIM_A, q, zero), jnp.where(lane >= HEAD_DIM_A, q, zero)


def _subln(o, g, scale):
    return o * lax.rsqrt(jnp.mean(o * o, axis=-1, keepdims=True) + EPS) * g * scale


def _attn_prompt_kernel(sc_ref, far_ref, q_ref, k_ref, v_ref, bias_ref, g_ref, o_ref, m_ref, l_ref, acc_ref):
    f32 = jnp.float32
    h = pl.program_id(1)
    qi = pl.program_id(2)
    tq, tk = ATT_TQ, ATT_TK
    qs = _split_maps(q_ref[0])
    m_ref[...] = jnp.full(m_ref.shape, NEG_INF, f32)
    l_ref[...] = jnp.zeros(l_ref.shape, f32)
    acc_ref[...] = jnp.zeros(acc_ref.shape, f32)
    far = far_ref[h]

    def tile(kj, general):
        k0 = pl.multiple_of(kj * tk, tk)
        kt = k_ref[0, pl.ds(k0, tk), :]
        vt = v_ref[0, pl.ds(k0, tk), :]
        if general:
            bias = bias_ref[0, jnp.clip(kj - qi + 2, 0, 2)]
            qpos = qi * tq + lax.broadcasted_iota(jnp.int32, (tq, tk), 0)
            kpos = kj * tk + lax.broadcasted_iota(jnp.int32, (tq, tk), 1)
            qchunk = jnp.where(qpos < FRONT, 0, 1 + jnp.right_shift(qpos - FRONT, 6))
            kchunk = jnp.where(kpos < FRONT, 0, 1 + jnp.right_shift(kpos - FRONT, 6))
            mask = (kpos >= PAD) & (kchunk <= qchunk)
        for c in range(2):
            s = lax.dot_general(qs[c], kt, (((1,), (1,)), ((), ())), preferred_element_type=f32)
            if general:
                s = jnp.where(mask, s + bias, NEG_INF)
            else:
                s = s + far
            m_old = m_ref[c]
            m_new = jnp.maximum(m_old, jnp.max(s, axis=1, keepdims=True))
            p = jnp.exp(s - m_new)
            alpha = jnp.exp(m_old - m_new)
            l_ref[c] = alpha * l_ref[c] + jnp.sum(p, axis=1, keepdims=True)
            acc_ref[c] = alpha * acc_ref[c] + jnp.dot(p.astype(jnp.bfloat16), vt, preferred_element_type=f32)
            m_ref[c] = m_new

    tile(0, True)

    def far_body(kj, carry):
        tile(kj, False)
        return carry

    lax.fori_loop(1, jnp.maximum(qi - 1, 1), far_body, 0)

    @pl.when(qi >= 2)
    def _():
        tile(qi - 1, True)

    @pl.when(qi >= 1)
    def _():
        tile(qi, True)

    o = acc_ref[0] / l_ref[0] - sc_ref[0] * (acc_ref[1] / l_ref[1])
    o_ref[0] = _subln(o, g_ref[...], sc_ref[1])


def attn_prompt(qn, knb, vb, scalars, far, bias_tiles, subln_g):
    B, Lp, _ = qn.shape
    hd = 2 * HEAD_DIM_A
    smem = pl.BlockSpec(memory_space=pltpu.SMEM)
    return pl.pallas_call(
        _attn_prompt_kernel,
        grid=(B, N_HEADS_A, Lp // ATT_TQ),
        in_specs=[smem, smem,
                  pl.BlockSpec((1, ATT_TQ, hd), lambda b, h, i: (b, i, h)),
                  pl.BlockSpec((1, Lp, hd), lambda b, h, i: (b, 0, h)),
                  pl.BlockSpec((1, Lp, hd), lambda b, h, i: (b, 0, h)),
                  pl.BlockSpec((1, 3, ATT_TQ, ATT_TK), lambda b, h, i: (h, 0, 0, 0)),
                  pl.BlockSpec((1, hd), lambda b, h, i: (0, 0))],
        out_specs=pl.BlockSpec((1, ATT_TQ, hd), lambda b, h, i: (b, i, h)),
        out_shape=jax.ShapeDtypeStruct((B, Lp, WIDTH_A), jnp.float32),
        scratch_shapes=[pltpu.VMEM((2, ATT_TQ, 1), jnp.float32), pltpu.VMEM((2, ATT_TQ, 1), jnp.float32),
                        pltpu.VMEM((2, ATT_TQ, hd), jnp.float32)],
        compiler_params=pltpu.CompilerParams(dimension_semantics=("parallel", "parallel", "parallel")),
        name="diff_attn_prompt",
    )(scalars, far, qn, knb, vb, bias_tiles, subln_g.reshape(1, hd))


def _attn_sample_kernel(sc_ref, q_ref, kp_ref, vp_ref, kn_ref, vn_ref, bp_ref, bn_ref, g_ref, o_ref):
    f32 = jnp.float32
    bf16 = jnp.bfloat16
    qs = _split_maps(q_ref[0])
    kp = kp_ref[0].astype(bf16)
    vp = vp_ref[0].astype(bf16)
    kn = kn_ref[0]
    vn = vn_ref[0]
    dn = (((1,), (1,)), ((), ()))
    outs = []
    for c in range(2):
        sp = lax.dot_general(qs[c], kp, dn, preferred_element_type=f32) + bp_ref[0]
        sn = lax.dot_general(qs[c], kn, dn, preferred_element_type=f32) + bn_ref[0]
        m = jnp.maximum(jnp.max(sp, axis=1, keepdims=True), jnp.max(sn, axis=1, keepdims=True))
        pp = jnp.exp(sp - m)
        pn = jnp.exp(sn - m)
        l = jnp.sum(pp, axis=1, keepdims=True) + jnp.sum(pn, axis=1, keepdims=True)
        acc = (jnp.dot(pp.astype(bf16), vp, preferred_element_type=f32)
               + jnp.dot(pn.astype(bf16), vn, preferred_element_type=f32))
        outs.append(acc / l)
    o_ref[0] = _subln(outs[0] - sc_ref[0] * outs[1], g_ref[...], sc_ref[1])


def attn_sample(qn, knb, vb, k_past, v_past, scalars, bias_past, bias_new, subln_g):
    B, L, _ = qn.shape
    P = k_past.shape[1]
    hd = 2 * HEAD_DIM_A
    new = pl.BlockSpec((1, L, hd), lambda b, h: (b, 0, h))
    past = pl.BlockSpec((1, P, hd), lambda b, h: (b, 0, h))
    return pl.pallas_call(
        _attn_sample_kernel,
        grid=(B, N_HEADS_A),
        in_specs=[pl.BlockSpec(memory_space=pltpu.SMEM), new, past, past, new, new,
                  pl.BlockSpec((1, L, P), lambda b, h: (h, 0, 0)),
                  pl.BlockSpec((1, L, L), lambda b, h: (h, 0, 0)),
                  pl.BlockSpec((1, hd), lambda b, h: (0, 0))],
        out_specs=new,
        out_shape=jax.ShapeDtypeStruct((B, L, WIDTH_A), jnp.float32),
        compiler_params=pltpu.CompilerParams(dimension_semantics=("parallel", "parallel")),
        name="diff_attn_sample",
    )(scalars, qn, k_past, v_past, knb, vb, bias_past, bias_new, subln_g.reshape(1, hd))


N_HC = 2 * PEER_HEADS
HALF_Q = PEER_QDIM // 2
PEER_EXPERT_TILE = 8 * N_KEYS


def _extract_top16(s, iota_f):
    n = float(s.shape[0])
    bits = jnp.zeros(s.shape, jnp.int32)
    vals = []
    for j in range(PEER_TOPK):
        m = jnp.max(s, axis=0, keepdims=True)
        idx = jnp.min(jnp.where(s == m, iota_f, n), axis=0, keepdims=True)
        hit = iota_f == idx
        bits = jnp.where(hit, 1 << j, bits)
        s = jnp.where(hit, -jnp.inf, s)
        vals.append(m)
    return vals, bits


def _peer_router_kernel(x_ref, g_ref, wqt_ref, sk_ref, h2_ref, rm_ref, b2_ref, e1_ref, e2_ref,
                        qt_ref, s_ref, bits_ref, v_ref):
    f32 = jnp.float32
    tl = x_ref.shape[0]
    x = x_ref[...]
    h2 = x * lax.rsqrt(jnp.mean(x * x, axis=-1, keepdims=True) + EPS) * g_ref[...]
    h2b = h2.astype(jnp.bfloat16)
    h2_ref[...] = h2b
    qt_ref[...] = lax.dot_general(wqt_ref[...], h2b, (((1,), (1,)), ((), ())),
                                  preferred_element_type=f32).astype(jnp.bfloat16)

    def score_body(hc, carry):
        r0 = pl.multiple_of(hc * HALF_Q, HALF_Q)
        s_ref[hc] = jnp.dot(sk_ref[hc], qt_ref[pl.ds(r0, HALF_Q), :], preferred_element_type=f32)
        return carry

    lax.fori_loop(0, N_HC, score_body, 0)

    key_iota = lax.broadcasted_iota(jnp.int32, (N_KEYS, LANE), 0).astype(f32)

    def key_body(i, carry):
        hc = i // (tl // LANE)
        c0 = pl.multiple_of((i % (tl // LANE)) * LANE, LANE)
        s = s_ref[hc, :, pl.ds(c0, LANE)]
        vals, bits = _extract_top16(s, key_iota)
        bits_ref[hc, :, pl.ds(c0, LANE)] = bits
        v_ref[hc, :, pl.ds(c0, LANE)] = jnp.concatenate(vals, axis=0)
        return carry

    lax.fori_loop(0, N_HC * (tl // LANE), key_body, 0)

    cand_iota = lax.broadcasted_iota(jnp.int32, (PEER_TOPK * PEER_TOPK, LANE), 0).astype(f32)
    row16 = lax.broadcasted_iota(jnp.int32, (PEER_TOPK, LANE), 0)
    pow2 = jnp.zeros((PEER_TOPK, LANE), f32)
    for k in range(PEER_TOPK):
        pow2 = jnp.where(row16 == k, float(1 << k), pow2)

    def head_body(i, carry):
        h = i // (tl // LANE)
        c0 = pl.multiple_of((i % (tl // LANE)) * LANE, LANE)
        cols = pl.ds(c0, LANE)
        v1 = v_ref[2 * h, :, cols]
        v2 = v_ref[2 * h + 1, :, cols]
        cand = jnp.concatenate([v1[jj:jj + 1, :] + v2 for jj in range(PEER_TOPK)], axis=0)
        best, cbits = _extract_top16(cand, cand_iota)
        z = jnp.zeros((1, LANE), f32)
        for k in range(PEER_TOPK):
            z = z + jnp.exp(best[k] - best[0])
        sel = cbits != 0
        bits1 = bits_ref[2 * h, :, cols]
        bits2 = bits_ref[2 * h + 1, :, cols]
        rm = jnp.zeros((N_KEYS, LANE), jnp.int32)
        for jj in range(PEER_TOPK):
            blk = sel[jj * PEER_TOPK:(jj + 1) * PEER_TOPK, :]
            prow = jnp.sum(jnp.where(blk, pow2, 0.0), axis=0, keepdims=True).astype(jnp.int32)
            rm = jnp.where(bits1 == (1 << jj), prow, rm)
        rm_ref[h, :, cols] = rm
        b2_ref[h, :, cols] = bits2
        e1_ref[h, :, cols] = jnp.exp(s_ref[2 * h, :, cols] - v1[0:1, :])
        e2_ref[h, :, cols] = jnp.exp(s_ref[2 * h + 1, :, cols] - v2[0:1, :]) / z
        return carry

    lax.fori_loop(0, PEER_HEADS * (tl // LANE), head_body, 0)


def _gelu_tanh(x):
    return 0.5 * x * (1.0 + jnp.tanh(math.sqrt(2.0 / math.pi) * (x + 0.044715 * (x * x * x))))


def _peer_expert_kernel(x_ref, h2_ref, pu_ref, pvt_ref, rm_ref, b2_ref, e1_ref, e2_ref, o_ref,
                        acc_ref, a_ref, c_ref):
    f32 = jnp.float32
    e_step = pl.program_id(1)
    tl = h2_ref.shape[0]
    n_i1 = pu_ref.shape[0] // N_KEYS

    @pl.when(e_step == 0)
    def _():
        acc_ref[...] = jnp.zeros_like(acc_ref)

    a_ref[...] = lax.dot_general(pu_ref[...], h2_ref[...], (((1,), (1,)), ((), ())),
                                 preferred_element_type=f32)

    def body(i, carry):
        cols = pl.ds(pl.multiple_of(i * LANE, LANE), LANE)
        for ii in range(n_i1):
            rows = slice(ii * N_KEYS, (ii + 1) * N_KEYS)
            w = jnp.zeros((N_KEYS, LANE), f32)
            for h in range(PEER_HEADS):
                sel = (rm_ref[h, ii:ii + 1, cols] & b2_ref[h, :, cols]) != 0
                w = w + jnp.where(sel, e2_ref[h, :, cols], 0.0) * e1_ref[h, ii:ii + 1, cols]
            c_ref[rows, cols] = (w * _gelu_tanh(a_ref[rows, cols])).astype(jnp.bfloat16)
        return carry

    lax.fori_loop(0, tl // LANE, body, 0)
    acc_ref[...] += jnp.dot(pvt_ref[...], c_ref[...], preferred_element_type=f32)

    @pl.when(e_step == pl.num_programs(1) - 1)
    def _():
        o_ref[...] = x_ref[...] + acc_ref[...].T


def _token_tile(T):
    for tl in (640, 512, 384, 256, 128):
        if T % tl == 0:
            return tl
    raise ValueError(f"token count {T} is not a multiple of {LANE}")


def peer_residual(x, g, wqt, sk, pu, pvt):
    T = x.shape[0]
    tl = _token_tile(T)
    nt = T // tl
    n_exp = pu.shape[0]
    head_shape = jax.ShapeDtypeStruct((PEER_HEADS, N_KEYS, T), jnp.int32)
    head_shape_f = jax.ShapeDtypeStruct((PEER_HEADS, N_KEYS, T), jnp.float32)
    head_spec = pl.BlockSpec((PEER_HEADS, N_KEYS, tl), lambda i: (0, 0, i))
    h2, rm, b2, e1, e2 = pl.pallas_call(
        _peer_router_kernel,
        grid=(nt,),
        in_specs=[pl.BlockSpec((tl, D_MODEL), lambda i: (i, 0)),
                  pl.BlockSpec((1, D_MODEL), lambda i: (0, 0)),
                  pl.BlockSpec((PEER_HEADS * PEER_QDIM, D_MODEL), lambda i: (0, 0)),
                  pl.BlockSpec((N_HC, N_KEYS, HALF_Q), lambda i: (0, 0, 0))],
        out_specs=[pl.BlockSpec((tl, D_MODEL), lambda i: (i, 0)), head_spec, head_spec, head_spec, head_spec],
        out_shape=[jax.ShapeDtypeStruct((T, D_MODEL), jnp.bfloat16), head_shape, head_shape,
                   head_shape_f, head_shape_f],
        scratch_shapes=[pltpu.VMEM((PEER_HEADS * PEER_QDIM, tl), jnp.bfloat16),
                        pltpu.VMEM((N_HC, N_KEYS, tl), jnp.float32),
                        pltpu.VMEM((N_HC, N_KEYS, tl), jnp.int32),
                        pltpu.VMEM((N_HC, PEER_TOPK, tl), jnp.float32)],
        compiler_params=pltpu.CompilerParams(dimension_semantics=("parallel",), vmem_limit_bytes=VMEM_LIMIT),
        name="peer_router",
    )(x, g.reshape(1, D_MODEL), wqt, sk)

    te = PEER_EXPERT_TILE
    i2_spec = pl.BlockSpec((PEER_HEADS, N_KEYS, tl), lambda i, e: (0, 0, i))
    i1_spec = pl.BlockSpec((PEER_HEADS, te // N_KEYS, tl), lambda i, e: (0, e, i))
    return pl.pallas_call(
        _peer_expert_kernel,
        grid=(nt, n_exp // te),
        in_specs=[pl.BlockSpec((tl, D_MODEL), lambda i, e: (i, 0)),
                  pl.BlockSpec((tl, D_MODEL), lambda i, e: (i, 0)),
                  pl.BlockSpec((te, D_MODEL), lambda i, e: (e, 0)),
                  pl.BlockSpec((D_MODEL, te), lambda i, e: (0, e)),
                  i1_spec, i2_spec, i1_spec, i2_spec],
        out_specs=pl.BlockSpec((tl, D_MODEL), lambda i, e: (i, 0)),
        out_shape=jax.ShapeDtypeStruct((T, D_MODEL), jnp.float32),
        scratch_shapes=[pltpu.VMEM((D_MODEL, tl), jnp.float32),
                        pltpu.VMEM((te, tl), jnp.float32),
                        pltpu.VMEM((te, tl), jnp.bfloat16)],
        compiler_params=pltpu.CompilerParams(dimension_semantics=("parallel", "arbitrary"),
                                             vmem_limit_bytes=VMEM_LIMIT),
        name="peer_experts",
    )(x, h2, pu, pvt, rm, b2, e1, e2)


C_QKVB = 3 * WIDTH_A
C_GB = C_QKVB + 3 * WIDTH_B
C_UC = C_GB + WIDTH_B
C_GATE = C_UC + WIDTH_C
C_BA = C_GATE + 3 * D_MODEL
D_IN2 = C_BA + LANE


def _reorder_w_in(w):
    cols = [w[:, OFF_QA:OFF_BETA], w[:, OFF_UC:OFF_GATE], w[:, OFF_GATE:], w[:, OFF_BETA:OFF_UC],
            jnp.zeros((w.shape[0], LANE - 2 * N_HEADS_B), w.dtype)]
    return jnp.concatenate(cols, axis=1).astype(jnp.bfloat16)


def trunk_layer(x, lp, layer_idx, att, prompt, k_past, v_past, conv_hist, delta0, ssm0_re, ssm0_im):
    f32 = jnp.float32
    B, Lx = x.shape[0], x.shape[1]
    T = B * Lx
    h = rmsnorm(x, lp['norm1'])
    z = matmul(h.reshape(T, D_MODEL), lp['w_in2'])

    qn, kn, knb, vb = qk_prep(z, lp['q_norm'], lp['k_norm'], _token_tile(T))
    lam_init = 0.8 - 0.6 * math.exp(-0.3 * layer_idx)
    lqk = lp['lambda_qk']
    lam = jnp.exp(jnp.sum(lqk[0] * lqk[1])) - jnp.exp(jnp.sum(lqk[2] * lqk[3])) + lam_init
    scalars = jnp.stack([lam, jnp.asarray(1.0 - lam_init, f32)]).astype(f32)
    r3 = lambda a: a.reshape(B, Lx, WIDTH_A)
    if prompt:
        oA = attn_prompt(r3(qn), r3(knb), r3(vb), scalars, att['far'], att['tiles'], lp['subln'])
    else:
        P = k_past.shape[1]
        oA = attn_sample(r3(qn), r3(knb), r3(vb), k_past.reshape(B, P, WIDTH_A), v_past.reshape(B, P, WIDTH_A),
                         scalars, att['bias_past'], att['bias_new'], lp['subln'])

    z3 = z.reshape(B, Lx, D_IN2)
    off = PAD if prompt else 0
    zu = z3[:, off:]
    L = Lx - off
    kA = r3(kn)[:, off:].reshape(B, L, N_HEADS_A, 2 * HEAD_DIM_A)
    vA = zu[..., 2 * WIDTH_A:3 * WIDTH_A].reshape(B, L, N_HEADS_A, 2 * HEAD_DIM_A)

    qkv_in = zu[..., C_QKVB:C_GB]
    hist = jnp.concatenate([conv_hist, qkv_in], axis=1)
    conv_state = hist[:, -(CONV_W - 1):]
    c = jax.nn.silu(causal_conv(hist, lp['conv_w']))
    qB = l2norm(c[..., :WIDTH_B].reshape(B, L, N_HEADS_B, HEAD_DIM_B)) * (HEAD_DIM_B ** -0.5)
    kB = l2norm(c[..., WIDTH_B:2 * WIDTH_B].reshape(B, L, N_HEADS_B, HEAD_DIM_B))
    vB = c[..., 2 * WIDTH_B:].reshape(B, L, N_HEADS_B, HEAD_DIM_B)
    beta = jax.nn.sigmoid(zu[..., C_BA:C_BA + N_HEADS_B])
    g = -jnp.exp(lp['a_log']) * jax.nn.softplus(zu[..., C_BA + N_HEADS_B:C_BA + 2 * N_HEADS_B] + lp['dt_bias'])
    segments = ((0, N_META, N_META), (N_META, L, CHUNK)) if prompt else ((0, L, L),)
    oB, delta = gdn_segments(qB, kB, vB, g, beta, delta0.astype(f32), segments)
    gate_b = zu[..., C_GB:C_UC].reshape(B, L, N_HEADS_B, HEAD_DIM_B)
    oB = (rmsnorm(oB, lp['gdn_norm']) * jax.nn.silu(gate_b)).reshape(B, L, WIDTH_B)

    u = zu[..., C_UC:C_GATE].reshape(B, L, S5_GROUPS, S5_GROUP)
    yC, ssm = s5_scan(u, ssm0_re, ssm0_im, lp['s5_a_re'], lp['s5_a_im'], lp['s5_b_re'], lp['s5_b_im'],
                      lp['s5_c_re'], lp['s5_c_im'], lp['s5_d'], lp['s5_log_dt'])
    gl = dense(jax.nn.gelu(yC.reshape(B, L, WIDTH_C)), lp['w_glu'])
    oC = gl[..., :WIDTH_C] * jax.nn.sigmoid(gl[..., WIDTH_C:])

    if prompt:
        oB = jnp.pad(oB, ((0, 0), (PAD, 0), (0, 0)))
        oC = jnp.pad(oC, ((0, 0), (PAD, 0), (0, 0)))
    gates = jax.nn.sigmoid(z3[..., C_GATE:C_BA].reshape(B, Lx, 3, D_MODEL))
    merged = (gates[:, :, 0] * dense(oA, lp['wb_a']) + gates[:, :, 1] * dense(oB, lp['wb_b'])
              + gates[:, :, 2] * dense(oC, lp['wb_c']))
    x = x + dense(merged, lp['w_out'])

    x = peer_residual(x.reshape(T, D_MODEL), lp['norm2'], lp['peer_wqt'], lp['peer_sk'], lp['peer_pu'],
                      lp['peer_pvt']).reshape(B, Lx, D_MODEL)
    return (x, kA, vA, conv_state, delta, ssm.real, ssm.imag)


def kernel(x_prompt, x_sample, cache_k, cache_v, state_conv, state_delta, state_ssm_re, state_ssm_im,
           meta_tokens, rel_bias, norm1_g, norm2_g, final_norm_g, w_in, q_norm_g, k_norm_g, lambda_qk,
           subln_g, conv_w, gdn_a_log, gdn_dt_bias, gdn_norm_g, s5_a_re, s5_a_im, s5_b_re, s5_b_im,
           s5_c_re, s5_c_im, s5_d, s5_log_dt, w_glu, w_branch_a, w_branch_b, w_branch_c, w_out,
           peer_wq, peer_subkeys, peer_u, peer_v):
    f32 = jnp.float32
    bf16 = jnp.bfloat16
    params = [dict(norm1=norm1_g[l], norm2=norm2_g[l], w_in2=_reorder_w_in(w_in[l]), q_norm=q_norm_g[l],
                   k_norm=k_norm_g[l], lambda_qk=lambda_qk[l], subln=subln_g[l], conv_w=conv_w[l],
                   a_log=gdn_a_log[l], dt_bias=gdn_dt_bias[l], gdn_norm=gdn_norm_g[l], s5_a_re=s5_a_re[l],
                   s5_a_im=s5_a_im[l], s5_b_re=s5_b_re[l], s5_b_im=s5_b_im[l], s5_c_re=s5_c_re[l],
                   s5_c_im=s5_c_im[l], s5_d=s5_d[l], s5_log_dt=s5_log_dt[l], w_glu=w_glu[l],
                   wb_a=w_branch_a[l], wb_b=w_branch_b[l], wb_c=w_branch_c[l], w_out=w_out[l],
                   peer_wqt=peer_wq[l].T.astype(bf16),
                   peer_sk=peer_subkeys[l].reshape(N_HC, N_KEYS, HALF_Q).astype(bf16),
                   peer_pu=peer_u[l].astype(bf16), peer_pvt=peer_v[l].T.astype(bf16))
              for l in range(DEPTH)]

    qpos = jnp.arange(ATT_TQ, dtype=jnp.int32)
    tiles = jnp.stack([t5_bias(qpos, d * ATT_TK + jnp.arange(ATT_TK, dtype=jnp.int32), rel_bias)
                       for d in (-2, -1, 0)], axis=1)
    far = t5_bias(jnp.full((1,), 2 * ATT_TK, jnp.int32), jnp.zeros((1,), jnp.int32), rel_bias).reshape(N_HEADS_A)
    P, Ls = cache_k.shape[2], x_sample.shape[1]
    bias_s = t5_bias(P + jnp.arange(Ls, dtype=jnp.int32), jnp.arange(P + Ls, dtype=jnp.int32), rel_bias)
    att_p = dict(tiles=tiles, far=far)
    att_s = dict(bias_past=bias_s[:, :, :P], bias_new=bias_s[:, :, P:])

    B = x_prompt.shape[0]
    xp = jnp.concatenate([jnp.zeros((B, PAD, D_MODEL), f32),
                          jnp.broadcast_to(meta_tokens[None], (B, N_META, D_MODEL)), x_prompt], axis=1)
    outs_p = [[] for _ in range(6)]
    for l in range(DEPTH):
        res = trunk_layer(
            xp, params[l], l, att_p, True, None, None,
            jnp.zeros((B, CONV_W - 1, 3 * WIDTH_B), f32),
            jnp.zeros((B, N_HEADS_B, HEAD_DIM_B, HEAD_DIM_B), f32),
            jnp.zeros((B, S5_GROUPS, S5_STATE), f32), jnp.zeros((B, S5_GROUPS, S5_STATE), f32))
        xp = res[0]
        for acc, r in zip(outs_p, res[1:]):
            acc.append(r)
    y_prompt = rmsnorm(xp, final_norm_g)[:, FRONT:]

    xs = x_sample
    outs_s = [[] for _ in range(6)]
    for l in range(DEPTH):
        res = trunk_layer(
            xs, params[l], l, att_s, False, cache_k[l], cache_v[l], state_conv[l],
            state_delta[l], state_ssm_re[l], state_ssm_im[l])
        xs = res[0]
        for acc, r in zip(outs_s, res[1:]):
            acc.append(r)
    y_sample = rmsnorm(xs, final_norm_g)

    kp, vp, cp, dp, srp, sip = [jnp.stack(a) for a in outs_p]
    ks_, vs_, cs_, ds_, srs, sis = [jnp.stack(a) for a in outs_s]
    return (y_prompt, y_sample, kp, vp, ks_, vs_, cp, cs_, dp, ds_, srp, sip, srs, sis)
```

```python
import functools
import math

import jax
import jax.numpy as jnp
from jax import lax
from jax.experimental import pallas as pl
from jax.experimental.pallas import tpu as pltpu

D_MODEL = 1024
DEPTH = 4
CHUNK = 64
N_META = 16
EPS = 1e-6
NEG_INF = -1e30
N_HEADS_A = 4
HEAD_DIM_A = 64
NUM_BUCKETS = 32
MAX_DISTANCE = 128
N_HEADS_B = 4
HEAD_DIM_B = 128
CONV_W = 4
S5_GROUP = 16
S5_GROUPS = 32
S5_STATE = 64
PEER_HEADS = 8
PEER_QDIM = 256
N_KEYS = 128
PEER_TOPK = 16

WIDTH_A = N_HEADS_A * 2 * HEAD_DIM_A
WIDTH_B = N_HEADS_B * HEAD_DIM_B
WIDTH_C = S5_GROUPS * S5_GROUP
OFF_QA = 0
OFF_KA = OFF_QA + WIDTH_A
OFF_VA = OFF_KA + WIDTH_A
OFF_QKVB = OFF_VA + WIDTH_A
OFF_GB = OFF_QKVB + 3 * WIDTH_B
OFF_BETA = OFF_GB + WIDTH_B
OFF_ALPHA = OFF_BETA + N_HEADS_B
OFF_UC = OFF_ALPHA + N_HEADS_B
OFF_GATE = OFF_UC + WIDTH_C
D_IN = OFF_GATE + 3 * D_MODEL

LANE = 128
VMEM_LIMIT = 56 * 1024 * 1024
HIGHEST = lax.Precision.HIGHEST


def _norm_mm_kernel(x_ref, g_ref, w_ref, o_ref, h_ref):
    @pl.when(pl.program_id(1) == 0)
    def _():
        x = x_ref[...]
        h = x * lax.rsqrt(jnp.mean(x * x, axis=-1, keepdims=True) + EPS) * g_ref[...]
        h_ref[...] = h.astype(jnp.bfloat16)

    o_ref[...] = jnp.dot(h_ref[...], w_ref[...], preferred_element_type=jnp.float32)


def norm_matmul(x, g, wb, tm=512):
    M, K = x.shape
    N = wb.shape[1]
    n_lanes = N // LANE
    tn = LANE * max(d for d in (4, 3, 2, 1) if n_lanes % d == 0)
    tm = min(tm, M)
    return pl.pallas_call(
        _norm_mm_kernel,
        grid=(pl.cdiv(M, tm), N // tn),
        in_specs=[pl.BlockSpec((tm, K), lambda i, j: (i, 0)),
                  pl.BlockSpec((1, K), lambda i, j: (0, 0)),
                  pl.BlockSpec((K, tn), lambda i, j: (0, j))],
        out_specs=pl.BlockSpec((tm, tn), lambda i, j: (i, j)),
        out_shape=jax.ShapeDtypeStruct((M, N), jnp.float32),
        scratch_shapes=[pltpu.VMEM((tm, K), jnp.bfloat16)],
        compiler_params=pltpu.CompilerParams(dimension_semantics=("parallel", "arbitrary")),
        name="norm_proj",
    )(x, g.reshape(1, K), wb)


def rmsnorm(x, g):
    xf = x.astype(jnp.float32)
    y = xf * lax.rsqrt(jnp.mean(xf * xf, axis=-1, keepdims=True) + EPS)
    return (y * g.astype(jnp.float32)).astype(x.dtype)


def t5_bias(q_pos, k_pos, table):
    rel = k_pos[None, :] - q_pos[:, None]
    half = NUM_BUCKETS // 2
    exact = half // 2
    n = jnp.abs(rel)
    nf = jnp.maximum(n, 1).astype(jnp.float32)
    far = exact + (jnp.log(nf / exact) / math.log(MAX_DISTANCE / exact) * (half - exact)).astype(jnp.int32)
    bucket = jnp.where(rel > 0, half, 0) + jnp.where(n < exact, n, jnp.minimum(far, half - 1))
    return jnp.moveaxis(table[bucket].astype(jnp.float32), -1, 0)


FRONT = 256
PAD = FRONT - N_META
ATT_TQ = 256
ATT_TK = 256


def _qk_prep_kernel(q_ref, k_ref, v_ref, gq_ref, gk_ref, seg_ref, qn_ref, kn_ref, knb_ref, vb_ref):
    f32 = jnp.float32
    seg = seg_ref[...]

    def norm(x, g):
        ms = jnp.dot(x * x, seg, precision=HIGHEST, preferred_element_type=f32)
        return x * lax.rsqrt(ms + EPS) * g

    qn_ref[...] = (norm(q_ref[...], gq_ref[...]) * (HEAD_DIM_A ** -0.5)).astype(jnp.bfloat16)
    kn = norm(k_ref[...], gk_ref[...])
    kn_ref[...] = kn
    knb_ref[...] = kn.astype(jnp.bfloat16)
    vb_ref[...] = v_ref[...].astype(jnp.bfloat16)


def qk_prep(z, gq, gk, tm):
    T = z.shape[0]
    seg = jnp.kron(jnp.eye(WIDTH_A // HEAD_DIM_A, dtype=jnp.float32),
                   jnp.full((HEAD_DIM_A, HEAD_DIM_A), 1.0 / HEAD_DIM_A, jnp.float32))
    row = lambda j: pl.BlockSpec((tm, WIDTH_A), lambda i, j=j: (i, j))
    const = lambda shape: pl.BlockSpec(shape, lambda i: (0,) * len(shape))
    out = lambda: pl.BlockSpec((tm, WIDTH_A), lambda i: (i, 0))
    bshape = jax.ShapeDtypeStruct((T, WIDTH_A), jnp.bfloat16)
    return pl.pallas_call(
        _qk_prep_kernel,
        grid=(T // tm,),
        in_specs=[row(0), row(1), row(2), const((1, WIDTH_A)), const((1, WIDTH_A)), const((WIDTH_A, WIDTH_A))],
        out_specs=[out(), out(), out(), out()],
        out_shape=[bshape, jax.ShapeDtypeStruct((T, WIDTH_A), jnp.float32), bshape, bshape],
        compiler_params=pltpu.CompilerParams(dimension_semantics=("parallel",)),
        name="qk_prep",
    )(z, z, z, jnp.tile(gq, WIDTH_A // HEAD_DIM_A).reshape(1, WIDTH_A),
      jnp.tile(gk, WIDTH_A // HEAD_DIM_A).reshape(1, WIDTH_A), seg)


def _split_maps(q):
    lane = lax.broadcasted_iota(jnp.int32, q.shape, 1)
    zero = jnp.zeros_like(q)
    return jnp.where(lane < HEAD_DIM_A, q, zero), jnp.where(lane >= HEAD_DIM_A, q, zero)


def _subln(o, g, scale):
    return o * lax.rsqrt(jnp.mean(o * o, axis=-1, keepdims=True) + EPS) * g * scale


def _attn_prompt_kernel(sc_ref, far_ref, q_ref, k_ref, v_ref, bias_ref, g_ref, o_ref, m_ref, l_ref, acc_ref):
    f32 = jnp.float32
    h = pl.program_id(1)
    qi = pl.program_id(2)
    tq, tk = ATT_TQ, ATT_TK
    qs = _split_maps(q_ref[0])
    m_ref[...] = jnp.full(m_ref.shape, NEG_INF, f32)
    l_ref[...] = jnp.zeros(l_ref.shape, f32)
    acc_ref[...] = jnp.zeros(acc_ref.shape, f32)
    far = far_ref[h]

    def tile(kj, general):
        k0 = pl.multiple_of(kj * tk, tk)
        kt = k_ref[0, pl.ds(k0, tk), :]
        vt = v_ref[0, pl.ds(k0, tk), :]
        if general:
            bias = bias_ref[0, jnp.clip(kj - qi + 2, 0, 2)]
            qpos = qi * tq + lax.broadcasted_iota(jnp.int32, (tq, tk), 0)
            kpos = kj * tk + lax.broadcasted_iota(jnp.int32, (tq, tk), 1)
            qchunk = jnp.where(qpos < FRONT, 0, 1 + jnp.right_shift(qpos - FRONT, 6))
            kchunk = jnp.where(kpos < FRONT, 0, 1 + jnp.right_shift(kpos - FRONT, 6))
            mask = (kpos >= PAD) & (kchunk <= qchunk)
        for c in range(2):
            s = lax.dot_general(qs[c], kt, (((1,), (1,)), ((), ())), preferred_element_type=f32)
            if general:
                s = jnp.where(mask, s + bias, NEG_INF)
            else:
                s = s + far
            m_old = m_ref[c]
            m_new = jnp.maximum(m_old, jnp.max(s, axis=1, keepdims=True))
            p = jnp.exp(s - m_new)
            alpha = jnp.exp(m_old - m_new)
            l_ref[c] = alpha * l_ref[c] + jnp.sum(p, axis=1, keepdims=True)
            acc_ref[c] = alpha * acc_ref[c] + jnp.dot(p.astype(jnp.bfloat16), vt, preferred_element_type=f32)
            m_ref[c] = m_new

    tile(0, True)

    def far_body(kj, carry):
        tile(kj, False)
        return carry

    lax.fori_loop(1, jnp.maximum(qi - 1, 1), far_body, 0)

    @pl.when(qi >= 2)
    def _():
        tile(qi - 1, True)

    @pl.when(qi >= 1)
    def _():
        tile(qi, True)

    o = acc_ref[0] / l_ref[0] - sc_ref[0] * (acc_ref[1] / l_ref[1])
    o_ref[0] = _subln(o, g_ref[...], sc_ref[1])


def attn_prompt(qn, knb, vb, scalars, far, bias_tiles, subln_g):
    B, Lp, _ = qn.shape
    hd = 2 * HEAD_DIM_A
    smem = pl.BlockSpec(memory_space=pltpu.SMEM)
    return pl.pallas_call(
        _attn_prompt_kernel,
        grid=(B, N_HEADS_A, Lp // ATT_TQ),
        in_specs=[smem, smem,
                  pl.BlockSpec((1, ATT_TQ, hd), lambda b, h, i: (b, i, h)),
                  pl.BlockSpec((1, Lp, hd), lambda b, h, i: (b, 0, h)),
                  pl.BlockSpec((1, Lp, hd), lambda b, h, i: (b, 0, h)),
                  pl.BlockSpec((1, 3, ATT_TQ, ATT_TK), lambda b, h, i: (h, 0, 0, 0)),
                  pl.BlockSpec((1, hd), lambda b, h, i: (0, 0))],
        out_specs=pl.BlockSpec((1, ATT_TQ, hd), lambda b, h, i: (b, i, h)),
        out_shape=jax.ShapeDtypeStruct((B, Lp, WIDTH_A), jnp.float32),
        scratch_shapes=[pltpu.VMEM((2, ATT_TQ, 1), jnp.float32), pltpu.VMEM((2, ATT_TQ, 1), jnp.float32),
                        pltpu.VMEM((2, ATT_TQ, hd), jnp.float32)],
        compiler_params=pltpu.CompilerParams(dimension_semantics=("parallel", "parallel", "parallel")),
        name="diff_attn_prompt",
    )(scalars, far, qn, knb, vb, bias_tiles, subln_g.reshape(1, hd))


def _attn_sample_kernel(sc_ref, q_ref, kp_ref, vp_ref, kn_ref, vn_ref, bp_ref, bn_ref, g_ref, o_ref):
    f32 = jnp.float32
    bf16 = jnp.bfloat16
    qs = _split_maps(q_ref[0])
    kp = kp_ref[0].astype(bf16)
    vp = vp_ref[0].astype(bf16)
    kn = kn_ref[0]
    vn = vn_ref[0]
    dn = (((1,), (1,)), ((), ()))
    outs = []
    for c in range(2):
        sp = lax.dot_general(qs[c], kp, dn, preferred_element_type=f32) + bp_ref[0]
        sn = lax.dot_general(qs[c], kn, dn, preferred_element_type=f32) + bn_ref[0]
        m = jnp.maximum(jnp.max(sp, axis=1, keepdims=True), jnp.max(sn, axis=1, keepdims=True))
        pp = jnp.exp(sp - m)
        pn = jnp.exp(sn - m)
        l = jnp.sum(pp, axis=1, keepdims=True) + jnp.sum(pn, axis=1, keepdims=True)
        acc = (jnp.dot(pp.astype(bf16), vp, preferred_element_type=f32)
               + jnp.dot(pn.astype(bf16), vn, preferred_element_type=f32))
        outs.append(acc / l)
    o_ref[0] = _subln(outs[0] - sc_ref[0] * outs[1], g_ref[...], sc_ref[1])


def attn_sample(qn, knb, vb, k_past, v_past, scalars, bias_past, bias_new, subln_g):
    B, L, _ = qn.shape
    P = k_past.shape[1]
    hd = 2 * HEAD_DIM_A
    new = pl.BlockSpec((1, L, hd), lambda b, h: (b, 0, h))
    past = pl.BlockSpec((1, P, hd), lambda b, h: (b, 0, h))
    return pl.pallas_call(
        _attn_sample_kernel,
        grid=(B, N_HEADS_A),
        in_specs=[pl.BlockSpec(memory_space=pltpu.SMEM), new, past, past, new, new,
                  pl.BlockSpec((1, L, P), lambda b, h: (h, 0, 0)),
                  pl.BlockSpec((1, L, L), lambda b, h: (h, 0, 0)),
                  pl.BlockSpec((1, hd), lambda b, h: (0, 0))],
        out_specs=new,
        out_shape=jax.ShapeDtypeStruct((B, L, WIDTH_A), jnp.float32),
        compiler_params=pltpu.CompilerParams(dimension_semantics=("parallel", "parallel")),
        name="diff_attn_sample",
    )(scalars, qn, k_past, v_past, knb, vb, bias_past, bias_new, subln_g.reshape(1, hd))


S5_N = S5_GROUPS * S5_STATE
S5_SLAB = 8
S5_LANES = 512


def _sigmoid(x):
    return 1.0 / (1.0 + jnp.exp(-x))


def _s5_kernel(u_ref, x0_ref, bmat_ref, cmat_ref, lamp_ref, ppow_ref, d_ref, wglu_ref, oc_ref, xf_ref,
               bu_ref, carry_ref):
    f32 = jnp.float32
    bf16 = jnp.bfloat16
    t = pl.program_id(1)
    tb = u_ref.shape[1]

    @pl.when(t == 0)
    def _():
        carry_ref[...] = x0_ref[0]

    u = u_ref[0]
    bu_ref[...] = jnp.dot(u.astype(bf16), bmat_ref[...], preferred_element_type=f32)

    row = lax.broadcasted_iota(jnp.int32, (S5_SLAB, S5_LANES), 0)
    for c in range(S5_N // S5_LANES):
        re = slice(c * S5_LANES, (c + 1) * S5_LANES)
        im = slice(S5_N + c * S5_LANES, S5_N + (c + 1) * S5_LANES)

        def slab(i, carry):
            cre, cim = carry
            rows = pl.ds(pl.multiple_of(i * S5_SLAB, S5_SLAB), S5_SLAB)
            yre = bu_ref[rows, re]
            yim = bu_ref[rows, im]
            for s in range(3):
                sh = 1 << s
                sre = jnp.where(row >= sh, pltpu.roll(yre, sh, 0), 0.0)
                sim = jnp.where(row >= sh, pltpu.roll(yim, sh, 0), 0.0)
                lr = lamp_ref[s, 0, :, re]
                li = lamp_ref[s, 1, :, re]
                yre, yim = yre + (lr * sre - li * sim), yim + (lr * sim + li * sre)
            pr = ppow_ref[0, :, re]
            pi = ppow_ref[1, :, re]
            yre, yim = yre + (pr * cre - pi * cim), yim + (pr * cim + pi * cre)
            bu_ref[rows, re] = yre
            bu_ref[rows, im] = yim
            last = S5_SLAB - 1
            return (jnp.broadcast_to(yre[last:last + 1, :], yre.shape),
                    jnp.broadcast_to(yim[last:last + 1, :], yim.shape))

        cre, cim = lax.fori_loop(0, tb // S5_SLAB, slab, (carry_ref[:, re], carry_ref[:, im]))
        carry_ref[:, re] = cre
        carry_ref[:, im] = cim

    y = jnp.dot(bu_ref[...].astype(bf16), cmat_ref[...], preferred_element_type=f32) + d_ref[...] * u
    gl = jnp.dot(_gelu_tanh(y).astype(bf16), wglu_ref[...], preferred_element_type=f32)
    oc_ref[0] = gl[:, :WIDTH_C] * _sigmoid(gl[:, WIDTH_C:])

    @pl.when(t == pl.num_programs(1) - 1)
    def _():
        xf_ref[0] = carry_ref[...]


def s5_params(a_re, a_im, b_re, b_im, c_re, c_im, d, log_dt):
    f32 = jnp.float32
    lam = lax.complex(a_re, a_im)
    lam_bar = jnp.exp(lam * jnp.exp(log_dt)[:, None])
    b_bar = ((lam_bar - 1.0) / lam)[..., None] * lax.complex(b_re, b_im)
    eye = jnp.eye(S5_GROUPS, dtype=f32)
    bd_in = lambda m: jnp.einsum('gpi,gh->gihp', m, eye).reshape(WIDTH_C, S5_N)
    bd_out = lambda m: jnp.einsum('gip,gh->gphi', m, eye).reshape(S5_N, WIDTH_C)
    bmat = jnp.concatenate([bd_in(b_bar.real), bd_in(b_bar.imag)], axis=1).astype(jnp.bfloat16)
    cmat = jnp.concatenate([bd_out(c_re), bd_out(-c_im)], axis=0).astype(jnp.bfloat16)
    lb = lam_bar.reshape(S5_N)
    rep = lambda v: jnp.broadcast_to(v[None, :], (S5_SLAB, S5_N))
    pows = [lb, lb * lb, (lb * lb) * (lb * lb)]
    lamp = jnp.stack([jnp.stack([rep(p.real), rep(p.imag)]) for p in pows])
    run = [lb]
    for _ in range(S5_SLAB - 1):
        run.append(run[-1] * lb)
    pp = jnp.stack(run)
    ppow = jnp.stack([pp.real, pp.imag])
    return dict(bmat=bmat, cmat=cmat, lamp=lamp.astype(f32), ppow=ppow.astype(f32), d=d.reshape(1, WIDTH_C))


def s5_glu(z3, col_block, x0_re, x0_im, sp, w_glu, tb):
    B, Lx, _ = z3.shape
    x0 = jnp.concatenate([x0_re.reshape(B, S5_N), x0_im.reshape(B, S5_N)], axis=1)
    x0 = jnp.broadcast_to(x0[:, None, :], (B, S5_SLAB, 2 * S5_N))
    const = lambda a: pl.BlockSpec(a.shape, lambda b, t: (0,) * a.ndim)
    wg = w_glu.astype(jnp.bfloat16)
    oc, xf = pl.pallas_call(
        _s5_kernel,
        grid=(B, Lx // tb),
        in_specs=[pl.BlockSpec((1, tb, WIDTH_C), lambda b, t: (b, t, col_block)),
                  pl.BlockSpec((1, S5_SLAB, 2 * S5_N), lambda b, t: (b, 0, 0)),
                  const(sp['bmat']), const(sp['cmat']), const(sp['lamp']), const(sp['ppow']), const(sp['d']),
                  const(wg)],
        out_specs=[pl.BlockSpec((1, tb, WIDTH_C), lambda b, t: (b, t, 0)),
                   pl.BlockSpec((1, S5_SLAB, 2 * S5_N), lambda b, t: (b, 0, 0))],
        out_shape=[jax.ShapeDtypeStruct((B, Lx, WIDTH_C), jnp.float32),
                   jax.ShapeDtypeStruct((B, S5_SLAB, 2 * S5_N), jnp.float32)],
        scratch_shapes=[pltpu.VMEM((tb, 2 * S5_N), jnp.float32), pltpu.VMEM((S5_SLAB, 2 * S5_N), jnp.float32)],
        compiler_params=pltpu.CompilerParams(dimension_semantics=("parallel", "arbitrary"),
                                             vmem_limit_bytes=VMEM_LIMIT),
        name="s5_glu",
    )(z3, x0, sp['bmat'], sp['cmat'], sp['lamp'], sp['ppow'], sp['d'], wg)
    shape = (B, S5_GROUPS, S5_STATE)
    return oc, xf[:, 0, :S5_N].reshape(shape), xf[:, 0, S5_N:].reshape(shape)


def _gdn_kernel(qkv_ref, gate_ref, ba_ref, hist_ref, s0_ref, cw_ref, av_ref, dt_ref, gn_ref,
                ob_ref, sout_ref, tail_ref, s_ref, prev_ref, *, pad_rows):
    f32 = jnp.float32
    t = pl.program_id(1)
    C = qkv_ref.shape[1]
    hd = HEAD_DIM_B
    dn_last = (((1,), (1,)), ((), ()))
    dn_first = (((0,), (0,)), ((), ()))
    dot = lambda a, b: jnp.dot(a, b, precision=HIGHEST, preferred_element_type=f32)

    @pl.when(t == 0)
    def _():
        s_ref[...] = s0_ref[0]
        prev_ref[...] = hist_ref[0]

    cur = qkv_ref[0]
    tail = prev_ref.shape[0]
    ext = jnp.concatenate([prev_ref[...], cur], axis=0)
    w = cw_ref[...]
    conv = cur * w[CONV_W - 1:CONV_W, :]
    for i in range(CONV_W - 1):
        lo = tail - (CONV_W - 1) + i
        conv = conv + ext[lo:lo + C, :] * w[i:i + 1, :]
    prev_ref[...] = cur[C - tail:, :]
    c = conv * _sigmoid(conv)

    ba = ba_ref[0]
    xg = ba + dt_ref[...]
    softplus = jnp.maximum(xg, 0.0) + jnp.log(1.0 + jnp.exp(-jnp.abs(xg)))
    g_all = -jnp.exp(av_ref[...]) * softplus
    if pad_rows:
        grow = t * C + lax.broadcasted_iota(jnp.int32, g_all.shape, 0)
        g_all = jnp.where(grow >= pad_rows, g_all, 0.0)
    ri = lax.broadcasted_iota(jnp.int32, (C, C), 0)
    ci = lax.broadcasted_iota(jnp.int32, (C, C), 1)
    tri = ri >= ci
    strict = ri > ci
    tril = jnp.where(tri, 1.0, 0.0).astype(f32)
    G_col = dot(tril, g_all)
    G_row = lax.dot_general(g_all, tril, (((0,), (1,)), ((), ())), precision=HIGHEST,
                            preferred_element_type=f32)
    eye = jnp.where(ri == ci, 1.0, 0.0).astype(f32)

    for h in range(N_HEADS_B):
        lane = N_HEADS_B + h
        Gc = G_col[:, lane:lane + 1]
        Gr = G_row[lane:lane + 1, :]
        decay = jnp.where(tri, jnp.exp(jnp.minimum(Gc - Gr, 0.0)), 0.0)
        beta = _sigmoid(ba[:, h:h + 1])
        q = c[:, h * hd:(h + 1) * hd]
        k = c[:, WIDTH_B + h * hd:WIDTH_B + (h + 1) * hd]
        v = c[:, 2 * WIDTH_B + h * hd:2 * WIDTH_B + (h + 1) * hd]
        q = q * lax.rsqrt(jnp.sum(q * q, axis=-1, keepdims=True) + EPS) * (hd ** -0.5)
        k = k * lax.rsqrt(jnp.sum(k * k, axis=-1, keepdims=True) + EPS)
        kb = k * beta
        A = jnp.where(strict, lax.dot_general(kb, k, dn_last, precision=HIGHEST,
                                              preferred_element_type=f32) * decay, 0.0)
        eG = jnp.exp(Gc)
        rhs = jnp.concatenate([v * beta, kb * eG], axis=1)
        tm = eye - A
        pw = A
        for _ in range(C.bit_length() - 2):
            pw = dot(pw, pw)
            tm = tm + dot(tm, pw)
        sol = dot(tm, rhs)
        S = s_ref[h]
        v_new = sol[:, :hd] - dot(sol[:, hd:], S)
        attn = lax.dot_general(q, k, dn_last, precision=HIGHEST, preferred_element_type=f32) * decay
        o = dot(q * eG, S) + dot(attn, v_new)
        GL = Gc[C - 1:C, :]
        s_ref[h] = S * jnp.exp(GL) + lax.dot_general(k * jnp.exp(GL - Gc), v_new, dn_first, precision=HIGHEST,
                                                     preferred_element_type=f32)
        gate = gate_ref[0, :, h * hd:(h + 1) * hd]
        on = o * lax.rsqrt(jnp.mean(o * o, axis=-1, keepdims=True) + EPS) * gn_ref[...]
        ob_ref[0, :, h * hd:(h + 1) * hd] = on * (gate * _sigmoid(gate))

    @pl.when(t == pl.num_programs(1) - 1)
    def _():
        sout_ref[0] = s_ref[...]
        tail_ref[0] = cur[C - tail:, :]


GDN_TAIL = 8


def gdn_mixer(z3, blocks, conv_hist, delta0, conv_w, a_log, dt_bias, gdn_norm, chunk, pad_rows):
    B, Lx, _ = z3.shape
    f32 = jnp.float32
    wq = 3 * WIDTH_B
    hist = jnp.pad(conv_hist, ((0, 0), (GDN_TAIL - (CONV_W - 1), 0), (0, 0)))
    lane_vec = lambda v: jnp.zeros((1, LANE), f32).at[0, N_HEADS_B:2 * N_HEADS_B].set(v)
    const = lambda shape: pl.BlockSpec(shape, lambda b, t: (0,) * len(shape))
    per_b = lambda shape: pl.BlockSpec((1,) + shape, lambda b, t: (b,) + (0,) * len(shape))
    qb, gb, bb = blocks
    return pl.pallas_call(
        functools.partial(_gdn_kernel, pad_rows=pad_rows),
        grid=(B, Lx // chunk),
        in_specs=[pl.BlockSpec((1, chunk, wq), lambda b, t: (b, t, qb)),
                  pl.BlockSpec((1, chunk, WIDTH_B), lambda b, t: (b, t, gb)),
                  pl.BlockSpec((1, chunk, LANE), lambda b, t: (b, t, bb)),
                  per_b((GDN_TAIL, wq)), per_b((N_HEADS_B, HEAD_DIM_B, HEAD_DIM_B)),
                  const((CONV_W, wq)), const((1, LANE)), const((1, LANE)), const((1, HEAD_DIM_B))],
        out_specs=[pl.BlockSpec((1, chunk, WIDTH_B), lambda b, t: (b, t, 0)),
                   per_b((N_HEADS_B, HEAD_DIM_B, HEAD_DIM_B)), per_b((GDN_TAIL, wq))],
        out_shape=[jax.ShapeDtypeStruct((B, Lx, WIDTH_B), f32),
                   jax.ShapeDtypeStruct((B, N_HEADS_B, HEAD_DIM_B, HEAD_DIM_B), f32),
                   jax.ShapeDtypeStruct((B, GDN_TAIL, wq), f32)],
        scratch_shapes=[pltpu.VMEM((N_HEADS_B, HEAD_DIM_B, HEAD_DIM_B), f32), pltpu.VMEM((GDN_TAIL, wq), f32)],
        compiler_params=pltpu.CompilerParams(dimension_semantics=("parallel", "arbitrary")),
        name="gdn_mixer",
    )(z3, z3, z3, hist, delta0, conv_w, lane_vec(a_log), lane_vec(dt_bias), gdn_norm.reshape(1, HEAD_DIM_B))


def _merge_kernel(x_ref, oa_ref, ob_ref, oc_ref, g0_ref, g1_ref, g2_ref, wa_ref, wb_ref, wc_ref, wo_ref, o_ref,
                  *, rows_per_seq, pad_rows):
    f32 = jnp.float32
    bf16 = jnp.bfloat16
    mm = lambda a, w_ref: jnp.dot(a.astype(bf16), w_ref[...], preferred_element_type=f32)
    merged = (_sigmoid(g0_ref[...]) * mm(oa_ref[...], wa_ref) + _sigmoid(g1_ref[...]) * mm(ob_ref[...], wb_ref)
              + _sigmoid(g2_ref[...]) * mm(oc_ref[...], wc_ref))
    x = x_ref[...] + mm(merged, wo_ref)
    if pad_rows:
        tm = x.shape[0]
        row = pl.program_id(0) * tm + lax.broadcasted_iota(jnp.int32, x.shape, 0)
        x = jnp.where(row % rows_per_seq >= pad_rows, x, 0.0)
    o_ref[...] = x


def merge_residual(x, oA, oB, oC, z, gate_block, wa, wb, wc, wo, rows_per_seq, pad_rows):
    T = x.shape[0]
    tm = _token_tile(T)
    bf16 = jnp.bfloat16
    rows = lambda w, j=0: pl.BlockSpec((tm, w), lambda i, j=j: (i, j))
    const = lambda a: pl.BlockSpec(a.shape, lambda i: (0, 0))
    ws = [w.astype(bf16) for w in (wa, wb, wc, wo)]
    return pl.pallas_call(
        functools.partial(_merge_kernel, rows_per_seq=rows_per_seq, pad_rows=pad_rows),
        grid=(T // tm,),
        in_specs=[rows(D_MODEL), rows(WIDTH_A), rows(WIDTH_B), rows(WIDTH_C),
                  rows(D_MODEL, gate_block), rows(D_MODEL, gate_block + 1), rows(D_MODEL, gate_block + 2)]
                 + [const(w) for w in ws],
        out_specs=rows(D_MODEL),
        out_shape=jax.ShapeDtypeStruct((T, D_MODEL), jnp.float32),
        compiler_params=pltpu.CompilerParams(dimension_semantics=("parallel",), vmem_limit_bytes=VMEM_LIMIT),
        name="merge_residual",
    )(x, oA, oB, oC, z, z, z, *ws)


N_HC = 2 * PEER_HEADS
HALF_Q = PEER_QDIM // 2
PEER_EXPERT_TILE = 8 * N_KEYS


def _extract_top16(s, iota_f):
    n = float(s.shape[0])
    bits = jnp.zeros(s.shape, jnp.int32)
    vals = []
    for j in range(PEER_TOPK):
        m = jnp.max(s, axis=0, keepdims=True)
        idx = jnp.min(jnp.where(s == m, iota_f, n), axis=0, keepdims=True)
        hit = iota_f == idx
        bits = jnp.where(hit, 1 << j, bits)
        s = jnp.where(hit, -jnp.inf, s)
        vals.append(m)
    return vals, bits


def _peer_router_kernel(x_ref, g_ref, wqt_ref, sk_ref, h2_ref, rm_ref, b2_ref, e1_ref, e2_ref,
                        qt_ref, s_ref, bits_ref, v_ref):
    f32 = jnp.float32
    tl = x_ref.shape[0]
    x = x_ref[...]
    h2 = x * lax.rsqrt(jnp.mean(x * x, axis=-1, keepdims=True) + EPS) * g_ref[...]
    h2b = h2.astype(jnp.bfloat16)
    h2_ref[...] = h2b
    qt_ref[...] = lax.dot_general(wqt_ref[...], h2b, (((1,), (1,)), ((), ())),
                                  preferred_element_type=f32).astype(jnp.bfloat16)

    def score_body(hc, carry):
        r0 = pl.multiple_of(hc * HALF_Q, HALF_Q)
        s_ref[hc] = jnp.dot(sk_ref[hc], qt_ref[pl.ds(r0, HALF_Q), :], preferred_element_type=f32)
        return carry

    lax.fori_loop(0, N_HC, score_body, 0)

    key_iota = lax.broadcasted_iota(jnp.int32, (N_KEYS, LANE), 0).astype(f32)

    def key_body(i, carry):
        hc = i // (tl // LANE)
        c0 = pl.multiple_of((i % (tl // LANE)) * LANE, LANE)
        s = s_ref[hc, :, pl.ds(c0, LANE)]
        vals, bits = _extract_top16(s, key_iota)
        bits_ref[hc, :, pl.ds(c0, LANE)] = bits
        v_ref[hc, :, pl.ds(c0, LANE)] = jnp.concatenate(vals, axis=0)
        return carry

    lax.fori_loop(0, N_HC * (tl // LANE), key_body, 0)

    cand_iota = lax.broadcasted_iota(jnp.int32, (PEER_TOPK * PEER_TOPK, LANE), 0).astype(f32)
    row16 = lax.broadcasted_iota(jnp.int32, (PEER_TOPK, LANE), 0)
    pow2 = jnp.zeros((PEER_TOPK, LANE), f32)
    for k in range(PEER_TOPK):
        pow2 = jnp.where(row16 == k, float(1 << k), pow2)

    def head_body(i, carry):
        h = i // (tl // LANE)
        c0 = pl.multiple_of((i % (tl // LANE)) * LANE, LANE)
        cols = pl.ds(c0, LANE)
        v1 = v_ref[2 * h, :, cols]
        v2 = v_ref[2 * h + 1, :, cols]
        cand = jnp.concatenate([v1[jj:jj + 1, :] + v2 for jj in range(PEER_TOPK)], axis=0)
        best, cbits = _extract_top16(cand, cand_iota)
        z = jnp.zeros((1, LANE), f32)
        for k in range(PEER_TOPK):
            z = z + jnp.exp(best[k] - best[0])
        sel = cbits != 0
        bits1 = bits_ref[2 * h, :, cols]
        bits2 = bits_ref[2 * h + 1, :, cols]
        rm = jnp.zeros((N_KEYS, LANE), jnp.int32)
        for jj in range(PEER_TOPK):
            blk = sel[jj * PEER_TOPK:(jj + 1) * PEER_TOPK, :]
            prow = jnp.sum(jnp.where(blk, pow2, 0.0), axis=0, keepdims=True).astype(jnp.int32)
            rm = jnp.where(bits1 == (1 << jj), prow, rm)
        rm_ref[h, :, cols] = rm
        b2_ref[h, :, cols] = bits2
        e1_ref[h, :, cols] = jnp.exp(s_ref[2 * h, :, cols] - v1[0:1, :])
        e2_ref[h, :, cols] = jnp.exp(s_ref[2 * h + 1, :, cols] - v2[0:1, :]) / z
        return carry

    lax.fori_loop(0, PEER_HEADS * (tl // LANE), head_body, 0)


def _gelu_tanh(x):
    return 0.5 * x * (1.0 + jnp.tanh(math.sqrt(2.0 / math.pi) * (x + 0.044715 * (x * x * x))))


def _peer_expert_kernel(x_ref, h2_ref, pu_ref, pvt_ref, rm_ref, b2_ref, e1_ref, e2_ref, o_ref,
                        acc_ref, a_ref, c_ref):
    f32 = jnp.float32
    e_step = pl.program_id(1)
    tl = h2_ref.shape[0]
    n_i1 = pu_ref.shape[0] // N_KEYS

    @pl.when(e_step == 0)
    def _():
        acc_ref[...] = jnp.zeros_like(acc_ref)

    a_ref[...] = lax.dot_general(pu_ref[...], h2_ref[...], (((1,), (1,)), ((), ())),
                                 preferred_element_type=f32)

    def body(i, carry):
        cols = pl.ds(pl.multiple_of(i * LANE, LANE), LANE)
        for ii in range(n_i1):
            rows = slice(ii * N_KEYS, (ii + 1) * N_KEYS)
            w = jnp.zeros((N_KEYS, LANE), f32)
            for h in range(PEER_HEADS):
                sel = (rm_ref[h, ii:ii + 1, cols] & b2_ref[h, :, cols]) != 0
                w = w + jnp.where(sel, e2_ref[h, :, cols], 0.0) * e1_ref[h, ii:ii + 1, cols]
            c_ref[rows, cols] = (w * _gelu_tanh(a_ref[rows, cols])).astype(jnp.bfloat16)
        return carry

    lax.fori_loop(0, tl // LANE, body, 0)
    acc_ref[...] += jnp.dot(pvt_ref[...], c_ref[...], preferred_element_type=f32)

    @pl.when(e_step == pl.num_programs(1) - 1)
    def _():
        o_ref[...] = x_ref[...] + acc_ref[...].T


def _token_tile(T):
    for tl in (640, 512, 384, 256, 128):
        if T % tl == 0:
            return tl
    raise ValueError(f"token count {T} is not a multiple of {LANE}")


def peer_residual(x, g, wqt, sk, pu, pvt):
    T = x.shape[0]
    tl = _token_tile(T)
    nt = T // tl
    n_exp = pu.shape[0]
    head_shape = jax.ShapeDtypeStruct((PEER_HEADS, N_KEYS, T), jnp.int32)
    head_shape_f = jax.ShapeDtypeStruct((PEER_HEADS, N_KEYS, T), jnp.float32)
    head_spec = pl.BlockSpec((PEER_HEADS, N_KEYS, tl), lambda i: (0, 0, i))
    h2, rm, b2, e1, e2 = pl.pallas_call(
        _peer_router_kernel,
        grid=(nt,),
        in_specs=[pl.BlockSpec((tl, D_MODEL), lambda i: (i, 0)),
                  pl.BlockSpec((1, D_MODEL), lambda i: (0, 0)),
                  pl.BlockSpec((PEER_HEADS * PEER_QDIM, D_MODEL), lambda i: (0, 0)),
                  pl.BlockSpec((N_HC, N_KEYS, HALF_Q), lambda i: (0, 0, 0))],
        out_specs=[pl.BlockSpec((tl, D_MODEL), lambda i: (i, 0)), head_spec, head_spec, head_spec, head_spec],
        out_shape=[jax.ShapeDtypeStruct((T, D_MODEL), jnp.bfloat16), head_shape, head_shape,
                   head_shape_f, head_shape_f],
        scratch_shapes=[pltpu.VMEM((PEER_HEADS * PEER_QDIM, tl), jnp.bfloat16),
                        pltpu.VMEM((N_HC, N_KEYS, tl), jnp.float32),
                        pltpu.VMEM((N_HC, N_KEYS, tl), jnp.int32),
                        pltpu.VMEM((N_HC, PEER_TOPK, tl), jnp.float32)],
        compiler_params=pltpu.CompilerParams(dimension_semantics=("parallel",), vmem_limit_bytes=VMEM_LIMIT),
        name="peer_router",
    )(x, g.reshape(1, D_MODEL), wqt, sk)

    te = PEER_EXPERT_TILE
    i2_spec = pl.BlockSpec((PEER_HEADS, N_KEYS, tl), lambda i, e: (0, 0, i))
    i1_spec = pl.BlockSpec((PEER_HEADS, te // N_KEYS, tl), lambda i, e: (0, e, i))
    return pl.pallas_call(
        _peer_expert_kernel,
        grid=(nt, n_exp // te),
        in_specs=[pl.BlockSpec((tl, D_MODEL), lambda i, e: (i, 0)),
                  pl.BlockSpec((tl, D_MODEL), lambda i, e: (i, 0)),
                  pl.BlockSpec((te, D_MODEL), lambda i, e: (e, 0)),
                  pl.BlockSpec((D_MODEL, te), lambda i, e: (0, e)),
                  i1_spec, i2_spec, i1_spec, i2_spec],
        out_specs=pl.BlockSpec((tl, D_MODEL), lambda i, e: (i, 0)),
        out_shape=jax.ShapeDtypeStruct((T, D_MODEL), jnp.float32),
        scratch_shapes=[pltpu.VMEM((D_MODEL, tl), jnp.float32),
                        pltpu.VMEM((te, tl), jnp.float32),
                        pltpu.VMEM((te, tl), jnp.bfloat16)],
        compiler_params=pltpu.CompilerParams(dimension_semantics=("parallel", "arbitrary"),
                                             vmem_limit_bytes=VMEM_LIMIT),
        name="peer_experts",
    )(x, h2, pu, pvt, rm, b2, e1, e2)


C_QKVB = 3 * WIDTH_A
C_GB = C_QKVB + 3 * WIDTH_B
C_UC = C_GB + WIDTH_B
C_GATE = C_UC + WIDTH_C
C_BA = C_GATE + 3 * D_MODEL
D_IN2 = C_BA + LANE
S5_BLOCK = 256


def _reorder_w_in(w):
    cols = [w[:, OFF_QA:OFF_BETA], w[:, OFF_UC:OFF_GATE], w[:, OFF_GATE:], w[:, OFF_BETA:OFF_UC],
            jnp.zeros((w.shape[0], LANE - 2 * N_HEADS_B), w.dtype)]
    return jnp.concatenate(cols, axis=1).astype(jnp.bfloat16)


def trunk_layer(x, lp, layer_idx, att, prompt, k_past, v_past, conv_hist, delta0, ssm0_re, ssm0_im):
    f32 = jnp.float32
    B, Lx = x.shape[0], x.shape[1]
    T = B * Lx
    z = norm_matmul(x.reshape(T, D_MODEL), lp['norm1'], lp['w_in2'])

    qn, kn, knb, vb = qk_prep(z, lp['q_norm'], lp['k_norm'], _token_tile(T))
    lam_init = 0.8 - 0.6 * math.exp(-0.3 * layer_idx)
    lqk = lp['lambda_qk']
    lam = jnp.exp(jnp.sum(lqk[0] * lqk[1])) - jnp.exp(jnp.sum(lqk[2] * lqk[3])) + lam_init
    scalars = jnp.stack([lam, jnp.asarray(1.0 - lam_init, f32)]).astype(f32)
    r3 = lambda a: a.reshape(B, Lx, WIDTH_A)
    if prompt:
        oA = attn_prompt(r3(qn), r3(knb), r3(vb), scalars, att['far'], att['tiles'], lp['subln'])
    else:
        P = k_past.shape[1]
        oA = attn_sample(r3(qn), r3(knb), r3(vb), k_past.reshape(B, P, WIDTH_A), v_past.reshape(B, P, WIDTH_A),
                         scalars, att['bias_past'], att['bias_new'], lp['subln'])

    z3 = z.reshape(B, Lx, D_IN2)
    off = PAD if prompt else 0
    L = Lx - off
    kA = r3(kn)[:, off:].reshape(B, L, N_HEADS_A, 2 * HEAD_DIM_A)
    vA = z3[:, off:, 2 * WIDTH_A:3 * WIDTH_A].reshape(B, L, N_HEADS_A, 2 * HEAD_DIM_A)

    oB, delta, tail = gdn_mixer(z3, (C_QKVB // (3 * WIDTH_B), C_GB // WIDTH_B, C_BA // LANE), conv_hist, delta0,
                                lp['conv_w'], lp['a_log'], lp['dt_bias'], lp['gdn_norm'],
                                CHUNK if prompt else Lx, off)
    conv_state = tail[:, GDN_TAIL - (CONV_W - 1):]

    oC, ssm_re, ssm_im = s5_glu(z3, C_UC // WIDTH_C, ssm0_re, ssm0_im, lp['s5'], lp['w_glu'],
                                S5_BLOCK if prompt else Lx)

    x2 = merge_residual(x.reshape(T, D_MODEL), oA.reshape(T, WIDTH_A), oB.reshape(T, WIDTH_B),
                        oC.reshape(T, WIDTH_C), z, C_GATE // D_MODEL, lp['wb_a'], lp['wb_b'], lp['wb_c'],
                        lp['w_out'], Lx, off)

    x = peer_residual(x2, lp['norm2'], lp['peer_wqt'], lp['peer_sk'], lp['peer_pu'],
                      lp['peer_pvt']).reshape(B, Lx, D_MODEL)
    return (x, kA, vA, conv_state, delta, ssm_re, ssm_im)


def kernel(x_prompt, x_sample, cache_k, cache_v, state_conv, state_delta, state_ssm_re, state_ssm_im,
           meta_tokens, rel_bias, norm1_g, norm2_g, final_norm_g, w_in, q_norm_g, k_norm_g, lambda_qk,
           subln_g, conv_w, gdn_a_log, gdn_dt_bias, gdn_norm_g, s5_a_re, s5_a_im, s5_b_re, s5_b_im,
           s5_c_re, s5_c_im, s5_d, s5_log_dt, w_glu, w_branch_a, w_branch_b, w_branch_c, w_out,
           peer_wq, peer_subkeys, peer_u, peer_v):
    f32 = jnp.float32
    bf16 = jnp.bfloat16
    params = [dict(norm1=norm1_g[l], norm2=norm2_g[l], w_in2=_reorder_w_in(w_in[l]), q_norm=q_norm_g[l],
                   k_norm=k_norm_g[l], lambda_qk=lambda_qk[l], subln=subln_g[l], conv_w=conv_w[l],
                   a_log=gdn_a_log[l], dt_bias=gdn_dt_bias[l], gdn_norm=gdn_norm_g[l],
                   s5=s5_params(s5_a_re[l], s5_a_im[l], s5_b_re[l], s5_b_im[l], s5_c_re[l], s5_c_im[l], s5_d[l],
                                s5_log_dt[l]),
                   w_glu=w_glu[l], wb_a=w_branch_a[l], wb_b=w_branch_b[l], wb_c=w_branch_c[l], w_out=w_out[l],
                   peer_wqt=peer_wq[l].T.astype(bf16),
                   peer_sk=peer_subkeys[l].reshape(N_HC, N_KEYS, HALF_Q).astype(bf16),
                   peer_pu=peer_u[l].astype(bf16), peer_pvt=peer_v[l].T.astype(bf16))
              for l in range(DEPTH)]

    qpos = jnp.arange(ATT_TQ, dtype=jnp.int32)
    tiles = jnp.stack([t5_bias(qpos, d * ATT_TK + jnp.arange(ATT_TK, dtype=jnp.int32), rel_bias)
                       for d in (-2, -1, 0)], axis=1)
    far = t5_bias(jnp.full((1,), 2 * ATT_TK, jnp.int32), jnp.zeros((1,), jnp.int32), rel_bias).reshape(N_HEADS_A)
    P, Ls = cache_k.shape[2], x_sample.shape[1]
    bias_s = t5_bias(P + jnp.arange(Ls, dtype=jnp.int32), jnp.arange(P + Ls, dtype=jnp.int32), rel_bias)
    att_p = dict(tiles=tiles, far=far)
    att_s = dict(bias_past=bias_s[:, :, :P], bias_new=bias_s[:, :, P:])

    B = x_prompt.shape[0]
    xp = jnp.concatenate([jnp.zeros((B, PAD, D_MODEL), f32),
                          jnp.broadcast_to(meta_tokens[None], (B, N_META, D_MODEL)), x_prompt], axis=1)
    outs_p = [[] for _ in range(6)]
    for l in range(DEPTH):
        res = trunk_layer(
            xp, params[l], l, att_p, True, None, None,
            jnp.zeros((B, CONV_W - 1, 3 * WIDTH_B), f32),
            jnp.zeros((B, N_HEADS_B, HEAD_DIM_B, HEAD_DIM_B), f32),
            jnp.zeros((B, S5_GROUPS, S5_STATE), f32), jnp.zeros((B, S5_GROUPS, S5_STATE), f32))
        xp = res[0]
        for acc, r in zip(outs_p, res[1:]):
            acc.append(r)
    y_prompt = rmsnorm(xp, final_norm_g)[:, FRONT:]

    xs = x_sample
    outs_s = [[] for _ in range(6)]
    for l in range(DEPTH):
        res = trunk_layer(
            xs, params[l], l, att_s, False, cache_k[l], cache_v[l], state_conv[l],
            state_delta[l], state_ssm_re[l], state_ssm_im[l])
        xs = res[0]
        for acc, r in zip(outs_s, res[1:]):
            acc.append(r)
    y_sample = rmsnorm(xs, final_norm_g)

    kp, vp, cp, dp, srp, sip = [jnp.stack(a) for a in outs_p]
    ks_, vs_, cs_, ds_, srs, sis = [jnp.stack(a) for a in outs_s]
    return (y_prompt, y_sample, kp, vp, ks_, vs_, cp, cs_, dp, ds_, srp, sip, srs, sis)
```

```python
import functools
import math

import jax
import jax.numpy as jnp
from jax import lax
from jax.experimental import pallas as pl
from jax.experimental.pallas import tpu as pltpu

D_MODEL = 1024
DEPTH = 4
CHUNK = 64
N_META = 16
EPS = 1e-6
NEG_INF = -1e30
N_HEADS_A = 4
HEAD_DIM_A = 64
NUM_BUCKETS = 32
MAX_DISTANCE = 128
N_HEADS_B = 4
HEAD_DIM_B = 128
CONV_W = 4
S5_GROUP = 16
S5_GROUPS = 32
S5_STATE = 64
PEER_HEADS = 8
PEER_QDIM = 256
N_KEYS = 128
PEER_TOPK = 16

WIDTH_A = N_HEADS_A * 2 * HEAD_DIM_A
WIDTH_B = N_HEADS_B * HEAD_DIM_B
WIDTH_C = S5_GROUPS * S5_GROUP
OFF_QA = 0
OFF_KA = OFF_QA + WIDTH_A
OFF_VA = OFF_KA + WIDTH_A
OFF_QKVB = OFF_VA + WIDTH_A
OFF_GB = OFF_QKVB + 3 * WIDTH_B
OFF_BETA = OFF_GB + WIDTH_B
OFF_ALPHA = OFF_BETA + N_HEADS_B
OFF_UC = OFF_ALPHA + N_HEADS_B
OFF_GATE = OFF_UC + WIDTH_C
D_IN = OFF_GATE + 3 * D_MODEL

LANE = 128
VMEM_LIMIT = 56 * 1024 * 1024
HIGHEST = lax.Precision.HIGHEST


def _norm_mm_kernel(x_ref, g_ref, w_ref, o_ref, h_ref):
    @pl.when(pl.program_id(1) == 0)
    def _():
        x = x_ref[...]
        h = x * lax.rsqrt(jnp.mean(x * x, axis=-1, keepdims=True) + EPS) * g_ref[...]
        h_ref[...] = h.astype(jnp.bfloat16)

    o_ref[...] = jnp.dot(h_ref[...], w_ref[...], preferred_element_type=jnp.float32)


def norm_matmul(x, g, wb, tm=512):
    M, K = x.shape
    N = wb.shape[1]
    n_lanes = N // LANE
    tn = LANE * max(d for d in (4, 3, 2, 1) if n_lanes % d == 0)
    tm = min(tm, M)
    return pl.pallas_call(
        _norm_mm_kernel,
        grid=(pl.cdiv(M, tm), N // tn),
        in_specs=[pl.BlockSpec((tm, K), lambda i, j: (i, 0)),
                  pl.BlockSpec((1, K), lambda i, j: (0, 0)),
                  pl.BlockSpec((K, tn), lambda i, j: (0, j))],
        out_specs=pl.BlockSpec((tm, tn), lambda i, j: (i, j)),
        out_shape=jax.ShapeDtypeStruct((M, N), jnp.float32),
        scratch_shapes=[pltpu.VMEM((tm, K), jnp.bfloat16)],
        compiler_params=pltpu.CompilerParams(dimension_semantics=("parallel", "arbitrary")),
        name="norm_proj",
    )(x, g.reshape(1, K), wb)


def rmsnorm(x, g):
    xf = x.astype(jnp.float32)
    y = xf * lax.rsqrt(jnp.mean(xf * xf, axis=-1, keepdims=True) + EPS)
    return (y * g.astype(jnp.float32)).astype(x.dtype)


def t5_bias(q_pos, k_pos, table):
    rel = k_pos[None, :] - q_pos[:, None]
    half = NUM_BUCKETS // 2
    exact = half // 2
    n = jnp.abs(rel)
    nf = jnp.maximum(n, 1).astype(jnp.float32)
    far = exact + (jnp.log(nf / exact) / math.log(MAX_DISTANCE / exact) * (half - exact)).astype(jnp.int32)
    bucket = jnp.where(rel > 0, half, 0) + jnp.where(n < exact, n, jnp.minimum(far, half - 1))
    return jnp.moveaxis(table[bucket].astype(jnp.float32), -1, 0)


FRONT = 256
PAD = FRONT - N_META
ATT_TQ = 256
ATT_TK = 256


def _qk_prep_kernel(q_ref, k_ref, v_ref, gq_ref, gk_ref, seg_ref, qn_ref, kn_ref, knb_ref, vb_ref):
    f32 = jnp.float32
    seg = seg_ref[...]

    def norm(x, g):
        ms = jnp.dot(x * x, seg, precision=HIGHEST, preferred_element_type=f32)
        return x * lax.rsqrt(ms + EPS) * g

    qn_ref[...] = (norm(q_ref[...], gq_ref[...]) * (HEAD_DIM_A ** -0.5)).astype(jnp.bfloat16)
    kn = norm(k_ref[...], gk_ref[...])
    kn_ref[...] = kn
    knb_ref[...] = kn.astype(jnp.bfloat16)
    v = v_ref[...].astype(jnp.bfloat16)
    hd = 2 * HEAD_DIM_A
    lane = lax.broadcasted_iota(jnp.int32, (v.shape[0], hd), 1)
    ones_col = jnp.where(lane == 0, 1.0, 0.0).astype(jnp.bfloat16)
    vb_ref[...] = jnp.concatenate(
        [piece for h in range(N_HEADS_A) for piece in (v[:, h * hd:(h + 1) * hd], ones_col)], axis=1)


def qk_prep(z, gq, gk, tm):
    T = z.shape[0]
    seg = jnp.kron(jnp.eye(WIDTH_A // HEAD_DIM_A, dtype=jnp.float32),
                   jnp.full((HEAD_DIM_A, HEAD_DIM_A), 1.0 / HEAD_DIM_A, jnp.float32))
    row = lambda j: pl.BlockSpec((tm, WIDTH_A), lambda i, j=j: (i, j))
    const = lambda shape: pl.BlockSpec(shape, lambda i: (0,) * len(shape))
    out = lambda: pl.BlockSpec((tm, WIDTH_A), lambda i: (i, 0))
    bshape = jax.ShapeDtypeStruct((T, WIDTH_A), jnp.bfloat16)
    return pl.pallas_call(
        _qk_prep_kernel,
        grid=(T // tm,),
        in_specs=[row(0), row(1), row(2), const((1, WIDTH_A)), const((1, WIDTH_A)), const((WIDTH_A, WIDTH_A))],
        out_specs=[out(), out(), out(), pl.BlockSpec((tm, 2 * WIDTH_A), lambda i: (i, 0))],
        out_shape=[bshape, jax.ShapeDtypeStruct((T, WIDTH_A), jnp.float32), bshape,
                   jax.ShapeDtypeStruct((T, 2 * WIDTH_A), jnp.bfloat16)],
        compiler_params=pltpu.CompilerParams(dimension_semantics=("parallel",)),
        name="qk_prep",
    )(z, z, z, jnp.tile(gq, WIDTH_A // HEAD_DIM_A).reshape(1, WIDTH_A),
      jnp.tile(gk, WIDTH_A // HEAD_DIM_A).reshape(1, WIDTH_A), seg)


def _split_maps(q):
    lane = lax.broadcasted_iota(jnp.int32, q.shape, 1)
    zero = jnp.zeros_like(q)
    return jnp.where(lane < HEAD_DIM_A, q, zero), jnp.where(lane >= HEAD_DIM_A, q, zero)


def _subln(o, g, scale):
    return o * lax.rsqrt(jnp.mean(o * o, axis=-1, keepdims=True) + EPS) * g * scale


def _attn_prompt_kernel(sc_ref, far_ref, q_ref, k_ref, v_ref, bias_ref, g_ref, o_ref, m_ref, acc_ref):
    f32 = jnp.float32
    h = pl.program_id(1)
    qi = pl.program_id(2)
    tq, tk = ATT_TQ, ATT_TK
    hd = 2 * HEAD_DIM_A
    qs = jnp.concatenate(_split_maps(q_ref[0]), axis=0)
    m_ref[...] = jnp.full(m_ref.shape, NEG_INF, f32)
    acc_ref[...] = jnp.zeros(acc_ref.shape, f32)
    far = far_ref[h]

    def tile(k0, width, general):
        k0 = pl.multiple_of(k0, tk)
        kt = k_ref[0, pl.ds(k0, width), :]
        vt = v_ref[0, pl.ds(k0, width), :]
        s = lax.dot_general(qs, kt, (((1,), (1,)), ((), ())), preferred_element_type=f32)
        if general:
            kj = k0 // tk
            bias = bias_ref[0, jnp.clip(kj - qi + 2, 0, 2)]
            qpos = qi * tq + lax.broadcasted_iota(jnp.int32, (tq, tk), 0)
            kpos = k0 + lax.broadcasted_iota(jnp.int32, (tq, tk), 1)
            qchunk = jnp.where(qpos < FRONT, 0, 1 + jnp.right_shift(qpos - FRONT, 6))
            kchunk = jnp.where(kpos < FRONT, 0, 1 + jnp.right_shift(kpos - FRONT, 6))
            mask = (kpos >= PAD) & (kchunk <= qchunk)
            s = jnp.where(jnp.concatenate([mask, mask], axis=0), s + jnp.concatenate([bias, bias], axis=0), NEG_INF)
        else:
            s = s + far
        n_lane_tiles = width // LANE
        smax = s[:, :LANE]
        for j in range(1, n_lane_tiles):
            smax = jnp.maximum(smax, s[:, j * LANE:(j + 1) * LANE])
        m_old = m_ref[...]
        m_new = jnp.maximum(m_old, jnp.broadcast_to(jnp.max(smax, axis=1, keepdims=True), m_old.shape))
        alpha = jnp.exp(m_old - m_new)
        p = jnp.exp(s - jnp.concatenate([m_new] * n_lane_tiles, axis=1))
        acc_ref[...] = (jnp.concatenate([alpha, alpha], axis=1) * acc_ref[...]
                        + jnp.dot(p.astype(jnp.bfloat16), vt, preferred_element_type=f32))
        m_ref[...] = m_new

    tile(0, tk, True)

    n_far = jnp.maximum(qi - 2, 0)
    odd = n_far % 2

    @pl.when(odd == 1)
    def _():
        tile(tk, tk, False)

    def far_body(j, carry):
        tile((1 + odd + 2 * j) * tk, 2 * tk, False)
        return carry

    lax.fori_loop(0, n_far // 2, far_body, 0)

    @pl.when(qi >= 2)
    def _():
        tile((qi - 1) * tk, tk, True)

    @pl.when(qi >= 1)
    def _():
        tile(qi * tk, tk, True)

    acc = acc_ref[...]
    out = acc[:, :hd] / acc[:, hd:hd + 1]
    o = out[:tq] - sc_ref[0] * out[tq:]
    o_ref[0] = _subln(o, g_ref[...], sc_ref[1])


def attn_prompt(qn, knb, vb, scalars, far, bias_tiles, subln_g):
    B, Lp, _ = qn.shape
    hd = 2 * HEAD_DIM_A
    smem = pl.BlockSpec(memory_space=pltpu.SMEM)
    return pl.pallas_call(
        _attn_prompt_kernel,
        grid=(B, N_HEADS_A, Lp // ATT_TQ),
        in_specs=[smem, smem,
                  pl.BlockSpec((1, ATT_TQ, hd), lambda b, h, i: (b, i, h)),
                  pl.BlockSpec((1, Lp, hd), lambda b, h, i: (b, 0, h)),
                  pl.BlockSpec((1, Lp, 2 * hd), lambda b, h, i: (b, 0, h)),
                  pl.BlockSpec((1, 3, ATT_TQ, ATT_TK), lambda b, h, i: (h, 0, 0, 0)),
                  pl.BlockSpec((1, hd), lambda b, h, i: (0, 0))],
        out_specs=pl.BlockSpec((1, ATT_TQ, hd), lambda b, h, i: (b, i, h)),
        out_shape=jax.ShapeDtypeStruct((B, Lp, WIDTH_A), jnp.float32),
        scratch_shapes=[pltpu.VMEM((2 * ATT_TQ, hd), jnp.float32), pltpu.VMEM((2 * ATT_TQ, 2 * hd), jnp.float32)],
        compiler_params=pltpu.CompilerParams(dimension_semantics=("parallel", "parallel", "parallel"),
                                             vmem_limit_bytes=VMEM_LIMIT),
        name="diff_attn_prompt",
    )(scalars, far, qn, knb, vb, bias_tiles, subln_g.reshape(1, hd))


def _attn_sample_kernel(sc_ref, q_ref, kp_ref, vp_ref, kn_ref, vn_ref, bp_ref, bn_ref, g_ref, o_ref):
    f32 = jnp.float32
    bf16 = jnp.bfloat16
    qs = _split_maps(q_ref[0])
    kp = kp_ref[0].astype(bf16)
    vp = vp_ref[0].astype(bf16)
    kn = kn_ref[0]
    vn = vn_ref[0][:, :2 * HEAD_DIM_A]
    dn = (((1,), (1,)), ((), ()))
    outs = []
    for c in range(2):
        sp = lax.dot_general(qs[c], kp, dn, preferred_element_type=f32) + bp_ref[0]
        sn = lax.dot_general(qs[c], kn, dn, preferred_element_type=f32) + bn_ref[0]
        m = jnp.maximum(jnp.max(sp, axis=1, keepdims=True), jnp.max(sn, axis=1, keepdims=True))
        pp = jnp.exp(sp - m)
        pn = jnp.exp(sn - m)
        l = jnp.sum(pp, axis=1, keepdims=True) + jnp.sum(pn, axis=1, keepdims=True)
        acc = (jnp.dot(pp.astype(bf16), vp, preferred_element_type=f32)
               + jnp.dot(pn.astype(bf16), vn, preferred_element_type=f32))
        outs.append(acc / l)
    o_ref[0] = _subln(outs[0] - sc_ref[0] * outs[1], g_ref[...], sc_ref[1])


def attn_sample(qn, knb, vb, k_past, v_past, scalars, bias_past, bias_new, subln_g):
    B, L, _ = qn.shape
    P = k_past.shape[1]
    hd = 2 * HEAD_DIM_A
    new = pl.BlockSpec((1, L, hd), lambda b, h: (b, 0, h))
    past = pl.BlockSpec((1, P, hd), lambda b, h: (b, 0, h))
    return pl.pallas_call(
        _attn_sample_kernel,
        grid=(B, N_HEADS_A),
        in_specs=[pl.BlockSpec(memory_space=pltpu.SMEM), new, past, past, new,
                  pl.BlockSpec((1, L, 2 * hd), lambda b, h: (b, 0, h)),
                  pl.BlockSpec((1, L, P), lambda b, h: (h, 0, 0)),
                  pl.BlockSpec((1, L, L), lambda b, h: (h, 0, 0)),
                  pl.BlockSpec((1, hd), lambda b, h: (0, 0))],
        out_specs=new,
        out_shape=jax.ShapeDtypeStruct((B, L, WIDTH_A), jnp.float32),
        compiler_params=pltpu.CompilerParams(dimension_semantics=("parallel", "parallel")),
        name="diff_attn_sample",
    )(scalars, qn, k_past, v_past, knb, vb, bias_past, bias_new, subln_g.reshape(1, hd))


S5_N = S5_GROUPS * S5_STATE
S5_SLAB = 8
S5_LANES = 512


def _sigmoid(x):
    return 1.0 / (1.0 + jnp.exp(-x))


def _s5_kernel(u_ref, x0_ref, bmat_ref, cmat_ref, lamp_ref, ppow_ref, d_ref, wglu_ref, oc_ref, xf_ref,
               bu_ref, carry_ref):
    f32 = jnp.float32
    bf16 = jnp.bfloat16
    t = pl.program_id(1)
    tb = u_ref.shape[1]

    @pl.when(t == 0)
    def _():
        carry_ref[...] = x0_ref[0]

    u = u_ref[0]
    bu_ref[...] = jnp.dot(u.astype(bf16), bmat_ref[...], preferred_element_type=f32)

    row = lax.broadcasted_iota(jnp.int32, (S5_SLAB, S5_LANES), 0)
    for c in range(S5_N // S5_LANES):
        re = slice(c * S5_LANES, (c + 1) * S5_LANES)
        im = slice(S5_N + c * S5_LANES, S5_N + (c + 1) * S5_LANES)

        def slab(i, carry):
            cre, cim = carry
            rows = pl.ds(pl.multiple_of(i * S5_SLAB, S5_SLAB), S5_SLAB)
            yre = bu_ref[rows, re]
            yim = bu_ref[rows, im]
            for s in range(3):
                sh = 1 << s
                sre = jnp.where(row >= sh, pltpu.roll(yre, sh, 0), 0.0)
                sim = jnp.where(row >= sh, pltpu.roll(yim, sh, 0), 0.0)
                lr = lamp_ref[s, 0, :, re]
                li = lamp_ref[s, 1, :, re]
                yre, yim = yre + (lr * sre - li * sim), yim + (lr * sim + li * sre)
            pr = ppow_ref[0, :, re]
            pi = ppow_ref[1, :, re]
            yre, yim = yre + (pr * cre - pi * cim), yim + (pr * cim + pi * cre)
            bu_ref[rows, re] = yre
            bu_ref[rows, im] = yim
            last = S5_SLAB - 1
            return (jnp.broadcast_to(yre[last:last + 1, :], yre.shape),
                    jnp.broadcast_to(yim[last:last + 1, :], yim.shape))

        cre, cim = lax.fori_loop(0, tb // S5_SLAB, slab, (carry_ref[:, re], carry_ref[:, im]))
        carry_ref[:, re] = cre
        carry_ref[:, im] = cim

    y = jnp.dot(bu_ref[...].astype(bf16), cmat_ref[...], preferred_element_type=f32) + d_ref[...] * u
    gl = jnp.dot(_gelu_tanh(y).astype(bf16), wglu_ref[...], preferred_element_type=f32)
    oc_ref[0] = gl[:, :WIDTH_C] * _sigmoid(gl[:, WIDTH_C:])

    @pl.when(t == pl.num_programs(1) - 1)
    def _():
        xf_ref[0] = carry_ref[...]


def s5_params(a_re, a_im, b_re, b_im, c_re, c_im, d, log_dt):
    f32 = jnp.float32
    lam = lax.complex(a_re, a_im)
    lam_bar = jnp.exp(lam * jnp.exp(log_dt)[:, None])
    b_bar = ((lam_bar - 1.0) / lam)[..., None] * lax.complex(b_re, b_im)
    eye = jnp.eye(S5_GROUPS, dtype=f32)
    bd_in = lambda m: jnp.einsum('gpi,gh->gihp', m, eye).reshape(WIDTH_C, S5_N)
    bd_out = lambda m: jnp.einsum('gip,gh->gphi', m, eye).reshape(S5_N, WIDTH_C)
    bmat = jnp.concatenate([bd_in(b_bar.real), bd_in(b_bar.imag)], axis=1).astype(jnp.bfloat16)
    cmat = jnp.concatenate([bd_out(c_re), bd_out(-c_im)], axis=0).astype(jnp.bfloat16)
    lb = lam_bar.reshape(S5_N)
    rep = lambda v: jnp.broadcast_to(v[None, :], (S5_SLAB, S5_N))
    pows = [lb, lb * lb, (lb * lb) * (lb * lb)]
    lamp = jnp.stack([jnp.stack([rep(p.real), rep(p.imag)]) for p in pows])
    run = [lb]
    for _ in range(S5_SLAB - 1):
        run.append(run[-1] * lb)
    pp = jnp.stack(run)
    ppow = jnp.stack([pp.real, pp.imag])
    return dict(bmat=bmat, cmat=cmat, lamp=lamp.astype(f32), ppow=ppow.astype(f32), d=d.reshape(1, WIDTH_C))


def s5_glu(z3, col_block, x0_re, x0_im, sp, w_glu, tb):
    B, Lx, _ = z3.shape
    x0 = jnp.concatenate([x0_re.reshape(B, S5_N), x0_im.reshape(B, S5_N)], axis=1)
    x0 = jnp.broadcast_to(x0[:, None, :], (B, S5_SLAB, 2 * S5_N))
    const = lambda a: pl.BlockSpec(a.shape, lambda b, t: (0,) * a.ndim)
    wg = w_glu.astype(jnp.bfloat16)
    oc, xf = pl.pallas_call(
        _s5_kernel,
        grid=(B, Lx // tb),
        in_specs=[pl.BlockSpec((1, tb, WIDTH_C), lambda b, t: (b, t, col_block)),
                  pl.BlockSpec((1, S5_SLAB, 2 * S5_N), lambda b, t: (b, 0, 0)),
                  const(sp['bmat']), const(sp['cmat']), const(sp['lamp']), const(sp['ppow']), const(sp['d']),
                  const(wg)],
        out_specs=[pl.BlockSpec((1, tb, WIDTH_C), lambda b, t: (b, t, 0)),
                   pl.BlockSpec((1, S5_SLAB, 2 * S5_N), lambda b, t: (b, 0, 0))],
        out_shape=[jax.ShapeDtypeStruct((B, Lx, WIDTH_C), jnp.float32),
                   jax.ShapeDtypeStruct((B, S5_SLAB, 2 * S5_N), jnp.float32)],
        scratch_shapes=[pltpu.VMEM((tb, 2 * S5_N), jnp.float32), pltpu.VMEM((S5_SLAB, 2 * S5_N), jnp.float32)],
        compiler_params=pltpu.CompilerParams(dimension_semantics=("parallel", "arbitrary"),
                                             vmem_limit_bytes=VMEM_LIMIT),
        name="s5_glu",
    )(z3, x0, sp['bmat'], sp['cmat'], sp['lamp'], sp['ppow'], sp['d'], wg)
    shape = (B, S5_GROUPS, S5_STATE)
    return oc, xf[:, 0, :S5_N].reshape(shape), xf[:, 0, S5_N:].reshape(shape)


def _gdn_kernel(qkv_ref, gate_ref, ba_ref, hist_ref, s0_ref, cw_ref, av_ref, dt_ref, gn_ref,
                ob_ref, sout_ref, tail_ref, s_ref, prev_ref, *, pad_rows):
    f32 = jnp.float32
    t = pl.program_id(1)
    C = qkv_ref.shape[1]
    hd = HEAD_DIM_B
    dn_last = (((1,), (1,)), ((), ()))
    dn_first = (((0,), (0,)), ((), ()))
    dot = lambda a, b: jnp.dot(a, b, precision=HIGHEST, preferred_element_type=f32)

    @pl.when(t == 0)
    def _():
        s_ref[...] = s0_ref[0]
        prev_ref[...] = hist_ref[0]

    cur = qkv_ref[0]
    tail = prev_ref.shape[0]
    ext = jnp.concatenate([prev_ref[...], cur], axis=0)
    w = cw_ref[...]
    conv = cur * w[CONV_W - 1:CONV_W, :]
    for i in range(CONV_W - 1):
        lo = tail - (CONV_W - 1) + i
        conv = conv + ext[lo:lo + C, :] * w[i:i + 1, :]
    prev_ref[...] = cur[C - tail:, :]
    c = conv * _sigmoid(conv)

    ba = ba_ref[0]
    xg = ba + dt_ref[...]
    softplus = jnp.maximum(xg, 0.0) + jnp.log(1.0 + jnp.exp(-jnp.abs(xg)))
    g_all = -jnp.exp(av_ref[...]) * softplus
    if pad_rows:
        grow = t * C + lax.broadcasted_iota(jnp.int32, g_all.shape, 0)
        g_all = jnp.where(grow >= pad_rows, g_all, 0.0)
    ri = lax.broadcasted_iota(jnp.int32, (C, C), 0)
    ci = lax.broadcasted_iota(jnp.int32, (C, C), 1)
    tri = ri >= ci
    strict = ri > ci
    tril = jnp.where(tri, 1.0, 0.0).astype(f32)
    G_col = dot(tril, g_all)
    G_row = lax.dot_general(g_all, tril, (((0,), (1,)), ((), ())), precision=HIGHEST,
                            preferred_element_type=f32)
    eye = jnp.where(ri == ci, 1.0, 0.0).astype(f32)

    for h in range(N_HEADS_B):
        lane = N_HEADS_B + h
        Gc = G_col[:, lane:lane + 1]
        Gr = G_row[lane:lane + 1, :]
        decay = jnp.where(tri, jnp.exp(jnp.minimum(Gc - Gr, 0.0)), 0.0)
        beta = _sigmoid(ba[:, h:h + 1])
        q = c[:, h * hd:(h + 1) * hd]
        k = c[:, WIDTH_B + h * hd:WIDTH_B + (h + 1) * hd]
        v = c[:, 2 * WIDTH_B + h * hd:2 * WIDTH_B + (h + 1) * hd]
        q = q * lax.rsqrt(jnp.sum(q * q, axis=-1, keepdims=True) + EPS) * (hd ** -0.5)
        k = k * lax.rsqrt(jnp.sum(k * k, axis=-1, keepdims=True) + EPS)
        kb = k * beta
        A = jnp.where(strict, lax.dot_general(kb, k, dn_last, precision=HIGHEST,
                                              preferred_element_type=f32) * decay, 0.0)
        eG = jnp.exp(Gc)
        rhs = jnp.concatenate([v * beta, kb * eG], axis=1)
        tm = eye - A
        pw = A
        for _ in range(C.bit_length() - 2):
            pw = dot(pw, pw)
            tm = tm + dot(tm, pw)
        sol = dot(tm, rhs)
        S = s_ref[h]
        v_new = sol[:, :hd] - dot(sol[:, hd:], S)
        attn = lax.dot_general(q, k, dn_last, precision=HIGHEST, preferred_element_type=f32) * decay
        o = dot(q * eG, S) + dot(attn, v_new)
        GL = Gc[C - 1:C, :]
        s_ref[h] = S * jnp.exp(GL) + lax.dot_general(k * jnp.exp(GL - Gc), v_new, dn_first, precision=HIGHEST,
                                                     preferred_element_type=f32)
        gate = gate_ref[0, :, h * hd:(h + 1) * hd]
        on = o * lax.rsqrt(jnp.mean(o * o, axis=-1, keepdims=True) + EPS) * gn_ref[...]
        ob_ref[0, :, h * hd:(h + 1) * hd] = on * (gate * _sigmoid(gate))

    @pl.when(t == pl.num_programs(1) - 1)
    def _():
        sout_ref[0] = s_ref[...]
        tail_ref[0] = cur[C - tail:, :]


GDN_TAIL = 8


def gdn_mixer(z3, blocks, conv_hist, delta0, conv_w, a_log, dt_bias, gdn_norm, chunk, pad_rows):
    B, Lx, _ = z3.shape
    f32 = jnp.float32
    wq = 3 * WIDTH_B
    hist = jnp.pad(conv_hist, ((0, 0), (GDN_TAIL - (CONV_W - 1), 0), (0, 0)))
    lane_vec = lambda v: jnp.zeros((1, LANE), f32).at[0, N_HEADS_B:2 * N_HEADS_B].set(v)
    const = lambda shape: pl.BlockSpec(shape, lambda b, t: (0,) * len(shape))
    per_b = lambda shape: pl.BlockSpec((1,) + shape, lambda b, t: (b,) + (0,) * len(shape))
    qb, gb, bb = blocks
    return pl.pallas_call(
        functools.partial(_gdn_kernel, pad_rows=pad_rows),
        grid=(B, Lx // chunk),
        in_specs=[pl.BlockSpec((1, chunk, wq), lambda b, t: (b, t, qb)),
                  pl.BlockSpec((1, chunk, WIDTH_B), lambda b, t: (b, t, gb)),
                  pl.BlockSpec((1, chunk, LANE), lambda b, t: (b, t, bb)),
                  per_b((GDN_TAIL, wq)), per_b((N_HEADS_B, HEAD_DIM_B, HEAD_DIM_B)),
                  const((CONV_W, wq)), const((1, LANE)), const((1, LANE)), const((1, HEAD_DIM_B))],
        out_specs=[pl.BlockSpec((1, chunk, WIDTH_B), lambda b, t: (b, t, 0)),
                   per_b((N_HEADS_B, HEAD_DIM_B, HEAD_DIM_B)), per_b((GDN_TAIL, wq))],
        out_shape=[jax.ShapeDtypeStruct((B, Lx, WIDTH_B), f32),
                   jax.ShapeDtypeStruct((B, N_HEADS_B, HEAD_DIM_B, HEAD_DIM_B), f32),
                   jax.ShapeDtypeStruct((B, GDN_TAIL, wq), f32)],
        scratch_shapes=[pltpu.VMEM((N_HEADS_B, HEAD_DIM_B, HEAD_DIM_B), f32), pltpu.VMEM((GDN_TAIL, wq), f32)],
        compiler_params=pltpu.CompilerParams(dimension_semantics=("parallel", "arbitrary")),
        name="gdn_mixer",
    )(z3, z3, z3, hist, delta0, conv_w, lane_vec(a_log), lane_vec(dt_bias), gdn_norm.reshape(1, HEAD_DIM_B))


def _merge_kernel(x_ref, oa_ref, ob_ref, oc_ref, g0_ref, g1_ref, g2_ref, wa_ref, wb_ref, wc_ref, wo_ref, o_ref,
                  *, rows_per_seq, pad_rows):
    f32 = jnp.float32
    bf16 = jnp.bfloat16
    mm = lambda a, w_ref: jnp.dot(a.astype(bf16), w_ref[...], preferred_element_type=f32)
    merged = (_sigmoid(g0_ref[...]) * mm(oa_ref[...], wa_ref) + _sigmoid(g1_ref[...]) * mm(ob_ref[...], wb_ref)
              + _sigmoid(g2_ref[...]) * mm(oc_ref[...], wc_ref))
    x = x_ref[...] + mm(merged, wo_ref)
    if pad_rows:
        tm = x.shape[0]
        row = pl.program_id(0) * tm + lax.broadcasted_iota(jnp.int32, x.shape, 0)
        x = jnp.where(row % rows_per_seq >= pad_rows, x, 0.0)
    o_ref[...] = x


def merge_residual(x, oA, oB, oC, z, gate_block, wa, wb, wc, wo, rows_per_seq, pad_rows):
    T = x.shape[0]
    tm = _token_tile(T)
    bf16 = jnp.bfloat16
    rows = lambda w, j=0: pl.BlockSpec((tm, w), lambda i, j=j: (i, j))
    const = lambda a: pl.BlockSpec(a.shape, lambda i: (0, 0))
    ws = [w.astype(bf16) for w in (wa, wb, wc, wo)]
    return pl.pallas_call(
        functools.partial(_merge_kernel, rows_per_seq=rows_per_seq, pad_rows=pad_rows),
        grid=(T // tm,),
        in_specs=[rows(D_MODEL), rows(WIDTH_A), rows(WIDTH_B), rows(WIDTH_C),
                  rows(D_MODEL, gate_block), rows(D_MODEL, gate_block + 1), rows(D_MODEL, gate_block + 2)]
                 + [const(w) for w in ws],
        out_specs=rows(D_MODEL),
        out_shape=jax.ShapeDtypeStruct((T, D_MODEL), jnp.float32),
        compiler_params=pltpu.CompilerParams(dimension_semantics=("parallel",), vmem_limit_bytes=VMEM_LIMIT),
        name="merge_residual",
    )(x, oA, oB, oC, z, z, z, *ws)


N_HC = 2 * PEER_HEADS
HALF_Q = PEER_QDIM // 2
PEER_EXPERT_TILE = 8 * N_KEYS
CAND_SUB = 8


def _extract_top16(s, iota_f):
    n = float(s.shape[0])
    rank = jnp.full(s.shape, PEER_TOPK, jnp.int32)
    vals = []
    for j in range(PEER_TOPK):
        m = jnp.max(s, axis=0, keepdims=True)
        idx = jnp.min(jnp.where(s == m, iota_f, n), axis=0, keepdims=True)
        hit = iota_f == idx
        rank = jnp.where(hit, j, rank)
        s = jnp.where(hit, -jnp.inf, s)
        vals.append(m)
    return vals, rank


def _peer_router_kernel(x_ref, g_ref, wqt_ref, sk_ref, h2_ref, lim_ref, rk2_ref, e1_ref, e2_ref,
                        qt_ref, s_ref, rank_ref, v_ref):
    f32 = jnp.float32
    tl = x_ref.shape[0]
    x = x_ref[...]
    h2 = x * lax.rsqrt(jnp.mean(x * x, axis=-1, keepdims=True) + EPS) * g_ref[...]
    h2b = h2.astype(jnp.bfloat16)
    h2_ref[...] = h2b
    qt_ref[...] = lax.dot_general(wqt_ref[...], h2b, (((1,), (1,)), ((), ())),
                                  preferred_element_type=f32).astype(jnp.bfloat16)

    def score_body(hc, carry):
        r0 = pl.multiple_of(hc * HALF_Q, HALF_Q)
        s_ref[hc] = jnp.dot(sk_ref[hc], qt_ref[pl.ds(r0, HALF_Q), :], preferred_element_type=f32)
        return carry

    lax.fori_loop(0, N_HC, score_body, 0)

    key_iota = lax.broadcasted_iota(jnp.int32, (N_KEYS, LANE), 0).astype(f32)

    def key_body(i, carry):
        hc = i // (tl // LANE)
        c0 = pl.multiple_of((i % (tl // LANE)) * LANE, LANE)
        s = s_ref[hc, :, pl.ds(c0, LANE)]
        vals, rank = _extract_top16(s, key_iota)
        rank_ref[hc, :, pl.ds(c0, LANE)] = rank
        v_ref[hc, :, pl.ds(c0, LANE)] = jnp.concatenate(vals, axis=0)
        return carry

    lax.fori_loop(0, N_HC * (tl // LANE), key_body, 0)

    sub = CAND_SUB
    n_mid = sub - 1
    n_cand = PEER_TOPK + n_mid * sub + (PEER_TOPK - sub)
    cand_iota = lax.broadcasted_iota(jnp.int32, (n_cand, LANE), 0).astype(f32)
    row8 = lax.broadcasted_iota(jnp.int32, (sub, LANE), 0)

    def head_body(i, carry):
        h = i // (tl // LANE)
        c0 = pl.multiple_of((i % (tl // LANE)) * LANE, LANE)
        cols = pl.ds(c0, LANE)
        v1 = v_ref[2 * h, :, cols]
        v2 = v_ref[2 * h + 1, :, cols]
        blocks = [v1[0:1, :] + v2]
        for r1 in range(1, sub):
            blocks.append(jnp.where(row8 < PEER_TOPK // (r1 + 1), v1[r1:r1 + 1, :] + v2[:sub, :], -jnp.inf))
        blocks.append(v1[sub:, :] + v2[0:1, :])
        best, crank = _extract_top16(jnp.concatenate(blocks, axis=0), cand_iota)
        z = jnp.zeros((1, LANE), f32)
        for k in range(PEER_TOPK):
            z = z + jnp.exp(best[k] - best[0])
        sel = jnp.where(crank < PEER_TOPK, 1.0, 0.0)
        rank1 = rank_ref[2 * h, :, cols]
        lim = jnp.zeros((N_KEYS, LANE), jnp.int32)
        for r1 in range(PEER_TOPK):
            if r1 == 0:
                cnt = jnp.sum(sel[:PEER_TOPK, :], axis=0, keepdims=True)
            elif r1 < sub:
                lo = PEER_TOPK + (r1 - 1) * sub
                cnt = jnp.sum(sel[lo:lo + sub, :], axis=0, keepdims=True)
            else:
                lo = PEER_TOPK + n_mid * sub + (r1 - sub)
                cnt = sel[lo:lo + 1, :]
            lim = jnp.where(rank1 == r1, cnt.astype(jnp.int32), lim)
        lim_ref[h, :, cols] = lim
        rk2_ref[h, :, cols] = rank_ref[2 * h + 1, :, cols]
        e1_ref[h, :, cols] = jnp.exp(s_ref[2 * h, :, cols] - v1[0:1, :])
        e2_ref[h, :, cols] = jnp.exp(s_ref[2 * h + 1, :, cols] - v2[0:1, :]) / z
        return carry

    lax.fori_loop(0, PEER_HEADS * (tl // LANE), head_body, 0)


def _gelu_tanh(x):
    return 0.5 * x * (1.0 + jnp.tanh(math.sqrt(2.0 / math.pi) * (x + 0.044715 * (x * x * x))))


def _peer_expert_kernel(x_ref, h2_ref, pu_ref, pvt_ref, lim_ref, rk2_ref, e1_ref, e2_ref, o_ref,
                        acc_ref, a_cur_ref, a_nxt_ref, c_ref):
    f32 = jnp.float32
    e_step = pl.program_id(1)
    tl = h2_ref.shape[0]
    n_i1 = pu_ref.shape[0] // N_KEYS
    chunk = 2 * LANE if tl % (2 * LANE) == 0 else LANE

    @pl.when(e_step == 0)
    def _():
        acc_ref[...] = jnp.zeros_like(acc_ref)
        a_nxt_ref[...] = jnp.zeros_like(a_nxt_ref)

    a_cur_ref[...] = a_nxt_ref[...]
    a_nxt_ref[...] = lax.dot_general(pu_ref[...], h2_ref[...], (((1,), (1,)), ((), ())),
                                     preferred_element_type=f32)
    for c0 in range(0, tl, chunk):
        for l0 in range(c0, c0 + chunk, LANE):
            cols = slice(l0, l0 + LANE)
            for ii in range(n_i1):
                rows = slice(ii * N_KEYS, (ii + 1) * N_KEYS)
                w = jnp.zeros((N_KEYS, LANE), f32)
                for h in range(PEER_HEADS):
                    sel = rk2_ref[h, :, cols] < lim_ref[h, ii:ii + 1, cols]
                    w = w + jnp.where(sel, e2_ref[h, :, cols], 0.0) * e1_ref[h, ii:ii + 1, cols]
                c_ref[rows, cols] = (w * _gelu_tanh(a_cur_ref[rows, cols])).astype(jnp.bfloat16)
        cc = slice(c0, c0 + chunk)
        acc_ref[:, cc] += jnp.dot(pvt_ref[...], c_ref[:, cc], preferred_element_type=f32)

    @pl.when(e_step == pl.num_programs(1) - 1)
    def _():
        o_ref[...] = x_ref[...] + acc_ref[...].T


def _token_tile(T):
    for tl in (640, 512, 384, 256, 128):
        if T % tl == 0:
            return tl
    raise ValueError(f"token count {T} is not a multiple of {LANE}")


def peer_residual(x, g, wqt, sk, pu, pvt):
    T = x.shape[0]
    tl = _token_tile(T)
    nt = T // tl
    n_exp = pu.shape[0]
    head_shape = jax.ShapeDtypeStruct((PEER_HEADS, N_KEYS, T), jnp.int32)
    head_shape_f = jax.ShapeDtypeStruct((PEER_HEADS, N_KEYS, T), jnp.float32)
    head_spec = pl.BlockSpec((PEER_HEADS, N_KEYS, tl), lambda i: (0, 0, i))
    h2, lim, rk2, e1, e2 = pl.pallas_call(
        _peer_router_kernel,
        grid=(nt,),
        in_specs=[pl.BlockSpec((tl, D_MODEL), lambda i: (i, 0)),
                  pl.BlockSpec((1, D_MODEL), lambda i: (0, 0)),
                  pl.BlockSpec((PEER_HEADS * PEER_QDIM, D_MODEL), lambda i: (0, 0)),
                  pl.BlockSpec((N_HC, N_KEYS, HALF_Q), lambda i: (0, 0, 0))],
        out_specs=[pl.BlockSpec((tl, D_MODEL), lambda i: (i, 0)), head_spec, head_spec, head_spec, head_spec],
        out_shape=[jax.ShapeDtypeStruct((T, D_MODEL), jnp.bfloat16), head_shape, head_shape,
                   head_shape_f, head_shape_f],
        scratch_shapes=[pltpu.VMEM((PEER_HEADS * PEER_QDIM, tl), jnp.bfloat16),
                        pltpu.VMEM((N_HC, N_KEYS, tl), jnp.float32),
                        pltpu.VMEM((N_HC, N_KEYS, tl), jnp.int32),
                        pltpu.VMEM((N_HC, PEER_TOPK, tl), jnp.float32)],
        compiler_params=pltpu.CompilerParams(dimension_semantics=("parallel",), vmem_limit_bytes=VMEM_LIMIT),
        name="peer_router",
    )(x, g.reshape(1, D_MODEL), wqt, sk)

    te = PEER_EXPERT_TILE
    n_tiles = n_exp // te
    nxt = lambda e: jnp.minimum(e, n_tiles - 1)
    cur = lambda e: jnp.maximum(e - 1, 0)
    i2_spec = pl.BlockSpec((PEER_HEADS, N_KEYS, tl), lambda i, e: (0, 0, i))
    i1_spec = pl.BlockSpec((PEER_HEADS, te // N_KEYS, tl), lambda i, e: (0, cur(e), i))
    return pl.pallas_call(
        _peer_expert_kernel,
        grid=(nt, n_tiles + 1),
        in_specs=[pl.BlockSpec((tl, D_MODEL), lambda i, e: (i, 0)),
                  pl.BlockSpec((tl, D_MODEL), lambda i, e: (i, 0)),
                  pl.BlockSpec((te, D_MODEL), lambda i, e: (nxt(e), 0)),
                  pl.BlockSpec((D_MODEL, te), lambda i, e: (0, cur(e))),
                  i1_spec, i2_spec, i1_spec, i2_spec],
        out_specs=pl.BlockSpec((tl, D_MODEL), lambda i, e: (i, 0)),
        out_shape=jax.ShapeDtypeStruct((T, D_MODEL), jnp.float32),
        scratch_shapes=[pltpu.VMEM((D_MODEL, tl), jnp.float32),
                        pltpu.VMEM((te, tl), jnp.float32),
                        pltpu.VMEM((te, tl), jnp.float32),
                        pltpu.VMEM((te, tl), jnp.bfloat16)],
        compiler_params=pltpu.CompilerParams(dimension_semantics=("parallel", "arbitrary"),
                                             vmem_limit_bytes=VMEM_LIMIT),
        name="peer_experts",
    )(x, h2, pu, pvt, lim, rk2, e1, e2)


C_QKVB = 3 * WIDTH_A
C_GB = C_QKVB + 3 * WIDTH_B
C_UC = C_GB + WIDTH_B
C_GATE = C_UC + WIDTH_C
C_BA = C_GATE + 3 * D_MODEL
D_IN2 = C_BA + LANE
S5_BLOCK = 256


def _reorder_w_in(w):
    cols = [w[:, OFF_QA:OFF_BETA], w[:, OFF_UC:OFF_GATE], w[:, OFF_GATE:], w[:, OFF_BETA:OFF_UC],
            jnp.zeros((w.shape[0], LANE - 2 * N_HEADS_B), w.dtype)]
    return jnp.concatenate(cols, axis=1).astype(jnp.bfloat16)


def trunk_layer(x, lp, layer_idx, att, prompt, k_past, v_past, conv_hist, delta0, ssm0_re, ssm0_im):
    f32 = jnp.float32
    B, Lx = x.shape[0], x.shape[1]
    T = B * Lx
    z = norm_matmul(x.reshape(T, D_MODEL), lp['norm1'], lp['w_in2'])

    qn, kn, knb, vb = qk_prep(z, lp['q_norm'], lp['k_norm'], _token_tile(T))
    lam_init = 0.8 - 0.6 * math.exp(-0.3 * layer_idx)
    lqk = lp['lambda_qk']
    lam = jnp.exp(jnp.sum(lqk[0] * lqk[1])) - jnp.exp(jnp.sum(lqk[2] * lqk[3])) + lam_init
    scalars = jnp.stack([lam, jnp.asarray(1.0 - lam_init, f32)]).astype(f32)
    r3 = lambda a: a.reshape(B, Lx, WIDTH_A)
    if prompt:
        oA = attn_prompt(r3(qn), r3(knb), vb.reshape(B, Lx, 2 * WIDTH_A), scalars, att['far'], att['tiles'], lp['subln'])
    else:
        P = k_past.shape[1]
        oA = attn_sample(r3(qn), r3(knb), vb.reshape(B, Lx, 2 * WIDTH_A), k_past.reshape(B, P, WIDTH_A), v_past.reshape(B, P, WIDTH_A),
                         scalars, att['bias_past'], att['bias_new'], lp['subln'])

    z3 = z.reshape(B, Lx, D_IN2)
    off = PAD if prompt else 0
    L = Lx - off
    kA = r3(kn)[:, off:].reshape(B, L, N_HEADS_A, 2 * HEAD_DIM_A)
    vA = z3[:, off:, 2 * WIDTH_A:3 * WIDTH_A].reshape(B, L, N_HEADS_A, 2 * HEAD_DIM_A)

    oB, delta, tail = gdn_mixer(z3, (C_QKVB // (3 * WIDTH_B), C_GB // WIDTH_B, C_BA // LANE), conv_hist, delta0,
                                lp['conv_w'], lp['a_log'], lp['dt_bias'], lp['gdn_norm'],
                                CHUNK if prompt else Lx, off)
    conv_state = tail[:, GDN_TAIL - (CONV_W - 1):]

    oC, ssm_re, ssm_im = s5_glu(z3, C_UC // WIDTH_C, ssm0_re, ssm0_im, lp['s5'], lp['w_glu'],
                                S5_BLOCK if prompt else Lx)

    x2 = merge_residual(x.reshape(T, D_MODEL), oA.reshape(T, WIDTH_A), oB.reshape(T, WIDTH_B),
                        oC.reshape(T, WIDTH_C), z, C_GATE // D_MODEL, lp['wb_a'], lp['wb_b'], lp['wb_c'],
                        lp['w_out'], Lx, off)

    x = peer_residual(x2, lp['norm2'], lp['peer_wqt'], lp['peer_sk'], lp['peer_pu'],
                      lp['peer_pvt']).reshape(B, Lx, D_MODEL)
    return (x, kA, vA, conv_state, delta, ssm_re, ssm_im)


def kernel(x_prompt, x_sample, cache_k, cache_v, state_conv, state_delta, state_ssm_re, state_ssm_im,
           meta_tokens, rel_bias, norm1_g, norm2_g, final_norm_g, w_in, q_norm_g, k_norm_g, lambda_qk,
           subln_g, conv_w, gdn_a_log, gdn_dt_bias, gdn_norm_g, s5_a_re, s5_a_im, s5_b_re, s5_b_im,
           s5_c_re, s5_c_im, s5_d, s5_log_dt, w_glu, w_branch_a, w_branch_b, w_branch_c, w_out,
           peer_wq, peer_subkeys, peer_u, peer_v):
    f32 = jnp.float32
    bf16 = jnp.bfloat16
    params = [dict(norm1=norm1_g[l], norm2=norm2_g[l], w_in2=_reorder_w_in(w_in[l]), q_norm=q_norm_g[l],
                   k_norm=k_norm_g[l], lambda_qk=lambda_qk[l], subln=subln_g[l], conv_w=conv_w[l],
                   a_log=gdn_a_log[l], dt_bias=gdn_dt_bias[l], gdn_norm=gdn_norm_g[l],
                   s5=s5_params(s5_a_re[l], s5_a_im[l], s5_b_re[l], s5_b_im[l], s5_c_re[l], s5_c_im[l], s5_d[l],
                                s5_log_dt[l]),
                   w_glu=w_glu[l], wb_a=w_branch_a[l], wb_b=w_branch_b[l], wb_c=w_branch_c[l], w_out=w_out[l],
                   peer_wqt=peer_wq[l].T.astype(bf16),
                   peer_sk=peer_subkeys[l].reshape(N_HC, N_KEYS, HALF_Q).astype(bf16),
                   peer_pu=peer_u[l].astype(bf16), peer_pvt=peer_v[l].T.astype(bf16))
              for l in range(DEPTH)]

    qpos = jnp.arange(ATT_TQ, dtype=jnp.int32)
    tiles = jnp.stack([t5_bias(qpos, d * ATT_TK + jnp.arange(ATT_TK, dtype=jnp.int32), rel_bias)
                       for d in (-2, -1, 0)], axis=1)
    far = t5_bias(jnp.full((1,), 2 * ATT_TK, jnp.int32), jnp.zeros((1,), jnp.int32), rel_bias).reshape(N_HEADS_A)
    P, Ls = cache_k.shape[2], x_sample.shape[1]
    bias_s = t5_bias(P + jnp.arange(Ls, dtype=jnp.int32), jnp.arange(P + Ls, dtype=jnp.int32), rel_bias)
    att_p = dict(tiles=tiles, far=far)
    att_s = dict(bias_past=bias_s[:, :, :P], bias_new=bias_s[:, :, P:])

    B = x_prompt.shape[0]
    xp = jnp.concatenate([jnp.zeros((B, PAD, D_MODEL), f32),
                          jnp.broadcast_to(meta_tokens[None], (B, N_META, D_MODEL)), x_prompt], axis=1)
    outs_p = [[] for _ in range(6)]
    for l in range(DEPTH):
        res = trunk_layer(
            xp, params[l], l, att_p, True, None, None,
            jnp.zeros((B, CONV_W - 1, 3 * WIDTH_B), f32),
            jnp.zeros((B, N_HEADS_B, HEAD_DIM_B, HEAD_DIM_B), f32),
            jnp.zeros((B, S5_GROUPS, S5_STATE), f32), jnp.zeros((B, S5_GROUPS, S5_STATE), f32))
        xp = res[0]
        for acc, r in zip(outs_p, res[1:]):
            acc.append(r)
    y_prompt = rmsnorm(xp, final_norm_g)[:, FRONT:]

    xs = x_sample
    outs_s = [[] for _ in range(6)]
    for l in range(DEPTH):
        res = trunk_layer(
            xs, params[l], l, att_s, False, cache_k[l], cache_v[l], state_conv[l],
            state_delta[l], state_ssm_re[l], state_ssm_im[l])
        xs = res[0]
        for acc, r in zip(outs_s, res[1:]):
            acc.append(r)
    y_sample = rmsnorm(xs, final_norm_g)

    kp, vp, cp, dp, srp, sip = [jnp.stack(a) for a in outs_p]
    ks_, vs_, cs_, ds_, srs, sis = [jnp.stack(a) for a in outs_s]
    return (y_prompt, y_sample, kp, vp, ks_, vs_, cp, cs_, dp, ds_, srp, sip, srs, sis)
```

```python
import functools
import math

import jax
import jax.numpy as jnp
from jax import lax
from jax.experimental import pallas as pl
from jax.experimental.pallas import tpu as pltpu

D_MODEL = 1024
DEPTH = 4
CHUNK = 64
N_META = 16
EPS = 1e-6
NEG_INF = -1e30
N_HEADS_A = 4
HEAD_DIM_A = 64
NUM_BUCKETS = 32
MAX_DISTANCE = 128
N_HEADS_B = 4
HEAD_DIM_B = 128
CONV_W = 4
S5_GROUP = 16
S5_GROUPS = 32
S5_STATE = 64
PEER_HEADS = 8
PEER_QDIM = 256
N_KEYS = 128
PEER_TOPK = 16

WIDTH_A = N_HEADS_A * 2 * HEAD_DIM_A
WIDTH_B = N_HEADS_B * HEAD_DIM_B
WIDTH_C = S5_GROUPS * S5_GROUP
OFF_QA = 0
OFF_KA = OFF_QA + WIDTH_A
OFF_VA = OFF_KA + WIDTH_A
OFF_QKVB = OFF_VA + WIDTH_A
OFF_GB = OFF_QKVB + 3 * WIDTH_B
OFF_BETA = OFF_GB + WIDTH_B
OFF_ALPHA = OFF_BETA + N_HEADS_B
OFF_UC = OFF_ALPHA + N_HEADS_B
OFF_GATE = OFF_UC + WIDTH_C
D_IN = OFF_GATE + 3 * D_MODEL

LANE = 128
VMEM_LIMIT = 56 * 1024 * 1024
WEIGHT_TILE_BYTES = 6 * 1024 * 1024
HIGHEST = lax.Precision.HIGHEST


def _norm_mm_kernel(x_ref, g_ref, w_ref, o_ref, h_ref):
    @pl.when(pl.program_id(1) == 0)
    def _():
        x = x_ref[...]
        h = x * lax.rsqrt(jnp.mean(x * x, axis=-1, keepdims=True) + EPS) * g_ref[...]
        h_ref[...] = h.astype(jnp.bfloat16)

    o_ref[...] = jnp.dot(h_ref[...], w_ref[...], preferred_element_type=jnp.float32)


def norm_matmul(x, g, wb, tm=512):
    M, K = x.shape
    N = wb.shape[1]
    n_lanes = N // LANE
    tn = LANE * max(d for d in range(1, n_lanes + 1) if n_lanes % d == 0 and d * LANE * K * 2 <= WEIGHT_TILE_BYTES)
    tm = min(tm, M)
    return pl.pallas_call(
        _norm_mm_kernel,
        grid=(pl.cdiv(M, tm), N // tn),
        in_specs=[pl.BlockSpec((tm, K), lambda i, j: (i, 0)),
                  pl.BlockSpec((1, K), lambda i, j: (0, 0)),
                  pl.BlockSpec((K, tn), lambda i, j: (0, j))],
        out_specs=pl.BlockSpec((tm, tn), lambda i, j: (i, j)),
        out_shape=jax.ShapeDtypeStruct((M, N), jnp.float32),
        scratch_shapes=[pltpu.VMEM((tm, K), jnp.bfloat16)],
        compiler_params=pltpu.CompilerParams(dimension_semantics=("parallel", "arbitrary"),
                                             vmem_limit_bytes=VMEM_LIMIT),
        name="norm_proj",
    )(x, g.reshape(1, K), wb)


def rmsnorm(x, g):
    xf = x.astype(jnp.float32)
    y = xf * lax.rsqrt(jnp.mean(xf * xf, axis=-1, keepdims=True) + EPS)
    return (y * g.astype(jnp.float32)).astype(x.dtype)


def t5_bias(q_pos, k_pos, table):
    rel = k_pos[None, :] - q_pos[:, None]
    half = NUM_BUCKETS // 2
    exact = half // 2
    n = jnp.abs(rel)
    nf = jnp.maximum(n, 1).astype(jnp.float32)
    far = exact + (jnp.log(nf / exact) / math.log(MAX_DISTANCE / exact) * (half - exact)).astype(jnp.int32)
    bucket = jnp.where(rel > 0, half, 0) + jnp.where(n < exact, n, jnp.minimum(far, half - 1))
    return jnp.moveaxis(table[bucket].astype(jnp.float32), -1, 0)


FRONT = 256
PAD = FRONT - N_META
ATT_TQ = 256
ATT_TK = 256


def _qk_prep_kernel(q_ref, k_ref, v_ref, gq_ref, gk_ref, seg_ref, qn_ref, kn_ref, knb_ref, vb_ref):
    f32 = jnp.float32
    seg = seg_ref[...]

    def norm(x, g):
        ms = jnp.dot(x * x, seg, precision=HIGHEST, preferred_element_type=f32)
        return x * lax.rsqrt(ms + EPS) * g

    qn_ref[...] = (norm(q_ref[...], gq_ref[...]) * (HEAD_DIM_A ** -0.5)).astype(jnp.bfloat16)
    kn = norm(k_ref[...], gk_ref[...])
    kn_ref[...] = kn
    knb_ref[...] = kn.astype(jnp.bfloat16)
    v = v_ref[...].astype(jnp.bfloat16)
    hd = 2 * HEAD_DIM_A
    lane = lax.broadcasted_iota(jnp.int32, (v.shape[0], hd), 1)
    ones_col = jnp.where(lane == 0, 1.0, 0.0).astype(jnp.bfloat16)
    vb_ref[...] = jnp.concatenate(
        [piece for h in range(N_HEADS_A) for piece in (v[:, h * hd:(h + 1) * hd], ones_col)], axis=1)


def qk_prep(z, gq, gk, tm):
    T = z.shape[0]
    seg = jnp.kron(jnp.eye(WIDTH_A // HEAD_DIM_A, dtype=jnp.float32),
                   jnp.full((HEAD_DIM_A, HEAD_DIM_A), 1.0 / HEAD_DIM_A, jnp.float32))
    row = lambda j: pl.BlockSpec((tm, WIDTH_A), lambda i, j=j: (i, j))
    const = lambda shape: pl.BlockSpec(shape, lambda i: (0,) * len(shape))
    out = lambda: pl.BlockSpec((tm, WIDTH_A), lambda i: (i, 0))
    bshape = jax.ShapeDtypeStruct((T, WIDTH_A), jnp.bfloat16)
    return pl.pallas_call(
        _qk_prep_kernel,
        grid=(T // tm,),
        in_specs=[row(0), row(1), row(2), const((1, WIDTH_A)), const((1, WIDTH_A)), const((WIDTH_A, WIDTH_A))],
        out_specs=[out(), out(), out(), pl.BlockSpec((tm, 2 * WIDTH_A), lambda i: (i, 0))],
        out_shape=[bshape, jax.ShapeDtypeStruct((T, WIDTH_A), jnp.float32), bshape,
                   jax.ShapeDtypeStruct((T, 2 * WIDTH_A), jnp.bfloat16)],
        compiler_params=pltpu.CompilerParams(dimension_semantics=("parallel",)),
        name="qk_prep",
    )(z, z, z, jnp.tile(gq, WIDTH_A // HEAD_DIM_A).reshape(1, WIDTH_A),
      jnp.tile(gk, WIDTH_A // HEAD_DIM_A).reshape(1, WIDTH_A), seg)


def _split_maps(q):
    lane = lax.broadcasted_iota(jnp.int32, q.shape, 1)
    zero = jnp.zeros_like(q)
    return jnp.where(lane < HEAD_DIM_A, q, zero), jnp.where(lane >= HEAD_DIM_A, q, zero)


def _subln(o, g, scale):
    return o * lax.rsqrt(jnp.mean(o * o, axis=-1, keepdims=True) + EPS) * g * scale


def _attn_prompt_kernel(sc_ref, far_ref, q_ref, k_ref, v_ref, bias_ref, g_ref, o_ref, m_ref, acc_ref):
    f32 = jnp.float32
    h = pl.program_id(1)
    qi = pl.program_id(2)
    tq, tk = ATT_TQ, ATT_TK
    hd = 2 * HEAD_DIM_A
    qs = jnp.concatenate(_split_maps(q_ref[0]), axis=0)
    m_ref[...] = jnp.full(m_ref.shape, NEG_INF, f32)
    acc_ref[...] = jnp.zeros(acc_ref.shape, f32)
    far = far_ref[h]

    def tile(k0, width, general):
        k0 = pl.multiple_of(k0, tk)
        kt = k_ref[0, pl.ds(k0, width), :]
        vt = v_ref[0, pl.ds(k0, width), :]
        s = lax.dot_general(qs, kt, (((1,), (1,)), ((), ())), preferred_element_type=f32)
        if general:
            kj = k0 // tk
            bias = bias_ref[0, jnp.clip(kj - qi + 2, 0, 2)]
            qpos = qi * tq + lax.broadcasted_iota(jnp.int32, (tq, tk), 0)
            kpos = k0 + lax.broadcasted_iota(jnp.int32, (tq, tk), 1)
            qchunk = jnp.where(qpos < FRONT, 0, 1 + jnp.right_shift(qpos - FRONT, 6))
            kchunk = jnp.where(kpos < FRONT, 0, 1 + jnp.right_shift(kpos - FRONT, 6))
            mask = (kpos >= PAD) & (kchunk <= qchunk)
            s = jnp.where(jnp.concatenate([mask, mask], axis=0), s + jnp.concatenate([bias, bias], axis=0), NEG_INF)
        else:
            s = s + far
        n_lane_tiles = width // LANE
        smax = s[:, :LANE]
        for j in range(1, n_lane_tiles):
            smax = jnp.maximum(smax, s[:, j * LANE:(j + 1) * LANE])
        m_old = m_ref[...]
        m_new = jnp.maximum(m_old, jnp.broadcast_to(jnp.max(smax, axis=1, keepdims=True), m_old.shape))
        alpha = jnp.exp(m_old - m_new)
        p = jnp.exp(s - jnp.concatenate([m_new] * n_lane_tiles, axis=1))
        acc_ref[...] = (jnp.concatenate([alpha, alpha], axis=1) * acc_ref[...]
                        + jnp.dot(p.astype(jnp.bfloat16), vt, preferred_element_type=f32))
        m_ref[...] = m_new

    tile(0, tk, True)

    n_far = jnp.maximum(qi - 2, 0)
    odd = n_far % 2

    @pl.when(odd == 1)
    def _():
        tile(tk, tk, False)

    def far_body(j, carry):
        tile((1 + odd + 2 * j) * tk, 2 * tk, False)
        return carry

    lax.fori_loop(0, n_far // 2, far_body, 0)

    @pl.when(qi >= 2)
    def _():
        tile((qi - 1) * tk, tk, True)

    @pl.when(qi >= 1)
    def _():
        tile(qi * tk, tk, True)

    acc = acc_ref[...]
    out = acc[:, :hd] / acc[:, hd:hd + 1]
    o = out[:tq] - sc_ref[0] * out[tq:]
    o_ref[0] = _subln(o, g_ref[...], sc_ref[1])


def attn_prompt(qn, knb, vb, scalars, far, bias_tiles, subln_g):
    B, Lp, _ = qn.shape
    hd = 2 * HEAD_DIM_A
    smem = pl.BlockSpec(memory_space=pltpu.SMEM)
    return pl.pallas_call(
        _attn_prompt_kernel,
        grid=(B, N_HEADS_A, Lp // ATT_TQ),
        in_specs=[smem, smem,
                  pl.BlockSpec((1, ATT_TQ, hd), lambda b, h, i: (b, i, h)),
                  pl.BlockSpec((1, Lp, hd), lambda b, h, i: (b, 0, h)),
                  pl.BlockSpec((1, Lp, 2 * hd), lambda b, h, i: (b, 0, h)),
                  pl.BlockSpec((1, 3, ATT_TQ, ATT_TK), lambda b, h, i: (h, 0, 0, 0)),
                  pl.BlockSpec((1, hd), lambda b, h, i: (0, 0))],
        out_specs=pl.BlockSpec((1, ATT_TQ, hd), lambda b, h, i: (b, i, h)),
        out_shape=jax.ShapeDtypeStruct((B, Lp, WIDTH_A), jnp.float32),
        scratch_shapes=[pltpu.VMEM((2 * ATT_TQ, hd), jnp.float32), pltpu.VMEM((2 * ATT_TQ, 2 * hd), jnp.float32)],
        compiler_params=pltpu.CompilerParams(dimension_semantics=("parallel", "parallel", "parallel"),
                                             vmem_limit_bytes=VMEM_LIMIT),
        name="diff_attn_prompt",
    )(scalars, far, qn, knb, vb, bias_tiles, subln_g.reshape(1, hd))


def _attn_sample_kernel(sc_ref, q_ref, kp_ref, vp_ref, kn_ref, vn_ref, bp_ref, bn_ref, g_ref, o_ref):
    f32 = jnp.float32
    bf16 = jnp.bfloat16
    qs = _split_maps(q_ref[0])
    kp = kp_ref[0].astype(bf16)
    vp = vp_ref[0].astype(bf16)
    kn = kn_ref[0]
    vn = vn_ref[0][:, :2 * HEAD_DIM_A]
    dn = (((1,), (1,)), ((), ()))
    outs = []
    for c in range(2):
        sp = lax.dot_general(qs[c], kp, dn, preferred_element_type=f32) + bp_ref[0]
        sn = lax.dot_general(qs[c], kn, dn, preferred_element_type=f32) + bn_ref[0]
        m = jnp.maximum(jnp.max(sp, axis=1, keepdims=True), jnp.max(sn, axis=1, keepdims=True))
        pp = jnp.exp(sp - m)
        pn = jnp.exp(sn - m)
        l = jnp.sum(pp, axis=1, keepdims=True) + jnp.sum(pn, axis=1, keepdims=True)
        acc = (jnp.dot(pp.astype(bf16), vp, preferred_element_type=f32)
               + jnp.dot(pn.astype(bf16), vn, preferred_element_type=f32))
        outs.append(acc / l)
    o_ref[0] = _subln(outs[0] - sc_ref[0] * outs[1], g_ref[...], sc_ref[1])


def attn_sample(qn, knb, vb, k_past, v_past, scalars, bias_past, bias_new, subln_g):
    B, L, _ = qn.shape
    P = k_past.shape[1]
    hd = 2 * HEAD_DIM_A
    new = pl.BlockSpec((1, L, hd), lambda b, h: (b, 0, h))
    past = pl.BlockSpec((1, P, hd), lambda b, h: (b, 0, h))
    return pl.pallas_call(
        _attn_sample_kernel,
        grid=(B, N_HEADS_A),
        in_specs=[pl.BlockSpec(memory_space=pltpu.SMEM), new, past, past, new,
                  pl.BlockSpec((1, L, 2 * hd), lambda b, h: (b, 0, h)),
                  pl.BlockSpec((1, L, P), lambda b, h: (h, 0, 0)),
                  pl.BlockSpec((1, L, L), lambda b, h: (h, 0, 0)),
                  pl.BlockSpec((1, hd), lambda b, h: (0, 0))],
        out_specs=new,
        out_shape=jax.ShapeDtypeStruct((B, L, WIDTH_A), jnp.float32),
        compiler_params=pltpu.CompilerParams(dimension_semantics=("parallel", "parallel")),
        name="diff_attn_sample",
    )(scalars, qn, k_past, v_past, knb, vb, bias_past, bias_new, subln_g.reshape(1, hd))


S5_N = S5_GROUPS * S5_STATE
S5_SLAB = 8
S5_LANES = 512


def _sigmoid(x):
    return 1.0 / (1.0 + jnp.exp(-x))


def _s5_kernel(u_ref, x0_ref, bmat_ref, cmat_ref, lamp_ref, ppow_ref, d_ref, wglu_ref, oc_ref, xf_ref,
               bu_ref, carry_ref):
    f32 = jnp.float32
    bf16 = jnp.bfloat16
    t = pl.program_id(1)
    tb = u_ref.shape[1]

    @pl.when(t == 0)
    def _():
        carry_ref[...] = x0_ref[0]

    u = u_ref[0]
    bu_ref[...] = jnp.dot(u.astype(bf16), bmat_ref[...], preferred_element_type=f32)

    row = lax.broadcasted_iota(jnp.int32, (S5_SLAB, S5_LANES), 0)
    for c in range(S5_N // S5_LANES):
        re = slice(c * S5_LANES, (c + 1) * S5_LANES)
        im = slice(S5_N + c * S5_LANES, S5_N + (c + 1) * S5_LANES)

        def slab(i, carry):
            cre, cim = carry
            rows = pl.ds(pl.multiple_of(i * S5_SLAB, S5_SLAB), S5_SLAB)
            yre = bu_ref[rows, re]
            yim = bu_ref[rows, im]
            for s in range(3):
                sh = 1 << s
                sre = jnp.where(row >= sh, pltpu.roll(yre, sh, 0), 0.0)
                sim = jnp.where(row >= sh, pltpu.roll(yim, sh, 0), 0.0)
                lr = lamp_ref[s, 0, :, re]
                li = lamp_ref[s, 1, :, re]
                yre, yim = yre + (lr * sre - li * sim), yim + (lr * sim + li * sre)
            pr = ppow_ref[0, :, re]
            pi = ppow_ref[1, :, re]
            yre, yim = yre + (pr * cre - pi * cim), yim + (pr * cim + pi * cre)
            bu_ref[rows, re] = yre
            bu_ref[rows, im] = yim
            last = S5_SLAB - 1
            return (jnp.broadcast_to(yre[last:last + 1, :], yre.shape),
                    jnp.broadcast_to(yim[last:last + 1, :], yim.shape))

        cre, cim = lax.fori_loop(0, tb // S5_SLAB, slab, (carry_ref[:, re], carry_ref[:, im]))
        carry_ref[:, re] = cre
        carry_ref[:, im] = cim

    y = jnp.dot(bu_ref[...].astype(bf16), cmat_ref[...], preferred_element_type=f32) + d_ref[...] * u
    gl = jnp.dot(_gelu_tanh(y).astype(bf16), wglu_ref[...], preferred_element_type=f32)
    oc_ref[0] = gl[:, :WIDTH_C] * _sigmoid(gl[:, WIDTH_C:])

    @pl.when(t == pl.num_programs(1) - 1)
    def _():
        xf_ref[0] = carry_ref[...]


def s5_params(a_re, a_im, b_re, b_im, c_re, c_im, d, log_dt):
    f32 = jnp.float32
    lam = lax.complex(a_re, a_im)
    lam_bar = jnp.exp(lam * jnp.exp(log_dt)[:, None])
    b_bar = ((lam_bar - 1.0) / lam)[..., None] * lax.complex(b_re, b_im)
    eye = jnp.eye(S5_GROUPS, dtype=f32)
    bd_in = lambda m: jnp.einsum('gpi,gh->gihp', m, eye).reshape(WIDTH_C, S5_N)
    bd_out = lambda m: jnp.einsum('gip,gh->gphi', m, eye).reshape(S5_N, WIDTH_C)
    bmat = jnp.concatenate([bd_in(b_bar.real), bd_in(b_bar.imag)], axis=1).astype(jnp.bfloat16)
    cmat = jnp.concatenate([bd_out(c_re), bd_out(-c_im)], axis=0).astype(jnp.bfloat16)
    lb = lam_bar.reshape(S5_N)
    rep = lambda v: jnp.broadcast_to(v[None, :], (S5_SLAB, S5_N))
    pows = [lb, lb * lb, (lb * lb) * (lb * lb)]
    lamp = jnp.stack([jnp.stack([rep(p.real), rep(p.imag)]) for p in pows])
    run = [lb]
    for _ in range(S5_SLAB - 1):
        run.append(run[-1] * lb)
    pp = jnp.stack(run)
    ppow = jnp.stack([pp.real, pp.imag])
    return dict(bmat=bmat, cmat=cmat, lamp=lamp.astype(f32), ppow=ppow.astype(f32), d=d.reshape(1, WIDTH_C))


def s5_glu(z3, col_block, x0_re, x0_im, sp, w_glu, tb):
    B, Lx, _ = z3.shape
    x0 = jnp.concatenate([x0_re.reshape(B, S5_N), x0_im.reshape(B, S5_N)], axis=1)
    x0 = jnp.broadcast_to(x0[:, None, :], (B, S5_SLAB, 2 * S5_N))
    const = lambda a: pl.BlockSpec(a.shape, lambda b, t: (0,) * a.ndim)
    wg = w_glu.astype(jnp.bfloat16)
    oc, xf = pl.pallas_call(
        _s5_kernel,
        grid=(B, Lx // tb),
        in_specs=[pl.BlockSpec((1, tb, WIDTH_C), lambda b, t: (b, t, col_block)),
                  pl.BlockSpec((1, S5_SLAB, 2 * S5_N), lambda b, t: (b, 0, 0)),
                  const(sp['bmat']), const(sp['cmat']), const(sp['lamp']), const(sp['ppow']), const(sp['d']),
                  const(wg)],
        out_specs=[pl.BlockSpec((1, tb, WIDTH_C), lambda b, t: (b, t, 0)),
                   pl.BlockSpec((1, S5_SLAB, 2 * S5_N), lambda b, t: (b, 0, 0))],
        out_shape=[jax.ShapeDtypeStruct((B, Lx, WIDTH_C), jnp.float32),
                   jax.ShapeDtypeStruct((B, S5_SLAB, 2 * S5_N), jnp.float32)],
        scratch_shapes=[pltpu.VMEM((tb, 2 * S5_N), jnp.float32), pltpu.VMEM((S5_SLAB, 2 * S5_N), jnp.float32)],
        compiler_params=pltpu.CompilerParams(dimension_semantics=("parallel", "arbitrary"),
                                             vmem_limit_bytes=VMEM_LIMIT),
        name="s5_glu",
    )(z3, x0, sp['bmat'], sp['cmat'], sp['lamp'], sp['ppow'], sp['d'], wg)
    shape = (B, S5_GROUPS, S5_STATE)
    return oc, xf[:, 0, :S5_N].reshape(shape), xf[:, 0, S5_N:].reshape(shape)


def _gdn_kernel(qkv_ref, gate_ref, ba_ref, hist_ref, s0_ref, cw_ref, av_ref, dt_ref, gn_ref,
                ob_ref, sout_ref, tail_ref, s_ref, prev_ref, *, pad_rows):
    f32 = jnp.float32
    t = pl.program_id(1)
    C = qkv_ref.shape[1]
    hd = HEAD_DIM_B
    dn_last = (((1,), (1,)), ((), ()))
    dn_first = (((0,), (0,)), ((), ()))
    dot = lambda a, b: jnp.dot(a, b, precision=HIGHEST, preferred_element_type=f32)

    @pl.when(t == 0)
    def _():
        s_ref[...] = s0_ref[0]
        prev_ref[...] = hist_ref[0]

    cur = qkv_ref[0]
    tail = prev_ref.shape[0]
    ext = jnp.concatenate([prev_ref[...], cur], axis=0)
    w = cw_ref[...]
    conv = cur * w[CONV_W - 1:CONV_W, :]
    for i in range(CONV_W - 1):
        lo = tail - (CONV_W - 1) + i
        conv = conv + ext[lo:lo + C, :] * w[i:i + 1, :]
    prev_ref[...] = cur[C - tail:, :]
    c = conv * _sigmoid(conv)

    ba = ba_ref[0]
    xg = ba + dt_ref[...]
    softplus = jnp.maximum(xg, 0.0) + jnp.log(1.0 + jnp.exp(-jnp.abs(xg)))
    g_all = -jnp.exp(av_ref[...]) * softplus
    if pad_rows:
        grow = t * C + lax.broadcasted_iota(jnp.int32, g_all.shape, 0)
        g_all = jnp.where(grow >= pad_rows, g_all, 0.0)
    ri = lax.broadcasted_iota(jnp.int32, (C, C), 0)
    ci = lax.broadcasted_iota(jnp.int32, (C, C), 1)
    tri = ri >= ci
    strict = ri > ci
    tril = jnp.where(tri, 1.0, 0.0).astype(f32)
    G_col = dot(tril, g_all)
    G_row = lax.dot_general(g_all, tril, (((0,), (1,)), ((), ())), precision=HIGHEST,
                            preferred_element_type=f32)
    eye = jnp.where(ri == ci, 1.0, 0.0).astype(f32)

    heads = range(N_HEADS_B)
    dot_t = lambda a, b: lax.dot_general(a, b, dn_last, precision=HIGHEST, preferred_element_type=f32)
    Gc = [G_col[:, N_HEADS_B + h:N_HEADS_B + h + 1] for h in heads]
    Gr = [G_row[N_HEADS_B + h:N_HEADS_B + h + 1, :] for h in heads]
    decay = [jnp.where(tri, jnp.exp(jnp.minimum(Gc[h] - Gr[h], 0.0)), 0.0) for h in heads]
    beta = [_sigmoid(ba[:, h:h + 1]) for h in heads]
    unit = lambda a: a * lax.rsqrt(jnp.sum(a * a, axis=-1, keepdims=True) + EPS)
    q = [unit(c[:, h * hd:(h + 1) * hd]) * (hd ** -0.5) for h in heads]
    k = [unit(c[:, WIDTH_B + h * hd:WIDTH_B + (h + 1) * hd]) for h in heads]
    kb = [k[h] * beta[h] for h in heads]
    eG = [jnp.exp(Gc[h]) for h in heads]
    rhs = [jnp.concatenate([c[:, 2 * WIDTH_B + h * hd:2 * WIDTH_B + (h + 1) * hd] * beta[h], kb[h] * eG[h]], axis=1)
           for h in heads]
    A = [jnp.where(strict, dot_t(kb[h], k[h]) * decay[h], 0.0) for h in heads]
    tm = [eye - A[h] for h in heads]
    pw = A
    for _ in range(C.bit_length() - 2):
        pw = [dot(pw[h], pw[h]) for h in heads]
        tm = [tm[h] + dot(tm[h], pw[h]) for h in heads]
    sol = [dot(tm[h], rhs[h]) for h in heads]
    attn = [dot_t(q[h], k[h]) * decay[h] for h in heads]
    S = [s_ref[h] for h in heads]
    v_new = [sol[h][:, :hd] - dot(sol[h][:, hd:], S[h]) for h in heads]
    o = [dot(q[h] * eG[h], S[h]) + dot(attn[h], v_new[h]) for h in heads]
    for h in heads:
        GL = Gc[h][C - 1:C, :]
        s_ref[h] = S[h] * jnp.exp(GL) + lax.dot_general(k[h] * jnp.exp(GL - Gc[h]), v_new[h], dn_first,
                                                        precision=HIGHEST, preferred_element_type=f32)
        gate = gate_ref[0, :, h * hd:(h + 1) * hd]
        on = o[h] * lax.rsqrt(jnp.mean(o[h] * o[h], axis=-1, keepdims=True) + EPS) * gn_ref[...]
        ob_ref[0, :, h * hd:(h + 1) * hd] = on * (gate * _sigmoid(gate))

    @pl.when(t == pl.num_programs(1) - 1)
    def _():
        sout_ref[0] = s_ref[...]
        tail_ref[0] = cur[C - tail:, :]


GDN_TAIL = 8
GDN_CHUNK = 128


def gdn_mixer(z3, blocks, conv_hist, delta0, conv_w, a_log, dt_bias, gdn_norm, chunk, pad_rows):
    B, Lx, _ = z3.shape
    f32 = jnp.float32
    wq = 3 * WIDTH_B
    hist = jnp.pad(conv_hist, ((0, 0), (GDN_TAIL - (CONV_W - 1), 0), (0, 0)))
    lane_vec = lambda v: jnp.zeros((1, LANE), f32).at[0, N_HEADS_B:2 * N_HEADS_B].set(v)
    const = lambda shape: pl.BlockSpec(shape, lambda b, t: (0,) * len(shape))
    per_b = lambda shape: pl.BlockSpec((1,) + shape, lambda b, t: (b,) + (0,) * len(shape))
    qb, gb, bb = blocks
    return pl.pallas_call(
        functools.partial(_gdn_kernel, pad_rows=pad_rows),
        grid=(B, Lx // chunk),
        in_specs=[pl.BlockSpec((1, chunk, wq), lambda b, t: (b, t, qb)),
                  pl.BlockSpec((1, chunk, WIDTH_B), lambda b, t: (b, t, gb)),
                  pl.BlockSpec((1, chunk, LANE), lambda b, t: (b, t, bb)),
                  per_b((GDN_TAIL, wq)), per_b((N_HEADS_B, HEAD_DIM_B, HEAD_DIM_B)),
                  const((CONV_W, wq)), const((1, LANE)), const((1, LANE)), const((1, HEAD_DIM_B))],
        out_specs=[pl.BlockSpec((1, chunk, WIDTH_B), lambda b, t: (b, t, 0)),
                   per_b((N_HEADS_B, HEAD_DIM_B, HEAD_DIM_B)), per_b((GDN_TAIL, wq))],
        out_shape=[jax.ShapeDtypeStruct((B, Lx, WIDTH_B), f32),
                   jax.ShapeDtypeStruct((B, N_HEADS_B, HEAD_DIM_B, HEAD_DIM_B), f32),
                   jax.ShapeDtypeStruct((B, GDN_TAIL, wq), f32)],
        scratch_shapes=[pltpu.VMEM((N_HEADS_B, HEAD_DIM_B, HEAD_DIM_B), f32), pltpu.VMEM((GDN_TAIL, wq), f32)],
        compiler_params=pltpu.CompilerParams(dimension_semantics=("parallel", "arbitrary")),
        name="gdn_mixer",
    )(z3, z3, z3, hist, delta0, conv_w, lane_vec(a_log), lane_vec(dt_bias), gdn_norm.reshape(1, HEAD_DIM_B))


def _merge_kernel(x_ref, oa_ref, ob_ref, oc_ref, g0_ref, g1_ref, g2_ref, wa_ref, wb_ref, wc_ref, wo_ref, o_ref,
                  *, rows_per_seq, pad_rows):
    f32 = jnp.float32
    bf16 = jnp.bfloat16
    mm = lambda a, w_ref: jnp.dot(a.astype(bf16), w_ref[...], preferred_element_type=f32)
    merged = (_sigmoid(g0_ref[...]) * mm(oa_ref[...], wa_ref) + _sigmoid(g1_ref[...]) * mm(ob_ref[...], wb_ref)
              + _sigmoid(g2_ref[...]) * mm(oc_ref[...], wc_ref))
    x = x_ref[...] + mm(merged, wo_ref)
    if pad_rows:
        tm = x.shape[0]
        row = pl.program_id(0) * tm + lax.broadcasted_iota(jnp.int32, x.shape, 0)
        x = jnp.where(row % rows_per_seq >= pad_rows, x, 0.0)
    o_ref[...] = x


def merge_residual(x, oA, oB, oC, z, gate_block, wa, wb, wc, wo, rows_per_seq, pad_rows):
    T = x.shape[0]
    tm = _token_tile(T)
    bf16 = jnp.bfloat16
    rows = lambda w, j=0: pl.BlockSpec((tm, w), lambda i, j=j: (i, j))
    const = lambda a: pl.BlockSpec(a.shape, lambda i: (0, 0))
    ws = [w.astype(bf16) for w in (wa, wb, wc, wo)]
    return pl.pallas_call(
        functools.partial(_merge_kernel, rows_per_seq=rows_per_seq, pad_rows=pad_rows),
        grid=(T // tm,),
        in_specs=[rows(D_MODEL), rows(WIDTH_A), rows(WIDTH_B), rows(WIDTH_C),
                  rows(D_MODEL, gate_block), rows(D_MODEL, gate_block + 1), rows(D_MODEL, gate_block + 2)]
                 + [const(w) for w in ws],
        out_specs=rows(D_MODEL),
        out_shape=jax.ShapeDtypeStruct((T, D_MODEL), jnp.float32),
        compiler_params=pltpu.CompilerParams(dimension_semantics=("parallel",), vmem_limit_bytes=VMEM_LIMIT),
        name="merge_residual",
    )(x, oA, oB, oC, z, z, z, *ws)


N_HC = 2 * PEER_HEADS
HALF_Q = PEER_QDIM // 2
PEER_EXPERT_TILE = 8 * N_KEYS
CAND_SUB = 8


def _extract_top16(s, iota_f):
    n = float(s.shape[0])
    rank = jnp.full(s.shape, PEER_TOPK, jnp.int32)
    vals = []
    for j in range(PEER_TOPK):
        m = jnp.max(s, axis=0, keepdims=True)
        idx = jnp.min(jnp.where(s == m, iota_f, n), axis=0, keepdims=True)
        hit = iota_f == idx
        rank = jnp.where(hit, j, rank)
        s = jnp.where(hit, -jnp.inf, s)
        vals.append(m)
    return vals, rank


def _peer_router_kernel(x_ref, g_ref, wqt_ref, sk_ref, h2_ref, lim_ref, rk2_ref, e1_ref, e2_ref,
                        qt_ref, s_ref, rank_ref, v_ref):
    f32 = jnp.float32
    tl = x_ref.shape[0]
    x = x_ref[...]
    h2 = x * lax.rsqrt(jnp.mean(x * x, axis=-1, keepdims=True) + EPS) * g_ref[...]
    h2b = h2.astype(jnp.bfloat16)
    h2_ref[...] = h2b
    qt_ref[...] = lax.dot_general(wqt_ref[...], h2b, (((1,), (1,)), ((), ())),
                                  preferred_element_type=f32).astype(jnp.bfloat16)

    def score_body(hc, carry):
        r0 = pl.multiple_of(hc * HALF_Q, HALF_Q)
        s_ref[hc] = jnp.dot(sk_ref[hc], qt_ref[pl.ds(r0, HALF_Q), :], preferred_element_type=f32)
        return carry

    lax.fori_loop(0, N_HC, score_body, 0)

    key_iota = lax.broadcasted_iota(jnp.int32, (N_KEYS, LANE), 0).astype(f32)

    def key_body(i, carry):
        hc = i // (tl // LANE)
        c0 = pl.multiple_of((i % (tl // LANE)) * LANE, LANE)
        s = s_ref[hc, :, pl.ds(c0, LANE)]
        vals, rank = _extract_top16(s, key_iota)
        rank_ref[hc, :, pl.ds(c0, LANE)] = rank
        v_ref[hc, :, pl.ds(c0, LANE)] = jnp.concatenate(vals, axis=0)
        return carry

    lax.fori_loop(0, N_HC * (tl // LANE), key_body, 0, unroll=2)

    sub = CAND_SUB
    n_mid = sub - 1
    n_cand = PEER_TOPK + n_mid * sub + (PEER_TOPK - sub)
    cand_iota = lax.broadcasted_iota(jnp.int32, (n_cand, LANE), 0).astype(f32)
    row8 = lax.broadcasted_iota(jnp.int32, (sub, LANE), 0)

    def head_body(i, carry):
        h = i // (tl // LANE)
        c0 = pl.multiple_of((i % (tl // LANE)) * LANE, LANE)
        cols = pl.ds(c0, LANE)
        v1 = v_ref[2 * h, :, cols]
        v2 = v_ref[2 * h + 1, :, cols]
        blocks = [v1[0:1, :] + v2]
        for r1 in range(1, sub):
            blocks.append(jnp.where(row8 < PEER_TOPK // (r1 + 1), v1[r1:r1 + 1, :] + v2[:sub, :], -jnp.inf))
        blocks.append(v1[sub:, :] + v2[0:1, :])
        best, crank = _extract_top16(jnp.concatenate(blocks, axis=0), cand_iota)
        z = jnp.zeros((1, LANE), f32)
        for k in range(PEER_TOPK):
            z = z + jnp.exp(best[k] - best[0])
        sel = jnp.where(crank < PEER_TOPK, 1.0, 0.0)
        rank1 = rank_ref[2 * h, :, cols]
        lim = jnp.zeros((N_KEYS, LANE), jnp.int32)
        for r1 in range(PEER_TOPK):
            if r1 == 0:
                cnt = jnp.sum(sel[:PEER_TOPK, :], axis=0, keepdims=True)
            elif r1 < sub:
                lo = PEER_TOPK + (r1 - 1) * sub
                cnt = jnp.sum(sel[lo:lo + sub, :], axis=0, keepdims=True)
            else:
                lo = PEER_TOPK + n_mid * sub + (r1 - sub)
                cnt = sel[lo:lo + 1, :]
            lim = jnp.where(rank1 == r1, cnt.astype(jnp.int32), lim)
        lim_ref[h, :, cols] = lim
        rk2_ref[h, :, cols] = rank_ref[2 * h + 1, :, cols]
        e1_ref[h, :, cols] = jnp.exp(s_ref[2 * h, :, cols] - v1[0:1, :])
        e2_ref[h, :, cols] = jnp.exp(s_ref[2 * h + 1, :, cols] - v2[0:1, :]) / z
        return carry

    lax.fori_loop(0, PEER_HEADS * (tl // LANE), head_body, 0, unroll=2)


def _gelu_tanh(x):
    return 0.5 * x * (1.0 + jnp.tanh(math.sqrt(2.0 / math.pi) * (x + 0.044715 * (x * x * x))))


def _peer_expert_kernel(x_ref, h2_ref, pu_ref, pvt_ref, lim_ref, rk2_ref, e1_ref, e2_ref, o_ref,
                        acc_ref, a_cur_ref, a_nxt_ref, c_ref):
    f32 = jnp.float32
    e_step = pl.program_id(1)
    tl = h2_ref.shape[0]
    n_i1 = pu_ref.shape[0] // N_KEYS
    chunk = 2 * LANE if tl % (2 * LANE) == 0 else LANE

    @pl.when(e_step == 0)
    def _():
        acc_ref[...] = jnp.zeros_like(acc_ref)
        a_nxt_ref[...] = jnp.zeros_like(a_nxt_ref)

    a_cur_ref[...] = a_nxt_ref[...]
    a_nxt_ref[...] = lax.dot_general(pu_ref[...], h2_ref[...], (((1,), (1,)), ((), ())),
                                     preferred_element_type=f32)
    for c0 in range(0, tl, chunk):
        for l0 in range(c0, c0 + chunk, LANE):
            cols = slice(l0, l0 + LANE)
            for ii in range(n_i1):
                rows = slice(ii * N_KEYS, (ii + 1) * N_KEYS)
                w = jnp.zeros((N_KEYS, LANE), f32)
                for h in range(PEER_HEADS):
                    sel = rk2_ref[h, :, cols] < lim_ref[h, ii:ii + 1, cols]
                    w = w + jnp.where(sel, e2_ref[h, :, cols], 0.0) * e1_ref[h, ii:ii + 1, cols]
                c_ref[rows, cols] = (w * _gelu_tanh(a_cur_ref[rows, cols])).astype(jnp.bfloat16)
        cc = slice(c0, c0 + chunk)
        acc_ref[:, cc] += jnp.dot(pvt_ref[...], c_ref[:, cc], preferred_element_type=f32)

    @pl.when(e_step == pl.num_programs(1) - 1)
    def _():
        o_ref[...] = x_ref[...] + acc_ref[...].T


def _token_tile(T):
    for tl in (640, 512, 384, 256, 128):
        if T % tl == 0:
            return tl
    raise ValueError(f"token count {T} is not a multiple of {LANE}")


def peer_residual(x, g, wqt, sk, pu, pvt):
    T = x.shape[0]
    tl = _token_tile(T)
    nt = T // tl
    n_exp = pu.shape[0]
    head_shape = jax.ShapeDtypeStruct((PEER_HEADS, N_KEYS, T), jnp.int32)
    head_shape_f = jax.ShapeDtypeStruct((PEER_HEADS, N_KEYS, T), jnp.float32)
    head_spec = pl.BlockSpec((PEER_HEADS, N_KEYS, tl), lambda i: (0, 0, i))
    h2, lim, rk2, e1, e2 = pl.pallas_call(
        _peer_router_kernel,
        grid=(nt,),
        in_specs=[pl.BlockSpec((tl, D_MODEL), lambda i: (i, 0)),
                  pl.BlockSpec((1, D_MODEL), lambda i: (0, 0)),
                  pl.BlockSpec((PEER_HEADS * PEER_QDIM, D_MODEL), lambda i: (0, 0)),
                  pl.BlockSpec((N_HC, N_KEYS, HALF_Q), lambda i: (0, 0, 0))],
        out_specs=[pl.BlockSpec((tl, D_MODEL), lambda i: (i, 0)), head_spec, head_spec, head_spec, head_spec],
        out_shape=[jax.ShapeDtypeStruct((T, D_MODEL), jnp.bfloat16), head_shape, head_shape,
                   head_shape_f, head_shape_f],
        scratch_shapes=[pltpu.VMEM((PEER_HEADS * PEER_QDIM, tl), jnp.bfloat16),
                        pltpu.VMEM((N_HC, N_KEYS, tl), jnp.float32),
                        pltpu.VMEM((N_HC, N_KEYS, tl), jnp.int32),
                        pltpu.VMEM((N_HC, PEER_TOPK, tl), jnp.float32)],
        compiler_params=pltpu.CompilerParams(dimension_semantics=("parallel",), vmem_limit_bytes=VMEM_LIMIT),
        name="peer_router",
    )(x, g.reshape(1, D_MODEL), wqt, sk)

    te = PEER_EXPERT_TILE
    n_tiles = n_exp // te
    nxt = lambda e: jnp.minimum(e, n_tiles - 1)
    cur = lambda e: jnp.maximum(e - 1, 0)
    i2_spec = pl.BlockSpec((PEER_HEADS, N_KEYS, tl), lambda i, e: (0, 0, i))
    i1_spec = pl.BlockSpec((PEER_HEADS, te // N_KEYS, tl), lambda i, e: (0, cur(e), i))
    return pl.pallas_call(
        _peer_expert_kernel,
        grid=(nt, n_tiles + 1),
        in_specs=[pl.BlockSpec((tl, D_MODEL), lambda i, e: (i, 0)),
                  pl.BlockSpec((tl, D_MODEL), lambda i, e: (i, 0)),
                  pl.BlockSpec((te, D_MODEL), lambda i, e: (nxt(e), 0)),
                  pl.BlockSpec((D_MODEL, te), lambda i, e: (0, cur(e))),
                  i1_spec, i2_spec, i1_spec, i2_spec],
        out_specs=pl.BlockSpec((tl, D_MODEL), lambda i, e: (i, 0)),
        out_shape=jax.ShapeDtypeStruct((T, D_MODEL), jnp.float32),
        scratch_shapes=[pltpu.VMEM((D_MODEL, tl), jnp.float32),
                        pltpu.VMEM((te, tl), jnp.float32),
                        pltpu.VMEM((te, tl), jnp.float32),
                        pltpu.VMEM((te, tl), jnp.bfloat16)],
        compiler_params=pltpu.CompilerParams(dimension_semantics=("parallel", "arbitrary"),
                                             vmem_limit_bytes=VMEM_LIMIT),
        name="peer_experts",
    )(x, h2, pu, pvt, lim, rk2, e1, e2)


C_QKVB = 3 * WIDTH_A
C_GB = C_QKVB + 3 * WIDTH_B
C_UC = C_GB + WIDTH_B
C_GATE = C_UC + WIDTH_C
C_BA = C_GATE + 3 * D_MODEL
D_IN2 = C_BA + LANE
S5_BLOCK = 256


def _reorder_w_in(w):
    cols = [w[:, OFF_QA:OFF_BETA], w[:, OFF_UC:OFF_GATE], w[:, OFF_GATE:], w[:, OFF_BETA:OFF_UC],
            jnp.zeros((w.shape[0], LANE - 2 * N_HEADS_B), w.dtype)]
    return jnp.concatenate(cols, axis=1).astype(jnp.bfloat16)


def trunk_layer(x, lp, layer_idx, att, prompt, k_past, v_past, conv_hist, delta0, ssm0_re, ssm0_im):
    f32 = jnp.float32
    B, Lx = x.shape[0], x.shape[1]
    T = B * Lx
    z = norm_matmul(x.reshape(T, D_MODEL), lp['norm1'], lp['w_in2'])

    qn, kn, knb, vb = qk_prep(z, lp['q_norm'], lp['k_norm'], _token_tile(T))
    lam_init = 0.8 - 0.6 * math.exp(-0.3 * layer_idx)
    lqk = lp['lambda_qk']
    lam = jnp.exp(jnp.sum(lqk[0] * lqk[1])) - jnp.exp(jnp.sum(lqk[2] * lqk[3])) + lam_init
    scalars = jnp.stack([lam, jnp.asarray(1.0 - lam_init, f32)]).astype(f32)
    r3 = lambda a: a.reshape(B, Lx, WIDTH_A)
    if prompt:
        oA = attn_prompt(r3(qn), r3(knb), vb.reshape(B, Lx, 2 * WIDTH_A), scalars, att['far'], att['tiles'], lp['subln'])
    else:
        P = k_past.shape[1]
        oA = attn_sample(r3(qn), r3(knb), vb.reshape(B, Lx, 2 * WIDTH_A), k_past.reshape(B, P, WIDTH_A), v_past.reshape(B, P, WIDTH_A),
                         scalars, att['bias_past'], att['bias_new'], lp['subln'])

    z3 = z.reshape(B, Lx, D_IN2)
    off = PAD if prompt else 0
    L = Lx - off
    kA = r3(kn)[:, off:].reshape(B, L, N_HEADS_A, 2 * HEAD_DIM_A)
    vA = z3[:, off:, 2 * WIDTH_A:3 * WIDTH_A].reshape(B, L, N_HEADS_A, 2 * HEAD_DIM_A)

    oB, delta, tail = gdn_mixer(z3, (C_QKVB // (3 * WIDTH_B), C_GB // WIDTH_B, C_BA // LANE), conv_hist, delta0,
                                lp['conv_w'], lp['a_log'], lp['dt_bias'], lp['gdn_norm'],
                                GDN_CHUNK if prompt else Lx, off)
    conv_state = tail[:, GDN_TAIL - (CONV_W - 1):]

    oC, ssm_re, ssm_im = s5_glu(z3, C_UC // WIDTH_C, ssm0_re, ssm0_im, lp['s5'], lp['w_glu'],
                                S5_BLOCK if prompt else Lx)

    x2 = merge_residual(x.reshape(T, D_MODEL), oA.reshape(T, WIDTH_A), oB.reshape(T, WIDTH_B),
                        oC.reshape(T, WIDTH_C), z, C_GATE // D_MODEL, lp['wb_a'], lp['wb_b'], lp['wb_c'],
                        lp['w_out'], Lx, off)

    x = peer_residual(x2, lp['norm2'], lp['peer_wqt'], lp['peer_sk'], lp['peer_pu'],
                      lp['peer_pvt']).reshape(B, Lx, D_MODEL)
    return (x, kA, vA, conv_state, delta, ssm_re, ssm_im)


def kernel(x_prompt, x_sample, cache_k, cache_v, state_conv, state_delta, state_ssm_re, state_ssm_im,
           meta_tokens, rel_bias, norm1_g, norm2_g, final_norm_g, w_in, q_norm_g, k_norm_g, lambda_qk,
           subln_g, conv_w, gdn_a_log, gdn_dt_bias, gdn_norm_g, s5_a_re, s5_a_im, s5_b_re, s5_b_im,
           s5_c_re, s5_c_im, s5_d, s5_log_dt, w_glu, w_branch_a, w_branch_b, w_branch_c, w_out,
           peer_wq, peer_subkeys, peer_u, peer_v):
    f32 = jnp.float32
    bf16 = jnp.bfloat16
    params = [dict(norm1=norm1_g[l], norm2=norm2_g[l], w_in2=_reorder_w_in(w_in[l]), q_norm=q_norm_g[l],
                   k_norm=k_norm_g[l], lambda_qk=lambda_qk[l], subln=subln_g[l], conv_w=conv_w[l],
                   a_log=gdn_a_log[l], dt_bias=gdn_dt_bias[l], gdn_norm=gdn_norm_g[l],
                   s5=s5_params(s5_a_re[l], s5_a_im[l], s5_b_re[l], s5_b_im[l], s5_c_re[l], s5_c_im[l], s5_d[l],
                                s5_log_dt[l]),
                   w_glu=w_glu[l], wb_a=w_branch_a[l], wb_b=w_branch_b[l], wb_c=w_branch_c[l], w_out=w_out[l],
                   peer_wqt=peer_wq[l].T.astype(bf16),
                   peer_sk=peer_subkeys[l].reshape(N_HC, N_KEYS, HALF_Q).astype(bf16),
                   peer_pu=peer_u[l].astype(bf16), peer_pvt=peer_v[l].T.astype(bf16))
              for l in range(DEPTH)]

    qpos = jnp.arange(ATT_TQ, dtype=jnp.int32)
    tiles = jnp.stack([t5_bias(qpos, d * ATT_TK + jnp.arange(ATT_TK, dtype=jnp.int32), rel_bias)
                       for d in (-2, -1, 0)], axis=1)
    far = t5_bias(jnp.full((1,), 2 * ATT_TK, jnp.int32), jnp.zeros((1,), jnp.int32), rel_bias).reshape(N_HEADS_A)
    P, Ls = cache_k.shape[2], x_sample.shape[1]
    bias_s = t5_bias(P + jnp.arange(Ls, dtype=jnp.int32), jnp.arange(P + Ls, dtype=jnp.int32), rel_bias)
    att_p = dict(tiles=tiles, far=far)
    att_s = dict(bias_past=bias_s[:, :, :P], bias_new=bias_s[:, :, P:])

    B = x_prompt.shape[0]
    xp = jnp.concatenate([jnp.zeros((B, PAD, D_MODEL), f32),
                          jnp.broadcast_to(meta_tokens[None], (B, N_META, D_MODEL)), x_prompt], axis=1)
    outs_p = [[] for _ in range(6)]
    for l in range(DEPTH):
        res = trunk_layer(
            xp, params[l], l, att_p, True, None, None,
            jnp.zeros((B, CONV_W - 1, 3 * WIDTH_B), f32),
            jnp.zeros((B, N_HEADS_B, HEAD_DIM_B, HEAD_DIM_B), f32),
            jnp.zeros((B, S5_GROUPS, S5_STATE), f32), jnp.zeros((B, S5_GROUPS, S5_STATE), f32))
        xp = res[0]
        for acc, r in zip(outs_p, res[1:]):
            acc.append(r)
    y_prompt = rmsnorm(xp, final_norm_g)[:, FRONT:]

    xs = x_sample
    outs_s = [[] for _ in range(6)]
    for l in range(DEPTH):
        res = trunk_layer(
            xs, params[l], l, att_s, False, cache_k[l], cache_v[l], state_conv[l],
            state_delta[l], state_ssm_re[l], state_ssm_im[l])
        xs = res[0]
        for acc, r in zip(outs_s, res[1:]):
            acc.append(r)
    y_sample = rmsnorm(xs, final_norm_g)

    kp, vp, cp, dp, srp, sip = [jnp.stack(a) for a in outs_p]
    ks_, vs_, cs_, ds_, srs, sis = [jnp.stack(a) for a in outs_s]
    return (y_prompt, y_sample, kp, vp, ks_, vs_, cp, cs_, dp, ds_, srp, sip, srs, sis)
```

```python
import functools
import math

import jax
import jax.numpy as jnp
from jax import lax
from jax.experimental import pallas as pl
from jax.experimental.pallas import tpu as pltpu

D_MODEL = 1024
DEPTH = 4
CHUNK = 64
N_META = 16
EPS = 1e-6
NEG_INF = -1e30
N_HEADS_A = 4
HEAD_DIM_A = 64
NUM_BUCKETS = 32
MAX_DISTANCE = 128
N_HEADS_B = 4
HEAD_DIM_B = 128
CONV_W = 4
S5_GROUP = 16
S5_GROUPS = 32
S5_STATE = 64
PEER_HEADS = 8
PEER_QDIM = 256
N_KEYS = 128
PEER_TOPK = 16

WIDTH_A = N_HEADS_A * 2 * HEAD_DIM_A
WIDTH_B = N_HEADS_B * HEAD_DIM_B
WIDTH_C = S5_GROUPS * S5_GROUP
OFF_QA = 0
OFF_KA = OFF_QA + WIDTH_A
OFF_VA = OFF_KA + WIDTH_A
OFF_QKVB = OFF_VA + WIDTH_A
OFF_GB = OFF_QKVB + 3 * WIDTH_B
OFF_BETA = OFF_GB + WIDTH_B
OFF_ALPHA = OFF_BETA + N_HEADS_B
OFF_UC = OFF_ALPHA + N_HEADS_B
OFF_GATE = OFF_UC + WIDTH_C
D_IN = OFF_GATE + 3 * D_MODEL

LANE = 128
VMEM_LIMIT = 56 * 1024 * 1024
WEIGHT_TILE_BYTES = 6 * 1024 * 1024
HIGHEST = lax.Precision.HIGHEST


def _norm_mm_kernel(x_ref, g_ref, w_ref, o_ref, h_ref):
    @pl.when(pl.program_id(1) == 0)
    def _():
        x = x_ref[...]
        h = x * lax.rsqrt(jnp.mean(x * x, axis=-1, keepdims=True) + EPS) * g_ref[...]
        h_ref[...] = h.astype(jnp.bfloat16)

    o_ref[...] = jnp.dot(h_ref[...], w_ref[...], preferred_element_type=jnp.float32)


def norm_matmul(x, g, wb, tm=512):
    M, K = x.shape
    N = wb.shape[1]
    n_lanes = N // LANE
    tn = LANE * max(d for d in range(1, n_lanes + 1) if n_lanes % d == 0 and d * LANE * K * 2 <= WEIGHT_TILE_BYTES)
    tm = min(tm, M)
    return pl.pallas_call(
        _norm_mm_kernel,
        grid=(pl.cdiv(M, tm), N // tn),
        in_specs=[pl.BlockSpec((tm, K), lambda i, j: (i, 0)),
                  pl.BlockSpec((1, K), lambda i, j: (0, 0)),
                  pl.BlockSpec((K, tn), lambda i, j: (0, j))],
        out_specs=pl.BlockSpec((tm, tn), lambda i, j: (i, j)),
        out_shape=jax.ShapeDtypeStruct((M, N), jnp.float32),
        scratch_shapes=[pltpu.VMEM((tm, K), jnp.bfloat16)],
        compiler_params=pltpu.CompilerParams(dimension_semantics=("parallel", "arbitrary"),
                                             vmem_limit_bytes=VMEM_LIMIT),
        name="norm_proj",
    )(x, g.reshape(1, K), wb)


def rmsnorm(x, g):
    xf = x.astype(jnp.float32)
    y = xf * lax.rsqrt(jnp.mean(xf * xf, axis=-1, keepdims=True) + EPS)
    return (y * g.astype(jnp.float32)).astype(x.dtype)


def t5_bias(q_pos, k_pos, table):
    rel = k_pos[None, :] - q_pos[:, None]
    half = NUM_BUCKETS // 2
    exact = half // 2
    n = jnp.abs(rel)
    nf = jnp.maximum(n, 1).astype(jnp.float32)
    far = exact + (jnp.log(nf / exact) / math.log(MAX_DISTANCE / exact) * (half - exact)).astype(jnp.int32)
    bucket = jnp.where(rel > 0, half, 0) + jnp.where(n < exact, n, jnp.minimum(far, half - 1))
    return jnp.moveaxis(table[bucket].astype(jnp.float32), -1, 0)


FRONT = 256
PAD = FRONT - N_META
ATT_TQ = 256
ATT_TK = 256


def _qk_prep_kernel(q_ref, k_ref, v_ref, gq_ref, gk_ref, seg_ref, qn_ref, kn_ref, knb_ref, vb_ref):
    f32 = jnp.float32
    seg = seg_ref[...]

    def norm(x, g):
        ms = jnp.dot(x * x, seg, precision=HIGHEST, preferred_element_type=f32)
        return x * lax.rsqrt(ms + EPS) * g

    qn_ref[...] = (norm(q_ref[...], gq_ref[...]) * (HEAD_DIM_A ** -0.5)).astype(jnp.bfloat16)
    kn = norm(k_ref[...], gk_ref[...])
    kn_ref[...] = kn
    knb_ref[...] = kn.astype(jnp.bfloat16)
    v = v_ref[...].astype(jnp.bfloat16)
    hd = 2 * HEAD_DIM_A
    lane = lax.broadcasted_iota(jnp.int32, (v.shape[0], hd), 1)
    ones_col = jnp.where(lane == 0, 1.0, 0.0).astype(jnp.bfloat16)
    vb_ref[...] = jnp.concatenate(
        [piece for h in range(N_HEADS_A) for piece in (v[:, h * hd:(h + 1) * hd], ones_col)], axis=1)


def qk_prep(z, gq, gk, tm):
    T = z.shape[0]
    seg = jnp.kron(jnp.eye(WIDTH_A // HEAD_DIM_A, dtype=jnp.float32),
                   jnp.full((HEAD_DIM_A, HEAD_DIM_A), 1.0 / HEAD_DIM_A, jnp.float32))
    row = lambda j: pl.BlockSpec((tm, WIDTH_A), lambda i, j=j: (i, j))
    const = lambda shape: pl.BlockSpec(shape, lambda i: (0,) * len(shape))
    out = lambda: pl.BlockSpec((tm, WIDTH_A), lambda i: (i, 0))
    bshape = jax.ShapeDtypeStruct((T, WIDTH_A), jnp.bfloat16)
    return pl.pallas_call(
        _qk_prep_kernel,
        grid=(T // tm,),
        in_specs=[row(0), row(1), row(2), const((1, WIDTH_A)), const((1, WIDTH_A)), const((WIDTH_A, WIDTH_A))],
        out_specs=[out(), out(), out(), pl.BlockSpec((tm, 2 * WIDTH_A), lambda i: (i, 0))],
        out_shape=[bshape, jax.ShapeDtypeStruct((T, WIDTH_A), jnp.float32), bshape,
                   jax.ShapeDtypeStruct((T, 2 * WIDTH_A), jnp.bfloat16)],
        compiler_params=pltpu.CompilerParams(dimension_semantics=("parallel",)),
        name="qk_prep",
    )(z, z, z, jnp.tile(gq, WIDTH_A // HEAD_DIM_A).reshape(1, WIDTH_A),
      jnp.tile(gk, WIDTH_A // HEAD_DIM_A).reshape(1, WIDTH_A), seg)


def _split_maps(q):
    lane = lax.broadcasted_iota(jnp.int32, q.shape, 1)
    zero = jnp.zeros_like(q)
    return jnp.where(lane < HEAD_DIM_A, q, zero), jnp.where(lane >= HEAD_DIM_A, q, zero)


def _subln(o, g, scale):
    return o * lax.rsqrt(jnp.mean(o * o, axis=-1, keepdims=True) + EPS) * g * scale


def _attn_prompt_kernel(sc_ref, far_ref, q_ref, k_ref, v_ref, bias_ref, g_ref, o_ref, m_ref, acc_ref):
    f32 = jnp.float32
    h = pl.program_id(1)
    qi = pl.program_id(2)
    tq, tk = ATT_TQ, ATT_TK
    hd = 2 * HEAD_DIM_A
    qs = jnp.concatenate(_split_maps(q_ref[0]), axis=0)
    m_ref[...] = jnp.full(m_ref.shape, NEG_INF, f32)
    acc_ref[...] = jnp.zeros(acc_ref.shape, f32)
    far = far_ref[h]

    def tile(k0, width, general):
        k0 = pl.multiple_of(k0, tk)
        kt = k_ref[0, pl.ds(k0, width), :]
        vt = v_ref[0, pl.ds(k0, width), :]
        s = lax.dot_general(qs, kt, (((1,), (1,)), ((), ())), preferred_element_type=f32)
        if general:
            kj = k0 // tk
            bias = bias_ref[0, jnp.clip(kj - qi + 2, 0, 2)]
            qpos = qi * tq + lax.broadcasted_iota(jnp.int32, (tq, tk), 0)
            kpos = k0 + lax.broadcasted_iota(jnp.int32, (tq, tk), 1)
            qchunk = jnp.where(qpos < FRONT, 0, 1 + jnp.right_shift(qpos - FRONT, 6))
            kchunk = jnp.where(kpos < FRONT, 0, 1 + jnp.right_shift(kpos - FRONT, 6))
            mask = (kpos >= PAD) & (kchunk <= qchunk)
            s = jnp.where(jnp.concatenate([mask, mask], axis=0), s + jnp.concatenate([bias, bias], axis=0), NEG_INF)
        else:
            s = s + far
        n_lane_tiles = width // LANE
        smax = s[:, :LANE]
        for j in range(1, n_lane_tiles):
            smax = jnp.maximum(smax, s[:, j * LANE:(j + 1) * LANE])
        m_old = m_ref[...]
        m_new = jnp.maximum(m_old, jnp.broadcast_to(jnp.max(smax, axis=1, keepdims=True), m_old.shape))
        alpha = jnp.exp(m_old - m_new)
        p = jnp.exp(s - jnp.concatenate([m_new] * n_lane_tiles, axis=1))
        acc_ref[...] = (jnp.concatenate([alpha, alpha], axis=1) * acc_ref[...]
                        + jnp.dot(p.astype(jnp.bfloat16), vt, preferred_element_type=f32))
        m_ref[...] = m_new

    tile(0, tk, True)

    n_far = jnp.maximum(qi - 2, 0)
    odd = n_far % 2

    @pl.when(odd == 1)
    def _():
        tile(tk, tk, False)

    def far_body(j, carry):
        tile((1 + odd + 2 * j) * tk, 2 * tk, False)
        return carry

    lax.fori_loop(0, n_far // 2, far_body, 0)

    @pl.when(qi >= 2)
    def _():
        tile((qi - 1) * tk, tk, True)

    @pl.when(qi >= 1)
    def _():
        tile(qi * tk, tk, True)

    acc = acc_ref[...]
    out = acc[:, :hd] / acc[:, hd:hd + 1]
    o = out[:tq] - sc_ref[0] * out[tq:]
    o_ref[0] = _subln(o, g_ref[...], sc_ref[1])


def attn_prompt(qn, knb, vb, scalars, far, bias_tiles, subln_g):
    B, Lp, _ = qn.shape
    hd = 2 * HEAD_DIM_A
    smem = pl.BlockSpec(memory_space=pltpu.SMEM)
    return pl.pallas_call(
        _attn_prompt_kernel,
        grid=(B, N_HEADS_A, Lp // ATT_TQ),
        in_specs=[smem, smem,
                  pl.BlockSpec((1, ATT_TQ, hd), lambda b, h, i: (b, i, h)),
                  pl.BlockSpec((1, Lp, hd), lambda b, h, i: (b, 0, h)),
                  pl.BlockSpec((1, Lp, 2 * hd), lambda b, h, i: (b, 0, h)),
                  pl.BlockSpec((1, 3, ATT_TQ, ATT_TK), lambda b, h, i: (h, 0, 0, 0)),
                  pl.BlockSpec((1, hd), lambda b, h, i: (0, 0))],
        out_specs=pl.BlockSpec((1, ATT_TQ, hd), lambda b, h, i: (b, i, h)),
        out_shape=jax.ShapeDtypeStruct((B, Lp, WIDTH_A), jnp.float32),
        scratch_shapes=[pltpu.VMEM((2 * ATT_TQ, hd), jnp.float32), pltpu.VMEM((2 * ATT_TQ, 2 * hd), jnp.float32)],
        compiler_params=pltpu.CompilerParams(dimension_semantics=("parallel", "parallel", "parallel"),
                                             vmem_limit_bytes=VMEM_LIMIT),
        name="diff_attn_prompt",
    )(scalars, far, qn, knb, vb, bias_tiles, subln_g.reshape(1, hd))


def _attn_sample_kernel(sc_ref, q_ref, kp_ref, vp_ref, kn_ref, vn_ref, bp_ref, bn_ref, g_ref, o_ref):
    f32 = jnp.float32
    bf16 = jnp.bfloat16
    qs = _split_maps(q_ref[0])
    kp = kp_ref[0].astype(bf16)
    vp = vp_ref[0].astype(bf16)
    kn = kn_ref[0]
    vn = vn_ref[0][:, :2 * HEAD_DIM_A]
    dn = (((1,), (1,)), ((), ()))
    outs = []
    for c in range(2):
        sp = lax.dot_general(qs[c], kp, dn, preferred_element_type=f32) + bp_ref[0]
        sn = lax.dot_general(qs[c], kn, dn, preferred_element_type=f32) + bn_ref[0]
        m = jnp.maximum(jnp.max(sp, axis=1, keepdims=True), jnp.max(sn, axis=1, keepdims=True))
        pp = jnp.exp(sp - m)
        pn = jnp.exp(sn - m)
        l = jnp.sum(pp, axis=1, keepdims=True) + jnp.sum(pn, axis=1, keepdims=True)
        acc = (jnp.dot(pp.astype(bf16), vp, preferred_element_type=f32)
               + jnp.dot(pn.astype(bf16), vn, preferred_element_type=f32))
        outs.append(acc / l)
    o_ref[0] = _subln(outs[0] - sc_ref[0] * outs[1], g_ref[...], sc_ref[1])


def attn_sample(qn, knb, vb, k_past, v_past, scalars, bias_past, bias_new, subln_g):
    B, L, _ = qn.shape
    P = k_past.shape[1]
    hd = 2 * HEAD_DIM_A
    new = pl.BlockSpec((1, L, hd), lambda b, h: (b, 0, h))
    past = pl.BlockSpec((1, P, hd), lambda b, h: (b, 0, h))
    return pl.pallas_call(
        _attn_sample_kernel,
        grid=(B, N_HEADS_A),
        in_specs=[pl.BlockSpec(memory_space=pltpu.SMEM), new, past, past, new,
                  pl.BlockSpec((1, L, 2 * hd), lambda b, h: (b, 0, h)),
                  pl.BlockSpec((1, L, P), lambda b, h: (h, 0, 0)),
                  pl.BlockSpec((1, L, L), lambda b, h: (h, 0, 0)),
                  pl.BlockSpec((1, hd), lambda b, h: (0, 0))],
        out_specs=new,
        out_shape=jax.ShapeDtypeStruct((B, L, WIDTH_A), jnp.float32),
        compiler_params=pltpu.CompilerParams(dimension_semantics=("parallel", "parallel")),
        name="diff_attn_sample",
    )(scalars, qn, k_past, v_past, knb, vb, bias_past, bias_new, subln_g.reshape(1, hd))


S5_N = S5_GROUPS * S5_STATE
S5_SLAB = 8
S5_LANES = 512


def _sigmoid(x):
    return 1.0 / (1.0 + jnp.exp(-x))


def _s5_kernel(u_ref, x0_ref, bmat_ref, cmat_ref, lamp_ref, ppow_ref, d_ref, wglu_ref, oc_ref, xf_ref,
               bu_ref, carry_ref):
    f32 = jnp.float32
    bf16 = jnp.bfloat16
    t = pl.program_id(1)
    tb = u_ref.shape[1]

    @pl.when(t == 0)
    def _():
        carry_ref[...] = x0_ref[0]

    u = u_ref[0]
    bu_ref[...] = jnp.dot(u.astype(bf16), bmat_ref[...], preferred_element_type=f32)

    row = lax.broadcasted_iota(jnp.int32, (S5_SLAB, S5_LANES), 0)
    for c in range(S5_N // S5_LANES):
        re = slice(c * S5_LANES, (c + 1) * S5_LANES)
        im = slice(S5_N + c * S5_LANES, S5_N + (c + 1) * S5_LANES)

        def slab(i, carry):
            cre, cim = carry
            rows = pl.ds(pl.multiple_of(i * S5_SLAB, S5_SLAB), S5_SLAB)
            yre = bu_ref[rows, re]
            yim = bu_ref[rows, im]
            for s in range(3):
                sh = 1 << s
                sre = jnp.where(row >= sh, pltpu.roll(yre, sh, 0), 0.0)
                sim = jnp.where(row >= sh, pltpu.roll(yim, sh, 0), 0.0)
                lr = lamp_ref[s, 0, :, re]
                li = lamp_ref[s, 1, :, re]
                yre, yim = yre + (lr * sre - li * sim), yim + (lr * sim + li * sre)
            pr = ppow_ref[0, :, re]
            pi = ppow_ref[1, :, re]
            yre, yim = yre + (pr * cre - pi * cim), yim + (pr * cim + pi * cre)
            bu_ref[rows, re] = yre
            bu_ref[rows, im] = yim
            last = S5_SLAB - 1
            return (jnp.broadcast_to(yre[last:last + 1, :], yre.shape),
                    jnp.broadcast_to(yim[last:last + 1, :], yim.shape))

        cre, cim = lax.fori_loop(0, tb // S5_SLAB, slab, (carry_ref[:, re], carry_ref[:, im]))
        carry_ref[:, re] = cre
        carry_ref[:, im] = cim

    y = jnp.dot(bu_ref[...].astype(bf16), cmat_ref[...], preferred_element_type=f32) + d_ref[...] * u
    gl = jnp.dot(_gelu_tanh(y).astype(bf16), wglu_ref[...], preferred_element_type=f32)
    oc_ref[0] = gl[:, :WIDTH_C] * _sigmoid(gl[:, WIDTH_C:])

    @pl.when(t == pl.num_programs(1) - 1)
    def _():
        xf_ref[0] = carry_ref[...]


def s5_params(a_re, a_im, b_re, b_im, c_re, c_im, d, log_dt):
    f32 = jnp.float32
    lam = lax.complex(a_re, a_im)
    lam_bar = jnp.exp(lam * jnp.exp(log_dt)[:, None])
    b_bar = ((lam_bar - 1.0) / lam)[..., None] * lax.complex(b_re, b_im)
    eye = jnp.eye(S5_GROUPS, dtype=f32)
    bd_in = lambda m: jnp.einsum('gpi,gh->gihp', m, eye).reshape(WIDTH_C, S5_N)
    bd_out = lambda m: jnp.einsum('gip,gh->gphi', m, eye).reshape(S5_N, WIDTH_C)
    bmat = jnp.concatenate([bd_in(b_bar.real), bd_in(b_bar.imag)], axis=1).astype(jnp.bfloat16)
    cmat = jnp.concatenate([bd_out(c_re), bd_out(-c_im)], axis=0).astype(jnp.bfloat16)
    lb = lam_bar.reshape(S5_N)
    rep = lambda v: jnp.broadcast_to(v[None, :], (S5_SLAB, S5_N))
    pows = [lb, lb * lb, (lb * lb) * (lb * lb)]
    lamp = jnp.stack([jnp.stack([rep(p.real), rep(p.imag)]) for p in pows])
    run = [lb]
    for _ in range(S5_SLAB - 1):
        run.append(run[-1] * lb)
    pp = jnp.stack(run)
    ppow = jnp.stack([pp.real, pp.imag])
    return dict(bmat=bmat, cmat=cmat, lamp=lamp.astype(f32), ppow=ppow.astype(f32), d=d.reshape(1, WIDTH_C))


def s5_glu(z3, col_block, x0_re, x0_im, sp, w_glu, tb):
    B, Lx, _ = z3.shape
    x0 = jnp.concatenate([x0_re.reshape(B, S5_N), x0_im.reshape(B, S5_N)], axis=1)
    x0 = jnp.broadcast_to(x0[:, None, :], (B, S5_SLAB, 2 * S5_N))
    const = lambda a: pl.BlockSpec(a.shape, lambda b, t: (0,) * a.ndim)
    wg = w_glu.astype(jnp.bfloat16)
    oc, xf = pl.pallas_call(
        _s5_kernel,
        grid=(B, Lx // tb),
        in_specs=[pl.BlockSpec((1, tb, WIDTH_C), lambda b, t: (b, t, col_block)),
                  pl.BlockSpec((1, S5_SLAB, 2 * S5_N), lambda b, t: (b, 0, 0)),
                  const(sp['bmat']), const(sp['cmat']), const(sp['lamp']), const(sp['ppow']), const(sp['d']),
                  const(wg)],
        out_specs=[pl.BlockSpec((1, tb, WIDTH_C), lambda b, t: (b, t, 0)),
                   pl.BlockSpec((1, S5_SLAB, 2 * S5_N), lambda b, t: (b, 0, 0))],
        out_shape=[jax.ShapeDtypeStruct((B, Lx, WIDTH_C), jnp.float32),
                   jax.ShapeDtypeStruct((B, S5_SLAB, 2 * S5_N), jnp.float32)],
        scratch_shapes=[pltpu.VMEM((tb, 2 * S5_N), jnp.float32), pltpu.VMEM((S5_SLAB, 2 * S5_N), jnp.float32)],
        compiler_params=pltpu.CompilerParams(dimension_semantics=("parallel", "arbitrary"),
                                             vmem_limit_bytes=VMEM_LIMIT),
        name="s5_glu",
    )(z3, x0, sp['bmat'], sp['cmat'], sp['lamp'], sp['ppow'], sp['d'], wg)
    shape = (B, S5_GROUPS, S5_STATE)
    return oc, xf[:, 0, :S5_N].reshape(shape), xf[:, 0, S5_N:].reshape(shape)


def _gdn_kernel(qkv_ref, gate_ref, ba_ref, hist_ref, s0_ref, cw_ref, av_ref, dt_ref, gn_ref,
                ob_ref, sout_ref, tail_ref, s_ref, prev_ref, *, pad_rows):
    f32 = jnp.float32
    t = pl.program_id(1)
    C = qkv_ref.shape[1]
    hd = HEAD_DIM_B
    dn_last = (((1,), (1,)), ((), ()))
    dn_first = (((0,), (0,)), ((), ()))
    dot = lambda a, b: _dot3(a, b, (((1,), (0,)), ((), ())))

    @pl.when(t == 0)
    def _():
        s_ref[...] = s0_ref[0]
        prev_ref[...] = hist_ref[0]

    cur = qkv_ref[0]
    tail = prev_ref.shape[0]
    ext = jnp.concatenate([prev_ref[...], cur], axis=0)
    w = cw_ref[...]
    conv = cur * w[CONV_W - 1:CONV_W, :]
    for i in range(CONV_W - 1):
        lo = tail - (CONV_W - 1) + i
        conv = conv + ext[lo:lo + C, :] * w[i:i + 1, :]
    prev_ref[...] = cur[C - tail:, :]
    c = conv * _sigmoid(conv)

    ba = ba_ref[0]
    xg = ba + dt_ref[...]
    softplus = jnp.maximum(xg, 0.0) + jnp.log(1.0 + jnp.exp(-jnp.abs(xg)))
    g_all = -jnp.exp(av_ref[...]) * softplus
    if pad_rows:
        grow = t * C + lax.broadcasted_iota(jnp.int32, g_all.shape, 0)
        g_all = jnp.where(grow >= pad_rows, g_all, 0.0)
    ri = lax.broadcasted_iota(jnp.int32, (C, C), 0)
    ci = lax.broadcasted_iota(jnp.int32, (C, C), 1)
    tri = ri >= ci
    strict = ri > ci
    tril = jnp.where(tri, 1.0, 0.0).astype(f32)
    G_col = jnp.dot(tril, g_all, precision=HIGHEST, preferred_element_type=f32)
    G_row = lax.dot_general(g_all, tril, (((0,), (1,)), ((), ())), precision=HIGHEST,
                            preferred_element_type=f32)
    eye = jnp.where(ri == ci, 1.0, 0.0).astype(f32)

    heads = range(N_HEADS_B)
    dot_t = lambda a, b: _dot3(a, b, dn_last)
    Gc = [G_col[:, N_HEADS_B + h:N_HEADS_B + h + 1] for h in heads]
    Gr = [G_row[N_HEADS_B + h:N_HEADS_B + h + 1, :] for h in heads]
    decay = [jnp.where(tri, jnp.exp(jnp.minimum(Gc[h] - Gr[h], 0.0)), 0.0) for h in heads]
    beta = [_sigmoid(ba[:, h:h + 1]) for h in heads]
    unit = lambda a: a * lax.rsqrt(jnp.sum(a * a, axis=-1, keepdims=True) + EPS)
    q = [unit(c[:, h * hd:(h + 1) * hd]) * (hd ** -0.5) for h in heads]
    k = [unit(c[:, WIDTH_B + h * hd:WIDTH_B + (h + 1) * hd]) for h in heads]
    kb = [k[h] * beta[h] for h in heads]
    eG = [jnp.exp(Gc[h]) for h in heads]
    rhs = [jnp.concatenate([c[:, 2 * WIDTH_B + h * hd:2 * WIDTH_B + (h + 1) * hd] * beta[h], kb[h] * eG[h]], axis=1)
           for h in heads]
    A = [jnp.where(strict, dot_t(kb[h], k[h]) * decay[h], 0.0) for h in heads]
    same_block = lambda b: jnp.right_shift(ri, b.bit_length() - 1) == jnp.right_shift(ci, b.bit_length() - 1)
    in_base = same_block(GDN_BASE)
    D = [jnp.where(in_base, A[h], 0.0) for h in heads]
    D2 = [dot(D[h], D[h]) for h in heads]
    D4 = [dot(D2[h], D2[h]) for h in heads]
    tm = [eye - D[h] for h in heads]
    tm = [tm[h] + dot(tm[h], D2[h]) for h in heads]
    tm = [tm[h] + dot(tm[h], D4[h]) for h in heads]
    b = GDN_BASE
    while b < C:
        level = same_block(2 * b) & jnp.logical_not(same_block(b))
        me = [dot(tm[h], jnp.where(level, A[h], 0.0)) for h in heads]
        tm = [tm[h] - dot(me[h], tm[h]) for h in heads]
        b *= 2
    sol = [dot(tm[h], rhs[h]) for h in heads]
    attn = [dot_t(q[h], k[h]) * decay[h] for h in heads]
    S = [s_ref[h] for h in heads]
    v_new = [sol[h][:, :hd] - dot(sol[h][:, hd:], S[h]) for h in heads]
    o = [dot(q[h] * eG[h], S[h]) + dot(attn[h], v_new[h]) for h in heads]
    for h in heads:
        GL = Gc[h][C - 1:C, :]
        s_ref[h] = S[h] * jnp.exp(GL) + _dot3(k[h] * jnp.exp(GL - Gc[h]), v_new[h], dn_first)
        gate = gate_ref[0, :, h * hd:(h + 1) * hd]
        on = o[h] * lax.rsqrt(jnp.mean(o[h] * o[h], axis=-1, keepdims=True) + EPS) * gn_ref[...]
        ob_ref[0, :, h * hd:(h + 1) * hd] = on * (gate * _sigmoid(gate))

    @pl.when(t == pl.num_programs(1) - 1)
    def _():
        sout_ref[0] = s_ref[...]
        tail_ref[0] = cur[C - tail:, :]


GDN_TAIL = 8
GDN_CHUNK = 128
GDN_BASE = 8


def _dot3(a, b, dims):
    f32 = jnp.float32
    bf16 = jnp.bfloat16
    a_hi = a.astype(bf16)
    b_hi = b.astype(bf16)
    a_lo = (a - a_hi.astype(f32)).astype(bf16)
    b_lo = (b - b_hi.astype(f32)).astype(bf16)
    dg = lambda x, y: lax.dot_general(x, y, dims, preferred_element_type=f32)
    return dg(a_hi, b_hi) + (dg(a_hi, b_lo) + dg(a_lo, b_hi))


def gdn_mixer(z3, blocks, conv_hist, delta0, conv_w, a_log, dt_bias, gdn_norm, chunk, pad_rows):
    B, Lx, _ = z3.shape
    f32 = jnp.float32
    wq = 3 * WIDTH_B
    hist = jnp.pad(conv_hist, ((0, 0), (GDN_TAIL - (CONV_W - 1), 0), (0, 0)))
    lane_vec = lambda v: jnp.zeros((1, LANE), f32).at[0, N_HEADS_B:2 * N_HEADS_B].set(v)
    const = lambda shape: pl.BlockSpec(shape, lambda b, t: (0,) * len(shape))
    per_b = lambda shape: pl.BlockSpec((1,) + shape, lambda b, t: (b,) + (0,) * len(shape))
    qb, gb, bb = blocks
    return pl.pallas_call(
        functools.partial(_gdn_kernel, pad_rows=pad_rows),
        grid=(B, Lx // chunk),
        in_specs=[pl.BlockSpec((1, chunk, wq), lambda b, t: (b, t, qb)),
                  pl.BlockSpec((1, chunk, WIDTH_B), lambda b, t: (b, t, gb)),
                  pl.BlockSpec((1, chunk, LANE), lambda b, t: (b, t, bb)),
                  per_b((GDN_TAIL, wq)), per_b((N_HEADS_B, HEAD_DIM_B, HEAD_DIM_B)),
                  const((CONV_W, wq)), const((1, LANE)), const((1, LANE)), const((1, HEAD_DIM_B))],
        out_specs=[pl.BlockSpec((1, chunk, WIDTH_B), lambda b, t: (b, t, 0)),
                   per_b((N_HEADS_B, HEAD_DIM_B, HEAD_DIM_B)), per_b((GDN_TAIL, wq))],
        out_shape=[jax.ShapeDtypeStruct((B, Lx, WIDTH_B), f32),
                   jax.ShapeDtypeStruct((B, N_HEADS_B, HEAD_DIM_B, HEAD_DIM_B), f32),
                   jax.ShapeDtypeStruct((B, GDN_TAIL, wq), f32)],
        scratch_shapes=[pltpu.VMEM((N_HEADS_B, HEAD_DIM_B, HEAD_DIM_B), f32), pltpu.VMEM((GDN_TAIL, wq), f32)],
        compiler_params=pltpu.CompilerParams(dimension_semantics=("parallel", "arbitrary")),
        name="gdn_mixer",
    )(z3, z3, z3, hist, delta0, conv_w, lane_vec(a_log), lane_vec(dt_bias), gdn_norm.reshape(1, HEAD_DIM_B))


def _merge_kernel(x_ref, oa_ref, ob_ref, oc_ref, g0_ref, g1_ref, g2_ref, wa_ref, wb_ref, wc_ref, wo_ref, o_ref,
                  *, rows_per_seq, pad_rows):
    f32 = jnp.float32
    bf16 = jnp.bfloat16
    mm = lambda a, w_ref: jnp.dot(a.astype(bf16), w_ref[...], preferred_element_type=f32)
    merged = (_sigmoid(g0_ref[...]) * mm(oa_ref[...], wa_ref) + _sigmoid(g1_ref[...]) * mm(ob_ref[...], wb_ref)
              + _sigmoid(g2_ref[...]) * mm(oc_ref[...], wc_ref))
    x = x_ref[...] + mm(merged, wo_ref)
    if pad_rows:
        tm = x.shape[0]
        row = pl.program_id(0) * tm + lax.broadcasted_iota(jnp.int32, x.shape, 0)
        x = jnp.where(row % rows_per_seq >= pad_rows, x, 0.0)
    o_ref[...] = x


def merge_residual(x, oA, oB, oC, z, gate_block, wa, wb, wc, wo, rows_per_seq, pad_rows):
    T = x.shape[0]
    tm = _token_tile(T)
    bf16 = jnp.bfloat16
    rows = lambda w, j=0: pl.BlockSpec((tm, w), lambda i, j=j: (i, j))
    const = lambda a: pl.BlockSpec(a.shape, lambda i: (0, 0))
    ws = [w.astype(bf16) for w in (wa, wb, wc, wo)]
    return pl.pallas_call(
        functools.partial(_merge_kernel, rows_per_seq=rows_per_seq, pad_rows=pad_rows),
        grid=(T // tm,),
        in_specs=[rows(D_MODEL), rows(WIDTH_A), rows(WIDTH_B), rows(WIDTH_C),
                  rows(D_MODEL, gate_block), rows(D_MODEL, gate_block + 1), rows(D_MODEL, gate_block + 2)]
                 + [const(w) for w in ws],
        out_specs=rows(D_MODEL),
        out_shape=jax.ShapeDtypeStruct((T, D_MODEL), jnp.float32),
        compiler_params=pltpu.CompilerParams(dimension_semantics=("parallel",), vmem_limit_bytes=VMEM_LIMIT),
        name="merge_residual",
    )(x, oA, oB, oC, z, z, z, *ws)


N_HC = 2 * PEER_HEADS
HALF_Q = PEER_QDIM // 2
PEER_EXPERT_TILE = 8 * N_KEYS
CAND_SUB = 8


def _extract_top16(s, iota_f):
    n = float(s.shape[0])
    rank = jnp.full(s.shape, PEER_TOPK, jnp.int32)
    vals = []
    for j in range(PEER_TOPK):
        m = jnp.max(s, axis=0, keepdims=True)
        idx = jnp.min(jnp.where(s == m, iota_f, n), axis=0, keepdims=True)
        hit = iota_f == idx
        rank = jnp.where(hit, j, rank)
        s = jnp.where(hit, -jnp.inf, s)
        vals.append(m)
    return vals, rank


def _peer_router_kernel(x_ref, g_ref, wqt_ref, sk_ref, h2_ref, lim_ref, rk2_ref, e1_ref, e2_ref,
                        qt_ref, s_ref, rank_ref, v_ref):
    f32 = jnp.float32
    tl = x_ref.shape[0]
    x = x_ref[...]
    h2 = x * lax.rsqrt(jnp.mean(x * x, axis=-1, keepdims=True) + EPS) * g_ref[...]
    h2b = h2.astype(jnp.bfloat16)
    h2_ref[...] = h2b
    qt_ref[...] = lax.dot_general(wqt_ref[...], h2b, (((1,), (1,)), ((), ())),
                                  preferred_element_type=f32).astype(jnp.bfloat16)

    def score_body(hc, carry):
        r0 = pl.multiple_of(hc * HALF_Q, HALF_Q)
        s_ref[hc] = jnp.dot(sk_ref[hc], qt_ref[pl.ds(r0, HALF_Q), :], preferred_element_type=f32)
        return carry

    lax.fori_loop(0, N_HC, score_body, 0)

    key_iota = lax.broadcasted_iota(jnp.int32, (N_KEYS, LANE), 0).astype(f32)

    def key_body(i, carry):
        hc = i // (tl // LANE)
        c0 = pl.multiple_of((i % (tl // LANE)) * LANE, LANE)
        s = s_ref[hc, :, pl.ds(c0, LANE)]
        vals, rank = _extract_top16(s, key_iota)
        rank_ref[hc, :, pl.ds(c0, LANE)] = rank
        v_ref[hc, :, pl.ds(c0, LANE)] = jnp.concatenate(vals, axis=0)
        return carry

    lax.fori_loop(0, N_HC * (tl // LANE), key_body, 0, unroll=2)

    sub = CAND_SUB
    n_mid = sub - 1
    n_cand = PEER_TOPK + n_mid * sub + (PEER_TOPK - sub)
    cand_iota = lax.broadcasted_iota(jnp.int32, (n_cand, LANE), 0).astype(f32)
    row8 = lax.broadcasted_iota(jnp.int32, (sub, LANE), 0)

    def head_body(i, carry):
        h = i // (tl // LANE)
        c0 = pl.multiple_of((i % (tl // LANE)) * LANE, LANE)
        cols = pl.ds(c0, LANE)
        v1 = v_ref[2 * h, :, cols]
        v2 = v_ref[2 * h + 1, :, cols]
        blocks = [v1[0:1, :] + v2]
        for r1 in range(1, sub):
            blocks.append(jnp.where(row8 < PEER_TOPK // (r1 + 1), v1[r1:r1 + 1, :] + v2[:sub, :], -jnp.inf))
        blocks.append(v1[sub:, :] + v2[0:1, :])
        best, crank = _extract_top16(jnp.concatenate(blocks, axis=0), cand_iota)
        z = jnp.zeros((1, LANE), f32)
        for k in range(PEER_TOPK):
            z = z + jnp.exp(best[k] - best[0])
        sel = jnp.where(crank < PEER_TOPK, 1.0, 0.0)
        rank1 = rank_ref[2 * h, :, cols]
        lim = jnp.zeros((N_KEYS, LANE), jnp.int32)
        for r1 in range(PEER_TOPK):
            if r1 == 0:
                cnt = jnp.sum(sel[:PEER_TOPK, :], axis=0, keepdims=True)
            elif r1 < sub:
                lo = PEER_TOPK + (r1 - 1) * sub
                cnt = jnp.sum(sel[lo:lo + sub, :], axis=0, keepdims=True)
            else:
                lo = PEER_TOPK + n_mid * sub + (r1 - sub)
                cnt = sel[lo:lo + 1, :]
            lim = jnp.where(rank1 == r1, cnt.astype(jnp.int32), lim)
        lim_ref[h, :, cols] = lim
        rk2_ref[h, :, cols] = rank_ref[2 * h + 1, :, cols].astype(f32).astype(jnp.bfloat16)
        e1_ref[h, :, cols] = jnp.exp(s_ref[2 * h, :, cols] - v1[0:1, :])
        e2_ref[h, :, cols] = (jnp.exp(s_ref[2 * h + 1, :, cols] - v2[0:1, :]) / z).astype(jnp.bfloat16)
        return carry

    lax.fori_loop(0, PEER_HEADS * (tl // LANE), head_body, 0, unroll=2)


def _gelu_tanh(x):
    return 0.5 * x * (1.0 + jnp.tanh(math.sqrt(2.0 / math.pi) * (x + 0.044715 * (x * x * x))))


BF16_ROWS = 16


def _rows_bf16(row):
    tile = jnp.broadcast_to(row, (BF16_ROWS, LANE)).astype(jnp.bfloat16)
    return jnp.concatenate([tile] * (N_KEYS // BF16_ROWS), axis=0)


def _peer_expert_kernel(x_ref, h2_ref, pu_ref, pvt_ref, lim_ref, rk2_ref, e1_ref, e2_ref, o_ref,
                        acc_ref, a_cur_ref, a_nxt_ref, c_ref):
    f32 = jnp.float32
    bf16 = jnp.bfloat16
    zero = jnp.zeros((N_KEYS, LANE), bf16)
    e_step = pl.program_id(1)
    tl = h2_ref.shape[0]
    n_i1 = pu_ref.shape[0] // N_KEYS
    chunk = 2 * LANE if tl % (2 * LANE) == 0 else LANE

    @pl.when(e_step == 0)
    def _():
        acc_ref[...] = jnp.zeros_like(acc_ref)
        a_nxt_ref[...] = jnp.zeros_like(a_nxt_ref)

    a_cur_ref[...] = a_nxt_ref[...]
    a_nxt_ref[...] = lax.dot_general(pu_ref[...], h2_ref[...], (((1,), (1,)), ((), ())),
                                     preferred_element_type=f32)
    for c0 in range(0, tl, chunk):
        for l0 in range(c0, c0 + chunk, LANE):
            cols = slice(l0, l0 + LANE)
            for ii in range(n_i1):
                rows = slice(ii * N_KEYS, (ii + 1) * N_KEYS)
                w = jnp.zeros((N_KEYS, LANE), bf16)
                for h in range(PEER_HEADS):
                    sel = rk2_ref[h, :, cols] < _rows_bf16(lim_ref[h, ii:ii + 1, cols].astype(f32))
                    w = w + jnp.where(sel, e2_ref[h, :, cols], zero) * _rows_bf16(e1_ref[h, ii:ii + 1, cols])
                c_ref[rows, cols] = w * _gelu_tanh(a_cur_ref[rows, cols]).astype(bf16)
        cc = slice(c0, c0 + chunk)
        acc_ref[:, cc] += jnp.dot(pvt_ref[...], c_ref[:, cc], preferred_element_type=f32)

    @pl.when(e_step == pl.num_programs(1) - 1)
    def _():
        o_ref[...] = x_ref[...] + acc_ref[...].T


def _token_tile(T):
    for tl in (640, 512, 384, 256, 128):
        if T % tl == 0:
            return tl
    raise ValueError(f"token count {T} is not a multiple of {LANE}")


def peer_residual(x, g, wqt, sk, pu, pvt):
    T = x.shape[0]
    tl = _token_tile(T)
    nt = T // tl
    n_exp = pu.shape[0]
    head_shape = jax.ShapeDtypeStruct((PEER_HEADS, N_KEYS, T), jnp.int32)
    head_shape_f = jax.ShapeDtypeStruct((PEER_HEADS, N_KEYS, T), jnp.float32)
    head_shape_b = jax.ShapeDtypeStruct((PEER_HEADS, N_KEYS, T), jnp.bfloat16)
    head_spec = pl.BlockSpec((PEER_HEADS, N_KEYS, tl), lambda i: (0, 0, i))
    h2, lim, rk2, e1, e2 = pl.pallas_call(
        _peer_router_kernel,
        grid=(nt,),
        in_specs=[pl.BlockSpec((tl, D_MODEL), lambda i: (i, 0)),
                  pl.BlockSpec((1, D_MODEL), lambda i: (0, 0)),
                  pl.BlockSpec((PEER_HEADS * PEER_QDIM, D_MODEL), lambda i: (0, 0)),
                  pl.BlockSpec((N_HC, N_KEYS, HALF_Q), lambda i: (0, 0, 0))],
        out_specs=[pl.BlockSpec((tl, D_MODEL), lambda i: (i, 0)), head_spec, head_spec, head_spec, head_spec],
        out_shape=[jax.ShapeDtypeStruct((T, D_MODEL), jnp.bfloat16), head_shape, head_shape_b,
                   head_shape_f, head_shape_b],
        scratch_shapes=[pltpu.VMEM((PEER_HEADS * PEER_QDIM, tl), jnp.bfloat16),
                        pltpu.VMEM((N_HC, N_KEYS, tl), jnp.float32),
                        pltpu.VMEM((N_HC, N_KEYS, tl), jnp.int32),
                        pltpu.VMEM((N_HC, PEER_TOPK, tl), jnp.float32)],
        compiler_params=pltpu.CompilerParams(dimension_semantics=("parallel",), vmem_limit_bytes=VMEM_LIMIT),
        name="peer_router",
    )(x, g.reshape(1, D_MODEL), wqt, sk)

    te = PEER_EXPERT_TILE
    n_tiles = n_exp // te
    nxt = lambda e: jnp.minimum(e, n_tiles - 1)
    cur = lambda e: jnp.maximum(e - 1, 0)
    i2_spec = pl.BlockSpec((PEER_HEADS, N_KEYS, tl), lambda i, e: (0, 0, i))
    i1_spec = pl.BlockSpec((PEER_HEADS, te // N_KEYS, tl), lambda i, e: (0, cur(e), i))
    return pl.pallas_call(
        _peer_expert_kernel,
        grid=(nt, n_tiles + 1),
        in_specs=[pl.BlockSpec((tl, D_MODEL), lambda i, e: (i, 0)),
                  pl.BlockSpec((tl, D_MODEL), lambda i, e: (i, 0)),
                  pl.BlockSpec((te, D_MODEL), lambda i, e: (nxt(e), 0)),
                  pl.BlockSpec((D_MODEL, te), lambda i, e: (0, cur(e))),
                  i1_spec, i2_spec, i1_spec, i2_spec],
        out_specs=pl.BlockSpec((tl, D_MODEL), lambda i, e: (i, 0)),
        out_shape=jax.ShapeDtypeStruct((T, D_MODEL), jnp.float32),
        scratch_shapes=[pltpu.VMEM((D_MODEL, tl), jnp.float32),
                        pltpu.VMEM((te, tl), jnp.float32),
                        pltpu.VMEM((te, tl), jnp.float32),
                        pltpu.VMEM((te, tl), jnp.bfloat16)],
        compiler_params=pltpu.CompilerParams(dimension_semantics=("parallel", "arbitrary"),
                                             vmem_limit_bytes=VMEM_LIMIT),
        name="peer_experts",
    )(x, h2, pu, pvt, lim, rk2, e1, e2)


C_QKVB = 3 * WIDTH_A
C_GB = C_QKVB + 3 * WIDTH_B
C_UC = C_GB + WIDTH_B
C_GATE = C_UC + WIDTH_C
C_BA = C_GATE + 3 * D_MODEL
D_IN2 = C_BA + LANE
S5_BLOCK = 256


def _reorder_w_in(w):
    cols = [w[:, OFF_QA:OFF_BETA], w[:, OFF_UC:OFF_GATE], w[:, OFF_GATE:], w[:, OFF_BETA:OFF_UC],
            jnp.zeros((w.shape[0], LANE - 2 * N_HEADS_B), w.dtype)]
    return jnp.concatenate(cols, axis=1).astype(jnp.bfloat16)


def trunk_layer(x, lp, layer_idx, att, prompt, k_past, v_past, conv_hist, delta0, ssm0_re, ssm0_im):
    f32 = jnp.float32
    B, Lx = x.shape[0], x.shape[1]
    T = B * Lx
    z = norm_matmul(x.reshape(T, D_MODEL), lp['norm1'], lp['w_in2'])

    qn, kn, knb, vb = qk_prep(z, lp['q_norm'], lp['k_norm'], _token_tile(T))
    lam_init = 0.8 - 0.6 * math.exp(-0.3 * layer_idx)
    lqk = lp['lambda_qk']
    lam = jnp.exp(jnp.sum(lqk[0] * lqk[1])) - jnp.exp(jnp.sum(lqk[2] * lqk[3])) + lam_init
    scalars = jnp.stack([lam, jnp.asarray(1.0 - lam_init, f32)]).astype(f32)
    r3 = lambda a: a.reshape(B, Lx, WIDTH_A)
    if prompt:
        oA = attn_prompt(r3(qn), r3(knb), vb.reshape(B, Lx, 2 * WIDTH_A), scalars, att['far'], att['tiles'], lp['subln'])
    else:
        P = k_past.shape[1]
        oA = attn_sample(r3(qn), r3(knb), vb.reshape(B, Lx, 2 * WIDTH_A), k_past.reshape(B, P, WIDTH_A), v_past.reshape(B, P, WIDTH_A),
                         scalars, att['bias_past'], att['bias_new'], lp['subln'])

    z3 = z.reshape(B, Lx, D_IN2)
    off = PAD if prompt else 0
    L = Lx - off
    kA = r3(kn)[:, off:].reshape(B, L, N_HEADS_A, 2 * HEAD_DIM_A)
    vA = z3[:, off:, 2 * WIDTH_A:3 * WIDTH_A].reshape(B, L, N_HEADS_A, 2 * HEAD_DIM_A)

    oB, delta, tail = gdn_mixer(z3, (C_QKVB // (3 * WIDTH_B), C_GB // WIDTH_B, C_BA // LANE), conv_hist, delta0,
                                lp['conv_w'], lp['a_log'], lp['dt_bias'], lp['gdn_norm'],
                                GDN_CHUNK if prompt else Lx, off)
    conv_state = tail[:, GDN_TAIL - (CONV_W - 1):]

    oC, ssm_re, ssm_im = s5_glu(z3, C_UC // WIDTH_C, ssm0_re, ssm0_im, lp['s5'], lp['w_glu'],
                                S5_BLOCK if prompt else Lx)

    x2 = merge_residual(x.reshape(T, D_MODEL), oA.reshape(T, WIDTH_A), oB.reshape(T, WIDTH_B),
                        oC.reshape(T, WIDTH_C), z, C_GATE // D_MODEL, lp['wb_a'], lp['wb_b'], lp['wb_c'],
                        lp['w_out'], Lx, off)

    x = peer_residual(x2, lp['norm2'], lp['peer_wqt'], lp['peer_sk'], lp['peer_pu'],
                      lp['peer_pvt']).reshape(B, Lx, D_MODEL)
    return (x, kA, vA, conv_state, delta, ssm_re, ssm_im)


def kernel(x_prompt, x_sample, cache_k, cache_v, state_conv, state_delta, state_ssm_re, state_ssm_im,
           meta_tokens, rel_bias, norm1_g, norm2_g, final_norm_g, w_in, q_norm_g, k_norm_g, lambda_qk,
           subln_g, conv_w, gdn_a_log, gdn_dt_bias, gdn_norm_g, s5_a_re, s5_a_im, s5_b_re, s5_b_im,
           s5_c_re, s5_c_im, s5_d, s5_log_dt, w_glu, w_branch_a, w_branch_b, w_branch_c, w_out,
           peer_wq, peer_subkeys, peer_u, peer_v):
    f32 = jnp.float32
    bf16 = jnp.bfloat16
    params = [dict(norm1=norm1_g[l], norm2=norm2_g[l], w_in2=_reorder_w_in(w_in[l]), q_norm=q_norm_g[l],
                   k_norm=k_norm_g[l], lambda_qk=lambda_qk[l], subln=subln_g[l], conv_w=conv_w[l],
                   a_log=gdn_a_log[l], dt_bias=gdn_dt_bias[l], gdn_norm=gdn_norm_g[l],
                   s5=s5_params(s5_a_re[l], s5_a_im[l], s5_b_re[l], s5_b_im[l], s5_c_re[l], s5_c_im[l], s5_d[l],
                                s5_log_dt[l]),
                   w_glu=w_glu[l], wb_a=w_branch_a[l], wb_b=w_branch_b[l], wb_c=w_branch_c[l], w_out=w_out[l],
                   peer_wqt=peer_wq[l].T.astype(bf16),
                   peer_sk=peer_subkeys[l].reshape(N_HC, N_KEYS, HALF_Q).astype(bf16),
                   peer_pu=peer_u[l].astype(bf16), peer_pvt=peer_v[l].T.astype(bf16))
              for l in range(DEPTH)]

    qpos = jnp.arange(ATT_TQ, dtype=jnp.int32)
    tiles = jnp.stack([t5_bias(qpos, d * ATT_TK + jnp.arange(ATT_TK, dtype=jnp.int32), rel_bias)
                       for d in (-2, -1, 0)], axis=1)
    far = t5_bias(jnp.full((1,), 2 * ATT_TK, jnp.int32), jnp.zeros((1,), jnp.int32), rel_bias).reshape(N_HEADS_A)
    P, Ls = cache_k.shape[2], x_sample.shape[1]
    bias_s = t5_bias(P + jnp.arange(Ls, dtype=jnp.int32), jnp.arange(P + Ls, dtype=jnp.int32), rel_bias)
    att_p = dict(tiles=tiles, far=far)
    att_s = dict(bias_past=bias_s[:, :, :P], bias_new=bias_s[:, :, P:])

    B = x_prompt.shape[0]
    xp = jnp.concatenate([jnp.zeros((B, PAD, D_MODEL), f32),
                          jnp.broadcast_to(meta_tokens[None], (B, N_META, D_MODEL)), x_prompt], axis=1)
    outs_p = [[] for _ in range(6)]
    for l in range(DEPTH):
        res = trunk_layer(
            xp, params[l], l, att_p, True, None, None,
            jnp.zeros((B, CONV_W - 1, 3 * WIDTH_B), f32),
            jnp.zeros((B, N_HEADS_B, HEAD_DIM_B, HEAD_DIM_B), f32),
            jnp.zeros((B, S5_GROUPS, S5_STATE), f32), jnp.zeros((B, S5_GROUPS, S5_STATE), f32))
        xp = res[0]
        for acc, r in zip(outs_p, res[1:]):
            acc.append(r)
    y_prompt = rmsnorm(xp, final_norm_g)[:, FRONT:]

    xs = x_sample
    outs_s = [[] for _ in range(6)]
    for l in range(DEPTH):
        res = trunk_layer(
            xs, params[l], l, att_s, False, cache_k[l], cache_v[l], state_conv[l],
            state_delta[l], state_ssm_re[l], state_ssm_im[l])
        xs = res[0]
        for acc, r in zip(outs_s, res[1:]):
            acc.append(r)
    y_sample = rmsnorm(xs, final_norm_g)

    kp, vp, cp, dp, srp, sip = [jnp.stack(a) for a in outs_p]
    ks_, vs_, cs_, ds_, srs, sis = [jnp.stack(a) for a in outs_s]
    return (y_prompt, y_sample, kp, vp, ks_, vs_, cp, cs_, dp, ds_, srp, sip, srs, sis)
```

```python
import functools
import math

import jax
import jax.numpy as jnp
from jax import lax
from jax.experimental import pallas as pl
from jax.experimental.pallas import tpu as pltpu

D_MODEL = 1024
DEPTH = 4
CHUNK = 64
N_META = 16
EPS = 1e-6
NEG_INF = -1e30
N_HEADS_A = 4
HEAD_DIM_A = 64
NUM_BUCKETS = 32
MAX_DISTANCE = 128
N_HEADS_B = 4
HEAD_DIM_B = 128
CONV_W = 4
S5_GROUP = 16
S5_GROUPS = 32
S5_STATE = 64
PEER_HEADS = 8
PEER_QDIM = 256
N_KEYS = 128
PEER_TOPK = 16

WIDTH_A = N_HEADS_A * 2 * HEAD_DIM_A
WIDTH_B = N_HEADS_B * HEAD_DIM_B
WIDTH_C = S5_GROUPS * S5_GROUP
OFF_QA = 0
OFF_KA = OFF_QA + WIDTH_A
OFF_VA = OFF_KA + WIDTH_A
OFF_QKVB = OFF_VA + WIDTH_A
OFF_GB = OFF_QKVB + 3 * WIDTH_B
OFF_BETA = OFF_GB + WIDTH_B
OFF_ALPHA = OFF_BETA + N_HEADS_B
OFF_UC = OFF_ALPHA + N_HEADS_B
OFF_GATE = OFF_UC + WIDTH_C
D_IN = OFF_GATE + 3 * D_MODEL

LANE = 128
VMEM_LIMIT = 56 * 1024 * 1024
WEIGHT_TILE_BYTES = 6 * 1024 * 1024
HIGHEST = lax.Precision.HIGHEST


def _norm_mm_kernel(x_ref, g_ref, w_ref, o_ref, h_ref):
    @pl.when(pl.program_id(1) == 0)
    def _():
        x = x_ref[...]
        h = x * lax.rsqrt(jnp.mean(x * x, axis=-1, keepdims=True) + EPS) * g_ref[...]
        h_ref[...] = h.astype(jnp.bfloat16)

    o_ref[...] = jnp.dot(h_ref[...], w_ref[...], preferred_element_type=jnp.float32)


def norm_matmul(x, g, wb, tm=512):
    M, K = x.shape
    N = wb.shape[1]
    n_lanes = N // LANE
    tn = LANE * max(d for d in range(1, n_lanes + 1) if n_lanes % d == 0 and d * LANE * K * 2 <= WEIGHT_TILE_BYTES)
    tm = min(tm, M)
    return pl.pallas_call(
        _norm_mm_kernel,
        grid=(pl.cdiv(M, tm), N // tn),
        in_specs=[pl.BlockSpec((tm, K), lambda i, j: (i, 0)),
                  pl.BlockSpec((1, K), lambda i, j: (0, 0)),
                  pl.BlockSpec((K, tn), lambda i, j: (0, j))],
        out_specs=pl.BlockSpec((tm, tn), lambda i, j: (i, j)),
        out_shape=jax.ShapeDtypeStruct((M, N), jnp.float32),
        scratch_shapes=[pltpu.VMEM((tm, K), jnp.bfloat16)],
        compiler_params=pltpu.CompilerParams(dimension_semantics=("parallel", "arbitrary"),
                                             vmem_limit_bytes=VMEM_LIMIT),
        name="norm_proj",
    )(x, g.reshape(1, K), wb)


def rmsnorm(x, g):
    xf = x.astype(jnp.float32)
    y = xf * lax.rsqrt(jnp.mean(xf * xf, axis=-1, keepdims=True) + EPS)
    return (y * g.astype(jnp.float32)).astype(x.dtype)


def t5_bias(q_pos, k_pos, table):
    rel = k_pos[None, :] - q_pos[:, None]
    half = NUM_BUCKETS // 2
    exact = half // 2
    n = jnp.abs(rel)
    nf = jnp.maximum(n, 1).astype(jnp.float32)
    far = exact + (jnp.log(nf / exact) / math.log(MAX_DISTANCE / exact) * (half - exact)).astype(jnp.int32)
    bucket = jnp.where(rel > 0, half, 0) + jnp.where(n < exact, n, jnp.minimum(far, half - 1))
    return jnp.moveaxis(table[bucket].astype(jnp.float32), -1, 0)


FRONT = 256
PAD = FRONT - N_META
ATT_TQ = 256
ATT_TK = 256


def _qk_prep_kernel(q_ref, k_ref, v_ref, gq_ref, gk_ref, seg_ref, qn_ref, kn_ref, knb_ref, vb_ref):
    f32 = jnp.float32
    seg = seg_ref[...]

    def norm(x, g):
        ms = jnp.dot(x * x, seg, precision=HIGHEST, preferred_element_type=f32)
        return x * lax.rsqrt(ms + EPS) * g

    qn_ref[...] = (norm(q_ref[...], gq_ref[...]) * (HEAD_DIM_A ** -0.5)).astype(jnp.bfloat16)
    kn = norm(k_ref[...], gk_ref[...])
    kn_ref[...] = kn
    knb_ref[...] = kn.astype(jnp.bfloat16)
    v = v_ref[...].astype(jnp.bfloat16)
    hd = 2 * HEAD_DIM_A
    lane = lax.broadcasted_iota(jnp.int32, (v.shape[0], hd), 1)
    ones_col = jnp.where(lane == 0, 1.0, 0.0).astype(jnp.bfloat16)
    vb_ref[...] = jnp.concatenate(
        [piece for h in range(N_HEADS_A) for piece in (v[:, h * hd:(h + 1) * hd], ones_col)], axis=1)


def qk_prep(z, gq, gk, tm):
    T = z.shape[0]
    seg = jnp.kron(jnp.eye(WIDTH_A // HEAD_DIM_A, dtype=jnp.float32),
                   jnp.full((HEAD_DIM_A, HEAD_DIM_A), 1.0 / HEAD_DIM_A, jnp.float32))
    row = lambda j: pl.BlockSpec((tm, WIDTH_A), lambda i, j=j: (i, j))
    const = lambda shape: pl.BlockSpec(shape, lambda i: (0,) * len(shape))
    out = lambda: pl.BlockSpec((tm, WIDTH_A), lambda i: (i, 0))
    bshape = jax.ShapeDtypeStruct((T, WIDTH_A), jnp.bfloat16)
    return pl.pallas_call(
        _qk_prep_kernel,
        grid=(T // tm,),
        in_specs=[row(0), row(1), row(2), const((1, WIDTH_A)), const((1, WIDTH_A)), const((WIDTH_A, WIDTH_A))],
        out_specs=[out(), out(), out(), pl.BlockSpec((tm, 2 * WIDTH_A), lambda i: (i, 0))],
        out_shape=[bshape, jax.ShapeDtypeStruct((T, WIDTH_A), jnp.float32), bshape,
                   jax.ShapeDtypeStruct((T, 2 * WIDTH_A), jnp.bfloat16)],
        compiler_params=pltpu.CompilerParams(dimension_semantics=("parallel",)),
        name="qk_prep",
    )(z, z, z, jnp.tile(gq, WIDTH_A // HEAD_DIM_A).reshape(1, WIDTH_A),
      jnp.tile(gk, WIDTH_A // HEAD_DIM_A).reshape(1, WIDTH_A), seg)


def _split_maps(q):
    lane = lax.broadcasted_iota(jnp.int32, q.shape, 1)
    zero = jnp.zeros_like(q)
    return jnp.where(lane < HEAD_DIM_A, q, zero), jnp.where(lane >= HEAD_DIM_A, q, zero)


def _subln(o, g, scale):
    return o * lax.rsqrt(jnp.mean(o * o, axis=-1, keepdims=True) + EPS) * g * scale


def _attn_prompt_kernel(sc_ref, far_ref, q_ref, k_ref, v_ref, bias_ref, g_ref, o_ref, m_ref, acc_ref,
                        sa_ref, sb_ref):
    f32 = jnp.float32
    h = pl.program_id(1)
    qi = pl.program_id(2)
    tq, tk = ATT_TQ, ATT_TK
    hd = 2 * HEAD_DIM_A
    qs = jnp.concatenate(_split_maps(q_ref[0]), axis=0)
    m_ref[...] = jnp.full(m_ref.shape, NEG_INF, f32)
    acc_ref[...] = jnp.zeros(acc_ref.shape, f32)
    far = far_ref[h]

    def logits(k0, width, general):
        k0 = pl.multiple_of(k0, tk)
        kt = k_ref[0, pl.ds(k0, width), :]
        s = lax.dot_general(qs, kt, (((1,), (1,)), ((), ())), preferred_element_type=f32)
        if general:
            kj = k0 // tk
            bias = bias_ref[0, jnp.clip(kj - qi + 2, 0, 2)]
            qpos = qi * tq + lax.broadcasted_iota(jnp.int32, (tq, tk), 0)
            kpos = k0 + lax.broadcasted_iota(jnp.int32, (tq, tk), 1)
            qchunk = jnp.where(qpos < FRONT, 0, 1 + jnp.right_shift(qpos - FRONT, 6))
            kchunk = jnp.where(kpos < FRONT, 0, 1 + jnp.right_shift(kpos - FRONT, 6))
            mask = (kpos >= PAD) & (kchunk <= qchunk)
            s = jnp.where(jnp.concatenate([mask, mask], axis=0), s + jnp.concatenate([bias, bias], axis=0), NEG_INF)
        else:
            s = s + far
        return s

    def accumulate(s, k0):
        width = s.shape[1]
        vt = v_ref[0, pl.ds(pl.multiple_of(k0, tk), width), :]
        n_lane_tiles = width // LANE
        smax = s[:, :LANE]
        for j in range(1, n_lane_tiles):
            smax = jnp.maximum(smax, s[:, j * LANE:(j + 1) * LANE])
        m_old = m_ref[...]
        m_new = jnp.maximum(m_old, jnp.broadcast_to(jnp.max(smax, axis=1, keepdims=True), m_old.shape))
        alpha = jnp.exp(m_old - m_new)
        p = jnp.exp(s - jnp.concatenate([m_new] * n_lane_tiles, axis=1))
        acc_ref[...] = (jnp.concatenate([alpha, alpha], axis=1) * acc_ref[...]
                        + jnp.dot(p.astype(jnp.bfloat16), vt, preferred_element_type=f32))
        m_ref[...] = m_new

    def tile(k0, width, general):
        accumulate(logits(k0, width, general), k0)

    tile(0, tk, True)

    wide = 2 * tk
    n_far = jnp.maximum(qi - 2, 0)
    n_single = n_far % 4
    for i in range(3):
        @pl.when(n_single > i)
        def _(i=i):
            tile((1 + i) * tk, tk, False)

    base = (1 + n_single) * tk
    n_pairs = n_far // 4

    @pl.when(n_pairs > 0)
    def _():
        sa_ref[...] = logits(base, wide, False)

    def pair_body(j, carry):
        k0 = base + 2 * j * wide
        sb_ref[...] = logits(k0 + wide, wide, False)
        accumulate(sa_ref[...], k0)
        nxt = jnp.minimum(k0 + 2 * wide, base + (2 * n_pairs - 1) * wide)
        sa_ref[...] = logits(nxt, wide, False)
        accumulate(sb_ref[...], k0 + wide)
        return carry

    lax.fori_loop(0, n_pairs, pair_body, 0)

    @pl.when(qi >= 2)
    def _():
        tile((qi - 1) * tk, tk, True)

    @pl.when(qi >= 1)
    def _():
        tile(qi * tk, tk, True)

    acc = acc_ref[...]
    out = acc[:, :hd] / acc[:, hd:hd + 1]
    o = out[:tq] - sc_ref[0] * out[tq:]
    o_ref[0] = _subln(o, g_ref[...], sc_ref[1])


def attn_prompt(qn, knb, vb, scalars, far, bias_tiles, subln_g):
    B, Lp, _ = qn.shape
    hd = 2 * HEAD_DIM_A
    smem = pl.BlockSpec(memory_space=pltpu.SMEM)
    return pl.pallas_call(
        _attn_prompt_kernel,
        grid=(B, N_HEADS_A, Lp // ATT_TQ),
        in_specs=[smem, smem,
                  pl.BlockSpec((1, ATT_TQ, hd), lambda b, h, i: (b, i, h)),
                  pl.BlockSpec((1, Lp, hd), lambda b, h, i: (b, 0, h)),
                  pl.BlockSpec((1, Lp, 2 * hd), lambda b, h, i: (b, 0, h)),
                  pl.BlockSpec((1, 3, ATT_TQ, ATT_TK), lambda b, h, i: (h, 0, 0, 0)),
                  pl.BlockSpec((1, hd), lambda b, h, i: (0, 0))],
        out_specs=pl.BlockSpec((1, ATT_TQ, hd), lambda b, h, i: (b, i, h)),
        out_shape=jax.ShapeDtypeStruct((B, Lp, WIDTH_A), jnp.float32),
        scratch_shapes=[pltpu.VMEM((2 * ATT_TQ, hd), jnp.float32), pltpu.VMEM((2 * ATT_TQ, 2 * hd), jnp.float32),
                        pltpu.VMEM((2 * ATT_TQ, 2 * ATT_TK), jnp.float32),
                        pltpu.VMEM((2 * ATT_TQ, 2 * ATT_TK), jnp.float32)],
        compiler_params=pltpu.CompilerParams(dimension_semantics=("parallel", "parallel", "parallel"),
                                             vmem_limit_bytes=VMEM_LIMIT),
        name="diff_attn_prompt",
    )(scalars, far, qn, knb, vb, bias_tiles, subln_g.reshape(1, hd))


def _attn_sample_kernel(sc_ref, q_ref, kp_ref, vp_ref, kn_ref, vn_ref, bp_ref, bn_ref, g_ref, o_ref):
    f32 = jnp.float32
    bf16 = jnp.bfloat16
    qs = _split_maps(q_ref[0])
    kp = kp_ref[0].astype(bf16)
    vp = vp_ref[0].astype(bf16)
    kn = kn_ref[0]
    vn = vn_ref[0][:, :2 * HEAD_DIM_A]
    dn = (((1,), (1,)), ((), ()))
    outs = []
    for c in range(2):
        sp = lax.dot_general(qs[c], kp, dn, preferred_element_type=f32) + bp_ref[0]
        sn = lax.dot_general(qs[c], kn, dn, preferred_element_type=f32) + bn_ref[0]
        m = jnp.maximum(jnp.max(sp, axis=1, keepdims=True), jnp.max(sn, axis=1, keepdims=True))
        pp = jnp.exp(sp - m)
        pn = jnp.exp(sn - m)
        l = jnp.sum(pp, axis=1, keepdims=True) + jnp.sum(pn, axis=1, keepdims=True)
        acc = (jnp.dot(pp.astype(bf16), vp, preferred_element_type=f32)
               + jnp.dot(pn.astype(bf16), vn, preferred_element_type=f32))
        outs.append(acc / l)
    o_ref[0] = _subln(outs[0] - sc_ref[0] * outs[1], g_ref[...], sc_ref[1])


def attn_sample(qn, knb, vb, k_past, v_past, scalars, bias_past, bias_new, subln_g):
    B, L, _ = qn.shape
    P = k_past.shape[1]
    hd = 2 * HEAD_DIM_A
    new = pl.BlockSpec((1, L, hd), lambda b, h: (b, 0, h))
    past = pl.BlockSpec((1, P, hd), lambda b, h: (b, 0, h))
    return pl.pallas_call(
        _attn_sample_kernel,
        grid=(B, N_HEADS_A),
        in_specs=[pl.BlockSpec(memory_space=pltpu.SMEM), new, past, past, new,
                  pl.BlockSpec((1, L, 2 * hd), lambda b, h: (b, 0, h)),
                  pl.BlockSpec((1, L, P), lambda b, h: (h, 0, 0)),
                  pl.BlockSpec((1, L, L), lambda b, h: (h, 0, 0)),
                  pl.BlockSpec((1, hd), lambda b, h: (0, 0))],
        out_specs=new,
        out_shape=jax.ShapeDtypeStruct((B, L, WIDTH_A), jnp.float32),
        compiler_params=pltpu.CompilerParams(dimension_semantics=("parallel", "parallel")),
        name="diff_attn_sample",
    )(scalars, qn, k_past, v_past, knb, vb, bias_past, bias_new, subln_g.reshape(1, hd))


S5_N = S5_GROUPS * S5_STATE
S5_SLAB = 8
S5_LANES = 512


def _sigmoid(x):
    return 1.0 / (1.0 + jnp.exp(-x))


def _s5_kernel(u_ref, x0_ref, bmat_ref, cmat_ref, lamp_ref, ppow_ref, d_ref, wglu_ref, oc_ref, xf_ref,
               bu_ref, carry_ref):
    f32 = jnp.float32
    bf16 = jnp.bfloat16
    t = pl.program_id(1)
    tb = u_ref.shape[1]

    @pl.when(t == 0)
    def _():
        carry_ref[...] = x0_ref[0]

    u = u_ref[0]
    bu_ref[...] = jnp.dot(u.astype(bf16), bmat_ref[...], preferred_element_type=f32)

    row = lax.broadcasted_iota(jnp.int32, (S5_SLAB, S5_LANES), 0)
    for c in range(S5_N // S5_LANES):
        re = slice(c * S5_LANES, (c + 1) * S5_LANES)
        im = slice(S5_N + c * S5_LANES, S5_N + (c + 1) * S5_LANES)

        def slab(i, carry):
            cre, cim = carry
            rows = pl.ds(pl.multiple_of(i * S5_SLAB, S5_SLAB), S5_SLAB)
            yre = bu_ref[rows, re]
            yim = bu_ref[rows, im]
            for s in range(3):
                sh = 1 << s
                sre = jnp.where(row >= sh, pltpu.roll(yre, sh, 0), 0.0)
                sim = jnp.where(row >= sh, pltpu.roll(yim, sh, 0), 0.0)
                lr = lamp_ref[s, 0, :, re]
                li = lamp_ref[s, 1, :, re]
                yre, yim = yre + (lr * sre - li * sim), yim + (lr * sim + li * sre)
            pr = ppow_ref[0, :, re]
            pi = ppow_ref[1, :, re]
            yre, yim = yre + (pr * cre - pi * cim), yim + (pr * cim + pi * cre)
            bu_ref[rows, re] = yre
            bu_ref[rows, im] = yim
            last = S5_SLAB - 1
            return (jnp.broadcast_to(yre[last:last + 1, :], yre.shape),
                    jnp.broadcast_to(yim[last:last + 1, :], yim.shape))

        cre, cim = lax.fori_loop(0, tb // S5_SLAB, slab, (carry_ref[:, re], carry_ref[:, im]))
        carry_ref[:, re] = cre
        carry_ref[:, im] = cim

    y = jnp.dot(bu_ref[...].astype(bf16), cmat_ref[...], preferred_element_type=f32) + d_ref[...] * u
    gl = jnp.dot(_gelu_tanh(y).astype(bf16), wglu_ref[...], preferred_element_type=f32)
    oc_ref[0] = gl[:, :WIDTH_C] * _sigmoid(gl[:, WIDTH_C:])

    @pl.when(t == pl.num_programs(1) - 1)
    def _():
        xf_ref[0] = carry_ref[...]


def s5_params(a_re, a_im, b_re, b_im, c_re, c_im, d, log_dt):
    f32 = jnp.float32
    lam = lax.complex(a_re, a_im)
    lam_bar = jnp.exp(lam * jnp.exp(log_dt)[:, None])
    b_bar = ((lam_bar - 1.0) / lam)[..., None] * lax.complex(b_re, b_im)
    eye = jnp.eye(S5_GROUPS, dtype=f32)
    bd_in = lambda m: jnp.einsum('gpi,gh->gihp', m, eye).reshape(WIDTH_C, S5_N)
    bd_out = lambda m: jnp.einsum('gip,gh->gphi', m, eye).reshape(S5_N, WIDTH_C)
    bmat = jnp.concatenate([bd_in(b_bar.real), bd_in(b_bar.imag)], axis=1).astype(jnp.bfloat16)
    cmat = jnp.concatenate([bd_out(c_re), bd_out(-c_im)], axis=0).astype(jnp.bfloat16)
    lb = lam_bar.reshape(S5_N)
    rep = lambda v: jnp.broadcast_to(v[None, :], (S5_SLAB, S5_N))
    pows = [lb, lb * lb, (lb * lb) * (lb * lb)]
    lamp = jnp.stack([jnp.stack([rep(p.real), rep(p.imag)]) for p in pows])
    run = [lb]
    for _ in range(S5_SLAB - 1):
        run.append(run[-1] * lb)
    pp = jnp.stack(run)
    ppow = jnp.stack([pp.real, pp.imag])
    return dict(bmat=bmat, cmat=cmat, lamp=lamp.astype(f32), ppow=ppow.astype(f32), d=d.reshape(1, WIDTH_C))


def s5_glu(z3, col_block, x0_re, x0_im, sp, w_glu, tb):
    B, Lx, _ = z3.shape
    x0 = jnp.concatenate([x0_re.reshape(B, S5_N), x0_im.reshape(B, S5_N)], axis=1)
    x0 = jnp.broadcast_to(x0[:, None, :], (B, S5_SLAB, 2 * S5_N))
    const = lambda a: pl.BlockSpec(a.shape, lambda b, t: (0,) * a.ndim)
    wg = w_glu.astype(jnp.bfloat16)
    oc, xf = pl.pallas_call(
        _s5_kernel,
        grid=(B, Lx // tb),
        in_specs=[pl.BlockSpec((1, tb, WIDTH_C), lambda b, t: (b, t, col_block)),
                  pl.BlockSpec((1, S5_SLAB, 2 * S5_N), lambda b, t: (b, 0, 0)),
                  const(sp['bmat']), const(sp['cmat']), const(sp['lamp']), const(sp['ppow']), const(sp['d']),
                  const(wg)],
        out_specs=[pl.BlockSpec((1, tb, WIDTH_C), lambda b, t: (b, t, 0)),
                   pl.BlockSpec((1, S5_SLAB, 2 * S5_N), lambda b, t: (b, 0, 0))],
        out_shape=[jax.ShapeDtypeStruct((B, Lx, WIDTH_C), jnp.float32),
                   jax.ShapeDtypeStruct((B, S5_SLAB, 2 * S5_N), jnp.float32)],
        scratch_shapes=[pltpu.VMEM((tb, 2 * S5_N), jnp.float32), pltpu.VMEM((S5_SLAB, 2 * S5_N), jnp.float32)],
        compiler_params=pltpu.CompilerParams(dimension_semantics=("parallel", "arbitrary"),
                                             vmem_limit_bytes=VMEM_LIMIT),
        name="s5_glu",
    )(z3, x0, sp['bmat'], sp['cmat'], sp['lamp'], sp['ppow'], sp['d'], wg)
    shape = (B, S5_GROUPS, S5_STATE)
    return oc, xf[:, 0, :S5_N].reshape(shape), xf[:, 0, S5_N:].reshape(shape)


def _gdn_kernel(qkv_ref, gate_ref, ba_ref, hist_ref, s0_ref, cw_ref, av_ref, dt_ref, gn_ref,
                ob_ref, sout_ref, tail_ref, s_ref, prev_ref, *, pad_rows):
    f32 = jnp.float32
    t = pl.program_id(1)
    C = qkv_ref.shape[1]
    hd = HEAD_DIM_B
    dn_last = (((1,), (1,)), ((), ()))
    dn_first = (((0,), (0,)), ((), ()))
    dot = lambda a, b: _dot3(a, b, (((1,), (0,)), ((), ())))

    @pl.when(t == 0)
    def _():
        s_ref[...] = s0_ref[0]
        prev_ref[...] = hist_ref[0]

    cur = qkv_ref[0]
    tail = prev_ref.shape[0]
    ext = jnp.concatenate([prev_ref[...], cur], axis=0)
    w = cw_ref[...]
    conv = cur * w[CONV_W - 1:CONV_W, :]
    for i in range(CONV_W - 1):
        lo = tail - (CONV_W - 1) + i
        conv = conv + ext[lo:lo + C, :] * w[i:i + 1, :]
    prev_ref[...] = cur[C - tail:, :]
    c = conv * _sigmoid(conv)

    ba = ba_ref[0]
    xg = ba + dt_ref[...]
    softplus = jnp.maximum(xg, 0.0) + jnp.log(1.0 + jnp.exp(-jnp.abs(xg)))
    g_all = -jnp.exp(av_ref[...]) * softplus
    if pad_rows:
        grow = t * C + lax.broadcasted_iota(jnp.int32, g_all.shape, 0)
        g_all = jnp.where(grow >= pad_rows, g_all, 0.0)
    ri = lax.broadcasted_iota(jnp.int32, (C, C), 0)
    ci = lax.broadcasted_iota(jnp.int32, (C, C), 1)
    tri = ri >= ci
    strict = ri > ci
    tril = jnp.where(tri, 1.0, 0.0).astype(f32)
    G_col = jnp.dot(tril, g_all, precision=HIGHEST, preferred_element_type=f32)
    G_row = lax.dot_general(g_all, tril, (((0,), (1,)), ((), ())), precision=HIGHEST,
                            preferred_element_type=f32)
    eye = jnp.where(ri == ci, 1.0, 0.0).astype(f32)

    heads = range(N_HEADS_B)
    dot_t = lambda a, b: _dot3(a, b, dn_last)
    Gc = [G_col[:, N_HEADS_B + h:N_HEADS_B + h + 1] for h in heads]
    Gr = [G_row[N_HEADS_B + h:N_HEADS_B + h + 1, :] for h in heads]
    decay = [jnp.where(tri, jnp.exp(jnp.minimum(Gc[h] - Gr[h], 0.0)), 0.0) for h in heads]
    beta = [_sigmoid(ba[:, h:h + 1]) for h in heads]
    unit = lambda a: a * lax.rsqrt(jnp.sum(a * a, axis=-1, keepdims=True) + EPS)
    q = [unit(c[:, h * hd:(h + 1) * hd]) * (hd ** -0.5) for h in heads]
    k = [unit(c[:, WIDTH_B + h * hd:WIDTH_B + (h + 1) * hd]) for h in heads]
    kb = [k[h] * beta[h] for h in heads]
    eG = [jnp.exp(Gc[h]) for h in heads]
    rhs = [jnp.concatenate([c[:, 2 * WIDTH_B + h * hd:2 * WIDTH_B + (h + 1) * hd] * beta[h], kb[h] * eG[h]], axis=1)
           for h in heads]
    A = [jnp.where(strict, dot_t(kb[h], k[h]) * decay[h], 0.0) for h in heads]
    same_block = lambda b: jnp.right_shift(ri, b.bit_length() - 1) == jnp.right_shift(ci, b.bit_length() - 1)
    in_base = same_block(GDN_BASE)
    D = [jnp.where(in_base, A[h], 0.0) for h in heads]
    D2 = [dot(D[h], D[h]) for h in heads]
    D4 = [dot(D2[h], D2[h]) for h in heads]
    tm = [eye - D[h] for h in heads]
    tm = [tm[h] + dot(tm[h], D2[h]) for h in heads]
    tm = [tm[h] + dot(tm[h], D4[h]) for h in heads]
    b = GDN_BASE
    while b < C:
        level = same_block(2 * b) & jnp.logical_not(same_block(b))
        me = [dot(tm[h], jnp.where(level, A[h], 0.0)) for h in heads]
        tm = [tm[h] - dot(me[h], tm[h]) for h in heads]
        b *= 2
    sol = [dot(tm[h], rhs[h]) for h in heads]
    attn = [dot_t(q[h], k[h]) * decay[h] for h in heads]
    S = [s_ref[h] for h in heads]
    v_new = [sol[h][:, :hd] - dot(sol[h][:, hd:], S[h]) for h in heads]
    o = [dot(q[h] * eG[h], S[h]) + dot(attn[h], v_new[h]) for h in heads]
    for h in heads:
        GL = Gc[h][C - 1:C, :]
        s_ref[h] = S[h] * jnp.exp(GL) + _dot3(k[h] * jnp.exp(GL - Gc[h]), v_new[h], dn_first)
        gate = gate_ref[0, :, h * hd:(h + 1) * hd]
        on = o[h] * lax.rsqrt(jnp.mean(o[h] * o[h], axis=-1, keepdims=True) + EPS) * gn_ref[...]
        ob_ref[0, :, h * hd:(h + 1) * hd] = on * (gate * _sigmoid(gate))

    @pl.when(t == pl.num_programs(1) - 1)
    def _():
        sout_ref[0] = s_ref[...]
        tail_ref[0] = cur[C - tail:, :]


GDN_TAIL = 8
GDN_CHUNK = 128
GDN_BASE = 8


def _dot3(a, b, dims):
    f32 = jnp.float32
    bf16 = jnp.bfloat16
    a_hi = a.astype(bf16)
    b_hi = b.astype(bf16)
    a_lo = (a - a_hi.astype(f32)).astype(bf16)
    b_lo = (b - b_hi.astype(f32)).astype(bf16)
    dg = lambda x, y: lax.dot_general(x, y, dims, preferred_element_type=f32)
    return dg(a_hi, b_hi) + (dg(a_hi, b_lo) + dg(a_lo, b_hi))


def gdn_mixer(z3, blocks, conv_hist, delta0, conv_w, a_log, dt_bias, gdn_norm, chunk, pad_rows):
    B, Lx, _ = z3.shape
    f32 = jnp.float32
    wq = 3 * WIDTH_B
    hist = jnp.pad(conv_hist, ((0, 0), (GDN_TAIL - (CONV_W - 1), 0), (0, 0)))
    lane_vec = lambda v: jnp.zeros((1, LANE), f32).at[0, N_HEADS_B:2 * N_HEADS_B].set(v)
    const = lambda shape: pl.BlockSpec(shape, lambda b, t: (0,) * len(shape))
    per_b = lambda shape: pl.BlockSpec((1,) + shape, lambda b, t: (b,) + (0,) * len(shape))
    qb, gb, bb = blocks
    return pl.pallas_call(
        functools.partial(_gdn_kernel, pad_rows=pad_rows),
        grid=(B, Lx // chunk),
        in_specs=[pl.BlockSpec((1, chunk, wq), lambda b, t: (b, t, qb)),
                  pl.BlockSpec((1, chunk, WIDTH_B), lambda b, t: (b, t, gb)),
                  pl.BlockSpec((1, chunk, LANE), lambda b, t: (b, t, bb)),
                  per_b((GDN_TAIL, wq)), per_b((N_HEADS_B, HEAD_DIM_B, HEAD_DIM_B)),
                  const((CONV_W, wq)), const((1, LANE)), const((1, LANE)), const((1, HEAD_DIM_B))],
        out_specs=[pl.BlockSpec((1, chunk, WIDTH_B), lambda b, t: (b, t, 0)),
                   per_b((N_HEADS_B, HEAD_DIM_B, HEAD_DIM_B)), per_b((GDN_TAIL, wq))],
        out_shape=[jax.ShapeDtypeStruct((B, Lx, WIDTH_B), f32),
                   jax.ShapeDtypeStruct((B, N_HEADS_B, HEAD_DIM_B, HEAD_DIM_B), f32),
                   jax.ShapeDtypeStruct((B, GDN_TAIL, wq), f32)],
        scratch_shapes=[pltpu.VMEM((N_HEADS_B, HEAD_DIM_B, HEAD_DIM_B), f32), pltpu.VMEM((GDN_TAIL, wq), f32)],
        compiler_params=pltpu.CompilerParams(dimension_semantics=("parallel", "arbitrary")),
        name="gdn_mixer",
    )(z3, z3, z3, hist, delta0, conv_w, lane_vec(a_log), lane_vec(dt_bias), gdn_norm.reshape(1, HEAD_DIM_B))


def _merge_kernel(x_ref, oa_ref, ob_ref, oc_ref, g0_ref, g1_ref, g2_ref, wa_ref, wb_ref, wc_ref, wo_ref, o_ref,
                  *, rows_per_seq, pad_rows):
    f32 = jnp.float32
    bf16 = jnp.bfloat16
    mm = lambda a, w_ref: jnp.dot(a.astype(bf16), w_ref[...], preferred_element_type=f32)
    merged = (_sigmoid(g0_ref[...]) * mm(oa_ref[...], wa_ref) + _sigmoid(g1_ref[...]) * mm(ob_ref[...], wb_ref)
              + _sigmoid(g2_ref[...]) * mm(oc_ref[...], wc_ref))
    x = x_ref[...] + mm(merged, wo_ref)
    if pad_rows:
        tm = x.shape[0]
        row = pl.program_id(0) * tm + lax.broadcasted_iota(jnp.int32, x.shape, 0)
        x = jnp.where(row % rows_per_seq >= pad_rows, x, 0.0)
    o_ref[...] = x


def merge_residual(x, oA, oB, oC, z, gate_block, wa, wb, wc, wo, rows_per_seq, pad_rows):
    T = x.shape[0]
    tm = _token_tile(T)
    bf16 = jnp.bfloat16
    rows = lambda w, j=0: pl.BlockSpec((tm, w), lambda i, j=j: (i, j))
    const = lambda a: pl.BlockSpec(a.shape, lambda i: (0, 0))
    ws = [w.astype(bf16) for w in (wa, wb, wc, wo)]
    return pl.pallas_call(
        functools.partial(_merge_kernel, rows_per_seq=rows_per_seq, pad_rows=pad_rows),
        grid=(T // tm,),
        in_specs=[rows(D_MODEL), rows(WIDTH_A), rows(WIDTH_B), rows(WIDTH_C),
                  rows(D_MODEL, gate_block), rows(D_MODEL, gate_block + 1), rows(D_MODEL, gate_block + 2)]
                 + [const(w) for w in ws],
        out_specs=rows(D_MODEL),
        out_shape=jax.ShapeDtypeStruct((T, D_MODEL), jnp.float32),
        compiler_params=pltpu.CompilerParams(dimension_semantics=("parallel",), vmem_limit_bytes=VMEM_LIMIT),
        name="merge_residual",
    )(x, oA, oB, oC, z, z, z, *ws)


N_HC = 2 * PEER_HEADS
HALF_Q = PEER_QDIM // 2
PEER_EXPERT_TILE = 8 * N_KEYS
CAND_SUB = 8


def _extract_top16(s, iota_f):
    n = float(s.shape[0])
    rank = jnp.full(s.shape, PEER_TOPK, jnp.int32)
    vals = []
    for j in range(PEER_TOPK):
        m = jnp.max(s, axis=0, keepdims=True)
        idx = jnp.min(jnp.where(s == m, iota_f, n), axis=0, keepdims=True)
        hit = iota_f == idx
        rank = jnp.where(hit, j, rank)
        s = jnp.where(hit, -jnp.inf, s)
        vals.append(m)
    return vals, rank


def _peer_router_kernel(x_ref, g_ref, wqt_ref, sk_ref, h2_ref, lim_ref, rk2_ref, e1_ref, e2_ref,
                        qt_ref, s_ref, rank_ref, v_ref):
    f32 = jnp.float32
    tl = x_ref.shape[0]
    x = x_ref[...]
    h2 = x * lax.rsqrt(jnp.mean(x * x, axis=-1, keepdims=True) + EPS) * g_ref[...]
    h2b = h2.astype(jnp.bfloat16)
    h2_ref[...] = h2b
    qt_ref[...] = lax.dot_general(wqt_ref[...], h2b, (((1,), (1,)), ((), ())),
                                  preferred_element_type=f32).astype(jnp.bfloat16)

    def score_body(hc, carry):
        r0 = pl.multiple_of(hc * HALF_Q, HALF_Q)
        s_ref[hc] = jnp.dot(sk_ref[hc], qt_ref[pl.ds(r0, HALF_Q), :], preferred_element_type=f32)
        return carry

    lax.fori_loop(0, N_HC, score_body, 0)

    key_iota = lax.broadcasted_iota(jnp.int32, (N_KEYS, LANE), 0).astype(f32)

    def key_body(i, carry):
        hc = i // (tl // LANE)
        c0 = pl.multiple_of((i % (tl // LANE)) * LANE, LANE)
        s = s_ref[hc, :, pl.ds(c0, LANE)]
        vals, rank = _extract_top16(s, key_iota)
        rank_ref[hc, :, pl.ds(c0, LANE)] = rank
        v_ref[hc, :, pl.ds(c0, LANE)] = jnp.concatenate(vals, axis=0)
        return carry

    lax.fori_loop(0, N_HC * (tl // LANE), key_body, 0, unroll=2)

    sub = CAND_SUB
    n_mid = sub - 1
    n_cand = PEER_TOPK + n_mid * sub + (PEER_TOPK - sub)
    cand_iota = lax.broadcasted_iota(jnp.int32, (n_cand, LANE), 0).astype(f32)
    row8 = lax.broadcasted_iota(jnp.int32, (sub, LANE), 0)

    def head_body(i, carry):
        h = i // (tl // LANE)
        c0 = pl.multiple_of((i % (tl // LANE)) * LANE, LANE)
        cols = pl.ds(c0, LANE)
        v1 = v_ref[2 * h, :, cols]
        v2 = v_ref[2 * h + 1, :, cols]
        blocks = [v1[0:1, :] + v2]
        for r1 in range(1, sub):
            blocks.append(jnp.where(row8 < PEER_TOPK // (r1 + 1), v1[r1:r1 + 1, :] + v2[:sub, :], -jnp.inf))
        blocks.append(v1[sub:, :] + v2[0:1, :])
        best, crank = _extract_top16(jnp.concatenate(blocks, axis=0), cand_iota)
        z = jnp.zeros((1, LANE), f32)
        for k in range(PEER_TOPK):
            z = z + jnp.exp(best[k] - best[0])
        sel = jnp.where(crank < PEER_TOPK, 1.0, 0.0)
        rank1 = rank_ref[2 * h, :, cols]
        lim = jnp.zeros((N_KEYS, LANE), jnp.int32)
        for r1 in range(PEER_TOPK):
            if r1 == 0:
                cnt = jnp.sum(sel[:PEER_TOPK, :], axis=0, keepdims=True)
            elif r1 < sub:
                lo = PEER_TOPK + (r1 - 1) * sub
                cnt = jnp.sum(sel[lo:lo + sub, :], axis=0, keepdims=True)
            else:
                lo = PEER_TOPK + n_mid * sub + (r1 - sub)
                cnt = sel[lo:lo + 1, :]
            lim = jnp.where(rank1 == r1, cnt.astype(jnp.int32), lim)
        lim_ref[h, :, cols] = lim
        rk2_ref[h, :, cols] = rank_ref[2 * h + 1, :, cols].astype(f32).astype(jnp.bfloat16)
        e1_ref[h, :, cols] = jnp.exp(s_ref[2 * h, :, cols] - v1[0:1, :])
        e2_ref[h, :, cols] = (jnp.exp(s_ref[2 * h + 1, :, cols] - v2[0:1, :]) / z).astype(jnp.bfloat16)
        return carry

    lax.fori_loop(0, PEER_HEADS * (tl // LANE), head_body, 0, unroll=2)


def _gelu_tanh(x):
    return 0.5 * x * (1.0 + jnp.tanh(math.sqrt(2.0 / math.pi) * (x + 0.044715 * (x * x * x))))


BF16_ROWS = 16


def _rows_bf16(row):
    tile = jnp.broadcast_to(row, (BF16_ROWS, LANE)).astype(jnp.bfloat16)
    return jnp.concatenate([tile] * (N_KEYS // BF16_ROWS), axis=0)


def _peer_expert_kernel(x_ref, h2_ref, pu_ref, pvt_ref, lim_ref, rk2_ref, e1_ref, e2_ref, o_ref,
                        acc_ref, a_cur_ref, a_nxt_ref, c_ref, rk2_s, e2_s):
    f32 = jnp.float32
    bf16 = jnp.bfloat16
    zero = jnp.zeros((N_KEYS, LANE), bf16)
    e_step = pl.program_id(1)
    tl = h2_ref.shape[0]
    n_i1 = pu_ref.shape[0] // N_KEYS
    chunk = 2 * LANE if tl % (2 * LANE) == 0 else LANE

    @pl.when(e_step == 0)
    def _():
        acc_ref[...] = jnp.zeros_like(acc_ref)
        a_nxt_ref[...] = jnp.zeros_like(a_nxt_ref)
        rk2_s[...] = rk2_ref[...]
        e2_s[...] = e2_ref[...]

    a_cur_ref[...] = a_nxt_ref[...]
    a_nxt_ref[...] = lax.dot_general(pu_ref[...], h2_ref[...], (((1,), (1,)), ((), ())),
                                     preferred_element_type=f32)
    for c0 in range(0, tl, chunk):
        for l0 in range(c0, c0 + chunk, LANE):
            cols = slice(l0, l0 + LANE)
            for ii in range(n_i1):
                rows = slice(ii * N_KEYS, (ii + 1) * N_KEYS)
                w = jnp.zeros((N_KEYS, LANE), bf16)
                for h in range(PEER_HEADS):
                    sel = rk2_s[h, :, cols] < _rows_bf16(lim_ref[h, ii:ii + 1, cols].astype(f32))
                    w = w + jnp.where(sel, e2_s[h, :, cols], zero) * _rows_bf16(e1_ref[h, ii:ii + 1, cols])
                c_ref[rows, cols] = w * _gelu_tanh(a_cur_ref[rows, cols]).astype(bf16)
        cc = slice(c0, c0 + chunk)
        acc_ref[:, cc] += jnp.dot(pvt_ref[...], c_ref[:, cc], preferred_element_type=f32)

    @pl.when(e_step == pl.num_programs(1) - 1)
    def _():
        o_ref[...] = x_ref[...] + acc_ref[...].T


def _token_tile(T):
    for tl in (640, 512, 384, 256, 128):
        if T % tl == 0:
            return tl
    raise ValueError(f"token count {T} is not a multiple of {LANE}")


def peer_residual(x, g, wqt, sk, pu, pvt):
    T = x.shape[0]
    tl = _token_tile(T)
    nt = T // tl
    n_exp = pu.shape[0]
    head_shape = jax.ShapeDtypeStruct((PEER_HEADS, N_KEYS, T), jnp.int32)
    head_shape_f = jax.ShapeDtypeStruct((PEER_HEADS, N_KEYS, T), jnp.float32)
    head_shape_b = jax.ShapeDtypeStruct((PEER_HEADS, N_KEYS, T), jnp.bfloat16)
    head_spec = pl.BlockSpec((PEER_HEADS, N_KEYS, tl), lambda i: (0, 0, i))
    h2, lim, rk2, e1, e2 = pl.pallas_call(
        _peer_router_kernel,
        grid=(nt,),
        in_specs=[pl.BlockSpec((tl, D_MODEL), lambda i: (i, 0)),
                  pl.BlockSpec((1, D_MODEL), lambda i: (0, 0)),
                  pl.BlockSpec((PEER_HEADS * PEER_QDIM, D_MODEL), lambda i: (0, 0)),
                  pl.BlockSpec((N_HC, N_KEYS, HALF_Q), lambda i: (0, 0, 0))],
        out_specs=[pl.BlockSpec((tl, D_MODEL), lambda i: (i, 0)), head_spec, head_spec, head_spec, head_spec],
        out_shape=[jax.ShapeDtypeStruct((T, D_MODEL), jnp.bfloat16), head_shape, head_shape_b,
                   head_shape_f, head_shape_b],
        scratch_shapes=[pltpu.VMEM((PEER_HEADS * PEER_QDIM, tl), jnp.bfloat16),
                        pltpu.VMEM((N_HC, N_KEYS, tl), jnp.float32),
                        pltpu.VMEM((N_HC, N_KEYS, tl), jnp.int32),
                        pltpu.VMEM((N_HC, PEER_TOPK, tl), jnp.float32)],
        compiler_params=pltpu.CompilerParams(dimension_semantics=("parallel",), vmem_limit_bytes=VMEM_LIMIT),
        name="peer_router",
    )(x, g.reshape(1, D_MODEL), wqt, sk)

    te = PEER_EXPERT_TILE
    n_tiles = n_exp // te
    nxt = lambda e: jnp.minimum(e, n_tiles - 1)
    cur = lambda e: jnp.maximum(e - 1, 0)
    i2_spec = pl.BlockSpec((PEER_HEADS, N_KEYS, tl), lambda i, e: (0, 0, i))
    i1_spec = pl.BlockSpec((PEER_HEADS, te // N_KEYS, tl), lambda i, e: (0, cur(e), i))
    return pl.pallas_call(
        _peer_expert_kernel,
        grid=(nt, n_tiles + 1),
        in_specs=[pl.BlockSpec((tl, D_MODEL), lambda i, e: (i, 0)),
                  pl.BlockSpec((tl, D_MODEL), lambda i, e: (i, 0)),
                  pl.BlockSpec((te, D_MODEL), lambda i, e: (nxt(e), 0)),
                  pl.BlockSpec((D_MODEL, te), lambda i, e: (0, cur(e))),
                  i1_spec, i2_spec, i1_spec, i2_spec],
        out_specs=pl.BlockSpec((tl, D_MODEL), lambda i, e: (i, 0)),
        out_shape=jax.ShapeDtypeStruct((T, D_MODEL), jnp.float32),
        scratch_shapes=[pltpu.VMEM((D_MODEL, tl), jnp.float32),
                        pltpu.VMEM((te, tl), jnp.float32),
                        pltpu.VMEM((te, tl), jnp.float32),
                        pltpu.VMEM((te, tl), jnp.bfloat16),
                        pltpu.VMEM((PEER_HEADS, N_KEYS, tl), jnp.bfloat16),
                        pltpu.VMEM((PEER_HEADS, N_KEYS, tl), jnp.bfloat16)],
        compiler_params=pltpu.CompilerParams(dimension_semantics=("parallel", "arbitrary"),
                                             vmem_limit_bytes=VMEM_LIMIT),
        name="peer_experts",
    )(x, h2, pu, pvt, lim, rk2, e1, e2)


C_QKVB = 3 * WIDTH_A
C_GB = C_QKVB + 3 * WIDTH_B
C_UC = C_GB + WIDTH_B
C_GATE = C_UC + WIDTH_C
C_BA = C_GATE + 3 * D_MODEL
D_IN2 = C_BA + LANE
S5_BLOCK = 256


def _reorder_w_in(w):
    cols = [w[:, OFF_QA:OFF_BETA], w[:, OFF_UC:OFF_GATE], w[:, OFF_GATE:], w[:, OFF_BETA:OFF_UC],
            jnp.zeros((w.shape[0], LANE - 2 * N_HEADS_B), w.dtype)]
    return jnp.concatenate(cols, axis=1).astype(jnp.bfloat16)


def trunk_layer(x, lp, layer_idx, att, prompt, k_past, v_past, conv_hist, delta0, ssm0_re, ssm0_im):
    f32 = jnp.float32
    B, Lx = x.shape[0], x.shape[1]
    T = B * Lx
    z = norm_matmul(x.reshape(T, D_MODEL), lp['norm1'], lp['w_in2'])

    qn, kn, knb, vb = qk_prep(z, lp['q_norm'], lp['k_norm'], _token_tile(T))
    lam_init = 0.8 - 0.6 * math.exp(-0.3 * layer_idx)
    lqk = lp['lambda_qk']
    lam = jnp.exp(jnp.sum(lqk[0] * lqk[1])) - jnp.exp(jnp.sum(lqk[2] * lqk[3])) + lam_init
    scalars = jnp.stack([lam, jnp.asarray(1.0 - lam_init, f32)]).astype(f32)
    r3 = lambda a: a.reshape(B, Lx, WIDTH_A)
    if prompt:
        oA = attn_prompt(r3(qn), r3(knb), vb.reshape(B, Lx, 2 * WIDTH_A), scalars, att['far'], att['tiles'], lp['subln'])
    else:
        P = k_past.shape[1]
        oA = attn_sample(r3(qn), r3(knb), vb.reshape(B, Lx, 2 * WIDTH_A), k_past.reshape(B, P, WIDTH_A), v_past.reshape(B, P, WIDTH_A),
                         scalars, att['bias_past'], att['bias_new'], lp['subln'])

    z3 = z.reshape(B, Lx, D_IN2)
    off = PAD if prompt else 0
    L = Lx - off
    kA = r3(kn)[:, off:].reshape(B, L, N_HEADS_A, 2 * HEAD_DIM_A)
    vA = z3[:, off:, 2 * WIDTH_A:3 * WIDTH_A].reshape(B, L, N_HEADS_A, 2 * HEAD_DIM_A)

    oB, delta, tail = gdn_mixer(z3, (C_QKVB // (3 * WIDTH_B), C_GB // WIDTH_B, C_BA // LANE), conv_hist, delta0,
                                lp['conv_w'], lp['a_log'], lp['dt_bias'], lp['gdn_norm'],
                                GDN_CHUNK if prompt else Lx, off)
    conv_state = tail[:, GDN_TAIL - (CONV_W - 1):]

    oC, ssm_re, ssm_im = s5_glu(z3, C_UC // WIDTH_C, ssm0_re, ssm0_im, lp['s5'], lp['w_glu'],
                                S5_BLOCK if prompt else Lx)

    x2 = merge_residual(x.reshape(T, D_MODEL), oA.reshape(T, WIDTH_A), oB.reshape(T, WIDTH_B),
                        oC.reshape(T, WIDTH_C), z, C_GATE // D_MODEL, lp['wb_a'], lp['wb_b'], lp['wb_c'],
                        lp['w_out'], Lx, off)

    x = peer_residual(x2, lp['norm2'], lp['peer_wqt'], lp['peer_sk'], lp['peer_pu'],
                      lp['peer_pvt']).reshape(B, Lx, D_MODEL)
    return (x, kA, vA, conv_state, delta, ssm_re, ssm_im)


def kernel(x_prompt, x_sample, cache_k, cache_v, state_conv, state_delta, state_ssm_re, state_ssm_im,
           meta_tokens, rel_bias, norm1_g, norm2_g, final_norm_g, w_in, q_norm_g, k_norm_g, lambda_qk,
           subln_g, conv_w, gdn_a_log, gdn_dt_bias, gdn_norm_g, s5_a_re, s5_a_im, s5_b_re, s5_b_im,
           s5_c_re, s5_c_im, s5_d, s5_log_dt, w_glu, w_branch_a, w_branch_b, w_branch_c, w_out,
           peer_wq, peer_subkeys, peer_u, peer_v):
    f32 = jnp.float32
    bf16 = jnp.bfloat16
    params = [dict(norm1=norm1_g[l], norm2=norm2_g[l], w_in2=_reorder_w_in(w_in[l]), q_norm=q_norm_g[l],
                   k_norm=k_norm_g[l], lambda_qk=lambda_qk[l], subln=subln_g[l], conv_w=conv_w[l],
                   a_log=gdn_a_log[l], dt_bias=gdn_dt_bias[l], gdn_norm=gdn_norm_g[l],
                   s5=s5_params(s5_a_re[l], s5_a_im[l], s5_b_re[l], s5_b_im[l], s5_c_re[l], s5_c_im[l], s5_d[l],
                                s5_log_dt[l]),
                   w_glu=w_glu[l], wb_a=w_branch_a[l], wb_b=w_branch_b[l], wb_c=w_branch_c[l], w_out=w_out[l],
                   peer_wqt=peer_wq[l].T.astype(bf16),
                   peer_sk=peer_subkeys[l].reshape(N_HC, N_KEYS, HALF_Q).astype(bf16),
                   peer_pu=peer_u[l].astype(bf16), peer_pvt=peer_v[l].T.astype(bf16))
              for l in range(DEPTH)]

    qpos = jnp.arange(ATT_TQ, dtype=jnp.int32)
    tiles = jnp.stack([t5_bias(qpos, d * ATT_TK + jnp.arange(ATT_TK, dtype=jnp.int32), rel_bias)
                       for d in (-2, -1, 0)], axis=1)
    far = t5_bias(jnp.full((1,), 2 * ATT_TK, jnp.int32), jnp.zeros((1,), jnp.int32), rel_bias).reshape(N_HEADS_A)
    P, Ls = cache_k.shape[2], x_sample.shape[1]
    bias_s = t5_bias(P + jnp.arange(Ls, dtype=jnp.int32), jnp.arange(P + Ls, dtype=jnp.int32), rel_bias)
    att_p = dict(tiles=tiles, far=far)
    att_s = dict(bias_past=bias_s[:, :, :P], bias_new=bias_s[:, :, P:])

    B = x_prompt.shape[0]
    xp = jnp.concatenate([jnp.zeros((B, PAD, D_MODEL), f32),
                          jnp.broadcast_to(meta_tokens[None], (B, N_META, D_MODEL)), x_prompt], axis=1)
    outs_p = [[] for _ in range(6)]
    for l in range(DEPTH):
        res = trunk_layer(
            xp, params[l], l, att_p, True, None, None,
            jnp.zeros((B, CONV_W - 1, 3 * WIDTH_B), f32),
            jnp.zeros((B, N_HEADS_B, HEAD_DIM_B, HEAD_DIM_B), f32),
            jnp.zeros((B, S5_GROUPS, S5_STATE), f32), jnp.zeros((B, S5_GROUPS, S5_STATE), f32))
        xp = res[0]
        for acc, r in zip(outs_p, res[1:]):
            acc.append(r)
    y_prompt = rmsnorm(xp, final_norm_g)[:, FRONT:]

    xs = x_sample
    outs_s = [[] for _ in range(6)]
    for l in range(DEPTH):
        res = trunk_layer(
            xs, params[l], l, att_s, False, cache_k[l], cache_v[l], state_conv[l],
            state_delta[l], state_ssm_re[l], state_ssm_im[l])
        xs = res[0]
        for acc, r in zip(outs_s, res[1:]):
            acc.append(r)
    y_sample = rmsnorm(xs, final_norm_g)

    kp, vp, cp, dp, srp, sip = [jnp.stack(a) for a in outs_p]
    ks_, vs_, cs_, ds_, srs, sis = [jnp.stack(a) for a in outs_s]
    return (y_prompt, y_sample, kp, vp, ks_, vs_, cp, cs_, dp, ds_, srp, sip, srs, sis)
```

```python
import functools
import math

import jax
import jax.numpy as jnp
from jax import lax
from jax.experimental import pallas as pl
from jax.experimental.pallas import tpu as pltpu

D_MODEL = 1024
DEPTH = 4
CHUNK = 64
N_META = 16
EPS = 1e-6
NEG_INF = -1e30
N_HEADS_A = 4
HEAD_DIM_A = 64
NUM_BUCKETS = 32
MAX_DISTANCE = 128
N_HEADS_B = 4
HEAD_DIM_B = 128
CONV_W = 4
S5_GROUP = 16
S5_GROUPS = 32
S5_STATE = 64
PEER_HEADS = 8
PEER_QDIM = 256
N_KEYS = 128
PEER_TOPK = 16

WIDTH_A = N_HEADS_A * 2 * HEAD_DIM_A
WIDTH_B = N_HEADS_B * HEAD_DIM_B
WIDTH_C = S5_GROUPS * S5_GROUP
OFF_QA = 0
OFF_KA = OFF_QA + WIDTH_A
OFF_VA = OFF_KA + WIDTH_A
OFF_QKVB = OFF_VA + WIDTH_A
OFF_GB = OFF_QKVB + 3 * WIDTH_B
OFF_BETA = OFF_GB + WIDTH_B
OFF_ALPHA = OFF_BETA + N_HEADS_B
OFF_UC = OFF_ALPHA + N_HEADS_B
OFF_GATE = OFF_UC + WIDTH_C
D_IN = OFF_GATE + 3 * D_MODEL

LANE = 128
VMEM_LIMIT = 56 * 1024 * 1024
WEIGHT_TILE_BYTES = 6 * 1024 * 1024
HIGHEST = lax.Precision.HIGHEST


def _norm_mm_kernel(x_ref, g_ref, w_ref, o_ref, h_ref):
    @pl.when(pl.program_id(1) == 0)
    def _():
        x = x_ref[...]
        h = x * lax.rsqrt(jnp.mean(x * x, axis=-1, keepdims=True) + EPS) * g_ref[...]
        h_ref[...] = h.astype(jnp.bfloat16)

    o_ref[...] = jnp.dot(h_ref[...], w_ref[...], preferred_element_type=jnp.float32)


def norm_matmul(x, g, wb, tm=512):
    M, K = x.shape
    N = wb.shape[1]
    n_lanes = N // LANE
    tn = LANE * max(d for d in range(1, n_lanes + 1) if n_lanes % d == 0 and d * LANE * K * 2 <= WEIGHT_TILE_BYTES)
    tm = min(tm, M)
    return pl.pallas_call(
        _norm_mm_kernel,
        grid=(pl.cdiv(M, tm), N // tn),
        in_specs=[pl.BlockSpec((tm, K), lambda i, j: (i, 0)),
                  pl.BlockSpec((1, K), lambda i, j: (0, 0)),
                  pl.BlockSpec((K, tn), lambda i, j: (0, j))],
        out_specs=pl.BlockSpec((tm, tn), lambda i, j: (i, j)),
        out_shape=jax.ShapeDtypeStruct((M, N), jnp.float32),
        scratch_shapes=[pltpu.VMEM((tm, K), jnp.bfloat16)],
        compiler_params=pltpu.CompilerParams(dimension_semantics=("parallel", "arbitrary"),
                                             vmem_limit_bytes=VMEM_LIMIT),
        name="norm_proj",
    )(x, g.reshape(1, K), wb)


def rmsnorm(x, g):
    xf = x.astype(jnp.float32)
    y = xf * lax.rsqrt(jnp.mean(xf * xf, axis=-1, keepdims=True) + EPS)
    return (y * g.astype(jnp.float32)).astype(x.dtype)


def t5_bias(q_pos, k_pos, table):
    rel = k_pos[None, :] - q_pos[:, None]
    half = NUM_BUCKETS // 2
    exact = half // 2
    n = jnp.abs(rel)
    nf = jnp.maximum(n, 1).astype(jnp.float32)
    far = exact + (jnp.log(nf / exact) / math.log(MAX_DISTANCE / exact) * (half - exact)).astype(jnp.int32)
    bucket = jnp.where(rel > 0, half, 0) + jnp.where(n < exact, n, jnp.minimum(far, half - 1))
    return jnp.moveaxis(table[bucket].astype(jnp.float32), -1, 0)


FRONT = 256
PAD = FRONT - N_META
ATT_TQ = 256
ATT_TK = 256


def _qk_prep_kernel(q_ref, k_ref, v_ref, gq_ref, gk_ref, seg_ref, qn_ref, kn_ref, knb_ref, vb_ref):
    f32 = jnp.float32
    bf16 = jnp.bfloat16
    seg = seg_ref[...]
    hd = 2 * HEAD_DIM_A

    def norm(x, g):
        ms = jnp.dot(x * x, seg, precision=HIGHEST, preferred_element_type=f32)
        return x * lax.rsqrt(ms + EPS) * g

    qn = (norm(q_ref[...], gq_ref[...]) * (HEAD_DIM_A ** -0.5)).astype(bf16)
    kn = norm(k_ref[...], gk_ref[...])
    kn_ref[...] = kn
    knb = kn.astype(bf16)
    v = v_ref[...].astype(bf16)
    lane = lax.broadcasted_iota(jnp.int32, (v.shape[0], hd), 1)
    ones_col = jnp.where(lane == 0, 1.0, 0.0).astype(bf16)
    for h in range(N_HEADS_A):
        cols = slice(h * hd, (h + 1) * hd)
        qn_ref[h] = qn[:, cols]
        knb_ref[h] = knb[:, cols]
        vb_ref[h] = jnp.concatenate([v[:, cols], ones_col], axis=1)


def qk_prep(z, gq, gk, tm):
    T = z.shape[0]
    hd = 2 * HEAD_DIM_A
    seg = jnp.kron(jnp.eye(WIDTH_A // HEAD_DIM_A, dtype=jnp.float32),
                   jnp.full((HEAD_DIM_A, HEAD_DIM_A), 1.0 / HEAD_DIM_A, jnp.float32))
    row = lambda j: pl.BlockSpec((tm, WIDTH_A), lambda i, j=j: (i, j))
    const = lambda shape: pl.BlockSpec(shape, lambda i: (0,) * len(shape))
    heads = lambda w: pl.BlockSpec((N_HEADS_A, tm, w), lambda i: (0, i, 0))
    hshape = lambda w: jax.ShapeDtypeStruct((N_HEADS_A, T, w), jnp.bfloat16)
    return pl.pallas_call(
        _qk_prep_kernel,
        grid=(T // tm,),
        in_specs=[row(0), row(1), row(2), const((1, WIDTH_A)), const((1, WIDTH_A)), const((WIDTH_A, WIDTH_A))],
        out_specs=[heads(hd), pl.BlockSpec((tm, WIDTH_A), lambda i: (i, 0)), heads(hd), heads(2 * hd)],
        out_shape=[hshape(hd), jax.ShapeDtypeStruct((T, WIDTH_A), jnp.float32), hshape(hd), hshape(2 * hd)],
        compiler_params=pltpu.CompilerParams(dimension_semantics=("parallel",)),
        name="qk_prep",
    )(z, z, z, jnp.tile(gq, WIDTH_A // HEAD_DIM_A).reshape(1, WIDTH_A),
      jnp.tile(gk, WIDTH_A // HEAD_DIM_A).reshape(1, WIDTH_A), seg)


def _split_maps(q):
    lane = lax.broadcasted_iota(jnp.int32, q.shape, 1)
    zero = jnp.zeros_like(q)
    return jnp.where(lane < HEAD_DIM_A, q, zero), jnp.where(lane >= HEAD_DIM_A, q, zero)


def _subln(o, g, scale):
    return o * lax.rsqrt(jnp.mean(o * o, axis=-1, keepdims=True) + EPS) * g * scale


def _attn_prompt_kernel(sc_ref, far_ref, q_ref, k_ref, v_ref, bias_ref, g_ref, o_ref, m_ref, acc_ref,
                        sa_ref, sb_ref):
    f32 = jnp.float32
    h = pl.program_id(1)
    qi = pl.program_id(2)
    tq, tk = ATT_TQ, ATT_TK
    hd = 2 * HEAD_DIM_A
    qs = jnp.concatenate(_split_maps(q_ref[0, 0]), axis=0)
    m_ref[...] = jnp.full(m_ref.shape, NEG_INF, f32)
    acc_ref[...] = jnp.zeros(acc_ref.shape, f32)
    far = far_ref[h]

    def logits(k0, width, general):
        k0 = pl.multiple_of(k0, tk)
        kt = k_ref[0, 0, pl.ds(k0, width), :]
        s = lax.dot_general(qs, kt, (((1,), (1,)), ((), ())), preferred_element_type=f32)
        if general:
            kj = k0 // tk
            bias = bias_ref[0, jnp.clip(kj - qi + 2, 0, 2)]
            qpos = qi * tq + lax.broadcasted_iota(jnp.int32, (tq, tk), 0)
            kpos = k0 + lax.broadcasted_iota(jnp.int32, (tq, tk), 1)
            qchunk = jnp.where(qpos < FRONT, 0, 1 + jnp.right_shift(qpos - FRONT, 6))
            kchunk = jnp.where(kpos < FRONT, 0, 1 + jnp.right_shift(kpos - FRONT, 6))
            mask = (kpos >= PAD) & (kchunk <= qchunk)
            s = jnp.where(jnp.concatenate([mask, mask], axis=0), s + jnp.concatenate([bias, bias], axis=0), NEG_INF)
        else:
            s = s + far
        return s

    def accumulate(s, k0):
        width = s.shape[1]
        vt = v_ref[0, 0, pl.ds(pl.multiple_of(k0, tk), width), :]
        n_lane_tiles = width // LANE
        smax = s[:, :LANE]
        for j in range(1, n_lane_tiles):
            smax = jnp.maximum(smax, s[:, j * LANE:(j + 1) * LANE])
        m_old = m_ref[...]
        m_new = jnp.maximum(m_old, jnp.broadcast_to(jnp.max(smax, axis=1, keepdims=True), m_old.shape))
        alpha = jnp.exp(m_old - m_new)
        p = jnp.exp(s - jnp.concatenate([m_new] * n_lane_tiles, axis=1))
        acc_ref[...] = (jnp.concatenate([alpha, alpha], axis=1) * acc_ref[...]
                        + jnp.dot(p.astype(jnp.bfloat16), vt, preferred_element_type=f32))
        m_ref[...] = m_new

    def tile(k0, width, general):
        accumulate(logits(k0, width, general), k0)

    tile(0, tk, True)

    wide = 2 * tk
    n_far = jnp.maximum(qi - 2, 0)
    n_single = n_far % 4
    for i in range(3):
        @pl.when(n_single > i)
        def _(i=i):
            tile((1 + i) * tk, tk, False)

    base = (1 + n_single) * tk
    n_pairs = n_far // 4

    @pl.when(n_pairs > 0)
    def _():
        sa_ref[...] = logits(base, wide, False)

    def pair_body(j, carry):
        k0 = base + 2 * j * wide
        sb_ref[...] = logits(k0 + wide, wide, False)
        accumulate(sa_ref[...], k0)
        nxt = jnp.minimum(k0 + 2 * wide, base + (2 * n_pairs - 1) * wide)
        sa_ref[...] = logits(nxt, wide, False)
        accumulate(sb_ref[...], k0 + wide)
        return carry

    lax.fori_loop(0, n_pairs, pair_body, 0)

    @pl.when(qi >= 2)
    def _():
        tile((qi - 1) * tk, tk, True)

    @pl.when(qi >= 1)
    def _():
        tile(qi * tk, tk, True)

    acc = acc_ref[...]
    out = acc[:, :hd] / acc[:, hd:hd + 1]
    o = out[:tq] - sc_ref[0] * out[tq:]
    o_ref[0] = _subln(o, g_ref[...], sc_ref[1])


def attn_prompt(qn, knb, vb, scalars, far, bias_tiles, subln_g):
    _, B, Lp, _ = qn.shape
    hd = 2 * HEAD_DIM_A
    smem = pl.BlockSpec(memory_space=pltpu.SMEM)
    return pl.pallas_call(
        _attn_prompt_kernel,
        grid=(B, N_HEADS_A, Lp // ATT_TQ),
        in_specs=[smem, smem,
                  pl.BlockSpec((1, 1, ATT_TQ, hd), lambda b, h, i: (h, b, i, 0)),
                  pl.BlockSpec((1, 1, Lp, hd), lambda b, h, i: (h, b, 0, 0)),
                  pl.BlockSpec((1, 1, Lp, 2 * hd), lambda b, h, i: (h, b, 0, 0)),
                  pl.BlockSpec((1, 3, ATT_TQ, ATT_TK), lambda b, h, i: (h, 0, 0, 0)),
                  pl.BlockSpec((1, hd), lambda b, h, i: (0, 0))],
        out_specs=pl.BlockSpec((1, ATT_TQ, hd), lambda b, h, i: (b, i, h)),
        out_shape=jax.ShapeDtypeStruct((B, Lp, WIDTH_A), jnp.float32),
        scratch_shapes=[pltpu.VMEM((2 * ATT_TQ, hd), jnp.float32), pltpu.VMEM((2 * ATT_TQ, 2 * hd), jnp.float32),
                        pltpu.VMEM((2 * ATT_TQ, 2 * ATT_TK), jnp.float32),
                        pltpu.VMEM((2 * ATT_TQ, 2 * ATT_TK), jnp.float32)],
        compiler_params=pltpu.CompilerParams(dimension_semantics=("parallel", "parallel", "parallel"),
                                             vmem_limit_bytes=VMEM_LIMIT),
        name="diff_attn_prompt",
    )(scalars, far, qn, knb, vb, bias_tiles, subln_g.reshape(1, hd))


def _attn_sample_kernel(sc_ref, q_ref, kp_ref, vp_ref, kn_ref, vn_ref, bp_ref, bn_ref, g_ref, o_ref):
    f32 = jnp.float32
    bf16 = jnp.bfloat16
    qs = _split_maps(q_ref[0, 0])
    kp = kp_ref[0].astype(bf16)
    vp = vp_ref[0].astype(bf16)
    kn = kn_ref[0, 0]
    vn = vn_ref[0, 0][:, :2 * HEAD_DIM_A]
    dn = (((1,), (1,)), ((), ()))
    outs = []
    for c in range(2):
        sp = lax.dot_general(qs[c], kp, dn, preferred_element_type=f32) + bp_ref[0]
        sn = lax.dot_general(qs[c], kn, dn, preferred_element_type=f32) + bn_ref[0]
        m = jnp.maximum(jnp.max(sp, axis=1, keepdims=True), jnp.max(sn, axis=1, keepdims=True))
        pp = jnp.exp(sp - m)
        pn = jnp.exp(sn - m)
        l = jnp.sum(pp, axis=1, keepdims=True) + jnp.sum(pn, axis=1, keepdims=True)
        acc = (jnp.dot(pp.astype(bf16), vp, preferred_element_type=f32)
               + jnp.dot(pn.astype(bf16), vn, preferred_element_type=f32))
        outs.append(acc / l)
    o_ref[0] = _subln(outs[0] - sc_ref[0] * outs[1], g_ref[...], sc_ref[1])


def attn_sample(qn, knb, vb, k_past, v_past, scalars, bias_past, bias_new, subln_g):
    _, B, L, _ = qn.shape
    P = k_past.shape[1]
    hd = 2 * HEAD_DIM_A
    new = pl.BlockSpec((1, L, hd), lambda b, h: (b, 0, h))
    new_h = lambda w: pl.BlockSpec((1, 1, L, w), lambda b, h: (h, b, 0, 0))
    past = pl.BlockSpec((1, P, hd), lambda b, h: (b, 0, h))
    return pl.pallas_call(
        _attn_sample_kernel,
        grid=(B, N_HEADS_A),
        in_specs=[pl.BlockSpec(memory_space=pltpu.SMEM), new_h(hd), past, past, new_h(hd), new_h(2 * hd),
                  pl.BlockSpec((1, L, P), lambda b, h: (h, 0, 0)),
                  pl.BlockSpec((1, L, L), lambda b, h: (h, 0, 0)),
                  pl.BlockSpec((1, hd), lambda b, h: (0, 0))],
        out_specs=new,
        out_shape=jax.ShapeDtypeStruct((B, L, WIDTH_A), jnp.float32),
        compiler_params=pltpu.CompilerParams(dimension_semantics=("parallel", "parallel")),
        name="diff_attn_sample",
    )(scalars, qn, k_past, v_past, knb, vb, bias_past, bias_new, subln_g.reshape(1, hd))


S5_N = S5_GROUPS * S5_STATE
S5_SLAB = 8
S5_LANES = 512


def _sigmoid(x):
    return 1.0 / (1.0 + jnp.exp(-x))


def _s5_kernel(u_ref, x0_ref, bmat_ref, cmat_ref, lamp_ref, ppow_ref, d_ref, wglu_ref, oc_ref, xf_ref,
               bu_ref, carry_ref):
    f32 = jnp.float32
    bf16 = jnp.bfloat16
    t = pl.program_id(1)
    tb = u_ref.shape[1]

    @pl.when(t == 0)
    def _():
        carry_ref[...] = x0_ref[0]

    u = u_ref[0]
    bu_ref[...] = jnp.dot(u.astype(bf16), bmat_ref[...], preferred_element_type=f32)

    row = lax.broadcasted_iota(jnp.int32, (S5_SLAB, S5_LANES), 0)
    for c in range(S5_N // S5_LANES):
        re = slice(c * S5_LANES, (c + 1) * S5_LANES)
        im = slice(S5_N + c * S5_LANES, S5_N + (c + 1) * S5_LANES)

        def slab(i, carry):
            cre, cim = carry
            rows = pl.ds(pl.multiple_of(i * S5_SLAB, S5_SLAB), S5_SLAB)
            yre = bu_ref[rows, re]
            yim = bu_ref[rows, im]
            for s in range(3):
                sh = 1 << s
                sre = jnp.where(row >= sh, pltpu.roll(yre, sh, 0), 0.0)
                sim = jnp.where(row >= sh, pltpu.roll(yim, sh, 0), 0.0)
                lr = lamp_ref[s, 0, :, re]
                li = lamp_ref[s, 1, :, re]
                yre, yim = yre + (lr * sre - li * sim), yim + (lr * sim + li * sre)
            pr = ppow_ref[0, :, re]
            pi = ppow_ref[1, :, re]
            yre, yim = yre + (pr * cre - pi * cim), yim + (pr * cim + pi * cre)
            bu_ref[rows, re] = yre
            bu_ref[rows, im] = yim
            last = S5_SLAB - 1
            return (jnp.broadcast_to(yre[last:last + 1, :], yre.shape),
                    jnp.broadcast_to(yim[last:last + 1, :], yim.shape))

        cre, cim = lax.fori_loop(0, tb // S5_SLAB, slab, (carry_ref[:, re], carry_ref[:, im]))
        carry_ref[:, re] = cre
        carry_ref[:, im] = cim

    y = jnp.dot(bu_ref[...].astype(bf16), cmat_ref[...], preferred_element_type=f32) + d_ref[...] * u
    gl = jnp.dot(_gelu_tanh(y).astype(bf16), wglu_ref[...], preferred_element_type=f32)
    oc_ref[0] = gl[:, :WIDTH_C] * _sigmoid(gl[:, WIDTH_C:])

    @pl.when(t == pl.num_programs(1) - 1)
    def _():
        xf_ref[0] = carry_ref[...]


def s5_params(a_re, a_im, b_re, b_im, c_re, c_im, d, log_dt):
    f32 = jnp.float32
    lam = lax.complex(a_re, a_im)
    lam_bar = jnp.exp(lam * jnp.exp(log_dt)[:, None])
    b_bar = ((lam_bar - 1.0) / lam)[..., None] * lax.complex(b_re, b_im)
    eye = jnp.eye(S5_GROUPS, dtype=f32)
    bd_in = lambda m: jnp.einsum('gpi,gh->gihp', m, eye).reshape(WIDTH_C, S5_N)
    bd_out = lambda m: jnp.einsum('gip,gh->gphi', m, eye).reshape(S5_N, WIDTH_C)
    bmat = jnp.concatenate([bd_in(b_bar.real), bd_in(b_bar.imag)], axis=1).astype(jnp.bfloat16)
    cmat = jnp.concatenate([bd_out(c_re), bd_out(-c_im)], axis=0).astype(jnp.bfloat16)
    lb = lam_bar.reshape(S5_N)
    rep = lambda v: jnp.broadcast_to(v[None, :], (S5_SLAB, S5_N))
    pows = [lb, lb * lb, (lb * lb) * (lb * lb)]
    lamp = jnp.stack([jnp.stack([rep(p.real), rep(p.imag)]) for p in pows])
    run = [lb]
    for _ in range(S5_SLAB - 1):
        run.append(run[-1] * lb)
    pp = jnp.stack(run)
    ppow = jnp.stack([pp.real, pp.imag])
    return dict(bmat=bmat, cmat=cmat, lamp=lamp.astype(f32), ppow=ppow.astype(f32), d=d.reshape(1, WIDTH_C))


def s5_glu(z3, col_block, x0_re, x0_im, sp, w_glu, tb):
    B, Lx, _ = z3.shape
    x0 = jnp.concatenate([x0_re.reshape(B, S5_N), x0_im.reshape(B, S5_N)], axis=1)
    x0 = jnp.broadcast_to(x0[:, None, :], (B, S5_SLAB, 2 * S5_N))
    const = lambda a: pl.BlockSpec(a.shape, lambda b, t: (0,) * a.ndim)
    wg = w_glu.astype(jnp.bfloat16)
    oc, xf = pl.pallas_call(
        _s5_kernel,
        grid=(B, Lx // tb),
        in_specs=[pl.BlockSpec((1, tb, WIDTH_C), lambda b, t: (b, t, col_block)),
                  pl.BlockSpec((1, S5_SLAB, 2 * S5_N), lambda b, t: (b, 0, 0)),
                  const(sp['bmat']), const(sp['cmat']), const(sp['lamp']), const(sp['ppow']), const(sp['d']),
                  const(wg)],
        out_specs=[pl.BlockSpec((1, tb, WIDTH_C), lambda b, t: (b, t, 0)),
                   pl.BlockSpec((1, S5_SLAB, 2 * S5_N), lambda b, t: (b, 0, 0))],
        out_shape=[jax.ShapeDtypeStruct((B, Lx, WIDTH_C), jnp.float32),
                   jax.ShapeDtypeStruct((B, S5_SLAB, 2 * S5_N), jnp.float32)],
        scratch_shapes=[pltpu.VMEM((tb, 2 * S5_N), jnp.float32), pltpu.VMEM((S5_SLAB, 2 * S5_N), jnp.float32)],
        compiler_params=pltpu.CompilerParams(dimension_semantics=("parallel", "arbitrary"),
                                             vmem_limit_bytes=VMEM_LIMIT),
        name="s5_glu",
    )(z3, x0, sp['bmat'], sp['cmat'], sp['lamp'], sp['ppow'], sp['d'], wg)
    shape = (B, S5_GROUPS, S5_STATE)
    return oc, xf[:, 0, :S5_N].reshape(shape), xf[:, 0, S5_N:].reshape(shape)


def _gdn_kernel(qkv_ref, gate_ref, ba_ref, hist_ref, s0_ref, cw_ref, av_ref, dt_ref, gn_ref,
                ob_ref, sout_ref, tail_ref, s_ref, prev_ref, *, pad_rows):
    f32 = jnp.float32
    t = pl.program_id(1)
    C = qkv_ref.shape[1]
    hd = HEAD_DIM_B
    dn_last = (((1,), (1,)), ((), ()))
    dn_first = (((0,), (0,)), ((), ()))
    dot = lambda a, b: _dot3(a, b, (((1,), (0,)), ((), ())))

    @pl.when(t == 0)
    def _():
        s_ref[...] = s0_ref[0]
        prev_ref[...] = hist_ref[0]

    cur = qkv_ref[0]
    tail = prev_ref.shape[0]
    ext = jnp.concatenate([prev_ref[...], cur], axis=0)
    w = cw_ref[...]
    conv = cur * w[CONV_W - 1:CONV_W, :]
    for i in range(CONV_W - 1):
        lo = tail - (CONV_W - 1) + i
        conv = conv + ext[lo:lo + C, :] * w[i:i + 1, :]
    prev_ref[...] = cur[C - tail:, :]
    c = conv * _sigmoid(conv)

    ba = ba_ref[0]
    xg = ba + dt_ref[...]
    softplus = jnp.maximum(xg, 0.0) + jnp.log(1.0 + jnp.exp(-jnp.abs(xg)))
    g_all = -jnp.exp(av_ref[...]) * softplus
    if pad_rows:
        grow = t * C + lax.broadcasted_iota(jnp.int32, g_all.shape, 0)
        g_all = jnp.where(grow >= pad_rows, g_all, 0.0)
    ri = lax.broadcasted_iota(jnp.int32, (C, C), 0)
    ci = lax.broadcasted_iota(jnp.int32, (C, C), 1)
    tri = ri >= ci
    strict = ri > ci
    tril = jnp.where(tri, 1.0, 0.0).astype(f32)
    G_col = jnp.dot(tril, g_all, precision=HIGHEST, preferred_element_type=f32)
    G_row = lax.dot_general(g_all, tril, (((0,), (1,)), ((), ())), precision=HIGHEST,
                            preferred_element_type=f32)
    eye = jnp.where(ri == ci, 1.0, 0.0).astype(f32)

    heads = range(N_HEADS_B)
    dot_t = lambda a, b: _dot3(a, b, dn_last)
    Gc = [G_col[:, N_HEADS_B + h:N_HEADS_B + h + 1] for h in heads]
    Gr = [G_row[N_HEADS_B + h:N_HEADS_B + h + 1, :] for h in heads]
    decay = [jnp.where(tri, jnp.exp(jnp.minimum(Gc[h] - Gr[h], 0.0)), 0.0) for h in heads]
    beta = [_sigmoid(ba[:, h:h + 1]) for h in heads]
    unit = lambda a: a * lax.rsqrt(jnp.sum(a * a, axis=-1, keepdims=True) + EPS)
    q = [unit(c[:, h * hd:(h + 1) * hd]) * (hd ** -0.5) for h in heads]
    k = [unit(c[:, WIDTH_B + h * hd:WIDTH_B + (h + 1) * hd]) for h in heads]
    kb = [k[h] * beta[h] for h in heads]
    eG = [jnp.exp(Gc[h]) for h in heads]
    rhs = [jnp.concatenate([c[:, 2 * WIDTH_B + h * hd:2 * WIDTH_B + (h + 1) * hd] * beta[h], kb[h] * eG[h]], axis=1)
           for h in heads]
    A = [jnp.where(strict, dot_t(kb[h], k[h]) * decay[h], 0.0) for h in heads]
    same_block = lambda b: jnp.right_shift(ri, b.bit_length() - 1) == jnp.right_shift(ci, b.bit_length() - 1)
    in_base = same_block(GDN_BASE)
    D = [jnp.where(in_base, A[h], 0.0) for h in heads]
    D2 = [dot(D[h], D[h]) for h in heads]
    D4 = [dot(D2[h], D2[h]) for h in heads]
    tm = [eye - D[h] for h in heads]
    tm = [tm[h] + dot(tm[h], D2[h]) for h in heads]
    tm = [tm[h] + dot(tm[h], D4[h]) for h in heads]
    b = GDN_BASE
    while b < C:
        level = same_block(2 * b) & jnp.logical_not(same_block(b))
        me = [dot(tm[h], jnp.where(level, A[h], 0.0)) for h in heads]
        tm = [tm[h] - dot(me[h], tm[h]) for h in heads]
        b *= 2
    sol = [dot(tm[h], rhs[h]) for h in heads]
    attn = [dot_t(q[h], k[h]) * decay[h] for h in heads]
    S = [s_ref[h] for h in heads]
    v_new = [sol[h][:, :hd] - dot(sol[h][:, hd:], S[h]) for h in heads]
    o = [dot(q[h] * eG[h], S[h]) + dot(attn[h], v_new[h]) for h in heads]
    for h in heads:
        GL = Gc[h][C - 1:C, :]
        s_ref[h] = S[h] * jnp.exp(GL) + _dot3(k[h] * jnp.exp(GL - Gc[h]), v_new[h], dn_first)
        gate = gate_ref[0, :, h * hd:(h + 1) * hd]
        on = o[h] * lax.rsqrt(jnp.mean(o[h] * o[h], axis=-1, keepdims=True) + EPS) * gn_ref[...]
        ob_ref[0, :, h * hd:(h + 1) * hd] = on * (gate * _sigmoid(gate))

    @pl.when(t == pl.num_programs(1) - 1)
    def _():
        sout_ref[0] = s_ref[...]
        tail_ref[0] = cur[C - tail:, :]


GDN_TAIL = 8
GDN_CHUNK = 128
GDN_BASE = 8


def _dot3(a, b, dims):
    f32 = jnp.float32
    bf16 = jnp.bfloat16
    a_hi = a.astype(bf16)
    b_hi = b.astype(bf16)
    a_lo = (a - a_hi.astype(f32)).astype(bf16)
    b_lo = (b - b_hi.astype(f32)).astype(bf16)
    dg = lambda x, y: lax.dot_general(x, y, dims, preferred_element_type=f32)
    return dg(a_hi, b_hi) + (dg(a_hi, b_lo) + dg(a_lo, b_hi))


def gdn_mixer(z3, blocks, conv_hist, delta0, conv_w, a_log, dt_bias, gdn_norm, chunk, pad_rows):
    B, Lx, _ = z3.shape
    f32 = jnp.float32
    wq = 3 * WIDTH_B
    hist = jnp.pad(conv_hist, ((0, 0), (GDN_TAIL - (CONV_W - 1), 0), (0, 0)))
    lane_vec = lambda v: jnp.zeros((1, LANE), f32).at[0, N_HEADS_B:2 * N_HEADS_B].set(v)
    const = lambda shape: pl.BlockSpec(shape, lambda b, t: (0,) * len(shape))
    per_b = lambda shape: pl.BlockSpec((1,) + shape, lambda b, t: (b,) + (0,) * len(shape))
    qb, gb, bb = blocks
    return pl.pallas_call(
        functools.partial(_gdn_kernel, pad_rows=pad_rows),
        grid=(B, Lx // chunk),
        in_specs=[pl.BlockSpec((1, chunk, wq), lambda b, t: (b, t, qb)),
                  pl.BlockSpec((1, chunk, WIDTH_B), lambda b, t: (b, t, gb)),
                  pl.BlockSpec((1, chunk, LANE), lambda b, t: (b, t, bb)),
                  per_b((GDN_TAIL, wq)), per_b((N_HEADS_B, HEAD_DIM_B, HEAD_DIM_B)),
                  const((CONV_W, wq)), const((1, LANE)), const((1, LANE)), const((1, HEAD_DIM_B))],
        out_specs=[pl.BlockSpec((1, chunk, WIDTH_B), lambda b, t: (b, t, 0)),
                   per_b((N_HEADS_B, HEAD_DIM_B, HEAD_DIM_B)), per_b((GDN_TAIL, wq))],
        out_shape=[jax.ShapeDtypeStruct((B, Lx, WIDTH_B), f32),
                   jax.ShapeDtypeStruct((B, N_HEADS_B, HEAD_DIM_B, HEAD_DIM_B), f32),
                   jax.ShapeDtypeStruct((B, GDN_TAIL, wq), f32)],
        scratch_shapes=[pltpu.VMEM((N_HEADS_B, HEAD_DIM_B, HEAD_DIM_B), f32), pltpu.VMEM((GDN_TAIL, wq), f32)],
        compiler_params=pltpu.CompilerParams(dimension_semantics=("parallel", "arbitrary")),
        name="gdn_mixer",
    )(z3, z3, z3, hist, delta0, conv_w, lane_vec(a_log), lane_vec(dt_bias), gdn_norm.reshape(1, HEAD_DIM_B))


def _merge_kernel(x_ref, oa_ref, ob_ref, oc_ref, g0_ref, g1_ref, g2_ref, wa_ref, wb_ref, wc_ref, wo_ref, o_ref,
                  *, rows_per_seq, pad_rows):
    f32 = jnp.float32
    bf16 = jnp.bfloat16
    mm = lambda a, w_ref: jnp.dot(a.astype(bf16), w_ref[...], preferred_element_type=f32)
    merged = (_sigmoid(g0_ref[...]) * mm(oa_ref[...], wa_ref) + _sigmoid(g1_ref[...]) * mm(ob_ref[...], wb_ref)
              + _sigmoid(g2_ref[...]) * mm(oc_ref[...], wc_ref))
    x = x_ref[...] + mm(merged, wo_ref)
    if pad_rows:
        tm = x.shape[0]
        row = pl.program_id(0) * tm + lax.broadcasted_iota(jnp.int32, x.shape, 0)
        x = jnp.where(row % rows_per_seq >= pad_rows, x, 0.0)
    o_ref[...] = x


def merge_residual(x, oA, oB, oC, z, gate_block, wa, wb, wc, wo, rows_per_seq, pad_rows):
    T = x.shape[0]
    tm = _token_tile(T)
    bf16 = jnp.bfloat16
    rows = lambda w, j=0: pl.BlockSpec((tm, w), lambda i, j=j: (i, j))
    const = lambda a: pl.BlockSpec(a.shape, lambda i: (0, 0))
    ws = [w.astype(bf16) for w in (wa, wb, wc, wo)]
    return pl.pallas_call(
        functools.partial(_merge_kernel, rows_per_seq=rows_per_seq, pad_rows=pad_rows),
        grid=(T // tm,),
        in_specs=[rows(D_MODEL), rows(WIDTH_A), rows(WIDTH_B), rows(WIDTH_C),
                  rows(D_MODEL, gate_block), rows(D_MODEL, gate_block + 1), rows(D_MODEL, gate_block + 2)]
                 + [const(w) for w in ws],
        out_specs=rows(D_MODEL),
        out_shape=jax.ShapeDtypeStruct((T, D_MODEL), jnp.float32),
        compiler_params=pltpu.CompilerParams(dimension_semantics=("parallel",), vmem_limit_bytes=VMEM_LIMIT),
        name="merge_residual",
    )(x, oA, oB, oC, z, z, z, *ws)


N_HC = 2 * PEER_HEADS
HALF_Q = PEER_QDIM // 2
PEER_EXPERT_TILE = 8 * N_KEYS
CAND_SUB = 8


def _extract_top16(s, iota_f):
    n = float(s.shape[0])
    rank = jnp.full(s.shape, PEER_TOPK, jnp.int32)
    vals = []
    for j in range(PEER_TOPK):
        m = jnp.max(s, axis=0, keepdims=True)
        idx = jnp.min(jnp.where(s == m, iota_f, n), axis=0, keepdims=True)
        hit = iota_f == idx
        rank = jnp.where(hit, j, rank)
        s = jnp.where(hit, -jnp.inf, s)
        vals.append(m)
    return vals, rank


def _peer_router_kernel(x_ref, g_ref, wqt_ref, sk_ref, h2_ref, lim_ref, rk2_ref, e1_ref, e2_ref,
                        qt_ref, s_ref, rank_ref, v_ref):
    f32 = jnp.float32
    tl = x_ref.shape[0]
    x = x_ref[...]
    h2 = x * lax.rsqrt(jnp.mean(x * x, axis=-1, keepdims=True) + EPS) * g_ref[...]
    h2b = h2.astype(jnp.bfloat16)
    h2_ref[...] = h2b
    qt_ref[...] = lax.dot_general(wqt_ref[...], h2b, (((1,), (1,)), ((), ())),
                                  preferred_element_type=f32).astype(jnp.bfloat16)

    def score_body(hc, carry):
        r0 = pl.multiple_of(hc * HALF_Q, HALF_Q)
        s_ref[hc] = jnp.dot(sk_ref[hc], qt_ref[pl.ds(r0, HALF_Q), :], preferred_element_type=f32)
        return carry

    lax.fori_loop(0, N_HC, score_body, 0)

    key_iota = lax.broadcasted_iota(jnp.int32, (N_KEYS, LANE), 0).astype(f32)

    def key_body(i, carry):
        hc = i // (tl // LANE)
        c0 = pl.multiple_of((i % (tl // LANE)) * LANE, LANE)
        s = s_ref[hc, :, pl.ds(c0, LANE)]
        vals, rank = _extract_top16(s, key_iota)
        rank_ref[hc, :, pl.ds(c0, LANE)] = rank
        v_ref[hc, :, pl.ds(c0, LANE)] = jnp.concatenate(vals, axis=0)
        return carry

    lax.fori_loop(0, N_HC * (tl // LANE), key_body, 0, unroll=2)

    sub = CAND_SUB
    n_mid = sub - 1
    n_cand = PEER_TOPK + n_mid * sub + (PEER_TOPK - sub)
    cand_iota = lax.broadcasted_iota(jnp.int32, (n_cand, LANE), 0).astype(f32)
    row8 = lax.broadcasted_iota(jnp.int32, (sub, LANE), 0)

    def head_body(i, carry):
        h = i // (tl // LANE)
        c0 = pl.multiple_of((i % (tl // LANE)) * LANE, LANE)
        cols = pl.ds(c0, LANE)
        v1 = v_ref[2 * h, :, cols]
        v2 = v_ref[2 * h + 1, :, cols]
        blocks = [v1[0:1, :] + v2]
        for r1 in range(1, sub):
            blocks.append(jnp.where(row8 < PEER_TOPK // (r1 + 1), v1[r1:r1 + 1, :] + v2[:sub, :], -jnp.inf))
        blocks.append(v1[sub:, :] + v2[0:1, :])
        best, crank = _extract_top16(jnp.concatenate(blocks, axis=0), cand_iota)
        z = jnp.zeros((1, LANE), f32)
        for k in range(PEER_TOPK):
            z = z + jnp.exp(best[k] - best[0])
        sel = jnp.where(crank < PEER_TOPK, 1.0, 0.0)
        rank1 = rank_ref[2 * h, :, cols]
        lim = jnp.zeros((N_KEYS, LANE), jnp.int32)
        for r1 in range(PEER_TOPK):
            if r1 == 0:
                cnt = jnp.sum(sel[:PEER_TOPK, :], axis=0, keepdims=True)
            elif r1 < sub:
                lo = PEER_TOPK + (r1 - 1) * sub
                cnt = jnp.sum(sel[lo:lo + sub, :], axis=0, keepdims=True)
            else:
                lo = PEER_TOPK + n_mid * sub + (r1 - sub)
                cnt = sel[lo:lo + 1, :]
            lim = jnp.where(rank1 == r1, cnt.astype(jnp.int32), lim)
        lim_ref[h, :, cols] = lim
        rk2_ref[h, :, cols] = rank_ref[2 * h + 1, :, cols].astype(f32).astype(jnp.bfloat16)
        e1_ref[h, :, cols] = jnp.exp(s_ref[2 * h, :, cols] - v1[0:1, :])
        e2_ref[h, :, cols] = (jnp.exp(s_ref[2 * h + 1, :, cols] - v2[0:1, :]) / z).astype(jnp.bfloat16)
        return carry

    lax.fori_loop(0, PEER_HEADS * (tl // LANE), head_body, 0, unroll=2)


def _gelu_tanh(x):
    return 0.5 * x * (1.0 + jnp.tanh(math.sqrt(2.0 / math.pi) * (x + 0.044715 * (x * x * x))))


BF16_ROWS = 16


def _rows_bf16(row):
    tile = jnp.broadcast_to(row, (BF16_ROWS, LANE)).astype(jnp.bfloat16)
    return jnp.concatenate([tile] * (N_KEYS // BF16_ROWS), axis=0)


def _peer_expert_kernel(x_ref, h2_ref, pu_ref, pvt_ref, lim_ref, rk2_ref, e1_ref, e2_ref, o_ref,
                        acc_ref, a_cur_ref, a_nxt_ref, c_ref, rk2_s, e2_s):
    f32 = jnp.float32
    bf16 = jnp.bfloat16
    zero = jnp.zeros((N_KEYS, LANE), bf16)
    e_step = pl.program_id(1)
    tl = h2_ref.shape[0]
    n_i1 = pu_ref.shape[0] // N_KEYS
    chunk = 2 * LANE if tl % (2 * LANE) == 0 else LANE

    @pl.when(e_step == 0)
    def _():
        acc_ref[...] = jnp.zeros_like(acc_ref)
        a_nxt_ref[...] = jnp.zeros_like(a_nxt_ref)
        rk2_s[...] = rk2_ref[...]
        e2_s[...] = e2_ref[...]

    a_cur_ref[...] = a_nxt_ref[...]
    a_nxt_ref[...] = lax.dot_general(pu_ref[...], h2_ref[...], (((1,), (1,)), ((), ())),
                                     preferred_element_type=f32)
    for c0 in range(0, tl, chunk):
        for l0 in range(c0, c0 + chunk, LANE):
            cols = slice(l0, l0 + LANE)
            for ii in range(n_i1):
                rows = slice(ii * N_KEYS, (ii + 1) * N_KEYS)
                w = jnp.zeros((N_KEYS, LANE), bf16)
                for h in range(PEER_HEADS):
                    sel = rk2_s[h, :, cols] < _rows_bf16(lim_ref[h, ii:ii + 1, cols].astype(f32))
                    w = w + jnp.where(sel, e2_s[h, :, cols], zero) * _rows_bf16(e1_ref[h, ii:ii + 1, cols])
                c_ref[rows, cols] = w * _gelu_tanh(a_cur_ref[rows, cols]).astype(bf16)
        cc = slice(c0, c0 + chunk)
        acc_ref[:, cc] += jnp.dot(pvt_ref[0], c_ref[:, cc], preferred_element_type=f32)

    @pl.when(e_step == pl.num_programs(1) - 1)
    def _():
        o_ref[...] = x_ref[...] + acc_ref[...].T


def _token_tile(T):
    for tl in (640, 512, 384, 256, 128):
        if T % tl == 0:
            return tl
    raise ValueError(f"token count {T} is not a multiple of {LANE}")


def peer_residual(x, g, wqt, sk, pu, pvt):
    T = x.shape[0]
    tl = _token_tile(T)
    nt = T // tl
    n_exp = pu.shape[0]
    head_shape = jax.ShapeDtypeStruct((PEER_HEADS, N_KEYS, T), jnp.int32)
    head_shape_f = jax.ShapeDtypeStruct((PEER_HEADS, N_KEYS, T), jnp.float32)
    head_shape_b = jax.ShapeDtypeStruct((PEER_HEADS, N_KEYS, T), jnp.bfloat16)
    head_spec = pl.BlockSpec((PEER_HEADS, N_KEYS, tl), lambda i: (0, 0, i))
    h2, lim, rk2, e1, e2 = pl.pallas_call(
        _peer_router_kernel,
        grid=(nt,),
        in_specs=[pl.BlockSpec((tl, D_MODEL), lambda i: (i, 0)),
                  pl.BlockSpec((1, D_MODEL), lambda i: (0, 0)),
                  pl.BlockSpec((PEER_HEADS * PEER_QDIM, D_MODEL), lambda i: (0, 0)),
                  pl.BlockSpec((N_HC, N_KEYS, HALF_Q), lambda i: (0, 0, 0))],
        out_specs=[pl.BlockSpec((tl, D_MODEL), lambda i: (i, 0)), head_spec, head_spec, head_spec, head_spec],
        out_shape=[jax.ShapeDtypeStruct((T, D_MODEL), jnp.bfloat16), head_shape, head_shape_b,
                   head_shape_f, head_shape_b],
        scratch_shapes=[pltpu.VMEM((PEER_HEADS * PEER_QDIM, tl), jnp.bfloat16),
                        pltpu.VMEM((N_HC, N_KEYS, tl), jnp.float32),
                        pltpu.VMEM((N_HC, N_KEYS, tl), jnp.int32),
                        pltpu.VMEM((N_HC, PEER_TOPK, tl), jnp.float32)],
        compiler_params=pltpu.CompilerParams(dimension_semantics=("parallel",), vmem_limit_bytes=VMEM_LIMIT),
        name="peer_router",
    )(x, g.reshape(1, D_MODEL), wqt, sk)

    te = PEER_EXPERT_TILE
    n_tiles = n_exp // te
    nxt = lambda e: jnp.minimum(e, n_tiles - 1)
    cur = lambda e: jnp.maximum(e - 1, 0)
    i2_spec = pl.BlockSpec((PEER_HEADS, N_KEYS, tl), lambda i, e: (0, 0, i))
    i1_spec = pl.BlockSpec((PEER_HEADS, te // N_KEYS, tl), lambda i, e: (0, cur(e), i))
    return pl.pallas_call(
        _peer_expert_kernel,
        grid=(nt, n_tiles + 1),
        in_specs=[pl.BlockSpec((tl, D_MODEL), lambda i, e: (i, 0)),
                  pl.BlockSpec((tl, D_MODEL), lambda i, e: (i, 0)),
                  pl.BlockSpec((te, D_MODEL), lambda i, e: (nxt(e), 0)),
                  pl.BlockSpec((1, D_MODEL, te), lambda i, e: (cur(e), 0, 0)),
                  i1_spec, i2_spec, i1_spec, i2_spec],
        out_specs=pl.BlockSpec((tl, D_MODEL), lambda i, e: (i, 0)),
        out_shape=jax.ShapeDtypeStruct((T, D_MODEL), jnp.float32),
        scratch_shapes=[pltpu.VMEM((D_MODEL, tl), jnp.float32),
                        pltpu.VMEM((te, tl), jnp.float32),
                        pltpu.VMEM((te, tl), jnp.float32),
                        pltpu.VMEM((te, tl), jnp.bfloat16),
                        pltpu.VMEM((PEER_HEADS, N_KEYS, tl), jnp.bfloat16),
                        pltpu.VMEM((PEER_HEADS, N_KEYS, tl), jnp.bfloat16)],
        compiler_params=pltpu.CompilerParams(dimension_semantics=("parallel", "arbitrary"),
                                             vmem_limit_bytes=VMEM_LIMIT),
        name="peer_experts",
    )(x, h2, pu, pvt, lim, rk2, e1, e2)


C_QKVB = 3 * WIDTH_A
C_GB = C_QKVB + 3 * WIDTH_B
C_UC = C_GB + WIDTH_B
C_GATE = C_UC + WIDTH_C
C_BA = C_GATE + 3 * D_MODEL
D_IN2 = C_BA + LANE
S5_BLOCK = 256


def _reorder_w_in(w):
    cols = [w[:, OFF_QA:OFF_BETA], w[:, OFF_UC:OFF_GATE], w[:, OFF_GATE:], w[:, OFF_BETA:OFF_UC],
            jnp.zeros((w.shape[0], LANE - 2 * N_HEADS_B), w.dtype)]
    return jnp.concatenate(cols, axis=1).astype(jnp.bfloat16)


def trunk_layer(x, lp, layer_idx, att, prompt, k_past, v_past, conv_hist, delta0, ssm0_re, ssm0_im):
    f32 = jnp.float32
    B, Lx = x.shape[0], x.shape[1]
    T = B * Lx
    z = norm_matmul(x.reshape(T, D_MODEL), lp['norm1'], lp['w_in2'])

    qn, kn, knb, vb = qk_prep(z, lp['q_norm'], lp['k_norm'], _token_tile(T))
    lam_init = 0.8 - 0.6 * math.exp(-0.3 * layer_idx)
    lqk = lp['lambda_qk']
    lam = jnp.exp(jnp.sum(lqk[0] * lqk[1])) - jnp.exp(jnp.sum(lqk[2] * lqk[3])) + lam_init
    scalars = jnp.stack([lam, jnp.asarray(1.0 - lam_init, f32)]).astype(f32)
    r3 = lambda a: a.reshape(B, Lx, WIDTH_A)
    r4 = lambda a: a.reshape(N_HEADS_A, B, Lx, a.shape[-1])
    if prompt:
        oA = attn_prompt(r4(qn), r4(knb), r4(vb), scalars, att['far'], att['tiles'], lp['subln'])
    else:
        P = k_past.shape[1]
        oA = attn_sample(r4(qn), r4(knb), r4(vb), k_past.reshape(B, P, WIDTH_A), v_past.reshape(B, P, WIDTH_A),
                         scalars, att['bias_past'], att['bias_new'], lp['subln'])

    z3 = z.reshape(B, Lx, D_IN2)
    off = PAD if prompt else 0
    L = Lx - off
    kA = r3(kn)[:, off:].reshape(B, L, N_HEADS_A, 2 * HEAD_DIM_A)
    vA = z3[:, off:, 2 * WIDTH_A:3 * WIDTH_A].reshape(B, L, N_HEADS_A, 2 * HEAD_DIM_A)

    oB, delta, tail = gdn_mixer(z3, (C_QKVB // (3 * WIDTH_B), C_GB // WIDTH_B, C_BA // LANE), conv_hist, delta0,
                                lp['conv_w'], lp['a_log'], lp['dt_bias'], lp['gdn_norm'],
                                GDN_CHUNK if prompt else Lx, off)
    conv_state = tail[:, GDN_TAIL - (CONV_W - 1):]

    oC, ssm_re, ssm_im = s5_glu(z3, C_UC // WIDTH_C, ssm0_re, ssm0_im, lp['s5'], lp['w_glu'],
                                S5_BLOCK if prompt else Lx)

    x2 = merge_residual(x.reshape(T, D_MODEL), oA.reshape(T, WIDTH_A), oB.reshape(T, WIDTH_B),
                        oC.reshape(T, WIDTH_C), z, C_GATE // D_MODEL, lp['wb_a'], lp['wb_b'], lp['wb_c'],
                        lp['w_out'], Lx, off)

    x = peer_residual(x2, lp['norm2'], lp['peer_wqt'], lp['peer_sk'], lp['peer_pu'],
                      lp['peer_pvt']).reshape(B, Lx, D_MODEL)
    return (x, kA, vA, conv_state, delta, ssm_re, ssm_im)


def kernel(x_prompt, x_sample, cache_k, cache_v, state_conv, state_delta, state_ssm_re, state_ssm_im,
           meta_tokens, rel_bias, norm1_g, norm2_g, final_norm_g, w_in, q_norm_g, k_norm_g, lambda_qk,
           subln_g, conv_w, gdn_a_log, gdn_dt_bias, gdn_norm_g, s5_a_re, s5_a_im, s5_b_re, s5_b_im,
           s5_c_re, s5_c_im, s5_d, s5_log_dt, w_glu, w_branch_a, w_branch_b, w_branch_c, w_out,
           peer_wq, peer_subkeys, peer_u, peer_v):
    f32 = jnp.float32
    bf16 = jnp.bfloat16
    params = [dict(norm1=norm1_g[l], norm2=norm2_g[l], w_in2=_reorder_w_in(w_in[l]), q_norm=q_norm_g[l],
                   k_norm=k_norm_g[l], lambda_qk=lambda_qk[l], subln=subln_g[l], conv_w=conv_w[l],
                   a_log=gdn_a_log[l], dt_bias=gdn_dt_bias[l], gdn_norm=gdn_norm_g[l],
                   s5=s5_params(s5_a_re[l], s5_a_im[l], s5_b_re[l], s5_b_im[l], s5_c_re[l], s5_c_im[l], s5_d[l],
                                s5_log_dt[l]),
                   w_glu=w_glu[l], wb_a=w_branch_a[l], wb_b=w_branch_b[l], wb_c=w_branch_c[l], w_out=w_out[l],
                   peer_wqt=peer_wq[l].T.astype(bf16),
                   peer_sk=peer_subkeys[l].reshape(N_HC, N_KEYS, HALF_Q).astype(bf16),
                   peer_pu=peer_u[l].astype(bf16), peer_pvt=peer_v[l].astype(bf16).reshape(-1, PEER_EXPERT_TILE, D_MODEL).transpose(0, 2, 1))
              for l in range(DEPTH)]

    qpos = jnp.arange(ATT_TQ, dtype=jnp.int32)
    tiles = jnp.stack([t5_bias(qpos, d * ATT_TK + jnp.arange(ATT_TK, dtype=jnp.int32), rel_bias)
                       for d in (-2, -1, 0)], axis=1)
    far = t5_bias(jnp.full((1,), 2 * ATT_TK, jnp.int32), jnp.zeros((1,), jnp.int32), rel_bias).reshape(N_HEADS_A)
    P, Ls = cache_k.shape[2], x_sample.shape[1]
    bias_s = t5_bias(P + jnp.arange(Ls, dtype=jnp.int32), jnp.arange(P + Ls, dtype=jnp.int32), rel_bias)
    att_p = dict(tiles=tiles, far=far)
    att_s = dict(bias_past=bias_s[:, :, :P], bias_new=bias_s[:, :, P:])

    B = x_prompt.shape[0]
    xp = jnp.concatenate([jnp.zeros((B, PAD, D_MODEL), f32),
                          jnp.broadcast_to(meta_tokens[None], (B, N_META, D_MODEL)), x_prompt], axis=1)
    outs_p = [[] for _ in range(6)]
    for l in range(DEPTH):
        res = trunk_layer(
            xp, params[l], l, att_p, True, None, None,
            jnp.zeros((B, CONV_W - 1, 3 * WIDTH_B), f32),
            jnp.zeros((B, N_HEADS_B, HEAD_DIM_B, HEAD_DIM_B), f32),
            jnp.zeros((B, S5_GROUPS, S5_STATE), f32), jnp.zeros((B, S5_GROUPS, S5_STATE), f32))
        xp = res[0]
        for acc, r in zip(outs_p, res[1:]):
            acc.append(r)
    y_prompt = rmsnorm(xp, final_norm_g)[:, FRONT:]

    xs = x_sample
    outs_s = [[] for _ in range(6)]
    for l in range(DEPTH):
        res = trunk_layer(
            xs, params[l], l, att_s, False, cache_k[l], cache_v[l], state_conv[l],
            state_delta[l], state_ssm_re[l], state_ssm_im[l])
        xs = res[0]
        for acc, r in zip(outs_s, res[1:]):
            acc.append(r)
    y_sample = rmsnorm(xs, final_norm_g)

    kp, vp, cp, dp, srp, sip = [jnp.stack(a) for a in outs_p]
    ks_, vs_, cs_, ds_, srs, sis = [jnp.stack(a) for a in outs_s]
    return (y_prompt, y_sample, kp, vp, ks_, vs_, cp, cs_, dp, ds_, srp, sip, srs, sis)
```

```python
import functools
import math

import jax
import jax.numpy as jnp
from jax import lax
from jax.experimental import pallas as pl
from jax.experimental.pallas import tpu as pltpu

D_MODEL = 1024
DEPTH = 4
CHUNK = 64
N_META = 16
EPS = 1e-6
NEG_INF = -1e30
N_HEADS_A = 4
HEAD_DIM_A = 64
NUM_BUCKETS = 32
MAX_DISTANCE = 128
N_HEADS_B = 4
HEAD_DIM_B = 128
CONV_W = 4
S5_GROUP = 16
S5_GROUPS = 32
S5_STATE = 64
PEER_HEADS = 8
PEER_QDIM = 256
N_KEYS = 128
PEER_TOPK = 16

WIDTH_A = N_HEADS_A * 2 * HEAD_DIM_A
WIDTH_B = N_HEADS_B * HEAD_DIM_B
WIDTH_C = S5_GROUPS * S5_GROUP
OFF_QA = 0
OFF_KA = OFF_QA + WIDTH_A
OFF_VA = OFF_KA + WIDTH_A
OFF_QKVB = OFF_VA + WIDTH_A
OFF_GB = OFF_QKVB + 3 * WIDTH_B
OFF_BETA = OFF_GB + WIDTH_B
OFF_ALPHA = OFF_BETA + N_HEADS_B
OFF_UC = OFF_ALPHA + N_HEADS_B
OFF_GATE = OFF_UC + WIDTH_C
D_IN = OFF_GATE + 3 * D_MODEL

LANE = 128
VMEM_LIMIT = 56 * 1024 * 1024
WEIGHT_TILE_BYTES = 6 * 1024 * 1024
HIGHEST = lax.Precision.HIGHEST


def _norm_mm_kernel(x_ref, g_ref, w_ref, o_ref, h_ref):
    @pl.when(pl.program_id(1) == 0)
    def _():
        x = x_ref[...]
        h = x * lax.rsqrt(jnp.mean(x * x, axis=-1, keepdims=True) + EPS) * g_ref[...]
        h_ref[...] = h.astype(jnp.bfloat16)

    o_ref[...] = jnp.dot(h_ref[...], w_ref[...], preferred_element_type=jnp.float32)


def norm_matmul(x, g, wb, tm=512):
    M, K = x.shape
    N = wb.shape[1]
    n_lanes = N // LANE
    tn = LANE * max(d for d in range(1, n_lanes + 1) if n_lanes % d == 0 and d * LANE * K * 2 <= WEIGHT_TILE_BYTES)
    tm = min(tm, M)
    return pl.pallas_call(
        _norm_mm_kernel,
        grid=(pl.cdiv(M, tm), N // tn),
        in_specs=[pl.BlockSpec((tm, K), lambda i, j: (i, 0)),
                  pl.BlockSpec((1, K), lambda i, j: (0, 0)),
                  pl.BlockSpec((K, tn), lambda i, j: (0, j))],
        out_specs=pl.BlockSpec((tm, tn), lambda i, j: (i, j)),
        out_shape=jax.ShapeDtypeStruct((M, N), jnp.float32),
        scratch_shapes=[pltpu.VMEM((tm, K), jnp.bfloat16)],
        compiler_params=pltpu.CompilerParams(dimension_semantics=("parallel", "arbitrary"),
                                             vmem_limit_bytes=VMEM_LIMIT),
        name="norm_proj",
    )(x, g.reshape(1, K), wb)


def rmsnorm(x, g):
    xf = x.astype(jnp.float32)
    y = xf * lax.rsqrt(jnp.mean(xf * xf, axis=-1, keepdims=True) + EPS)
    return (y * g.astype(jnp.float32)).astype(x.dtype)


def t5_bias(q_pos, k_pos, table):
    rel = k_pos[None, :] - q_pos[:, None]
    half = NUM_BUCKETS // 2
    exact = half // 2
    n = jnp.abs(rel)
    nf = jnp.maximum(n, 1).astype(jnp.float32)
    far = exact + (jnp.log(nf / exact) / math.log(MAX_DISTANCE / exact) * (half - exact)).astype(jnp.int32)
    bucket = jnp.where(rel > 0, half, 0) + jnp.where(n < exact, n, jnp.minimum(far, half - 1))
    return jnp.moveaxis(table[bucket].astype(jnp.float32), -1, 0)


FRONT = 256
PAD = FRONT - N_META
ATT_TQ = 256
ATT_TK = 256


def _qk_prep_kernel(q_ref, k_ref, v_ref, gq_ref, gk_ref, seg_ref, qn_ref, kn_ref, knb_ref, vb_ref):
    f32 = jnp.float32
    bf16 = jnp.bfloat16
    seg = seg_ref[...]
    hd = 2 * HEAD_DIM_A

    def norm(x, g):
        ms = jnp.dot(x * x, seg, precision=HIGHEST, preferred_element_type=f32)
        return x * lax.rsqrt(ms + EPS) * g

    qn = (norm(q_ref[...], gq_ref[...]) * (HEAD_DIM_A ** -0.5)).astype(bf16)
    kn = norm(k_ref[...], gk_ref[...])
    kn_ref[...] = kn
    knb = kn.astype(bf16)
    v = v_ref[...].astype(bf16)
    lane = lax.broadcasted_iota(jnp.int32, (v.shape[0], hd), 1)
    ones_col = jnp.where(lane == 0, 1.0, 0.0).astype(bf16)
    for h in range(N_HEADS_A):
        cols = slice(h * hd, (h + 1) * hd)
        qn_ref[h] = qn[:, cols]
        knb_ref[h] = knb[:, cols]
        vb_ref[h] = jnp.concatenate([v[:, cols], ones_col], axis=1)


def qk_prep(z, gq, gk, tm):
    T = z.shape[0]
    hd = 2 * HEAD_DIM_A
    seg = jnp.kron(jnp.eye(WIDTH_A // HEAD_DIM_A, dtype=jnp.float32),
                   jnp.full((HEAD_DIM_A, HEAD_DIM_A), 1.0 / HEAD_DIM_A, jnp.float32))
    row = lambda j: pl.BlockSpec((tm, WIDTH_A), lambda i, j=j: (i, j))
    const = lambda shape: pl.BlockSpec(shape, lambda i: (0,) * len(shape))
    heads = lambda w: pl.BlockSpec((N_HEADS_A, tm, w), lambda i: (0, i, 0))
    hshape = lambda w: jax.ShapeDtypeStruct((N_HEADS_A, T, w), jnp.bfloat16)
    return pl.pallas_call(
        _qk_prep_kernel,
        grid=(T // tm,),
        in_specs=[row(0), row(1), row(2), const((1, WIDTH_A)), const((1, WIDTH_A)), const((WIDTH_A, WIDTH_A))],
        out_specs=[heads(hd), pl.BlockSpec((tm, WIDTH_A), lambda i: (i, 0)), heads(hd), heads(2 * hd)],
        out_shape=[hshape(hd), jax.ShapeDtypeStruct((T, WIDTH_A), jnp.float32), hshape(hd), hshape(2 * hd)],
        compiler_params=pltpu.CompilerParams(dimension_semantics=("parallel",)),
        name="qk_prep",
    )(z, z, z, jnp.tile(gq, WIDTH_A // HEAD_DIM_A).reshape(1, WIDTH_A),
      jnp.tile(gk, WIDTH_A // HEAD_DIM_A).reshape(1, WIDTH_A), seg)


def _split_maps(q):
    lane = lax.broadcasted_iota(jnp.int32, q.shape, 1)
    zero = jnp.zeros_like(q)
    return jnp.where(lane < HEAD_DIM_A, q, zero), jnp.where(lane >= HEAD_DIM_A, q, zero)


def _subln(o, g, scale):
    return o * lax.rsqrt(jnp.mean(o * o, axis=-1, keepdims=True) + EPS) * g * scale


def _attn_prompt_kernel(sc_ref, far_ref, q_ref, k_ref, v_ref, bias_ref, g_ref, o_ref, m_ref, acc_ref,
                        sa_ref, sb_ref):
    f32 = jnp.float32
    h = pl.program_id(1)
    qi = pl.program_id(2)
    tq, tk = ATT_TQ, ATT_TK
    hd = 2 * HEAD_DIM_A
    qs = jnp.concatenate(_split_maps(q_ref[0, 0]), axis=0)
    m_ref[...] = jnp.full(m_ref.shape, NEG_INF, f32)
    acc_ref[...] = jnp.zeros(acc_ref.shape, f32)
    far = far_ref[h]

    def logits(k0, width, general):
        k0 = pl.multiple_of(k0, tk)
        kt = k_ref[0, 0, pl.ds(k0, width), :]
        s = lax.dot_general(qs, kt, (((1,), (1,)), ((), ())), preferred_element_type=f32)
        if general:
            kj = k0 // tk
            bias = bias_ref[0, jnp.clip(kj - qi + 2, 0, 2)]
            qpos = qi * tq + lax.broadcasted_iota(jnp.int32, (tq, tk), 0)
            kpos = k0 + lax.broadcasted_iota(jnp.int32, (tq, tk), 1)
            qchunk = jnp.where(qpos < FRONT, 0, 1 + jnp.right_shift(qpos - FRONT, 6))
            kchunk = jnp.where(kpos < FRONT, 0, 1 + jnp.right_shift(kpos - FRONT, 6))
            mask = (kpos >= PAD) & (kchunk <= qchunk)
            s = jnp.where(jnp.concatenate([mask, mask], axis=0), s + jnp.concatenate([bias, bias], axis=0), NEG_INF)
        else:
            s = s + far
        return s

    def accumulate(s, k0):
        width = s.shape[1]
        vt = v_ref[0, 0, pl.ds(pl.multiple_of(k0, tk), width), :]
        n_lane_tiles = width // LANE
        smax = s[:, :LANE]
        for j in range(1, n_lane_tiles):
            smax = jnp.maximum(smax, s[:, j * LANE:(j + 1) * LANE])
        m_old = m_ref[...]
        m_new = jnp.maximum(m_old, jnp.broadcast_to(jnp.max(smax, axis=1, keepdims=True), m_old.shape))
        alpha = jnp.exp(m_old - m_new)
        p = jnp.exp(s - jnp.concatenate([m_new] * n_lane_tiles, axis=1))
        acc_ref[...] = (jnp.concatenate([alpha, alpha], axis=1) * acc_ref[...]
                        + jnp.dot(p.astype(jnp.bfloat16), vt, preferred_element_type=f32))
        m_ref[...] = m_new

    def tile(k0, width, general):
        accumulate(logits(k0, width, general), k0)

    tile(0, tk, True)

    wide = 2 * tk
    n_far = jnp.maximum(qi - 2, 0)
    n_single = n_far % 4
    for i in range(3):
        @pl.when(n_single > i)
        def _(i=i):
            tile((1 + i) * tk, tk, False)

    base = (1 + n_single) * tk
    n_pairs = n_far // 4

    @pl.when(n_pairs > 0)
    def _():
        sa_ref[...] = logits(base, wide, False)

    def pair_body(j, carry):
        k0 = base + 2 * j * wide
        sb_ref[...] = logits(k0 + wide, wide, False)
        accumulate(sa_ref[...], k0)
        nxt = jnp.minimum(k0 + 2 * wide, base + (2 * n_pairs - 1) * wide)
        sa_ref[...] = logits(nxt, wide, False)
        accumulate(sb_ref[...], k0 + wide)
        return carry

    lax.fori_loop(0, n_pairs, pair_body, 0)

    @pl.when(qi >= 2)
    def _():
        tile((qi - 1) * tk, tk, True)

    @pl.when(qi >= 1)
    def _():
        tile(qi * tk, tk, True)

    acc = acc_ref[...]
    out = acc[:, :hd] / acc[:, hd:hd + 1]
    o = out[:tq] - sc_ref[0] * out[tq:]
    o_ref[0] = _subln(o, g_ref[...], sc_ref[1])


def attn_prompt(qn, knb, vb, scalars, far, bias_tiles, subln_g):
    _, B, Lp, _ = qn.shape
    hd = 2 * HEAD_DIM_A
    smem = pl.BlockSpec(memory_space=pltpu.SMEM)
    return pl.pallas_call(
        _attn_prompt_kernel,
        grid=(B, N_HEADS_A, Lp // ATT_TQ),
        in_specs=[smem, smem,
                  pl.BlockSpec((1, 1, ATT_TQ, hd), lambda b, h, i: (h, b, i, 0)),
                  pl.BlockSpec((1, 1, Lp, hd), lambda b, h, i: (h, b, 0, 0)),
                  pl.BlockSpec((1, 1, Lp, 2 * hd), lambda b, h, i: (h, b, 0, 0)),
                  pl.BlockSpec((1, 3, ATT_TQ, ATT_TK), lambda b, h, i: (h, 0, 0, 0)),
                  pl.BlockSpec((1, hd), lambda b, h, i: (0, 0))],
        out_specs=pl.BlockSpec((1, ATT_TQ, hd), lambda b, h, i: (b, i, h)),
        out_shape=jax.ShapeDtypeStruct((B, Lp, WIDTH_A), jnp.float32),
        scratch_shapes=[pltpu.VMEM((2 * ATT_TQ, hd), jnp.float32), pltpu.VMEM((2 * ATT_TQ, 2 * hd), jnp.float32),
                        pltpu.VMEM((2 * ATT_TQ, 2 * ATT_TK), jnp.float32),
                        pltpu.VMEM((2 * ATT_TQ, 2 * ATT_TK), jnp.float32)],
        compiler_params=pltpu.CompilerParams(dimension_semantics=("parallel", "parallel", "parallel"),
                                             vmem_limit_bytes=VMEM_LIMIT),
        name="diff_attn_prompt",
    )(scalars, far, qn, knb, vb, bias_tiles, subln_g.reshape(1, hd))


def _attn_sample_kernel(sc_ref, q_ref, kp_ref, vp_ref, kn_ref, vn_ref, bp_ref, bn_ref, g_ref, o_ref):
    f32 = jnp.float32
    bf16 = jnp.bfloat16
    qs = _split_maps(q_ref[0, 0])
    kp = kp_ref[0].astype(bf16)
    vp = vp_ref[0].astype(bf16)
    kn = kn_ref[0, 0]
    vn = vn_ref[0, 0][:, :2 * HEAD_DIM_A]
    dn = (((1,), (1,)), ((), ()))
    outs = []
    for c in range(2):
        sp = lax.dot_general(qs[c], kp, dn, preferred_element_type=f32) + bp_ref[0]
        sn = lax.dot_general(qs[c], kn, dn, preferred_element_type=f32) + bn_ref[0]
        m = jnp.maximum(jnp.max(sp, axis=1, keepdims=True), jnp.max(sn, axis=1, keepdims=True))
        pp = jnp.exp(sp - m)
        pn = jnp.exp(sn - m)
        l = jnp.sum(pp, axis=1, keepdims=True) + jnp.sum(pn, axis=1, keepdims=True)
        acc = (jnp.dot(pp.astype(bf16), vp, preferred_element_type=f32)
               + jnp.dot(pn.astype(bf16), vn, preferred_element_type=f32))
        outs.append(acc / l)
    o_ref[0] = _subln(outs[0] - sc_ref[0] * outs[1], g_ref[...], sc_ref[1])


def attn_sample(qn, knb, vb, k_past, v_past, scalars, bias_past, bias_new, subln_g):
    _, B, L, _ = qn.shape
    P = k_past.shape[1]
    hd = 2 * HEAD_DIM_A
    new = pl.BlockSpec((1, L, hd), lambda b, h: (b, 0, h))
    new_h = lambda w: pl.BlockSpec((1, 1, L, w), lambda b, h: (h, b, 0, 0))
    past = pl.BlockSpec((1, P, hd), lambda b, h: (b, 0, h))
    return pl.pallas_call(
        _attn_sample_kernel,
        grid=(B, N_HEADS_A),
        in_specs=[pl.BlockSpec(memory_space=pltpu.SMEM), new_h(hd), past, past, new_h(hd), new_h(2 * hd),
                  pl.BlockSpec((1, L, P), lambda b, h: (h, 0, 0)),
                  pl.BlockSpec((1, L, L), lambda b, h: (h, 0, 0)),
                  pl.BlockSpec((1, hd), lambda b, h: (0, 0))],
        out_specs=new,
        out_shape=jax.ShapeDtypeStruct((B, L, WIDTH_A), jnp.float32),
        compiler_params=pltpu.CompilerParams(dimension_semantics=("parallel", "parallel")),
        name="diff_attn_sample",
    )(scalars, qn, k_past, v_past, knb, vb, bias_past, bias_new, subln_g.reshape(1, hd))


S5_N = S5_GROUPS * S5_STATE
S5_SLAB = 8
S5_LANES = 512


def _sigmoid(x):
    return 1.0 / (1.0 + jnp.exp(-x))


def _s5_kernel(u_ref, x0_ref, bmat_ref, cmat_ref, lamp_ref, ppow_ref, d_ref, wglu_ref, oc_ref, xf_ref,
               bu_ref, carry_ref):
    f32 = jnp.float32
    bf16 = jnp.bfloat16
    t = pl.program_id(1)
    tb = u_ref.shape[1]

    @pl.when(t == 0)
    def _():
        carry_ref[...] = x0_ref[0]

    u = u_ref[0]
    bu_ref[...] = jnp.dot(u.astype(bf16), bmat_ref[...], preferred_element_type=f32)

    row = lax.broadcasted_iota(jnp.int32, (S5_SLAB, S5_LANES), 0)
    for c in range(S5_N // S5_LANES):
        re = slice(c * S5_LANES, (c + 1) * S5_LANES)
        im = slice(S5_N + c * S5_LANES, S5_N + (c + 1) * S5_LANES)

        def slab(i, carry):
            cre, cim = carry
            rows = pl.ds(pl.multiple_of(i * S5_SLAB, S5_SLAB), S5_SLAB)
            yre = bu_ref[rows, re]
            yim = bu_ref[rows, im]
            for s in range(3):
                sh = 1 << s
                sre = jnp.where(row >= sh, pltpu.roll(yre, sh, 0), 0.0)
                sim = jnp.where(row >= sh, pltpu.roll(yim, sh, 0), 0.0)
                lr = lamp_ref[s, 0, :, re]
                li = lamp_ref[s, 1, :, re]
                yre, yim = yre + (lr * sre - li * sim), yim + (lr * sim + li * sre)
            pr = ppow_ref[0, :, re]
            pi = ppow_ref[1, :, re]
            yre, yim = yre + (pr * cre - pi * cim), yim + (pr * cim + pi * cre)
            bu_ref[rows, re] = yre
            bu_ref[rows, im] = yim
            last = S5_SLAB - 1
            return (jnp.broadcast_to(yre[last:last + 1, :], yre.shape),
                    jnp.broadcast_to(yim[last:last + 1, :], yim.shape))

        cre, cim = lax.fori_loop(0, tb // S5_SLAB, slab, (carry_ref[:, re], carry_ref[:, im]))
        carry_ref[:, re] = cre
        carry_ref[:, im] = cim

    y = jnp.dot(bu_ref[...].astype(bf16), cmat_ref[...], preferred_element_type=f32) + d_ref[...] * u
    gl = jnp.dot(_gelu_tanh(y).astype(bf16), wglu_ref[...], preferred_element_type=f32)
    oc_ref[0] = gl[:, :WIDTH_C] * _sigmoid(gl[:, WIDTH_C:])

    @pl.when(t == pl.num_programs(1) - 1)
    def _():
        xf_ref[0] = carry_ref[...]


def s5_params(a_re, a_im, b_re, b_im, c_re, c_im, d, log_dt):
    f32 = jnp.float32
    lam = lax.complex(a_re, a_im)
    lam_bar = jnp.exp(lam * jnp.exp(log_dt)[:, None])
    b_bar = ((lam_bar - 1.0) / lam)[..., None] * lax.complex(b_re, b_im)
    eye = jnp.eye(S5_GROUPS, dtype=f32)
    bd_in = lambda m: jnp.einsum('gpi,gh->gihp', m, eye).reshape(WIDTH_C, S5_N)
    bd_out = lambda m: jnp.einsum('gip,gh->gphi', m, eye).reshape(S5_N, WIDTH_C)
    bmat = jnp.concatenate([bd_in(b_bar.real), bd_in(b_bar.imag)], axis=1).astype(jnp.bfloat16)
    cmat = jnp.concatenate([bd_out(c_re), bd_out(-c_im)], axis=0).astype(jnp.bfloat16)
    lb = lam_bar.reshape(S5_N)
    rep = lambda v: jnp.broadcast_to(v[None, :], (S5_SLAB, S5_N))
    pows = [lb, lb * lb, (lb * lb) * (lb * lb)]
    lamp = jnp.stack([jnp.stack([rep(p.real), rep(p.imag)]) for p in pows])
    run = [lb]
    for _ in range(S5_SLAB - 1):
        run.append(run[-1] * lb)
    pp = jnp.stack(run)
    ppow = jnp.stack([pp.real, pp.imag])
    return dict(bmat=bmat, cmat=cmat, lamp=lamp.astype(f32), ppow=ppow.astype(f32), d=d.reshape(1, WIDTH_C))


def s5_glu(z3, col_block, x0_re, x0_im, sp, w_glu, tb):
    B, Lx, _ = z3.shape
    x0 = jnp.concatenate([x0_re.reshape(B, S5_N), x0_im.reshape(B, S5_N)], axis=1)
    x0 = jnp.broadcast_to(x0[:, None, :], (B, S5_SLAB, 2 * S5_N))
    const = lambda a: pl.BlockSpec(a.shape, lambda b, t: (0,) * a.ndim)
    wg = w_glu.astype(jnp.bfloat16)
    oc, xf = pl.pallas_call(
        _s5_kernel,
        grid=(B, Lx // tb),
        in_specs=[pl.BlockSpec((1, tb, WIDTH_C), lambda b, t: (b, t, col_block)),
                  pl.BlockSpec((1, S5_SLAB, 2 * S5_N), lambda b, t: (b, 0, 0)),
                  const(sp['bmat']), const(sp['cmat']), const(sp['lamp']), const(sp['ppow']), const(sp['d']),
                  const(wg)],
        out_specs=[pl.BlockSpec((1, tb, WIDTH_C), lambda b, t: (b, t, 0)),
                   pl.BlockSpec((1, S5_SLAB, 2 * S5_N), lambda b, t: (b, 0, 0))],
        out_shape=[jax.ShapeDtypeStruct((B, Lx, WIDTH_C), jnp.float32),
                   jax.ShapeDtypeStruct((B, S5_SLAB, 2 * S5_N), jnp.float32)],
        scratch_shapes=[pltpu.VMEM((tb, 2 * S5_N), jnp.float32), pltpu.VMEM((S5_SLAB, 2 * S5_N), jnp.float32)],
        compiler_params=pltpu.CompilerParams(dimension_semantics=("parallel", "arbitrary"),
                                             vmem_limit_bytes=VMEM_LIMIT),
        name="s5_glu",
    )(z3, x0, sp['bmat'], sp['cmat'], sp['lamp'], sp['ppow'], sp['d'], wg)
    shape = (B, S5_GROUPS, S5_STATE)
    return oc, xf[:, 0, :S5_N].reshape(shape), xf[:, 0, S5_N:].reshape(shape)


def _gdn_kernel(qkv_ref, gate_ref, ba_ref, hist_ref, s0_ref, cw_ref, av_ref, dt_ref, gn_ref,
                ob_ref, sout_ref, tail_ref, s_ref, prev_ref, *, pad_rows):
    f32 = jnp.float32
    t = pl.program_id(1)
    C = qkv_ref.shape[1]
    hd = HEAD_DIM_B
    dn_last = (((1,), (1,)), ((), ()))
    dn_first = (((0,), (0,)), ((), ()))
    dot = lambda a, b: _dot3(a, b, (((1,), (0,)), ((), ())))

    @pl.when(t == 0)
    def _():
        s_ref[...] = s0_ref[0]
        prev_ref[...] = hist_ref[0]

    cur = qkv_ref[0]
    tail = prev_ref.shape[0]
    ext = jnp.concatenate([prev_ref[...], cur], axis=0)
    w = cw_ref[...]
    conv = cur * w[CONV_W - 1:CONV_W, :]
    for i in range(CONV_W - 1):
        lo = tail - (CONV_W - 1) + i
        conv = conv + ext[lo:lo + C, :] * w[i:i + 1, :]
    prev_ref[...] = cur[C - tail:, :]
    c = conv * _sigmoid(conv)

    ba = ba_ref[0]
    xg = ba + dt_ref[...]
    softplus = jnp.maximum(xg, 0.0) + jnp.log(1.0 + jnp.exp(-jnp.abs(xg)))
    g_all = -jnp.exp(av_ref[...]) * softplus
    if pad_rows:
        grow = t * C + lax.broadcasted_iota(jnp.int32, g_all.shape, 0)
        g_all = jnp.where(grow >= pad_rows, g_all, 0.0)
    ri = lax.broadcasted_iota(jnp.int32, (C, C), 0)
    ci = lax.broadcasted_iota(jnp.int32, (C, C), 1)
    tri = ri >= ci
    strict = ri > ci
    tril = jnp.where(tri, 1.0, 0.0).astype(f32)
    G_col = jnp.dot(tril, g_all, precision=HIGHEST, preferred_element_type=f32)
    G_row = lax.dot_general(g_all, tril, (((0,), (1,)), ((), ())), precision=HIGHEST,
                            preferred_element_type=f32)
    eye = jnp.where(ri == ci, 1.0, 0.0).astype(f32)

    heads = range(N_HEADS_B)
    dot_t = lambda a, b: _dot3(a, b, dn_last)
    Gc = [G_col[:, N_HEADS_B + h:N_HEADS_B + h + 1] for h in heads]
    Gr = [G_row[N_HEADS_B + h:N_HEADS_B + h + 1, :] for h in heads]
    decay = [jnp.where(tri, jnp.exp(jnp.minimum(Gc[h] - Gr[h], 0.0)), 0.0) for h in heads]
    beta = [_sigmoid(ba[:, h:h + 1]) for h in heads]
    unit = lambda a: a * lax.rsqrt(jnp.sum(a * a, axis=-1, keepdims=True) + EPS)
    q = [unit(c[:, h * hd:(h + 1) * hd]) * (hd ** -0.5) for h in heads]
    k = [unit(c[:, WIDTH_B + h * hd:WIDTH_B + (h + 1) * hd]) for h in heads]
    kb = [k[h] * beta[h] for h in heads]
    eG = [jnp.exp(Gc[h]) for h in heads]
    rhs = [jnp.concatenate([c[:, 2 * WIDTH_B + h * hd:2 * WIDTH_B + (h + 1) * hd] * beta[h], kb[h] * eG[h]], axis=1)
           for h in heads]
    A = [jnp.where(strict, dot_t(kb[h], k[h]) * decay[h], 0.0) for h in heads]
    same_block = lambda b: jnp.right_shift(ri, b.bit_length() - 1) == jnp.right_shift(ci, b.bit_length() - 1)
    in_base = same_block(GDN_BASE)
    D = [jnp.where(in_base, A[h], 0.0) for h in heads]
    D2 = [dot(D[h], D[h]) for h in heads]
    D4 = [dot(D2[h], D2[h]) for h in heads]
    tm = [eye - D[h] for h in heads]
    tm = [tm[h] + dot(tm[h], D2[h]) for h in heads]
    tm = [tm[h] + dot(tm[h], D4[h]) for h in heads]
    b = GDN_BASE
    while b < C:
        level = same_block(2 * b) & jnp.logical_not(same_block(b))
        me = [dot(tm[h], jnp.where(level, A[h], 0.0)) for h in heads]
        tm = [tm[h] - dot(me[h], tm[h]) for h in heads]
        b *= 2
    sol = [dot(tm[h], rhs[h]) for h in heads]
    attn = [dot_t(q[h], k[h]) * decay[h] for h in heads]
    S = [s_ref[h] for h in heads]
    v_new = [sol[h][:, :hd] - dot(sol[h][:, hd:], S[h]) for h in heads]
    o = [dot(q[h] * eG[h], S[h]) + dot(attn[h], v_new[h]) for h in heads]
    for h in heads:
        GL = Gc[h][C - 1:C, :]
        s_ref[h] = S[h] * jnp.exp(GL) + _dot3(k[h] * jnp.exp(GL - Gc[h]), v_new[h], dn_first)
        gate = gate_ref[0, :, h * hd:(h + 1) * hd]
        on = o[h] * lax.rsqrt(jnp.mean(o[h] * o[h], axis=-1, keepdims=True) + EPS) * gn_ref[...]
        ob_ref[0, :, h * hd:(h + 1) * hd] = on * (gate * _sigmoid(gate))

    @pl.when(t == pl.num_programs(1) - 1)
    def _():
        sout_ref[0] = s_ref[...]
        tail_ref[0] = cur[C - tail:, :]


GDN_TAIL = 8
GDN_CHUNK = 128
GDN_BASE = 8


def _dot3(a, b, dims):
    f32 = jnp.float32
    bf16 = jnp.bfloat16
    a_hi = a.astype(bf16)
    b_hi = b.astype(bf16)
    a_lo = (a - a_hi.astype(f32)).astype(bf16)
    b_lo = (b - b_hi.astype(f32)).astype(bf16)
    dg = lambda x, y: lax.dot_general(x, y, dims, preferred_element_type=f32)
    return dg(a_hi, b_hi) + (dg(a_hi, b_lo) + dg(a_lo, b_hi))


def gdn_mixer(z3, blocks, conv_hist, delta0, conv_w, a_log, dt_bias, gdn_norm, chunk, pad_rows):
    B, Lx, _ = z3.shape
    f32 = jnp.float32
    wq = 3 * WIDTH_B
    hist = jnp.pad(conv_hist, ((0, 0), (GDN_TAIL - (CONV_W - 1), 0), (0, 0)))
    lane_vec = lambda v: jnp.zeros((1, LANE), f32).at[0, N_HEADS_B:2 * N_HEADS_B].set(v)
    const = lambda shape: pl.BlockSpec(shape, lambda b, t: (0,) * len(shape))
    per_b = lambda shape: pl.BlockSpec((1,) + shape, lambda b, t: (b,) + (0,) * len(shape))
    qb, gb, bb = blocks
    return pl.pallas_call(
        functools.partial(_gdn_kernel, pad_rows=pad_rows),
        grid=(B, Lx // chunk),
        in_specs=[pl.BlockSpec((1, chunk, wq), lambda b, t: (b, t, qb)),
                  pl.BlockSpec((1, chunk, WIDTH_B), lambda b, t: (b, t, gb)),
                  pl.BlockSpec((1, chunk, LANE), lambda b, t: (b, t, bb)),
                  per_b((GDN_TAIL, wq)), per_b((N_HEADS_B, HEAD_DIM_B, HEAD_DIM_B)),
                  const((CONV_W, wq)), const((1, LANE)), const((1, LANE)), const((1, HEAD_DIM_B))],
        out_specs=[pl.BlockSpec((1, chunk, WIDTH_B), lambda b, t: (b, t, 0)),
                   per_b((N_HEADS_B, HEAD_DIM_B, HEAD_DIM_B)), per_b((GDN_TAIL, wq))],
        out_shape=[jax.ShapeDtypeStruct((B, Lx, WIDTH_B), f32),
                   jax.ShapeDtypeStruct((B, N_HEADS_B, HEAD_DIM_B, HEAD_DIM_B), f32),
                   jax.ShapeDtypeStruct((B, GDN_TAIL, wq), f32)],
        scratch_shapes=[pltpu.VMEM((N_HEADS_B, HEAD_DIM_B, HEAD_DIM_B), f32), pltpu.VMEM((GDN_TAIL, wq), f32)],
        compiler_params=pltpu.CompilerParams(dimension_semantics=("parallel", "arbitrary")),
        name="gdn_mixer",
    )(z3, z3, z3, hist, delta0, conv_w, lane_vec(a_log), lane_vec(dt_bias), gdn_norm.reshape(1, HEAD_DIM_B))


def _merge_kernel(x_ref, oa_ref, ob_ref, oc_ref, g0_ref, g1_ref, g2_ref, wa_ref, wb_ref, wc_ref, wo_ref, o_ref,
                  *, rows_per_seq, pad_rows):
    f32 = jnp.float32
    bf16 = jnp.bfloat16
    mm = lambda a, w_ref: jnp.dot(a.astype(bf16), w_ref[...], preferred_element_type=f32)
    merged = (_sigmoid(g0_ref[...]) * mm(oa_ref[...], wa_ref) + _sigmoid(g1_ref[...]) * mm(ob_ref[...], wb_ref)
              + _sigmoid(g2_ref[...]) * mm(oc_ref[...], wc_ref))
    x = x_ref[...] + mm(merged, wo_ref)
    if pad_rows:
        tm = x.shape[0]
        row = pl.program_id(0) * tm + lax.broadcasted_iota(jnp.int32, x.shape, 0)
        x = jnp.where(row % rows_per_seq >= pad_rows, x, 0.0)
    o_ref[...] = x


def merge_residual(x, oA, oB, oC, z, gate_block, wa, wb, wc, wo, rows_per_seq, pad_rows):
    T = x.shape[0]
    tm = _token_tile(T)
    bf16 = jnp.bfloat16
    rows = lambda w, j=0: pl.BlockSpec((tm, w), lambda i, j=j: (i, j))
    const = lambda a: pl.BlockSpec(a.shape, lambda i: (0, 0))
    ws = [w.astype(bf16) for w in (wa, wb, wc, wo)]
    return pl.pallas_call(
        functools.partial(_merge_kernel, rows_per_seq=rows_per_seq, pad_rows=pad_rows),
        grid=(T // tm,),
        in_specs=[rows(D_MODEL), rows(WIDTH_A), rows(WIDTH_B), rows(WIDTH_C),
                  rows(D_MODEL, gate_block), rows(D_MODEL, gate_block + 1), rows(D_MODEL, gate_block + 2)]
                 + [const(w) for w in ws],
        out_specs=rows(D_MODEL),
        out_shape=jax.ShapeDtypeStruct((T, D_MODEL), jnp.float32),
        compiler_params=pltpu.CompilerParams(dimension_semantics=("parallel",), vmem_limit_bytes=VMEM_LIMIT),
        name="merge_residual",
    )(x, oA, oB, oC, z, z, z, *ws)


N_HC = 2 * PEER_HEADS
HALF_Q = PEER_QDIM // 2
PEER_EXPERT_TILE = 8 * N_KEYS
CAND_SUB = 8
PEER_DMA_PARTS = 2


def _extract_top16(s, iota_f):
    n = float(s.shape[0])
    rank = jnp.full(s.shape, PEER_TOPK, jnp.int32)
    vals = []
    for j in range(PEER_TOPK):
        m = jnp.max(s, axis=0, keepdims=True)
        idx = jnp.min(jnp.where(s == m, iota_f, n), axis=0, keepdims=True)
        hit = iota_f == idx
        rank = jnp.where(hit, j, rank)
        s = jnp.where(hit, -jnp.inf, s)
        vals.append(m)
    return vals, rank


def _peer_router_kernel(x_ref, g_ref, wqt_ref, sk_ref, h2_ref, lim_ref, rk2_ref, e1_ref, e2_ref,
                        qt_ref, s_ref, rank_ref, v_ref):
    f32 = jnp.float32
    tl = x_ref.shape[0]
    x = x_ref[...]
    h2 = x * lax.rsqrt(jnp.mean(x * x, axis=-1, keepdims=True) + EPS) * g_ref[...]
    h2b = h2.astype(jnp.bfloat16)
    h2_ref[...] = h2b
    qt_ref[...] = lax.dot_general(wqt_ref[...], h2b, (((1,), (1,)), ((), ())),
                                  preferred_element_type=f32).astype(jnp.bfloat16)

    def score_body(hc, carry):
        r0 = pl.multiple_of(hc * HALF_Q, HALF_Q)
        s_ref[hc] = jnp.dot(sk_ref[hc], qt_ref[pl.ds(r0, HALF_Q), :], preferred_element_type=f32)
        return carry

    lax.fori_loop(0, N_HC, score_body, 0)

    key_iota = lax.broadcasted_iota(jnp.int32, (N_KEYS, LANE), 0).astype(f32)

    def key_body(i, carry):
        hc = i // (tl // LANE)
        c0 = pl.multiple_of((i % (tl // LANE)) * LANE, LANE)
        s = s_ref[hc, :, pl.ds(c0, LANE)]
        vals, rank = _extract_top16(s, key_iota)
        rank_ref[hc, :, pl.ds(c0, LANE)] = rank
        v_ref[hc, :, pl.ds(c0, LANE)] = jnp.concatenate(vals, axis=0)
        return carry

    lax.fori_loop(0, N_HC * (tl // LANE), key_body, 0, unroll=2)

    sub = CAND_SUB
    n_mid = sub - 1
    n_cand = PEER_TOPK + n_mid * sub + (PEER_TOPK - sub)
    cand_iota = lax.broadcasted_iota(jnp.int32, (n_cand, LANE), 0).astype(f32)
    row8 = lax.broadcasted_iota(jnp.int32, (sub, LANE), 0)

    def head_body(i, carry):
        h = i // (tl // LANE)
        c0 = pl.multiple_of((i % (tl // LANE)) * LANE, LANE)
        cols = pl.ds(c0, LANE)
        v1 = v_ref[2 * h, :, cols]
        v2 = v_ref[2 * h + 1, :, cols]
        blocks = [v1[0:1, :] + v2]
        for r1 in range(1, sub):
            blocks.append(jnp.where(row8 < PEER_TOPK // (r1 + 1), v1[r1:r1 + 1, :] + v2[:sub, :], -jnp.inf))
        blocks.append(v1[sub:, :] + v2[0:1, :])
        best, crank = _extract_top16(jnp.concatenate(blocks, axis=0), cand_iota)
        z = jnp.zeros((1, LANE), f32)
        for k in range(PEER_TOPK):
            z = z + jnp.exp(best[k] - best[0])
        sel = jnp.where(crank < PEER_TOPK, 1.0, 0.0)
        rank1 = rank_ref[2 * h, :, cols]
        lim = jnp.zeros((N_KEYS, LANE), jnp.int32)
        for r1 in range(PEER_TOPK):
            if r1 == 0:
                cnt = jnp.sum(sel[:PEER_TOPK, :], axis=0, keepdims=True)
            elif r1 < sub:
                lo = PEER_TOPK + (r1 - 1) * sub
                cnt = jnp.sum(sel[lo:lo + sub, :], axis=0, keepdims=True)
            else:
                lo = PEER_TOPK + n_mid * sub + (r1 - sub)
                cnt = sel[lo:lo + 1, :]
            lim = jnp.where(rank1 == r1, cnt.astype(jnp.int32), lim)
        lim_ref[h, :, cols] = lim
        rk2_ref[h, :, cols] = rank_ref[2 * h + 1, :, cols].astype(f32).astype(jnp.bfloat16)
        e1_ref[h, :, cols] = jnp.exp(s_ref[2 * h, :, cols] - v1[0:1, :])
        e2_ref[h, :, cols] = (jnp.exp(s_ref[2 * h + 1, :, cols] - v2[0:1, :]) / z).astype(jnp.bfloat16)
        return carry

    lax.fori_loop(0, PEER_HEADS * (tl // LANE), head_body, 0, unroll=2)


def _gelu_tanh(x):
    return 0.5 * x * (1.0 + jnp.tanh(math.sqrt(2.0 / math.pi) * (x + 0.044715 * (x * x * x))))


BF16_ROWS = 16


def _rows_bf16(row):
    tile = jnp.broadcast_to(row, (BF16_ROWS, LANE)).astype(jnp.bfloat16)
    return jnp.concatenate([tile] * (N_KEYS // BF16_ROWS), axis=0)


def _peer_expert_kernel(x_ref, h2_ref, *refs):
    pu_refs, pvt_refs = refs[:PEER_DMA_PARTS], refs[PEER_DMA_PARTS:2 * PEER_DMA_PARTS]
    lim_ref, rk2_ref, e1_ref, e2_ref, o_ref, acc_ref, a_cur_ref, a_nxt_ref, c_ref, rk2_s, e2_s = refs[2 * PEER_DMA_PARTS:]
    f32 = jnp.float32
    bf16 = jnp.bfloat16
    zero = jnp.zeros((N_KEYS, LANE), bf16)
    e_step = pl.program_id(1)
    tl = h2_ref.shape[0]
    part = pu_refs[0].shape[0]
    n_i1 = PEER_DMA_PARTS * part // N_KEYS
    chunk = 2 * LANE if tl % (2 * LANE) == 0 else LANE

    @pl.when(e_step == 0)
    def _():
        acc_ref[...] = jnp.zeros_like(acc_ref)
        a_nxt_ref[...] = jnp.zeros_like(a_nxt_ref)
        rk2_s[...] = rk2_ref[...]
        e2_s[...] = e2_ref[...]

    a_cur_ref[...] = a_nxt_ref[...]
    for p, pu_ref in enumerate(pu_refs):
        a_nxt_ref[p * part:(p + 1) * part, :] = lax.dot_general(pu_ref[...], h2_ref[...], (((1,), (1,)), ((), ())),
                                                                preferred_element_type=f32)
    for c0 in range(0, tl, chunk):
        for l0 in range(c0, c0 + chunk, LANE):
            cols = slice(l0, l0 + LANE)
            for ii in range(n_i1):
                rows = slice(ii * N_KEYS, (ii + 1) * N_KEYS)
                w = jnp.zeros((N_KEYS, LANE), bf16)
                for h in range(PEER_HEADS):
                    sel = rk2_s[h, :, cols] < _rows_bf16(lim_ref[h, ii:ii + 1, cols].astype(f32))
                    w = w + jnp.where(sel, e2_s[h, :, cols], zero) * _rows_bf16(e1_ref[h, ii:ii + 1, cols])
                c_ref[rows, cols] = w * _gelu_tanh(a_cur_ref[rows, cols]).astype(bf16)
        cc = slice(c0, c0 + chunk)
        proj = [jnp.dot(pvt_ref[0], c_ref[p * part:(p + 1) * part, cc], preferred_element_type=f32)
                for p, pvt_ref in enumerate(pvt_refs)]
        acc_ref[:, cc] += functools.reduce(lambda a, b: a + b, proj)

    @pl.when(e_step == pl.num_programs(1) - 1)
    def _():
        o_ref[...] = x_ref[...] + acc_ref[...].T


def _token_tile(T, sizes=(640, 512, 384, 256, 128)):
    for tl in sizes:
        if T % tl == 0:
            return tl
    raise ValueError(f"token count {T} is not a multiple of {LANE}")


def peer_residual(x, g, wqt, sk, pu, pvt):
    T = x.shape[0]
    tl = _token_tile(T)
    nt = T // tl
    n_exp = pu.shape[0]
    head_shape = jax.ShapeDtypeStruct((PEER_HEADS, N_KEYS, T), jnp.int32)
    head_shape_f = jax.ShapeDtypeStruct((PEER_HEADS, N_KEYS, T), jnp.float32)
    head_shape_b = jax.ShapeDtypeStruct((PEER_HEADS, N_KEYS, T), jnp.bfloat16)
    head_spec = pl.BlockSpec((PEER_HEADS, N_KEYS, tl), lambda i: (0, 0, i))
    h2, lim, rk2, e1, e2 = pl.pallas_call(
        _peer_router_kernel,
        grid=(nt,),
        in_specs=[pl.BlockSpec((tl, D_MODEL), lambda i: (i, 0)),
                  pl.BlockSpec((1, D_MODEL), lambda i: (0, 0)),
                  pl.BlockSpec((PEER_HEADS * PEER_QDIM, D_MODEL), lambda i: (0, 0)),
                  pl.BlockSpec((N_HC, N_KEYS, HALF_Q), lambda i: (0, 0, 0))],
        out_specs=[pl.BlockSpec((tl, D_MODEL), lambda i: (i, 0)), head_spec, head_spec, head_spec, head_spec],
        out_shape=[jax.ShapeDtypeStruct((T, D_MODEL), jnp.bfloat16), head_shape, head_shape_b,
                   head_shape_f, head_shape_b],
        scratch_shapes=[pltpu.VMEM((PEER_HEADS * PEER_QDIM, tl), jnp.bfloat16),
                        pltpu.VMEM((N_HC, N_KEYS, tl), jnp.float32),
                        pltpu.VMEM((N_HC, N_KEYS, tl), jnp.int32),
                        pltpu.VMEM((N_HC, PEER_TOPK, tl), jnp.float32)],
        compiler_params=pltpu.CompilerParams(dimension_semantics=("parallel",), vmem_limit_bytes=VMEM_LIMIT),
        name="peer_router",
    )(x, g.reshape(1, D_MODEL), wqt, sk)

    tl = _token_tile(T, (768, 512, 640, 384, 256, 128))
    nt = T // tl
    te = PEER_EXPERT_TILE
    parts = PEER_DMA_PARTS
    part = te // parts
    n_tiles = n_exp // te
    nxt = lambda e: jnp.minimum(e, n_tiles - 1)
    cur = lambda e: jnp.maximum(e - 1, 0)
    i2_spec = pl.BlockSpec((PEER_HEADS, N_KEYS, tl), lambda i, e: (0, 0, i))
    i1_spec = pl.BlockSpec((PEER_HEADS, te // N_KEYS, tl), lambda i, e: (0, cur(e), i))
    return pl.pallas_call(
        _peer_expert_kernel,
        grid=(nt, n_tiles + 1),
        in_specs=[pl.BlockSpec((tl, D_MODEL), lambda i, e: (i, 0)),
                  pl.BlockSpec((tl, D_MODEL), lambda i, e: (i, 0)),
                  *[pl.BlockSpec((part, D_MODEL), lambda i, e, p=p: (parts * nxt(e) + p, 0)) for p in range(parts)],
                  *[pl.BlockSpec((1, D_MODEL, part), lambda i, e, p=p: (parts * cur(e) + p, 0, 0))
                    for p in range(parts)],
                  i1_spec, i2_spec, i1_spec, i2_spec],
        out_specs=pl.BlockSpec((tl, D_MODEL), lambda i, e: (i, 0)),
        out_shape=jax.ShapeDtypeStruct((T, D_MODEL), jnp.float32),
        scratch_shapes=[pltpu.VMEM((D_MODEL, tl), jnp.float32),
                        pltpu.VMEM((te, tl), jnp.float32),
                        pltpu.VMEM((te, tl), jnp.float32),
                        pltpu.VMEM((te, tl), jnp.bfloat16),
                        pltpu.VMEM((PEER_HEADS, N_KEYS, tl), jnp.bfloat16),
                        pltpu.VMEM((PEER_HEADS, N_KEYS, tl), jnp.bfloat16)],
        compiler_params=pltpu.CompilerParams(dimension_semantics=("parallel", "arbitrary"),
                                             vmem_limit_bytes=VMEM_LIMIT),
        name="peer_experts",
    )(x, h2, *([pu] * parts), *([pvt] * parts), lim, rk2, e1, e2)


C_QKVB = 3 * WIDTH_A
C_GB = C_QKVB + 3 * WIDTH_B
C_UC = C_GB + WIDTH_B
C_GATE = C_UC + WIDTH_C
C_BA = C_GATE + 3 * D_MODEL
D_IN2 = C_BA + LANE
S5_BLOCK = 256


def _reorder_w_in(w):
    cols = [w[:, OFF_QA:OFF_BETA], w[:, OFF_UC:OFF_GATE], w[:, OFF_GATE:], w[:, OFF_BETA:OFF_UC],
            jnp.zeros((w.shape[0], LANE - 2 * N_HEADS_B), w.dtype)]
    return jnp.concatenate(cols, axis=1).astype(jnp.bfloat16)


def trunk_layer(x, lp, layer_idx, att, prompt, k_past, v_past, conv_hist, delta0, ssm0_re, ssm0_im):
    f32 = jnp.float32
    B, Lx = x.shape[0], x.shape[1]
    T = B * Lx
    z = norm_matmul(x.reshape(T, D_MODEL), lp['norm1'], lp['w_in2'])

    qn, kn, knb, vb = qk_prep(z, lp['q_norm'], lp['k_norm'], _token_tile(T))
    lam_init = 0.8 - 0.6 * math.exp(-0.3 * layer_idx)
    lqk = lp['lambda_qk']
    lam = jnp.exp(jnp.sum(lqk[0] * lqk[1])) - jnp.exp(jnp.sum(lqk[2] * lqk[3])) + lam_init
    scalars = jnp.stack([lam, jnp.asarray(1.0 - lam_init, f32)]).astype(f32)
    r3 = lambda a: a.reshape(B, Lx, WIDTH_A)
    r4 = lambda a: a.reshape(N_HEADS_A, B, Lx, a.shape[-1])
    if prompt:
        oA = attn_prompt(r4(qn), r4(knb), r4(vb), scalars, att['far'], att['tiles'], lp['subln'])
    else:
        P = k_past.shape[1]
        oA = attn_sample(r4(qn), r4(knb), r4(vb), k_past.reshape(B, P, WIDTH_A), v_past.reshape(B, P, WIDTH_A),
                         scalars, att['bias_past'], att['bias_new'], lp['subln'])

    z3 = z.reshape(B, Lx, D_IN2)
    off = PAD if prompt else 0
    L = Lx - off
    kA = r3(kn)[:, off:].reshape(B, L, N_HEADS_A, 2 * HEAD_DIM_A)
    vA = z3[:, off:, 2 * WIDTH_A:3 * WIDTH_A].reshape(B, L, N_HEADS_A, 2 * HEAD_DIM_A)

    oB, delta, tail = gdn_mixer(z3, (C_QKVB // (3 * WIDTH_B), C_GB // WIDTH_B, C_BA // LANE), conv_hist, delta0,
                                lp['conv_w'], lp['a_log'], lp['dt_bias'], lp['gdn_norm'],
                                GDN_CHUNK if prompt else Lx, off)
    conv_state = tail[:, GDN_TAIL - (CONV_W - 1):]

    oC, ssm_re, ssm_im = s5_glu(z3, C_UC // WIDTH_C, ssm0_re, ssm0_im, lp['s5'], lp['w_glu'],
                                S5_BLOCK if prompt else Lx)

    x2 = merge_residual(x.reshape(T, D_MODEL), oA.reshape(T, WIDTH_A), oB.reshape(T, WIDTH_B),
                        oC.reshape(T, WIDTH_C), z, C_GATE // D_MODEL, lp['wb_a'], lp['wb_b'], lp['wb_c'],
                        lp['w_out'], Lx, off)

    x = peer_residual(x2, lp['norm2'], lp['peer_wqt'], lp['peer_sk'], lp['peer_pu'],
                      lp['peer_pvt']).reshape(B, Lx, D_MODEL)
    return (x, kA, vA, conv_state, delta, ssm_re, ssm_im)


def kernel(x_prompt, x_sample, cache_k, cache_v, state_conv, state_delta, state_ssm_re, state_ssm_im,
           meta_tokens, rel_bias, norm1_g, norm2_g, final_norm_g, w_in, q_norm_g, k_norm_g, lambda_qk,
           subln_g, conv_w, gdn_a_log, gdn_dt_bias, gdn_norm_g, s5_a_re, s5_a_im, s5_b_re, s5_b_im,
           s5_c_re, s5_c_im, s5_d, s5_log_dt, w_glu, w_branch_a, w_branch_b, w_branch_c, w_out,
           peer_wq, peer_subkeys, peer_u, peer_v):
    f32 = jnp.float32
    bf16 = jnp.bfloat16
    params = [dict(norm1=norm1_g[l], norm2=norm2_g[l], w_in2=_reorder_w_in(w_in[l]), q_norm=q_norm_g[l],
                   k_norm=k_norm_g[l], lambda_qk=lambda_qk[l], subln=subln_g[l], conv_w=conv_w[l],
                   a_log=gdn_a_log[l], dt_bias=gdn_dt_bias[l], gdn_norm=gdn_norm_g[l],
                   s5=s5_params(s5_a_re[l], s5_a_im[l], s5_b_re[l], s5_b_im[l], s5_c_re[l], s5_c_im[l], s5_d[l],
                                s5_log_dt[l]),
                   w_glu=w_glu[l], wb_a=w_branch_a[l], wb_b=w_branch_b[l], wb_c=w_branch_c[l], w_out=w_out[l],
                   peer_wqt=peer_wq[l].T.astype(bf16),
                   peer_sk=peer_subkeys[l].reshape(N_HC, N_KEYS, HALF_Q).astype(bf16),
                   peer_pu=peer_u[l].astype(bf16), peer_pvt=peer_v[l].astype(bf16).reshape(-1, PEER_EXPERT_TILE // PEER_DMA_PARTS, D_MODEL).transpose(0, 2, 1))
              for l in range(DEPTH)]

    qpos = jnp.arange(ATT_TQ, dtype=jnp.int32)
    tiles = jnp.stack([t5_bias(qpos, d * ATT_TK + jnp.arange(ATT_TK, dtype=jnp.int32), rel_bias)
                       for d in (-2, -1, 0)], axis=1)
    far = t5_bias(jnp.full((1,), 2 * ATT_TK, jnp.int32), jnp.zeros((1,), jnp.int32), rel_bias).reshape(N_HEADS_A)
    P, Ls = cache_k.shape[2], x_sample.shape[1]
    bias_s = t5_bias(P + jnp.arange(Ls, dtype=jnp.int32), jnp.arange(P + Ls, dtype=jnp.int32), rel_bias)
    att_p = dict(tiles=tiles, far=far)
    att_s = dict(bias_past=bias_s[:, :, :P], bias_new=bias_s[:, :, P:])

    B = x_prompt.shape[0]
    xp = jnp.concatenate([jnp.zeros((B, PAD, D_MODEL), f32),
                          jnp.broadcast_to(meta_tokens[None], (B, N_META, D_MODEL)), x_prompt], axis=1)
    outs_p = [[] for _ in range(6)]
    for l in range(DEPTH):
        res = trunk_layer(
            xp, params[l], l, att_p, True, None, None,
            jnp.zeros((B, CONV_W - 1, 3 * WIDTH_B), f32),
            jnp.zeros((B, N_HEADS_B, HEAD_DIM_B, HEAD_DIM_B), f32),
            jnp.zeros((B, S5_GROUPS, S5_STATE), f32), jnp.zeros((B, S5_GROUPS, S5_STATE), f32))
        xp = res[0]
        for acc, r in zip(outs_p, res[1:]):
            acc.append(r)
    y_prompt = rmsnorm(xp, final_norm_g)[:, FRONT:]

    xs = x_sample
    outs_s = [[] for _ in range(6)]
    for l in range(DEPTH):
        res = trunk_layer(
            xs, params[l], l, att_s, False, cache_k[l], cache_v[l], state_conv[l],
            state_delta[l], state_ssm_re[l], state_ssm_im[l])
        xs = res[0]
        for acc, r in zip(outs_s, res[1:]):
            acc.append(r)
    y_sample = rmsnorm(xs, final_norm_g)

    kp, vp, cp, dp, srp, sip = [jnp.stack(a) for a in outs_p]
    ks_, vs_, cs_, ds_, srs, sis = [jnp.stack(a) for a in outs_s]
    return (y_prompt, y_sample, kp, vp, ks_, vs_, cp, cs_, dp, ds_, srp, sip, srs, sis)
```

```python
import functools
import math

import jax
import jax.numpy as jnp
from jax import lax
from jax.experimental import pallas as pl
from jax.experimental.pallas import tpu as pltpu

D_MODEL = 1024
DEPTH = 4
CHUNK = 64
N_META = 16
EPS = 1e-6
NEG_INF = -1e30
N_HEADS_A = 4
HEAD_DIM_A = 64
NUM_BUCKETS = 32
MAX_DISTANCE = 128
N_HEADS_B = 4
HEAD_DIM_B = 128
CONV_W = 4
S5_GROUP = 16
S5_GROUPS = 32
S5_STATE = 64
PEER_HEADS = 8
PEER_QDIM = 256
N_KEYS = 128
PEER_TOPK = 16

WIDTH_A = N_HEADS_A * 2 * HEAD_DIM_A
WIDTH_B = N_HEADS_B * HEAD_DIM_B
WIDTH_C = S5_GROUPS * S5_GROUP
OFF_QA = 0
OFF_KA = OFF_QA + WIDTH_A
OFF_VA = OFF_KA + WIDTH_A
OFF_QKVB = OFF_VA + WIDTH_A
OFF_GB = OFF_QKVB + 3 * WIDTH_B
OFF_BETA = OFF_GB + WIDTH_B
OFF_ALPHA = OFF_BETA + N_HEADS_B
OFF_UC = OFF_ALPHA + N_HEADS_B
OFF_GATE = OFF_UC + WIDTH_C
D_IN = OFF_GATE + 3 * D_MODEL

LANE = 128
VMEM_LIMIT = 56 * 1024 * 1024
WEIGHT_TILE_BYTES = 6 * 1024 * 1024
HIGHEST = lax.Precision.HIGHEST


def _norm_mm_kernel(x_ref, g_ref, w_ref, o_ref, h_ref):
    @pl.when(pl.program_id(1) == 0)
    def _():
        x = x_ref[...]
        h = x * lax.rsqrt(jnp.mean(x * x, axis=-1, keepdims=True) + EPS) * g_ref[...]
        h_ref[...] = h.astype(jnp.bfloat16)

    o_ref[...] = jnp.dot(h_ref[...], w_ref[...], preferred_element_type=jnp.float32)


def norm_matmul(x, g, wb, tm=512):
    M, K = x.shape
    N = wb.shape[1]
    n_lanes = N // LANE
    tn = LANE * max(d for d in range(1, n_lanes + 1) if n_lanes % d == 0 and d * LANE * K * 2 <= WEIGHT_TILE_BYTES)
    tm = min(tm, M)
    return pl.pallas_call(
        _norm_mm_kernel,
        grid=(pl.cdiv(M, tm), N // tn),
        in_specs=[pl.BlockSpec((tm, K), lambda i, j: (i, 0)),
                  pl.BlockSpec((1, K), lambda i, j: (0, 0)),
                  pl.BlockSpec((K, tn), lambda i, j: (0, j))],
        out_specs=pl.BlockSpec((tm, tn), lambda i, j: (i, j)),
        out_shape=jax.ShapeDtypeStruct((M, N), jnp.float32),
        scratch_shapes=[pltpu.VMEM((tm, K), jnp.bfloat16)],
        compiler_params=pltpu.CompilerParams(dimension_semantics=("parallel", "arbitrary"),
                                             vmem_limit_bytes=VMEM_LIMIT),
        name="norm_proj",
    )(x, g.reshape(1, K), wb)


def rmsnorm(x, g):
    xf = x.astype(jnp.float32)
    y = xf * lax.rsqrt(jnp.mean(xf * xf, axis=-1, keepdims=True) + EPS)
    return (y * g.astype(jnp.float32)).astype(x.dtype)


def t5_bias(q_pos, k_pos, table):
    rel = k_pos[None, :] - q_pos[:, None]
    half = NUM_BUCKETS // 2
    exact = half // 2
    n = jnp.abs(rel)
    nf = jnp.maximum(n, 1).astype(jnp.float32)
    far = exact + (jnp.log(nf / exact) / math.log(MAX_DISTANCE / exact) * (half - exact)).astype(jnp.int32)
    bucket = jnp.where(rel > 0, half, 0) + jnp.where(n < exact, n, jnp.minimum(far, half - 1))
    hit = bucket[None, :, :] == jnp.arange(NUM_BUCKETS, dtype=bucket.dtype)[:, None, None]
    tab = table.astype(jnp.float32)
    return jnp.stack([jnp.sum(jnp.where(hit, tab[:, h, None, None], 0.0), axis=0) for h in range(tab.shape[1])])


FRONT = 256
PAD = FRONT - N_META
ATT_TQ = 256
ATT_TK = 256


def _qk_prep_kernel(q_ref, k_ref, v_ref, gq_ref, gk_ref, seg_ref, qn_ref, kn_ref, knb_ref, vb_ref):
    f32 = jnp.float32
    bf16 = jnp.bfloat16
    seg = seg_ref[...]
    hd = 2 * HEAD_DIM_A

    def norm(x, g):
        ms = jnp.dot(x * x, seg, precision=HIGHEST, preferred_element_type=f32)
        return x * lax.rsqrt(ms + EPS) * g

    qn = (norm(q_ref[...], gq_ref[...]) * (HEAD_DIM_A ** -0.5)).astype(bf16)
    kn = norm(k_ref[...], gk_ref[...])
    kn_ref[...] = kn
    knb = kn.astype(bf16)
    v = v_ref[...].astype(bf16)
    lane = lax.broadcasted_iota(jnp.int32, (v.shape[0], hd), 1)
    ones_col = jnp.where(lane == 0, 1.0, 0.0).astype(bf16)
    for h in range(N_HEADS_A):
        cols = slice(h * hd, (h + 1) * hd)
        qn_ref[h] = qn[:, cols]
        knb_ref[h] = knb[:, cols]
        vb_ref[h] = jnp.concatenate([v[:, cols], ones_col], axis=1)


def qk_prep(z, gq, gk, tm):
    T = z.shape[0]
    hd = 2 * HEAD_DIM_A
    seg = jnp.kron(jnp.eye(WIDTH_A // HEAD_DIM_A, dtype=jnp.float32),
                   jnp.full((HEAD_DIM_A, HEAD_DIM_A), 1.0 / HEAD_DIM_A, jnp.float32))
    row = lambda j: pl.BlockSpec((tm, WIDTH_A), lambda i, j=j: (i, j))
    const = lambda shape: pl.BlockSpec(shape, lambda i: (0,) * len(shape))
    heads = lambda w: pl.BlockSpec((N_HEADS_A, tm, w), lambda i: (0, i, 0))
    hshape = lambda w: jax.ShapeDtypeStruct((N_HEADS_A, T, w), jnp.bfloat16)
    return pl.pallas_call(
        _qk_prep_kernel,
        grid=(T // tm,),
        in_specs=[row(0), row(1), row(2), const((1, WIDTH_A)), const((1, WIDTH_A)), const((WIDTH_A, WIDTH_A))],
        out_specs=[heads(hd), pl.BlockSpec((tm, WIDTH_A), lambda i: (i, 0)), heads(hd), heads(2 * hd)],
        out_shape=[hshape(hd), jax.ShapeDtypeStruct((T, WIDTH_A), jnp.float32), hshape(hd), hshape(2 * hd)],
        compiler_params=pltpu.CompilerParams(dimension_semantics=("parallel",)),
        name="qk_prep",
    )(z, z, z, jnp.tile(gq, WIDTH_A // HEAD_DIM_A).reshape(1, WIDTH_A),
      jnp.tile(gk, WIDTH_A // HEAD_DIM_A).reshape(1, WIDTH_A), seg)


def _split_maps(q):
    lane = lax.broadcasted_iota(jnp.int32, q.shape, 1)
    zero = jnp.zeros_like(q)
    return jnp.where(lane < HEAD_DIM_A, q, zero), jnp.where(lane >= HEAD_DIM_A, q, zero)


def _subln(o, g, scale):
    return o * lax.rsqrt(jnp.mean(o * o, axis=-1, keepdims=True) + EPS) * g * scale


def _attn_prompt_kernel(sc_ref, far_ref, q_ref, k_ref, v_ref, bias_ref, g_ref, o_ref, m_ref, acc_ref,
                        sa_ref, sb_ref):
    f32 = jnp.float32
    h = pl.program_id(1)
    qi = pl.program_id(2)
    tq, tk = ATT_TQ, ATT_TK
    hd = 2 * HEAD_DIM_A
    qs = jnp.concatenate(_split_maps(q_ref[0, 0]), axis=0)
    m_ref[...] = jnp.full(m_ref.shape, NEG_INF, f32)
    acc_ref[...] = jnp.zeros(acc_ref.shape, f32)
    far = far_ref[h]

    def logits(k0, width, general):
        k0 = pl.multiple_of(k0, tk)
        kt = k_ref[0, 0, pl.ds(k0, width), :]
        s = lax.dot_general(qs, kt, (((1,), (1,)), ((), ())), preferred_element_type=f32)
        if general:
            kj = k0 // tk
            bias = bias_ref[0, jnp.clip(kj - qi + 2, 0, 2)]
            qpos = qi * tq + lax.broadcasted_iota(jnp.int32, (tq, tk), 0)
            kpos = k0 + lax.broadcasted_iota(jnp.int32, (tq, tk), 1)
            qchunk = jnp.where(qpos < FRONT, 0, 1 + jnp.right_shift(qpos - FRONT, 6))
            kchunk = jnp.where(kpos < FRONT, 0, 1 + jnp.right_shift(kpos - FRONT, 6))
            mask = (kpos >= PAD) & (kchunk <= qchunk)
            s = jnp.where(jnp.concatenate([mask, mask], axis=0), s + jnp.concatenate([bias, bias], axis=0), NEG_INF)
        else:
            s = s + far
        return s

    def accumulate(s, k0):
        width = s.shape[1]
        vt = v_ref[0, 0, pl.ds(pl.multiple_of(k0, tk), width), :]
        n_lane_tiles = width // LANE
        smax = s[:, :LANE]
        for j in range(1, n_lane_tiles):
            smax = jnp.maximum(smax, s[:, j * LANE:(j + 1) * LANE])
        m_old = m_ref[...]
        m_new = jnp.maximum(m_old, jnp.broadcast_to(jnp.max(smax, axis=1, keepdims=True), m_old.shape))
        alpha = jnp.exp(m_old - m_new)
        p = jnp.exp(s - jnp.concatenate([m_new] * n_lane_tiles, axis=1))
        acc_ref[...] = (jnp.concatenate([alpha, alpha], axis=1) * acc_ref[...]
                        + jnp.dot(p.astype(jnp.bfloat16), vt, preferred_element_type=f32))
        m_ref[...] = m_new

    def tile(k0, width, general):
        accumulate(logits(k0, width, general), k0)

    tile(0, tk, True)

    wide = 2 * tk
    n_far = jnp.maximum(qi - 2, 0)
    n_single = n_far % 4
    for i in range(3):
        @pl.when(n_single > i)
        def _(i=i):
            tile((1 + i) * tk, tk, False)

    base = (1 + n_single) * tk
    n_pairs = n_far // 4

    @pl.when(n_pairs > 0)
    def _():
        sa_ref[...] = logits(base, wide, False)

    def pair_body(j, carry):
        k0 = base + 2 * j * wide
        sb_ref[...] = logits(k0 + wide, wide, False)
        accumulate(sa_ref[...], k0)
        nxt = jnp.minimum(k0 + 2 * wide, base + (2 * n_pairs - 1) * wide)
        sa_ref[...] = logits(nxt, wide, False)
        accumulate(sb_ref[...], k0 + wide)
        return carry

    lax.fori_loop(0, n_pairs, pair_body, 0)

    @pl.when(qi >= 2)
    def _():
        tile((qi - 1) * tk, tk, True)

    @pl.when(qi >= 1)
    def _():
        tile(qi * tk, tk, True)

    acc = acc_ref[...]
    out = acc[:, :hd] / acc[:, hd:hd + 1]
    o = out[:tq] - sc_ref[0] * out[tq:]
    o_ref[0] = _subln(o, g_ref[...], sc_ref[1])


def attn_prompt(qn, knb, vb, scalars, far, bias_tiles, subln_g):
    _, B, Lp, _ = qn.shape
    hd = 2 * HEAD_DIM_A
    smem = pl.BlockSpec(memory_space=pltpu.SMEM)
    return pl.pallas_call(
        _attn_prompt_kernel,
        grid=(B, N_HEADS_A, Lp // ATT_TQ),
        in_specs=[smem, smem,
                  pl.BlockSpec((1, 1, ATT_TQ, hd), lambda b, h, i: (h, b, i, 0)),
                  pl.BlockSpec((1, 1, Lp, hd), lambda b, h, i: (h, b, 0, 0)),
                  pl.BlockSpec((1, 1, Lp, 2 * hd), lambda b, h, i: (h, b, 0, 0)),
                  pl.BlockSpec((1, 3, ATT_TQ, ATT_TK), lambda b, h, i: (h, 0, 0, 0)),
                  pl.BlockSpec((1, hd), lambda b, h, i: (0, 0))],
        out_specs=pl.BlockSpec((1, ATT_TQ, hd), lambda b, h, i: (b, i, h)),
        out_shape=jax.ShapeDtypeStruct((B, Lp, WIDTH_A), jnp.float32),
        scratch_shapes=[pltpu.VMEM((2 * ATT_TQ, hd), jnp.float32), pltpu.VMEM((2 * ATT_TQ, 2 * hd), jnp.float32),
                        pltpu.VMEM((2 * ATT_TQ, 2 * ATT_TK), jnp.float32),
                        pltpu.VMEM((2 * ATT_TQ, 2 * ATT_TK), jnp.float32)],
        compiler_params=pltpu.CompilerParams(dimension_semantics=("parallel", "parallel", "parallel"),
                                             vmem_limit_bytes=VMEM_LIMIT),
        name="diff_attn_prompt",
    )(scalars, far, qn, knb, vb, bias_tiles, subln_g.reshape(1, hd))


def _attn_sample_kernel(sc_ref, q_ref, kp_ref, vp_ref, kn_ref, vn_ref, bp_ref, bn_ref, g_ref, o_ref):
    f32 = jnp.float32
    bf16 = jnp.bfloat16
    qs = _split_maps(q_ref[0, 0])
    kp = kp_ref[0].astype(bf16)
    vp = vp_ref[0].astype(bf16)
    kn = kn_ref[0, 0]
    vn = vn_ref[0, 0][:, :2 * HEAD_DIM_A]
    dn = (((1,), (1,)), ((), ()))
    outs = []
    for c in range(2):
        sp = lax.dot_general(qs[c], kp, dn, preferred_element_type=f32) + bp_ref[0]
        sn = lax.dot_general(qs[c], kn, dn, preferred_element_type=f32) + bn_ref[0]
        m = jnp.maximum(jnp.max(sp, axis=1, keepdims=True), jnp.max(sn, axis=1, keepdims=True))
        pp = jnp.exp(sp - m)
        pn = jnp.exp(sn - m)
        l = jnp.sum(pp, axis=1, keepdims=True) + jnp.sum(pn, axis=1, keepdims=True)
        acc = (jnp.dot(pp.astype(bf16), vp, preferred_element_type=f32)
               + jnp.dot(pn.astype(bf16), vn, preferred_element_type=f32))
        outs.append(acc / l)
    o_ref[0] = _subln(outs[0] - sc_ref[0] * outs[1], g_ref[...], sc_ref[1])


def attn_sample(qn, knb, vb, k_past, v_past, scalars, bias_past, bias_new, subln_g):
    _, B, L, _ = qn.shape
    P = k_past.shape[1]
    hd = 2 * HEAD_DIM_A
    new = pl.BlockSpec((1, L, hd), lambda b, h: (b, 0, h))
    new_h = lambda w: pl.BlockSpec((1, 1, L, w), lambda b, h: (h, b, 0, 0))
    past = pl.BlockSpec((1, P, hd), lambda b, h: (b, 0, h))
    return pl.pallas_call(
        _attn_sample_kernel,
        grid=(B, N_HEADS_A),
        in_specs=[pl.BlockSpec(memory_space=pltpu.SMEM), new_h(hd), past, past, new_h(hd), new_h(2 * hd),
                  pl.BlockSpec((1, L, P), lambda b, h: (h, 0, 0)),
                  pl.BlockSpec((1, L, L), lambda b, h: (h, 0, 0)),
                  pl.BlockSpec((1, hd), lambda b, h: (0, 0))],
        out_specs=new,
        out_shape=jax.ShapeDtypeStruct((B, L, WIDTH_A), jnp.float32),
        compiler_params=pltpu.CompilerParams(dimension_semantics=("parallel", "parallel")),
        name="diff_attn_sample",
    )(scalars, qn, k_past, v_past, knb, vb, bias_past, bias_new, subln_g.reshape(1, hd))


S5_N = S5_GROUPS * S5_STATE
S5_SLAB = 8
S5_LANES = 512


def _sigmoid(x):
    return 1.0 / (1.0 + jnp.exp(-x))


def _s5_kernel(u_ref, x0_ref, bmat_ref, cmat_ref, lamp_ref, ppow_ref, d_ref, wglu_ref, oc_ref, xf_ref,
               bu_ref, carry_ref):
    f32 = jnp.float32
    bf16 = jnp.bfloat16
    t = pl.program_id(1)
    tb = u_ref.shape[1]

    @pl.when(t == 0)
    def _():
        carry_ref[...] = x0_ref[0]

    u = u_ref[0]
    bu_ref[...] = jnp.dot(u.astype(bf16), bmat_ref[...], preferred_element_type=f32)

    row = lax.broadcasted_iota(jnp.int32, (S5_SLAB, S5_LANES), 0)
    for c in range(S5_N // S5_LANES):
        re = slice(c * S5_LANES, (c + 1) * S5_LANES)
        im = slice(S5_N + c * S5_LANES, S5_N + (c + 1) * S5_LANES)

        def slab(i, carry):
            cre, cim = carry
            rows = pl.ds(pl.multiple_of(i * S5_SLAB, S5_SLAB), S5_SLAB)
            yre = bu_ref[rows, re]
            yim = bu_ref[rows, im]
            for s in range(3):
                sh = 1 << s
                sre = jnp.where(row >= sh, pltpu.roll(yre, sh, 0), 0.0)
                sim = jnp.where(row >= sh, pltpu.roll(yim, sh, 0), 0.0)
                lr = lamp_ref[s, 0, :, re]
                li = lamp_ref[s, 1, :, re]
                yre, yim = yre + (lr * sre - li * sim), yim + (lr * sim + li * sre)
            pr = ppow_ref[0, :, re]
            pi = ppow_ref[1, :, re]
            yre, yim = yre + (pr * cre - pi * cim), yim + (pr * cim + pi * cre)
            bu_ref[rows, re] = yre
            bu_ref[rows, im] = yim
            last = S5_SLAB - 1
            return (jnp.broadcast_to(yre[last:last + 1, :], yre.shape),
                    jnp.broadcast_to(yim[last:last + 1, :], yim.shape))

        cre, cim = lax.fori_loop(0, tb // S5_SLAB, slab, (carry_ref[:, re], carry_ref[:, im]))
        carry_ref[:, re] = cre
        carry_ref[:, im] = cim

    y = jnp.dot(bu_ref[...].astype(bf16), cmat_ref[...], preferred_element_type=f32) + d_ref[...] * u
    gl = jnp.dot(_gelu_tanh(y).astype(bf16), wglu_ref[...], preferred_element_type=f32)
    oc_ref[0] = gl[:, :WIDTH_C] * _sigmoid(gl[:, WIDTH_C:])

    @pl.when(t == pl.num_programs(1) - 1)
    def _():
        xf_ref[0] = carry_ref[...]


def s5_params(a_re, a_im, b_re, b_im, c_re, c_im, d, log_dt):
    f32 = jnp.float32
    lam = lax.complex(a_re, a_im)
    lam_bar = jnp.exp(lam * jnp.exp(log_dt)[:, None])
    b_bar = ((lam_bar - 1.0) / lam)[..., None] * lax.complex(b_re, b_im)
    eye = jnp.eye(S5_GROUPS, dtype=f32)
    bd_in = lambda m: jnp.einsum('gpi,gh->gihp', m, eye).reshape(WIDTH_C, S5_N)
    bd_out = lambda m: jnp.einsum('gip,gh->gphi', m, eye).reshape(S5_N, WIDTH_C)
    bmat = jnp.concatenate([bd_in(b_bar.real), bd_in(b_bar.imag)], axis=1).astype(jnp.bfloat16)
    cmat = jnp.concatenate([bd_out(c_re), bd_out(-c_im)], axis=0).astype(jnp.bfloat16)
    lb = lam_bar.reshape(S5_N)
    rep = lambda v: jnp.broadcast_to(v[None, :], (S5_SLAB, S5_N))
    pows = [lb, lb * lb, (lb * lb) * (lb * lb)]
    lamp = jnp.stack([jnp.stack([rep(p.real), rep(p.imag)]) for p in pows])
    run = [lb]
    for _ in range(S5_SLAB - 1):
        run.append(run[-1] * lb)
    pp = jnp.stack(run)
    ppow = jnp.stack([pp.real, pp.imag])
    return dict(bmat=bmat, cmat=cmat, lamp=lamp.astype(f32), ppow=ppow.astype(f32), d=d.reshape(1, WIDTH_C))


def s5_glu(z3, col_block, x0_re, x0_im, sp, w_glu, tb):
    B, Lx, _ = z3.shape
    x0 = jnp.concatenate([x0_re.reshape(B, S5_N), x0_im.reshape(B, S5_N)], axis=1)
    x0 = jnp.broadcast_to(x0[:, None, :], (B, S5_SLAB, 2 * S5_N))
    const = lambda a: pl.BlockSpec(a.shape, lambda b, t: (0,) * a.ndim)
    wg = w_glu.astype(jnp.bfloat16)
    oc, xf = pl.pallas_call(
        _s5_kernel,
        grid=(B, Lx // tb),
        in_specs=[pl.BlockSpec((1, tb, WIDTH_C), lambda b, t: (b, t, col_block)),
                  pl.BlockSpec((1, S5_SLAB, 2 * S5_N), lambda b, t: (b, 0, 0)),
                  const(sp['bmat']), const(sp['cmat']), const(sp['lamp']), const(sp['ppow']), const(sp['d']),
                  const(wg)],
        out_specs=[pl.BlockSpec((1, tb, WIDTH_C), lambda b, t: (b, t, 0)),
                   pl.BlockSpec((1, S5_SLAB, 2 * S5_N), lambda b, t: (b, 0, 0))],
        out_shape=[jax.ShapeDtypeStruct((B, Lx, WIDTH_C), jnp.float32),
                   jax.ShapeDtypeStruct((B, S5_SLAB, 2 * S5_N), jnp.float32)],
        scratch_shapes=[pltpu.VMEM((tb, 2 * S5_N), jnp.float32), pltpu.VMEM((S5_SLAB, 2 * S5_N), jnp.float32)],
        compiler_params=pltpu.CompilerParams(dimension_semantics=("parallel", "arbitrary"),
                                             vmem_limit_bytes=VMEM_LIMIT),
        name="s5_glu",
    )(z3, x0, sp['bmat'], sp['cmat'], sp['lamp'], sp['ppow'], sp['d'], wg)
    shape = (B, S5_GROUPS, S5_STATE)
    return oc, xf[:, 0, :S5_N].reshape(shape), xf[:, 0, S5_N:].reshape(shape)


def _gdn_kernel(qkv_ref, gate_ref, ba_ref, hist_ref, s0_ref, cw_ref, av_ref, dt_ref, gn_ref,
                ob_ref, sout_ref, tail_ref, s_ref, prev_ref, *, pad_rows):
    f32 = jnp.float32
    t = pl.program_id(1)
    C = qkv_ref.shape[1]
    hd = HEAD_DIM_B
    dn_last = (((1,), (1,)), ((), ()))
    dn_first = (((0,), (0,)), ((), ()))
    dot = lambda a, b: _dot3(a, b, (((1,), (0,)), ((), ())))

    @pl.when(t == 0)
    def _():
        s_ref[...] = s0_ref[0]
        prev_ref[...] = hist_ref[0]

    cur = qkv_ref[0]
    tail = prev_ref.shape[0]
    ext = jnp.concatenate([prev_ref[...], cur], axis=0)
    w = cw_ref[...]
    conv = cur * w[CONV_W - 1:CONV_W, :]
    for i in range(CONV_W - 1):
        lo = tail - (CONV_W - 1) + i
        conv = conv + ext[lo:lo + C, :] * w[i:i + 1, :]
    prev_ref[...] = cur[C - tail:, :]
    c = conv * _sigmoid(conv)

    ba = ba_ref[0]
    xg = ba + dt_ref[...]
    softplus = jnp.maximum(xg, 0.0) + jnp.log(1.0 + jnp.exp(-jnp.abs(xg)))
    g_all = -jnp.exp(av_ref[...]) * softplus
    if pad_rows:
        grow = t * C + lax.broadcasted_iota(jnp.int32, g_all.shape, 0)
        g_all = jnp.where(grow >= pad_rows, g_all, 0.0)
    ri = lax.broadcasted_iota(jnp.int32, (C, C), 0)
    ci = lax.broadcasted_iota(jnp.int32, (C, C), 1)
    tri = ri >= ci
    strict = ri > ci
    tril = jnp.where(tri, 1.0, 0.0).astype(f32)
    G_col = jnp.dot(tril, g_all, precision=HIGHEST, preferred_element_type=f32)
    G_row = lax.dot_general(g_all, tril, (((0,), (1,)), ((), ())), precision=HIGHEST,
                            preferred_element_type=f32)
    eye = jnp.where(ri == ci, 1.0, 0.0).astype(f32)

    heads = range(N_HEADS_B)
    dot_t = lambda a, b: _dot3(a, b, dn_last)
    Gc = [G_col[:, N_HEADS_B + h:N_HEADS_B + h + 1] for h in heads]
    Gr = [G_row[N_HEADS_B + h:N_HEADS_B + h + 1, :] for h in heads]
    decay = [jnp.where(tri, jnp.exp(jnp.minimum(Gc[h] - Gr[h], 0.0)), 0.0) for h in heads]
    beta = [_sigmoid(ba[:, h:h + 1]) for h in heads]
    unit = lambda a: a * lax.rsqrt(jnp.sum(a * a, axis=-1, keepdims=True) + EPS)
    q = [unit(c[:, h * hd:(h + 1) * hd]) * (hd ** -0.5) for h in heads]
    k = [unit(c[:, WIDTH_B + h * hd:WIDTH_B + (h + 1) * hd]) for h in heads]
    kb = [k[h] * beta[h] for h in heads]
    eG = [jnp.exp(Gc[h]) for h in heads]
    rhs = [jnp.concatenate([c[:, 2 * WIDTH_B + h * hd:2 * WIDTH_B + (h + 1) * hd] * beta[h], kb[h] * eG[h]], axis=1)
           for h in heads]
    A = [jnp.where(strict, dot_t(kb[h], k[h]) * decay[h], 0.0) for h in heads]
    same_block = lambda b: jnp.right_shift(ri, b.bit_length() - 1) == jnp.right_shift(ci, b.bit_length() - 1)
    in_base = same_block(GDN_BASE)
    D = [jnp.where(in_base, A[h], 0.0) for h in heads]
    D2 = [dot(D[h], D[h]) for h in heads]
    D4 = [dot(D2[h], D2[h]) for h in heads]
    tm = [eye - D[h] for h in heads]
    tm = [tm[h] + dot(tm[h], D2[h]) for h in heads]
    tm = [tm[h] + dot(tm[h], D4[h]) for h in heads]
    b = GDN_BASE
    while b < C:
        level = same_block(2 * b) & jnp.logical_not(same_block(b))
        me = [dot(tm[h], jnp.where(level, A[h], 0.0)) for h in heads]
        tm = [tm[h] - dot(me[h], tm[h]) for h in heads]
        b *= 2
    sol = [dot(tm[h], rhs[h]) for h in heads]
    attn = [dot_t(q[h], k[h]) * decay[h] for h in heads]
    S = [s_ref[h] for h in heads]
    v_new = [sol[h][:, :hd] - dot(sol[h][:, hd:], S[h]) for h in heads]
    o = [dot(q[h] * eG[h], S[h]) + dot(attn[h], v_new[h]) for h in heads]
    for h in heads:
        GL = Gc[h][C - 1:C, :]
        s_ref[h] = S[h] * jnp.exp(GL) + _dot3(k[h] * jnp.exp(GL - Gc[h]), v_new[h], dn_first)
        gate = gate_ref[0, :, h * hd:(h + 1) * hd]
        on = o[h] * lax.rsqrt(jnp.mean(o[h] * o[h], axis=-1, keepdims=True) + EPS) * gn_ref[...]
        ob_ref[0, :, h * hd:(h + 1) * hd] = on * (gate * _sigmoid(gate))

    @pl.when(t == pl.num_programs(1) - 1)
    def _():
        sout_ref[0] = s_ref[...]
        tail_ref[0] = cur[C - tail:, :]


GDN_TAIL = 8
GDN_CHUNK = 128
GDN_BASE = 8


def _dot3(a, b, dims):
    f32 = jnp.float32
    bf16 = jnp.bfloat16
    a_hi = a.astype(bf16)
    b_hi = b.astype(bf16)
    a_lo = (a - a_hi.astype(f32)).astype(bf16)
    b_lo = (b - b_hi.astype(f32)).astype(bf16)
    dg = lambda x, y: lax.dot_general(x, y, dims, preferred_element_type=f32)
    return dg(a_hi, b_hi) + (dg(a_hi, b_lo) + dg(a_lo, b_hi))


def gdn_mixer(z3, blocks, conv_hist, delta0, conv_w, a_log, dt_bias, gdn_norm, chunk, pad_rows):
    B, Lx, _ = z3.shape
    f32 = jnp.float32
    wq = 3 * WIDTH_B
    hist = jnp.pad(conv_hist, ((0, 0), (GDN_TAIL - (CONV_W - 1), 0), (0, 0)))
    lane_vec = lambda v: jnp.zeros((1, LANE), f32).at[0, N_HEADS_B:2 * N_HEADS_B].set(v)
    const = lambda shape: pl.BlockSpec(shape, lambda b, t: (0,) * len(shape))
    per_b = lambda shape: pl.BlockSpec((1,) + shape, lambda b, t: (b,) + (0,) * len(shape))
    qb, gb, bb = blocks
    return pl.pallas_call(
        functools.partial(_gdn_kernel, pad_rows=pad_rows),
        grid=(B, Lx // chunk),
        in_specs=[pl.BlockSpec((1, chunk, wq), lambda b, t: (b, t, qb)),
                  pl.BlockSpec((1, chunk, WIDTH_B), lambda b, t: (b, t, gb)),
                  pl.BlockSpec((1, chunk, LANE), lambda b, t: (b, t, bb)),
                  per_b((GDN_TAIL, wq)), per_b((N_HEADS_B, HEAD_DIM_B, HEAD_DIM_B)),
                  const((CONV_W, wq)), const((1, LANE)), const((1, LANE)), const((1, HEAD_DIM_B))],
        out_specs=[pl.BlockSpec((1, chunk, WIDTH_B), lambda b, t: (b, t, 0)),
                   per_b((N_HEADS_B, HEAD_DIM_B, HEAD_DIM_B)), per_b((GDN_TAIL, wq))],
        out_shape=[jax.ShapeDtypeStruct((B, Lx, WIDTH_B), f32),
                   jax.ShapeDtypeStruct((B, N_HEADS_B, HEAD_DIM_B, HEAD_DIM_B), f32),
                   jax.ShapeDtypeStruct((B, GDN_TAIL, wq), f32)],
        scratch_shapes=[pltpu.VMEM((N_HEADS_B, HEAD_DIM_B, HEAD_DIM_B), f32), pltpu.VMEM((GDN_TAIL, wq), f32)],
        compiler_params=pltpu.CompilerParams(dimension_semantics=("parallel", "arbitrary")),
        name="gdn_mixer",
    )(z3, z3, z3, hist, delta0, conv_w, lane_vec(a_log), lane_vec(dt_bias), gdn_norm.reshape(1, HEAD_DIM_B))


def _merge_kernel(x_ref, oa_ref, ob_ref, oc_ref, g0_ref, g1_ref, g2_ref, wa_ref, wb_ref, wc_ref, wo_ref, o_ref,
                  *, rows_per_seq, pad_rows):
    f32 = jnp.float32
    bf16 = jnp.bfloat16
    mm = lambda a, w_ref: jnp.dot(a.astype(bf16), w_ref[...], preferred_element_type=f32)
    merged = (_sigmoid(g0_ref[...]) * mm(oa_ref[...], wa_ref) + _sigmoid(g1_ref[...]) * mm(ob_ref[...], wb_ref)
              + _sigmoid(g2_ref[...]) * mm(oc_ref[...], wc_ref))
    x = x_ref[...] + mm(merged, wo_ref)
    if pad_rows:
        tm = x.shape[0]
        row = pl.program_id(0) * tm + lax.broadcasted_iota(jnp.int32, x.shape, 0)
        x = jnp.where(row % rows_per_seq >= pad_rows, x, 0.0)
    o_ref[...] = x


def merge_residual(x, oA, oB, oC, z, gate_block, wa, wb, wc, wo, rows_per_seq, pad_rows):
    T = x.shape[0]
    tm = _token_tile(T)
    bf16 = jnp.bfloat16
    rows = lambda w, j=0: pl.BlockSpec((tm, w), lambda i, j=j: (i, j))
    const = lambda a: pl.BlockSpec(a.shape, lambda i: (0, 0))
    ws = [w.astype(bf16) for w in (wa, wb, wc, wo)]
    return pl.pallas_call(
        functools.partial(_merge_kernel, rows_per_seq=rows_per_seq, pad_rows=pad_rows),
        grid=(T // tm,),
        in_specs=[rows(D_MODEL), rows(WIDTH_A), rows(WIDTH_B), rows(WIDTH_C),
                  rows(D_MODEL, gate_block), rows(D_MODEL, gate_block + 1), rows(D_MODEL, gate_block + 2)]
                 + [const(w) for w in ws],
        out_specs=rows(D_MODEL),
        out_shape=jax.ShapeDtypeStruct((T, D_MODEL), jnp.float32),
        compiler_params=pltpu.CompilerParams(dimension_semantics=("parallel",), vmem_limit_bytes=VMEM_LIMIT),
        name="merge_residual",
    )(x, oA, oB, oC, z, z, z, *ws)


N_HC = 2 * PEER_HEADS
HALF_Q = PEER_QDIM // 2
PEER_EXPERT_TILE = 8 * N_KEYS
CAND_SUB = 8
PEER_DMA_PARTS = 2


def _extract_top16(s, iota_f):
    n = float(s.shape[0])
    rank = jnp.full(s.shape, PEER_TOPK, jnp.int32)
    vals = []
    for j in range(PEER_TOPK):
        m = jnp.max(s, axis=0, keepdims=True)
        idx = jnp.min(jnp.where(s == m, iota_f, n), axis=0, keepdims=True)
        hit = iota_f == idx
        rank = jnp.where(hit, j, rank)
        s = jnp.where(hit, -jnp.inf, s)
        vals.append(m)
    return vals, rank


def _peer_router_kernel(x_ref, g_ref, wqt_ref, sk_ref, h2_ref, lim_ref, rk2_ref, e1_ref, e2_ref,
                        qt_ref, s_ref, rank_ref, v_ref):
    f32 = jnp.float32
    tl = x_ref.shape[0]
    x = x_ref[...]
    h2 = x * lax.rsqrt(jnp.mean(x * x, axis=-1, keepdims=True) + EPS) * g_ref[...]
    h2b = h2.astype(jnp.bfloat16)
    h2_ref[...] = h2b
    qt_ref[...] = lax.dot_general(wqt_ref[...], h2b, (((1,), (1,)), ((), ())),
                                  preferred_element_type=f32).astype(jnp.bfloat16)

    def score_body(hc, carry):
        r0 = pl.multiple_of(hc * HALF_Q, HALF_Q)
        s_ref[hc] = jnp.dot(sk_ref[hc], qt_ref[pl.ds(r0, HALF_Q), :], preferred_element_type=f32)
        return carry

    lax.fori_loop(0, N_HC, score_body, 0)

    key_iota = lax.broadcasted_iota(jnp.int32, (N_KEYS, LANE), 0).astype(f32)

    def key_body(i, carry):
        hc = i // (tl // LANE)
        c0 = pl.multiple_of((i % (tl // LANE)) * LANE, LANE)
        s = s_ref[hc, :, pl.ds(c0, LANE)]
        vals, rank = _extract_top16(s, key_iota)
        rank_ref[hc, :, pl.ds(c0, LANE)] = rank
        v_ref[hc, :, pl.ds(c0, LANE)] = jnp.concatenate(vals, axis=0)
        return carry

    lax.fori_loop(0, N_HC * (tl // LANE), key_body, 0, unroll=4)

    sub = CAND_SUB
    n_mid = sub - 1
    n_cand = PEER_TOPK + n_mid * sub + (PEER_TOPK - sub)
    cand_iota = lax.broadcasted_iota(jnp.int32, (n_cand, LANE), 0).astype(f32)
    row8 = lax.broadcasted_iota(jnp.int32, (sub, LANE), 0)

    def head_body(i, carry):
        h = i // (tl // LANE)
        c0 = pl.multiple_of((i % (tl // LANE)) * LANE, LANE)
        cols = pl.ds(c0, LANE)
        v1 = v_ref[2 * h, :, cols]
        v2 = v_ref[2 * h + 1, :, cols]
        blocks = [v1[0:1, :] + v2]
        for r1 in range(1, sub):
            blocks.append(jnp.where(row8 < PEER_TOPK // (r1 + 1), v1[r1:r1 + 1, :] + v2[:sub, :], -jnp.inf))
        blocks.append(v1[sub:, :] + v2[0:1, :])
        best, crank = _extract_top16(jnp.concatenate(blocks, axis=0), cand_iota)
        z = jnp.zeros((1, LANE), f32)
        for k in range(PEER_TOPK):
            z = z + jnp.exp(best[k] - best[0])
        sel = jnp.where(crank < PEER_TOPK, 1.0, 0.0)
        rank1 = rank_ref[2 * h, :, cols]
        lim = jnp.zeros((N_KEYS, LANE), jnp.int32)
        for r1 in range(PEER_TOPK):
            if r1 == 0:
                cnt = jnp.sum(sel[:PEER_TOPK, :], axis=0, keepdims=True)
            elif r1 < sub:
                lo = PEER_TOPK + (r1 - 1) * sub
                cnt = jnp.sum(sel[lo:lo + sub, :], axis=0, keepdims=True)
            else:
                lo = PEER_TOPK + n_mid * sub + (r1 - sub)
                cnt = sel[lo:lo + 1, :]
            lim = jnp.where(rank1 == r1, cnt.astype(jnp.int32), lim)
        lim_ref[h, :, cols] = lim
        rk2_ref[h, :, cols] = rank_ref[2 * h + 1, :, cols].astype(f32).astype(jnp.bfloat16)
        e1_ref[h, :, cols] = jnp.exp(s_ref[2 * h, :, cols] - v1[0:1, :])
        e2_ref[h, :, cols] = (jnp.exp(s_ref[2 * h + 1, :, cols] - v2[0:1, :]) / z).astype(jnp.bfloat16)
        return carry

    lax.fori_loop(0, PEER_HEADS * (tl // LANE), head_body, 0, unroll=2)


def _gelu_tanh(x):
    c = math.sqrt(2.0 / math.pi)
    half = 0.5 * x
    return half + half * jnp.tanh(x * (c + (c * 0.044715) * (x * x)))


BF16_ROWS = 16


def _rows_bf16(row):
    tile = jnp.broadcast_to(row, (BF16_ROWS, LANE)).astype(jnp.bfloat16)
    return jnp.concatenate([tile] * (N_KEYS // BF16_ROWS), axis=0)


def _peer_expert_kernel(x_ref, h2_ref, *refs):
    pu_refs, pvt_refs = refs[:PEER_DMA_PARTS], refs[PEER_DMA_PARTS:2 * PEER_DMA_PARTS]
    lim_ref, rk2_ref, e1_ref, e2_ref, o_ref, acc_ref, a_ref, c_ref, rk2_s, e2_s = refs[2 * PEER_DMA_PARTS:]
    f32 = jnp.float32
    bf16 = jnp.bfloat16
    zero = jnp.zeros((N_KEYS, LANE), bf16)
    e_step = pl.program_id(1)
    tl = h2_ref.shape[0]
    part = pu_refs[0].shape[0]
    n_i1 = PEER_DMA_PARTS * part // N_KEYS
    chunk = 2 * LANE if tl % (2 * LANE) == 0 else LANE

    @pl.when(e_step == 0)
    def _():
        acc_ref[...] = jnp.zeros_like(acc_ref)
        rk2_s[...] = rk2_ref[...]
        e2_s[...] = e2_ref[...]

    for p, pu_ref in enumerate(pu_refs):
        a_ref[p * part:(p + 1) * part, :] = lax.dot_general(pu_ref[...], h2_ref[...], (((1,), (1,)), ((), ())),
                                                            preferred_element_type=f32)
    for c0 in range(0, tl, chunk):
        for l0 in range(c0, c0 + chunk, LANE):
            cols = slice(l0, l0 + LANE)
            for ii in range(n_i1):
                rows = slice(ii * N_KEYS, (ii + 1) * N_KEYS)
                w = jnp.zeros((N_KEYS, LANE), bf16)
                for h in range(PEER_HEADS):
                    sel = rk2_s[h, :, cols] < _rows_bf16(lim_ref[h, ii:ii + 1, cols].astype(f32))
                    w = w + jnp.where(sel, e2_s[h, :, cols], zero) * _rows_bf16(e1_ref[h, ii:ii + 1, cols])
                c_ref[rows, cols] = w * _gelu_tanh(a_ref[rows, cols]).astype(bf16)
        cc = slice(c0, c0 + chunk)
        proj = [jnp.dot(pvt_ref[0], c_ref[p * part:(p + 1) * part, cc], preferred_element_type=f32)
                for p, pvt_ref in enumerate(pvt_refs)]
        acc_ref[:, cc] += functools.reduce(lambda a, b: a + b, proj)

    @pl.when(e_step == pl.num_programs(1) - 1)
    def _():
        o_ref[...] = x_ref[...] + acc_ref[...].T


def _token_tile(T, sizes=(640, 512, 384, 256, 128)):
    for tl in sizes:
        if T % tl == 0:
            return tl
    raise ValueError(f"token count {T} is not a multiple of {LANE}")


def peer_residual(x, g, wqt, sk, pu, pvt):
    T = x.shape[0]
    tl = _token_tile(T)
    nt = T // tl
    n_exp = pu.shape[0]
    head_shape = jax.ShapeDtypeStruct((PEER_HEADS, N_KEYS, T), jnp.int32)
    head_shape_f = jax.ShapeDtypeStruct((PEER_HEADS, N_KEYS, T), jnp.float32)
    head_shape_b = jax.ShapeDtypeStruct((PEER_HEADS, N_KEYS, T), jnp.bfloat16)
    head_spec = pl.BlockSpec((PEER_HEADS, N_KEYS, tl), lambda i: (0, 0, i))
    h2, lim, rk2, e1, e2 = pl.pallas_call(
        _peer_router_kernel,
        grid=(nt,),
        in_specs=[pl.BlockSpec((tl, D_MODEL), lambda i: (i, 0)),
                  pl.BlockSpec((1, D_MODEL), lambda i: (0, 0)),
                  pl.BlockSpec((PEER_HEADS * PEER_QDIM, D_MODEL), lambda i: (0, 0)),
                  pl.BlockSpec((N_HC, N_KEYS, HALF_Q), lambda i: (0, 0, 0))],
        out_specs=[pl.BlockSpec((tl, D_MODEL), lambda i: (i, 0)), head_spec, head_spec, head_spec, head_spec],
        out_shape=[jax.ShapeDtypeStruct((T, D_MODEL), jnp.bfloat16), head_shape, head_shape_b,
                   head_shape_f, head_shape_b],
        scratch_shapes=[pltpu.VMEM((PEER_HEADS * PEER_QDIM, tl), jnp.bfloat16),
                        pltpu.VMEM((N_HC, N_KEYS, tl), jnp.float32),
                        pltpu.VMEM((N_HC, N_KEYS, tl), jnp.int32),
                        pltpu.VMEM((N_HC, PEER_TOPK, tl), jnp.float32)],
        compiler_params=pltpu.CompilerParams(dimension_semantics=("parallel",), vmem_limit_bytes=VMEM_LIMIT),
        name="peer_router",
    )(x, g.reshape(1, D_MODEL), wqt, sk)

    tl = _token_tile(T, (768, 512, 640, 384, 256, 128))
    nt = T // tl
    te = PEER_EXPERT_TILE
    parts = PEER_DMA_PARTS
    part = te // parts
    i2_spec = pl.BlockSpec((PEER_HEADS, N_KEYS, tl), lambda i, e: (0, 0, i))
    i1_spec = pl.BlockSpec((PEER_HEADS, te // N_KEYS, tl), lambda i, e: (0, e, i))
    return pl.pallas_call(
        _peer_expert_kernel,
        grid=(nt, n_exp // te),
        in_specs=[pl.BlockSpec((tl, D_MODEL), lambda i, e: (i, 0)),
                  pl.BlockSpec((tl, D_MODEL), lambda i, e: (i, 0)),
                  *[pl.BlockSpec((part, D_MODEL), lambda i, e, p=p: (parts * e + p, 0)) for p in range(parts)],
                  *[pl.BlockSpec((1, D_MODEL, part), lambda i, e, p=p: (parts * e + p, 0, 0)) for p in range(parts)],
                  i1_spec, i2_spec, i1_spec, i2_spec],
        out_specs=pl.BlockSpec((tl, D_MODEL), lambda i, e: (i, 0)),
        out_shape=jax.ShapeDtypeStruct((T, D_MODEL), jnp.float32),
        scratch_shapes=[pltpu.VMEM((D_MODEL, tl), jnp.float32),
                        pltpu.VMEM((te, tl), jnp.float32),
                        pltpu.VMEM((te, tl), jnp.bfloat16),
                        pltpu.VMEM((PEER_HEADS, N_KEYS, tl), jnp.bfloat16),
                        pltpu.VMEM((PEER_HEADS, N_KEYS, tl), jnp.bfloat16)],
        compiler_params=pltpu.CompilerParams(dimension_semantics=("parallel", "arbitrary"),
                                             vmem_limit_bytes=VMEM_LIMIT),
        name="peer_experts",
    )(x, h2, *([pu] * parts), *([pvt] * parts), lim, rk2, e1, e2)


C_QKVB = 3 * WIDTH_A
C_GB = C_QKVB + 3 * WIDTH_B
C_UC = C_GB + WIDTH_B
C_GATE = C_UC + WIDTH_C
C_BA = C_GATE + 3 * D_MODEL
D_IN2 = C_BA + LANE
S5_BLOCK = 256


def _reorder_w_in(w):
    cols = [w[:, OFF_QA:OFF_BETA], w[:, OFF_UC:OFF_GATE], w[:, OFF_GATE:], w[:, OFF_BETA:OFF_UC],
            jnp.zeros((w.shape[0], LANE - 2 * N_HEADS_B), w.dtype)]
    return jnp.concatenate(cols, axis=1).astype(jnp.bfloat16)


def trunk_layer(x, lp, layer_idx, att, prompt, k_past, v_past, conv_hist, delta0, ssm0_re, ssm0_im):
    f32 = jnp.float32
    B, Lx = x.shape[0], x.shape[1]
    T = B * Lx
    z = norm_matmul(x.reshape(T, D_MODEL), lp['norm1'], lp['w_in2'])

    qn, kn, knb, vb = qk_prep(z, lp['q_norm'], lp['k_norm'], _token_tile(T))
    lam_init = 0.8 - 0.6 * math.exp(-0.3 * layer_idx)
    lqk = lp['lambda_qk']
    lam = jnp.exp(jnp.sum(lqk[0] * lqk[1])) - jnp.exp(jnp.sum(lqk[2] * lqk[3])) + lam_init
    scalars = jnp.stack([lam, jnp.asarray(1.0 - lam_init, f32)]).astype(f32)
    r3 = lambda a: a.reshape(B, Lx, WIDTH_A)
    r4 = lambda a: a.reshape(N_HEADS_A, B, Lx, a.shape[-1])
    if prompt:
        oA = attn_prompt(r4(qn), r4(knb), r4(vb), scalars, att['far'], att['tiles'], lp['subln'])
    else:
        P = k_past.shape[1]
        oA = attn_sample(r4(qn), r4(knb), r4(vb), k_past.reshape(B, P, WIDTH_A), v_past.reshape(B, P, WIDTH_A),
                         scalars, att['bias_past'], att['bias_new'], lp['subln'])

    z3 = z.reshape(B, Lx, D_IN2)
    off = PAD if prompt else 0
    L = Lx - off
    kA = r3(kn)[:, off:].reshape(B, L, N_HEADS_A, 2 * HEAD_DIM_A)
    vA = z3[:, off:, 2 * WIDTH_A:3 * WIDTH_A].reshape(B, L, N_HEADS_A, 2 * HEAD_DIM_A)

    oB, delta, tail = gdn_mixer(z3, (C_QKVB // (3 * WIDTH_B), C_GB // WIDTH_B, C_BA // LANE), conv_hist, delta0,
                                lp['conv_w'], lp['a_log'], lp['dt_bias'], lp['gdn_norm'],
                                GDN_CHUNK if prompt else Lx, off)
    conv_state = tail[:, GDN_TAIL - (CONV_W - 1):]

    oC, ssm_re, ssm_im = s5_glu(z3, C_UC // WIDTH_C, ssm0_re, ssm0_im, lp['s5'], lp['w_glu'],
                                S5_BLOCK if prompt else Lx)

    x2 = merge_residual(x.reshape(T, D_MODEL), oA.reshape(T, WIDTH_A), oB.reshape(T, WIDTH_B),
                        oC.reshape(T, WIDTH_C), z, C_GATE // D_MODEL, lp['wb_a'], lp['wb_b'], lp['wb_c'],
                        lp['w_out'], Lx, off)

    x = peer_residual(x2, lp['norm2'], lp['peer_wqt'], lp['peer_sk'], lp['peer_pu'],
                      lp['peer_pvt']).reshape(B, Lx, D_MODEL)
    return (x, kA, vA, conv_state, delta, ssm_re, ssm_im)


def kernel(x_prompt, x_sample, cache_k, cache_v, state_conv, state_delta, state_ssm_re, state_ssm_im,
           meta_tokens, rel_bias, norm1_g, norm2_g, final_norm_g, w_in, q_norm_g, k_norm_g, lambda_qk,
           subln_g, conv_w, gdn_a_log, gdn_dt_bias, gdn_norm_g, s5_a_re, s5_a_im, s5_b_re, s5_b_im,
           s5_c_re, s5_c_im, s5_d, s5_log_dt, w_glu, w_branch_a, w_branch_b, w_branch_c, w_out,
           peer_wq, peer_subkeys, peer_u, peer_v):
    f32 = jnp.float32
    bf16 = jnp.bfloat16
    params = [dict(norm1=norm1_g[l], norm2=norm2_g[l], w_in2=_reorder_w_in(w_in[l]), q_norm=q_norm_g[l],
                   k_norm=k_norm_g[l], lambda_qk=lambda_qk[l], subln=subln_g[l], conv_w=conv_w[l],
                   a_log=gdn_a_log[l], dt_bias=gdn_dt_bias[l], gdn_norm=gdn_norm_g[l],
                   s5=s5_params(s5_a_re[l], s5_a_im[l], s5_b_re[l], s5_b_im[l], s5_c_re[l], s5_c_im[l], s5_d[l],
                                s5_log_dt[l]),
                   w_glu=w_glu[l], wb_a=w_branch_a[l], wb_b=w_branch_b[l], wb_c=w_branch_c[l], w_out=w_out[l],
                   peer_wqt=peer_wq[l].T.astype(bf16),
                   peer_sk=peer_subkeys[l].reshape(N_HC, N_KEYS, HALF_Q).astype(bf16),
                   peer_pu=peer_u[l].astype(bf16), peer_pvt=peer_v[l].astype(bf16).reshape(-1, PEER_EXPERT_TILE // PEER_DMA_PARTS, D_MODEL).transpose(0, 2, 1))
              for l in range(DEPTH)]

    qpos = jnp.arange(ATT_TQ, dtype=jnp.int32)
    tiles = jnp.stack([t5_bias(qpos, d * ATT_TK + jnp.arange(ATT_TK, dtype=jnp.int32), rel_bias)
                       for d in (-2, -1, 0)], axis=1)
    far = t5_bias(jnp.full((1,), 2 * ATT_TK, jnp.int32), jnp.zeros((1,), jnp.int32), rel_bias).reshape(N_HEADS_A)
    P, Ls = cache_k.shape[2], x_sample.shape[1]
    bias_s = t5_bias(P + jnp.arange(Ls, dtype=jnp.int32), jnp.arange(P + Ls, dtype=jnp.int32), rel_bias)
    att_p = dict(tiles=tiles, far=far)
    att_s = dict(bias_past=bias_s[:, :, :P], bias_new=bias_s[:, :, P:])

    B = x_prompt.shape[0]
    xp = jnp.concatenate([jnp.zeros((B, PAD, D_MODEL), f32),
                          jnp.broadcast_to(meta_tokens[None], (B, N_META, D_MODEL)), x_prompt], axis=1)
    outs_p = [[] for _ in range(6)]
    for l in range(DEPTH):
        res = trunk_layer(
            xp, params[l], l, att_p, True, None, None,
            jnp.zeros((B, CONV_W - 1, 3 * WIDTH_B), f32),
            jnp.zeros((B, N_HEADS_B, HEAD_DIM_B, HEAD_DIM_B), f32),
            jnp.zeros((B, S5_GROUPS, S5_STATE), f32), jnp.zeros((B, S5_GROUPS, S5_STATE), f32))
        xp = res[0]
        for acc, r in zip(outs_p, res[1:]):
            acc.append(r)
    y_prompt = rmsnorm(xp, final_norm_g)[:, FRONT:]

    xs = x_sample
    outs_s = [[] for _ in range(6)]
    for l in range(DEPTH):
        res = trunk_layer(
            xs, params[l], l, att_s, False, cache_k[l], cache_v[l], state_conv[l],
            state_delta[l], state_ssm_re[l], state_ssm_im[l])
        xs = res[0]
        for acc, r in zip(outs_s, res[1:]):
            acc.append(r)
    y_sample = rmsnorm(xs, final_norm_g)

    kp, vp, cp, dp, srp, sip = [jnp.stack(a) for a in outs_p]
    ks_, vs_, cs_, ds_, srs, sis = [jnp.stack(a) for a in outs_s]
    return (y_prompt, y_sample, kp, vp, ks_, vs_, cp, cs_, dp, ds_, srp, sip, srs, sis)
```

```python
import functools
import math

import jax
import jax.numpy as jnp
from jax import lax
from jax.experimental import pallas as pl
from jax.experimental.pallas import tpu as pltpu

D_MODEL = 1024
DEPTH = 4
CHUNK = 64
N_META = 16
EPS = 1e-6
NEG_INF = -1e30
N_HEADS_A = 4
HEAD_DIM_A = 64
NUM_BUCKETS = 32
MAX_DISTANCE = 128
N_HEADS_B = 4
HEAD_DIM_B = 128
CONV_W = 4
S5_GROUP = 16
S5_GROUPS = 32
S5_STATE = 64
PEER_HEADS = 8
PEER_QDIM = 256
N_KEYS = 128
PEER_TOPK = 16

WIDTH_A = N_HEADS_A * 2 * HEAD_DIM_A
WIDTH_B = N_HEADS_B * HEAD_DIM_B
WIDTH_C = S5_GROUPS * S5_GROUP
OFF_QA = 0
OFF_KA = OFF_QA + WIDTH_A
OFF_VA = OFF_KA + WIDTH_A
OFF_QKVB = OFF_VA + WIDTH_A
OFF_GB = OFF_QKVB + 3 * WIDTH_B
OFF_BETA = OFF_GB + WIDTH_B
OFF_ALPHA = OFF_BETA + N_HEADS_B
OFF_UC = OFF_ALPHA + N_HEADS_B
OFF_GATE = OFF_UC + WIDTH_C
D_IN = OFF_GATE + 3 * D_MODEL

LANE = 128
VMEM_LIMIT = 56 * 1024 * 1024
WEIGHT_TILE_BYTES = 6 * 1024 * 1024
HIGHEST = lax.Precision.HIGHEST


def _norm_mm_kernel(x_ref, g_ref, w_ref, o_ref, h_ref):
    @pl.when(pl.program_id(1) == 0)
    def _():
        x = x_ref[...]
        h = x * lax.rsqrt(jnp.mean(x * x, axis=-1, keepdims=True) + EPS) * g_ref[...]
        h_ref[...] = h.astype(jnp.bfloat16)

    o_ref[...] = jnp.dot(h_ref[...], w_ref[...], preferred_element_type=jnp.float32)


def norm_matmul(x, g, wb, tm=512):
    M, K = x.shape
    N = wb.shape[1]
    n_lanes = N // LANE
    tn = LANE * max(d for d in range(1, n_lanes + 1) if n_lanes % d == 0 and d * LANE * K * 2 <= WEIGHT_TILE_BYTES)
    tm = min(tm, M)
    return pl.pallas_call(
        _norm_mm_kernel,
        grid=(pl.cdiv(M, tm), N // tn),
        in_specs=[pl.BlockSpec((tm, K), lambda i, j: (i, 0)),
                  pl.BlockSpec((1, K), lambda i, j: (0, 0)),
                  pl.BlockSpec((K, tn), lambda i, j: (0, j))],
        out_specs=pl.BlockSpec((tm, tn), lambda i, j: (i, j)),
        out_shape=jax.ShapeDtypeStruct((M, N), jnp.float32),
        scratch_shapes=[pltpu.VMEM((tm, K), jnp.bfloat16)],
        compiler_params=pltpu.CompilerParams(dimension_semantics=("parallel", "arbitrary"),
                                             vmem_limit_bytes=VMEM_LIMIT),
        name="norm_proj",
    )(x, g.reshape(1, K), wb)


def rmsnorm(x, g):
    xf = x.astype(jnp.float32)
    y = xf * lax.rsqrt(jnp.mean(xf * xf, axis=-1, keepdims=True) + EPS)
    return (y * g.astype(jnp.float32)).astype(x.dtype)


def t5_bias(q_pos, k_pos, table):
    rel = k_pos[None, :] - q_pos[:, None]
    half = NUM_BUCKETS // 2
    exact = half // 2
    n = jnp.abs(rel)
    nf = jnp.maximum(n, 1).astype(jnp.float32)
    far = exact + (jnp.log(nf / exact) / math.log(MAX_DISTANCE / exact) * (half - exact)).astype(jnp.int32)
    bucket = jnp.where(rel > 0, half, 0) + jnp.where(n < exact, n, jnp.minimum(far, half - 1))
    hit = bucket[None, :, :] == jnp.arange(NUM_BUCKETS, dtype=bucket.dtype)[:, None, None]
    tab = table.astype(jnp.float32)
    return jnp.stack([jnp.sum(jnp.where(hit, tab[:, h, None, None], 0.0), axis=0) for h in range(tab.shape[1])])


FRONT = 256
PAD = FRONT - N_META
ATT_TQ = 256
ATT_TK = 256


def _qk_prep_kernel(q_ref, k_ref, v_ref, gq_ref, gk_ref, seg_ref, qn_ref, kn_ref, knb_ref, vb_ref):
    f32 = jnp.float32
    bf16 = jnp.bfloat16
    seg = seg_ref[...]
    hd = 2 * HEAD_DIM_A

    def norm(x, g):
        ms = jnp.dot(x * x, seg, precision=HIGHEST, preferred_element_type=f32)
        return x * lax.rsqrt(ms + EPS) * g

    qn = (norm(q_ref[...], gq_ref[...]) * (HEAD_DIM_A ** -0.5)).astype(bf16)
    kn = norm(k_ref[...], gk_ref[...])
    kn_ref[...] = kn
    knb = kn.astype(bf16)
    v = v_ref[...].astype(bf16)
    lane = lax.broadcasted_iota(jnp.int32, (v.shape[0], hd), 1)
    ones_col = jnp.where(lane == 0, 1.0, 0.0).astype(bf16)
    for h in range(N_HEADS_A):
        cols = slice(h * hd, (h + 1) * hd)
        qn_ref[h] = qn[:, cols]
        knb_ref[h] = knb[:, cols]
        vb_ref[h] = jnp.concatenate([v[:, cols], ones_col], axis=1)


def qk_prep(z, gq, gk, tm):
    T = z.shape[0]
    hd = 2 * HEAD_DIM_A
    seg = jnp.kron(jnp.eye(WIDTH_A // HEAD_DIM_A, dtype=jnp.float32),
                   jnp.full((HEAD_DIM_A, HEAD_DIM_A), 1.0 / HEAD_DIM_A, jnp.float32))
    row = lambda j: pl.BlockSpec((tm, WIDTH_A), lambda i, j=j: (i, j))
    const = lambda shape: pl.BlockSpec(shape, lambda i: (0,) * len(shape))
    heads = lambda w: pl.BlockSpec((N_HEADS_A, tm, w), lambda i: (0, i, 0))
    hshape = lambda w: jax.ShapeDtypeStruct((N_HEADS_A, T, w), jnp.bfloat16)
    return pl.pallas_call(
        _qk_prep_kernel,
        grid=(T // tm,),
        in_specs=[row(0), row(1), row(2), const((1, WIDTH_A)), const((1, WIDTH_A)), const((WIDTH_A, WIDTH_A))],
        out_specs=[heads(hd), pl.BlockSpec((tm, WIDTH_A), lambda i: (i, 0)), heads(hd), heads(2 * hd)],
        out_shape=[hshape(hd), jax.ShapeDtypeStruct((T, WIDTH_A), jnp.float32), hshape(hd), hshape(2 * hd)],
        compiler_params=pltpu.CompilerParams(dimension_semantics=("parallel",)),
        name="qk_prep",
    )(z, z, z, jnp.tile(gq, WIDTH_A // HEAD_DIM_A).reshape(1, WIDTH_A),
      jnp.tile(gk, WIDTH_A // HEAD_DIM_A).reshape(1, WIDTH_A), seg)


def _split_maps(q):
    lane = lax.broadcasted_iota(jnp.int32, q.shape, 1)
    zero = jnp.zeros_like(q)
    return jnp.where(lane < HEAD_DIM_A, q, zero), jnp.where(lane >= HEAD_DIM_A, q, zero)


def _subln(o, g, scale):
    return o * lax.rsqrt(jnp.mean(o * o, axis=-1, keepdims=True) + EPS) * g * scale


def _attn_prompt_kernel(sc_ref, far_ref, q_ref, k_ref, v_ref, bias_ref, g_ref, o_ref, m_ref, acc_ref,
                        sa_ref, sb_ref):
    f32 = jnp.float32
    h = pl.program_id(1)
    qi = pl.program_id(2)
    tq, tk = ATT_TQ, ATT_TK
    hd = 2 * HEAD_DIM_A
    qs = jnp.concatenate(_split_maps(q_ref[0, 0]), axis=0)
    m_ref[...] = jnp.full(m_ref.shape, NEG_INF, f32)
    acc_ref[...] = jnp.zeros(acc_ref.shape, f32)
    far = far_ref[h]

    def logits(k0, width, general):
        k0 = pl.multiple_of(k0, tk)
        kt = k_ref[0, 0, pl.ds(k0, width), :]
        s = lax.dot_general(qs, kt, (((1,), (1,)), ((), ())), preferred_element_type=f32)
        if general:
            kj = k0 // tk
            bias = jnp.concatenate([bias_ref[0, jnp.clip(kj + j - qi + 2, 0, 2)] for j in range(width // tk)], axis=1)
            qpos = qi * tq + lax.broadcasted_iota(jnp.int32, (tq, width), 0)
            kpos = k0 + lax.broadcasted_iota(jnp.int32, (tq, width), 1)
            qchunk = jnp.where(qpos < FRONT, 0, 1 + jnp.right_shift(qpos - FRONT, 6))
            kchunk = jnp.where(kpos < FRONT, 0, 1 + jnp.right_shift(kpos - FRONT, 6))
            mask = (kpos >= PAD) & (kchunk <= qchunk)
            s = jnp.where(jnp.concatenate([mask, mask], axis=0), s + jnp.concatenate([bias, bias], axis=0), NEG_INF)
        else:
            s = s + far
        return s

    def accumulate(s, k0):
        width = s.shape[1]
        vt = v_ref[0, 0, pl.ds(pl.multiple_of(k0, tk), width), :]
        n_lane_tiles = width // LANE
        smax = s[:, :LANE]
        for j in range(1, n_lane_tiles):
            smax = jnp.maximum(smax, s[:, j * LANE:(j + 1) * LANE])
        m_old = m_ref[...]
        m_new = jnp.maximum(m_old, jnp.broadcast_to(jnp.max(smax, axis=1, keepdims=True), m_old.shape))
        alpha = jnp.exp(m_old - m_new)
        p = jnp.exp(s - jnp.concatenate([m_new] * n_lane_tiles, axis=1))
        acc_ref[...] = (jnp.concatenate([alpha, alpha], axis=1) * acc_ref[...]
                        + jnp.dot(p.astype(jnp.bfloat16), vt, preferred_element_type=f32))
        m_ref[...] = m_new

    def tile(k0, width, general):
        accumulate(logits(k0, width, general), k0)

    tile(0, tk, True)

    wide = 2 * tk
    n_far = jnp.maximum(qi - 2, 0)
    n_single = n_far % 4

    @pl.when(n_single >= 2)
    def _():
        tile(tk, wide, False)

    @pl.when(n_single % 2 == 1)
    def _():
        tile((1 + 2 * (n_single // 2)) * tk, tk, False)

    base = (1 + n_single) * tk
    n_pairs = n_far // 4

    @pl.when(n_pairs > 0)
    def _():
        sa_ref[...] = logits(base, wide, False)

    def pair_body(j, carry):
        k0 = base + 2 * j * wide
        sb_ref[...] = logits(k0 + wide, wide, False)
        accumulate(sa_ref[...], k0)
        nxt = jnp.minimum(k0 + 2 * wide, base + (2 * n_pairs - 1) * wide)
        sa_ref[...] = logits(nxt, wide, False)
        accumulate(sb_ref[...], k0 + wide)
        return carry

    lax.fori_loop(0, n_pairs, pair_body, 0)

    @pl.when(qi >= 2)
    def _():
        tile((qi - 1) * tk, wide, True)

    @pl.when(qi == 1)
    def _():
        tile(tk, tk, True)

    acc = acc_ref[...]
    out = acc[:, :hd] / acc[:, hd:hd + 1]
    o = out[:tq] - sc_ref[0] * out[tq:]
    o_ref[0] = _subln(o, g_ref[...], sc_ref[1])


def attn_prompt(qn, knb, vb, scalars, far, bias_tiles, subln_g):
    _, B, Lp, _ = qn.shape
    hd = 2 * HEAD_DIM_A
    smem = pl.BlockSpec(memory_space=pltpu.SMEM)
    return pl.pallas_call(
        _attn_prompt_kernel,
        grid=(B, N_HEADS_A, Lp // ATT_TQ),
        in_specs=[smem, smem,
                  pl.BlockSpec((1, 1, ATT_TQ, hd), lambda b, h, i: (h, b, i, 0)),
                  pl.BlockSpec((1, 1, Lp, hd), lambda b, h, i: (h, b, 0, 0)),
                  pl.BlockSpec((1, 1, Lp, 2 * hd), lambda b, h, i: (h, b, 0, 0)),
                  pl.BlockSpec((1, 3, ATT_TQ, ATT_TK), lambda b, h, i: (h, 0, 0, 0)),
                  pl.BlockSpec((1, hd), lambda b, h, i: (0, 0))],
        out_specs=pl.BlockSpec((1, ATT_TQ, hd), lambda b, h, i: (b, i, h)),
        out_shape=jax.ShapeDtypeStruct((B, Lp, WIDTH_A), jnp.float32),
        scratch_shapes=[pltpu.VMEM((2 * ATT_TQ, hd), jnp.float32), pltpu.VMEM((2 * ATT_TQ, 2 * hd), jnp.float32),
                        pltpu.VMEM((2 * ATT_TQ, 2 * ATT_TK), jnp.float32),
                        pltpu.VMEM((2 * ATT_TQ, 2 * ATT_TK), jnp.float32)],
        compiler_params=pltpu.CompilerParams(dimension_semantics=("parallel", "parallel", "parallel"),
                                             vmem_limit_bytes=VMEM_LIMIT),
        name="diff_attn_prompt",
    )(scalars, far, qn, knb, vb, bias_tiles, subln_g.reshape(1, hd))


def _attn_sample_kernel(sc_ref, q_ref, kp_ref, vp_ref, kn_ref, vn_ref, bp_ref, bn_ref, g_ref, o_ref):
    f32 = jnp.float32
    bf16 = jnp.bfloat16
    qs = _split_maps(q_ref[0, 0])
    kp = kp_ref[0].astype(bf16)
    vp = vp_ref[0].astype(bf16)
    kn = kn_ref[0, 0]
    vn = vn_ref[0, 0][:, :2 * HEAD_DIM_A]
    dn = (((1,), (1,)), ((), ()))
    outs = []
    for c in range(2):
        sp = lax.dot_general(qs[c], kp, dn, preferred_element_type=f32) + bp_ref[0]
        sn = lax.dot_general(qs[c], kn, dn, preferred_element_type=f32) + bn_ref[0]
        m = jnp.maximum(jnp.max(sp, axis=1, keepdims=True), jnp.max(sn, axis=1, keepdims=True))
        pp = jnp.exp(sp - m)
        pn = jnp.exp(sn - m)
        l = jnp.sum(pp, axis=1, keepdims=True) + jnp.sum(pn, axis=1, keepdims=True)
        acc = (jnp.dot(pp.astype(bf16), vp, preferred_element_type=f32)
               + jnp.dot(pn.astype(bf16), vn, preferred_element_type=f32))
        outs.append(acc / l)
    o_ref[0] = _subln(outs[0] - sc_ref[0] * outs[1], g_ref[...], sc_ref[1])


def attn_sample(qn, knb, vb, k_past, v_past, scalars, bias_past, bias_new, subln_g):
    _, B, L, _ = qn.shape
    P = k_past.shape[1]
    hd = 2 * HEAD_DIM_A
    new = pl.BlockSpec((1, L, hd), lambda b, h: (b, 0, h))
    new_h = lambda w: pl.BlockSpec((1, 1, L, w), lambda b, h: (h, b, 0, 0))
    past = pl.BlockSpec((1, P, hd), lambda b, h: (b, 0, h))
    return pl.pallas_call(
        _attn_sample_kernel,
        grid=(B, N_HEADS_A),
        in_specs=[pl.BlockSpec(memory_space=pltpu.SMEM), new_h(hd), past, past, new_h(hd), new_h(2 * hd),
                  pl.BlockSpec((1, L, P), lambda b, h: (h, 0, 0)),
                  pl.BlockSpec((1, L, L), lambda b, h: (h, 0, 0)),
                  pl.BlockSpec((1, hd), lambda b, h: (0, 0))],
        out_specs=new,
        out_shape=jax.ShapeDtypeStruct((B, L, WIDTH_A), jnp.float32),
        compiler_params=pltpu.CompilerParams(dimension_semantics=("parallel", "parallel")),
        name="diff_attn_sample",
    )(scalars, qn, k_past, v_past, knb, vb, bias_past, bias_new, subln_g.reshape(1, hd))


S5_N = S5_GROUPS * S5_STATE
S5_SLAB = 8
S5_LANES = 512
S5_SPLIT = 2


def _sigmoid(x):
    return 1.0 / (1.0 + jnp.exp(-x))


def _s5_kernel(u_ref, x0_ref, bmat_ref, cmat_ref, lamp_ref, ppow_ref, d_ref, wglu_ref, oc_ref, xf_ref,
               bu_ref, carry_ref):
    f32 = jnp.float32
    bf16 = jnp.bfloat16
    t = pl.program_id(1)
    tb = u_ref.shape[1]

    @pl.when(t == 0)
    def _():
        carry_ref[...] = x0_ref[0]

    u = u_ref[0]
    ub = u.astype(bf16)
    ch, st = WIDTH_C // S5_SPLIT, S5_N // S5_SPLIT
    for s in range(S5_SPLIT):
        for part in range(2):
            cols = slice(part * S5_N + s * st, part * S5_N + (s + 1) * st)
            bu_ref[:, cols] = jnp.dot(ub[:, s * ch:(s + 1) * ch], bmat_ref[s * ch:(s + 1) * ch, cols],
                                      preferred_element_type=f32)

    row = lax.broadcasted_iota(jnp.int32, (S5_SLAB, S5_LANES), 0)
    for c in range(S5_N // S5_LANES):
        re = slice(c * S5_LANES, (c + 1) * S5_LANES)
        im = slice(S5_N + c * S5_LANES, S5_N + (c + 1) * S5_LANES)

        def slab(i, carry):
            cre, cim = carry
            rows = pl.ds(pl.multiple_of(i * S5_SLAB, S5_SLAB), S5_SLAB)
            yre = bu_ref[rows, re]
            yim = bu_ref[rows, im]
            for s in range(3):
                sh = 1 << s
                sre = jnp.where(row >= sh, pltpu.roll(yre, sh, 0), 0.0)
                sim = jnp.where(row >= sh, pltpu.roll(yim, sh, 0), 0.0)
                lr = lamp_ref[s, 0, :, re]
                li = lamp_ref[s, 1, :, re]
                yre, yim = yre + (lr * sre - li * sim), yim + (lr * sim + li * sre)
            pr = ppow_ref[0, :, re]
            pi = ppow_ref[1, :, re]
            yre, yim = yre + (pr * cre - pi * cim), yim + (pr * cim + pi * cre)
            bu_ref[rows, re] = yre
            bu_ref[rows, im] = yim
            last = S5_SLAB - 1
            return (jnp.broadcast_to(yre[last:last + 1, :], yre.shape),
                    jnp.broadcast_to(yim[last:last + 1, :], yim.shape))

        cre, cim = lax.fori_loop(0, tb // S5_SLAB, slab, (carry_ref[:, re], carry_ref[:, im]))
        carry_ref[:, re] = cre
        carry_ref[:, im] = cim

    ys = []
    for s in range(S5_SPLIT):
        out = slice(s * ch, (s + 1) * ch)
        acc = None
        for part in range(2):
            cols = slice(part * S5_N + s * st, part * S5_N + (s + 1) * st)
            term = jnp.dot(bu_ref[:, cols].astype(bf16), cmat_ref[cols, out], preferred_element_type=f32)
            acc = term if acc is None else acc + term
        ys.append(acc)
    y = jnp.concatenate(ys, axis=1) + d_ref[...] * u
    gl = jnp.dot(_gelu_tanh(y).astype(bf16), wglu_ref[...], preferred_element_type=f32)
    oc_ref[0] = gl[:, :WIDTH_C] * _sigmoid(gl[:, WIDTH_C:])

    @pl.when(t == pl.num_programs(1) - 1)
    def _():
        xf_ref[0] = carry_ref[...]


def s5_params(a_re, a_im, b_re, b_im, c_re, c_im, d, log_dt):
    f32 = jnp.float32
    lam = lax.complex(a_re, a_im)
    lam_bar = jnp.exp(lam * jnp.exp(log_dt)[:, None])
    b_bar = ((lam_bar - 1.0) / lam)[..., None] * lax.complex(b_re, b_im)
    eye = jnp.eye(S5_GROUPS, dtype=f32)
    bd_in = lambda m: jnp.einsum('gpi,gh->gihp', m, eye).reshape(WIDTH_C, S5_N)
    bd_out = lambda m: jnp.einsum('gip,gh->gphi', m, eye).reshape(S5_N, WIDTH_C)
    bmat = jnp.concatenate([bd_in(b_bar.real), bd_in(b_bar.imag)], axis=1).astype(jnp.bfloat16)
    cmat = jnp.concatenate([bd_out(c_re), bd_out(-c_im)], axis=0).astype(jnp.bfloat16)
    lb = lam_bar.reshape(S5_N)
    rep = lambda v: jnp.broadcast_to(v[None, :], (S5_SLAB, S5_N))
    pows = [lb, lb * lb, (lb * lb) * (lb * lb)]
    lamp = jnp.stack([jnp.stack([rep(p.real), rep(p.imag)]) for p in pows])
    run = [lb]
    for _ in range(S5_SLAB - 1):
        run.append(run[-1] * lb)
    pp = jnp.stack(run)
    ppow = jnp.stack([pp.real, pp.imag])
    return dict(bmat=bmat, cmat=cmat, lamp=lamp.astype(f32), ppow=ppow.astype(f32), d=d.reshape(1, WIDTH_C))


def s5_glu(z3, col_block, x0_re, x0_im, sp, w_glu, tb):
    B, Lx, _ = z3.shape
    x0 = jnp.concatenate([x0_re.reshape(B, S5_N), x0_im.reshape(B, S5_N)], axis=1)
    x0 = jnp.broadcast_to(x0[:, None, :], (B, S5_SLAB, 2 * S5_N))
    const = lambda a: pl.BlockSpec(a.shape, lambda b, t: (0,) * a.ndim)
    wg = w_glu.astype(jnp.bfloat16)
    oc, xf = pl.pallas_call(
        _s5_kernel,
        grid=(B, Lx // tb),
        in_specs=[pl.BlockSpec((1, tb, WIDTH_C), lambda b, t: (b, t, col_block)),
                  pl.BlockSpec((1, S5_SLAB, 2 * S5_N), lambda b, t: (b, 0, 0)),
                  const(sp['bmat']), const(sp['cmat']), const(sp['lamp']), const(sp['ppow']), const(sp['d']),
                  const(wg)],
        out_specs=[pl.BlockSpec((1, tb, WIDTH_C), lambda b, t: (b, t, 0)),
                   pl.BlockSpec((1, S5_SLAB, 2 * S5_N), lambda b, t: (b, 0, 0))],
        out_shape=[jax.ShapeDtypeStruct((B, Lx, WIDTH_C), jnp.float32),
                   jax.ShapeDtypeStruct((B, S5_SLAB, 2 * S5_N), jnp.float32)],
        scratch_shapes=[pltpu.VMEM((tb, 2 * S5_N), jnp.float32), pltpu.VMEM((S5_SLAB, 2 * S5_N), jnp.float32)],
        compiler_params=pltpu.CompilerParams(dimension_semantics=("parallel", "arbitrary"),
                                             vmem_limit_bytes=VMEM_LIMIT),
        name="s5_glu",
    )(z3, x0, sp['bmat'], sp['cmat'], sp['lamp'], sp['ppow'], sp['d'], wg)
    shape = (B, S5_GROUPS, S5_STATE)
    return oc, xf[:, 0, :S5_N].reshape(shape), xf[:, 0, S5_N:].reshape(shape)


def _gdn_kernel(qkv_ref, gate_ref, ba_ref, hist_ref, s0_ref, cw_ref, av_ref, dt_ref, gn_ref,
                ob_ref, sout_ref, tail_ref, s_ref, prev_ref, *, pad_rows):
    f32 = jnp.float32
    t = pl.program_id(1)
    C = qkv_ref.shape[1]
    hd = HEAD_DIM_B
    dn_last = (((1,), (1,)), ((), ()))
    dn_first = (((0,), (0,)), ((), ()))
    dot = lambda a, b: _dot3(a, b, (((1,), (0,)), ((), ())))

    @pl.when(t == 0)
    def _():
        s_ref[...] = s0_ref[0]
        prev_ref[...] = hist_ref[0]

    cur = qkv_ref[0]
    tail = prev_ref.shape[0]
    ext = jnp.concatenate([prev_ref[...], cur], axis=0)
    w = cw_ref[...]
    conv = cur * w[CONV_W - 1:CONV_W, :]
    for i in range(CONV_W - 1):
        lo = tail - (CONV_W - 1) + i
        conv = conv + ext[lo:lo + C, :] * w[i:i + 1, :]
    prev_ref[...] = cur[C - tail:, :]
    c = conv * _sigmoid(conv)

    ba = ba_ref[0]
    xg = ba + dt_ref[...]
    softplus = jnp.maximum(xg, 0.0) + jnp.log(1.0 + jnp.exp(-jnp.abs(xg)))
    g_all = -jnp.exp(av_ref[...]) * softplus
    if pad_rows:
        grow = t * C + lax.broadcasted_iota(jnp.int32, g_all.shape, 0)
        g_all = jnp.where(grow >= pad_rows, g_all, 0.0)
    ri = lax.broadcasted_iota(jnp.int32, (C, C), 0)
    ci = lax.broadcasted_iota(jnp.int32, (C, C), 1)
    tri = ri >= ci
    strict = ri > ci
    tril = jnp.where(tri, 1.0, 0.0).astype(f32)
    G_col = jnp.dot(tril, g_all, precision=HIGHEST, preferred_element_type=f32)
    G_row = lax.dot_general(g_all, tril, (((0,), (1,)), ((), ())), precision=HIGHEST,
                            preferred_element_type=f32)
    eye = jnp.where(ri == ci, 1.0, 0.0).astype(f32)

    heads = range(N_HEADS_B)
    dot_t = lambda a, b: _dot3(a, b, dn_last)
    Gc = [G_col[:, N_HEADS_B + h:N_HEADS_B + h + 1] for h in heads]
    Gr = [G_row[N_HEADS_B + h:N_HEADS_B + h + 1, :] for h in heads]
    decay = [jnp.where(tri, jnp.exp(jnp.minimum(Gc[h] - Gr[h], 0.0)), 0.0) for h in heads]
    beta = [_sigmoid(ba[:, h:h + 1]) for h in heads]
    unit = lambda a: a * lax.rsqrt(jnp.sum(a * a, axis=-1, keepdims=True) + EPS)
    q = [unit(c[:, h * hd:(h + 1) * hd]) * (hd ** -0.5) for h in heads]
    k = [unit(c[:, WIDTH_B + h * hd:WIDTH_B + (h + 1) * hd]) for h in heads]
    kb = [k[h] * beta[h] for h in heads]
    eG = [jnp.exp(Gc[h]) for h in heads]
    rhs = [jnp.concatenate([c[:, 2 * WIDTH_B + h * hd:2 * WIDTH_B + (h + 1) * hd] * beta[h], kb[h] * eG[h]], axis=1)
           for h in heads]
    A = [jnp.where(strict, dot_t(kb[h], k[h]) * decay[h], 0.0) for h in heads]
    same_block = lambda b: jnp.right_shift(ri, b.bit_length() - 1) == jnp.right_shift(ci, b.bit_length() - 1)
    in_base = same_block(GDN_BASE)
    D = [jnp.where(in_base, A[h], 0.0) for h in heads]
    D2 = [dot(D[h], D[h]) for h in heads]
    D4 = [dot(D2[h], D2[h]) for h in heads]
    tm = [eye - D[h] for h in heads]
    tm = [tm[h] + dot(tm[h], D2[h]) for h in heads]
    tm = [tm[h] + dot(tm[h], D4[h]) for h in heads]
    b = GDN_BASE
    while b < C:
        level = same_block(2 * b) & jnp.logical_not(same_block(b))
        me = [dot(tm[h], jnp.where(level, A[h], 0.0)) for h in heads]
        tm = [tm[h] - dot(me[h], tm[h]) for h in heads]
        b *= 2
    sol = [dot(tm[h], rhs[h]) for h in heads]
    attn = [dot_t(q[h], k[h]) * decay[h] for h in heads]
    S = [s_ref[h] for h in heads]
    v_new = [sol[h][:, :hd] - dot(sol[h][:, hd:], S[h]) for h in heads]
    o = [dot(q[h] * eG[h], S[h]) + dot(attn[h], v_new[h]) for h in heads]
    for h in heads:
        GL = Gc[h][C - 1:C, :]
        s_ref[h] = S[h] * jnp.exp(GL) + _dot3(k[h] * jnp.exp(GL - Gc[h]), v_new[h], dn_first)
        gate = gate_ref[0, :, h * hd:(h + 1) * hd]
        on = o[h] * lax.rsqrt(jnp.mean(o[h] * o[h], axis=-1, keepdims=True) + EPS) * gn_ref[...]
        ob_ref[0, :, h * hd:(h + 1) * hd] = on * (gate * _sigmoid(gate))

    @pl.when(t == pl.num_programs(1) - 1)
    def _():
        sout_ref[0] = s_ref[...]
        tail_ref[0] = cur[C - tail:, :]


GDN_TAIL = 8
GDN_CHUNK = 128
GDN_BASE = 8


def _dot3(a, b, dims):
    f32 = jnp.float32
    bf16 = jnp.bfloat16
    a_hi = a.astype(bf16)
    b_hi = b.astype(bf16)
    a_lo = (a - a_hi.astype(f32)).astype(bf16)
    b_lo = (b - b_hi.astype(f32)).astype(bf16)
    dg = lambda x, y: lax.dot_general(x, y, dims, preferred_element_type=f32)
    return dg(a_hi, b_hi) + (dg(a_hi, b_lo) + dg(a_lo, b_hi))


def gdn_mixer(z3, blocks, conv_hist, delta0, conv_w, a_log, dt_bias, gdn_norm, chunk, pad_rows):
    B, Lx, _ = z3.shape
    f32 = jnp.float32
    wq = 3 * WIDTH_B
    hist = jnp.pad(conv_hist, ((0, 0), (GDN_TAIL - (CONV_W - 1), 0), (0, 0)))
    lane_vec = lambda v: jnp.zeros((1, LANE), f32).at[0, N_HEADS_B:2 * N_HEADS_B].set(v)
    const = lambda shape: pl.BlockSpec(shape, lambda b, t: (0,) * len(shape))
    per_b = lambda shape: pl.BlockSpec((1,) + shape, lambda b, t: (b,) + (0,) * len(shape))
    qb, gb, bb = blocks
    return pl.pallas_call(
        functools.partial(_gdn_kernel, pad_rows=pad_rows),
        grid=(B, Lx // chunk),
        in_specs=[pl.BlockSpec((1, chunk, wq), lambda b, t: (b, t, qb)),
                  pl.BlockSpec((1, chunk, WIDTH_B), lambda b, t: (b, t, gb)),
                  pl.BlockSpec((1, chunk, LANE), lambda b, t: (b, t, bb)),
                  per_b((GDN_TAIL, wq)), per_b((N_HEADS_B, HEAD_DIM_B, HEAD_DIM_B)),
                  const((CONV_W, wq)), const((1, LANE)), const((1, LANE)), const((1, HEAD_DIM_B))],
        out_specs=[pl.BlockSpec((1, chunk, WIDTH_B), lambda b, t: (b, t, 0)),
                   per_b((N_HEADS_B, HEAD_DIM_B, HEAD_DIM_B)), per_b((GDN_TAIL, wq))],
        out_shape=[jax.ShapeDtypeStruct((B, Lx, WIDTH_B), f32),
                   jax.ShapeDtypeStruct((B, N_HEADS_B, HEAD_DIM_B, HEAD_DIM_B), f32),
                   jax.ShapeDtypeStruct((B, GDN_TAIL, wq), f32)],
        scratch_shapes=[pltpu.VMEM((N_HEADS_B, HEAD_DIM_B, HEAD_DIM_B), f32), pltpu.VMEM((GDN_TAIL, wq), f32)],
        compiler_params=pltpu.CompilerParams(dimension_semantics=("parallel", "arbitrary")),
        name="gdn_mixer",
    )(z3, z3, z3, hist, delta0, conv_w, lane_vec(a_log), lane_vec(dt_bias), gdn_norm.reshape(1, HEAD_DIM_B))


def _merge_kernel(x_ref, oa_ref, ob_ref, oc_ref, g0_ref, g1_ref, g2_ref, wa_ref, wb_ref, wc_ref, wo_ref, o_ref,
                  *, rows_per_seq, pad_rows):
    f32 = jnp.float32
    bf16 = jnp.bfloat16
    mm = lambda a, w_ref: jnp.dot(a.astype(bf16), w_ref[...], preferred_element_type=f32)
    merged = (_sigmoid(g0_ref[...]) * mm(oa_ref[...], wa_ref) + _sigmoid(g1_ref[...]) * mm(ob_ref[...], wb_ref)
              + _sigmoid(g2_ref[...]) * mm(oc_ref[...], wc_ref))
    x = x_ref[...] + mm(merged, wo_ref)
    if pad_rows:
        tm = x.shape[0]
        row = pl.program_id(0) * tm + lax.broadcasted_iota(jnp.int32, x.shape, 0)
        x = jnp.where(row % rows_per_seq >= pad_rows, x, 0.0)
    o_ref[...] = x


def merge_residual(x, oA, oB, oC, z, gate_block, wa, wb, wc, wo, rows_per_seq, pad_rows):
    T = x.shape[0]
    tm = _token_tile(T)
    bf16 = jnp.bfloat16
    rows = lambda w, j=0: pl.BlockSpec((tm, w), lambda i, j=j: (i, j))
    const = lambda a: pl.BlockSpec(a.shape, lambda i: (0, 0))
    ws = [w.astype(bf16) for w in (wa, wb, wc, wo)]
    return pl.pallas_call(
        functools.partial(_merge_kernel, rows_per_seq=rows_per_seq, pad_rows=pad_rows),
        grid=(T // tm,),
        in_specs=[rows(D_MODEL), rows(WIDTH_A), rows(WIDTH_B), rows(WIDTH_C),
                  rows(D_MODEL, gate_block), rows(D_MODEL, gate_block + 1), rows(D_MODEL, gate_block + 2)]
                 + [const(w) for w in ws],
        out_specs=rows(D_MODEL),
        out_shape=jax.ShapeDtypeStruct((T, D_MODEL), jnp.float32),
        compiler_params=pltpu.CompilerParams(dimension_semantics=("parallel",), vmem_limit_bytes=VMEM_LIMIT),
        name="merge_residual",
    )(x, oA, oB, oC, z, z, z, *ws)


N_HC = 2 * PEER_HEADS
HALF_Q = PEER_QDIM // 2
PEER_EXPERT_TILE = 8 * N_KEYS
CAND_SUB = 8
PEER_DMA_PARTS = 2


def _extract_top16(s, iota_f):
    n = float(s.shape[0])
    rank = jnp.full(s.shape, PEER_TOPK, jnp.int32)
    vals = []
    for j in range(PEER_TOPK):
        m = jnp.max(s, axis=0, keepdims=True)
        idx = jnp.min(jnp.where(s == m, iota_f, n), axis=0, keepdims=True)
        hit = iota_f == idx
        rank = jnp.where(hit, j, rank)
        s = jnp.where(hit, -jnp.inf, s)
        vals.append(m)
    return vals, rank


def _peer_router_kernel(x_ref, g_ref, wqt_ref, sk_ref, h2_ref, lim_ref, rk2_ref, e1_ref, e2_ref,
                        qt_ref, s_ref, rank_ref, v_ref):
    f32 = jnp.float32
    tl = x_ref.shape[0]
    x = x_ref[...]
    h2 = x * lax.rsqrt(jnp.mean(x * x, axis=-1, keepdims=True) + EPS) * g_ref[...]
    h2b = h2.astype(jnp.bfloat16)
    h2_ref[...] = h2b
    qt_ref[...] = lax.dot_general(wqt_ref[...], h2b, (((1,), (1,)), ((), ())),
                                  preferred_element_type=f32).astype(jnp.bfloat16)

    def score_body(hc, carry):
        r0 = pl.multiple_of(hc * HALF_Q, HALF_Q)
        s_ref[hc] = jnp.dot(sk_ref[hc], qt_ref[pl.ds(r0, HALF_Q), :], preferred_element_type=f32)
        return carry

    lax.fori_loop(0, N_HC, score_body, 0)

    key_iota = lax.broadcasted_iota(jnp.int32, (N_KEYS, LANE), 0).astype(f32)

    def key_body(i, carry):
        hc = i // (tl // LANE)
        c0 = pl.multiple_of((i % (tl // LANE)) * LANE, LANE)
        s = s_ref[hc, :, pl.ds(c0, LANE)]
        vals, rank = _extract_top16(s, key_iota)
        rank_ref[hc, :, pl.ds(c0, LANE)] = rank
        v_ref[hc, :, pl.ds(c0, LANE)] = jnp.concatenate(vals, axis=0)
        return carry

    lax.fori_loop(0, N_HC * (tl // LANE), key_body, 0, unroll=4)

    sub = CAND_SUB
    n_mid = sub - 1
    n_cand = PEER_TOPK + n_mid * sub + (PEER_TOPK - sub)
    cand_iota = lax.broadcasted_iota(jnp.int32, (n_cand, LANE), 0).astype(f32)
    row8 = lax.broadcasted_iota(jnp.int32, (sub, LANE), 0)

    def head_body(i, carry):
        h = i // (tl // LANE)
        c0 = pl.multiple_of((i % (tl // LANE)) * LANE, LANE)
        cols = pl.ds(c0, LANE)
        v1 = v_ref[2 * h, :, cols]
        v2 = v_ref[2 * h + 1, :, cols]
        blocks = [v1[0:1, :] + v2]
        for r1 in range(1, sub):
            blocks.append(jnp.where(row8 < PEER_TOPK // (r1 + 1), v1[r1:r1 + 1, :] + v2[:sub, :], -jnp.inf))
        blocks.append(v1[sub:, :] + v2[0:1, :])
        best, crank = _extract_top16(jnp.concatenate(blocks, axis=0), cand_iota)
        z = jnp.zeros((1, LANE), f32)
        for k in range(PEER_TOPK):
            z = z + jnp.exp(best[k] - best[0])
        sel = jnp.where(crank < PEER_TOPK, 1.0, 0.0)
        rank1 = rank_ref[2 * h, :, cols]
        lim = jnp.zeros((N_KEYS, LANE), jnp.int32)
        for r1 in range(PEER_TOPK):
            if r1 == 0:
                cnt = jnp.sum(sel[:PEER_TOPK, :], axis=0, keepdims=True)
            elif r1 < sub:
                lo = PEER_TOPK + (r1 - 1) * sub
                cnt = jnp.sum(sel[lo:lo + sub, :], axis=0, keepdims=True)
            else:
                lo = PEER_TOPK + n_mid * sub + (r1 - sub)
                cnt = sel[lo:lo + 1, :]
            lim = jnp.where(rank1 == r1, cnt.astype(jnp.int32), lim)
        lim_ref[h, :, cols] = lim
        rk2_ref[h, :, cols] = rank_ref[2 * h + 1, :, cols].astype(f32).astype(jnp.bfloat16)
        e1_ref[h, :, cols] = jnp.exp(s_ref[2 * h, :, cols] - v1[0:1, :])
        e2_ref[h, :, cols] = (jnp.exp(s_ref[2 * h + 1, :, cols] - v2[0:1, :]) / z).astype(jnp.bfloat16)
        return carry

    lax.fori_loop(0, PEER_HEADS * (tl // LANE), head_body, 0, unroll=2)


def _gelu_tanh(x):
    c = math.sqrt(2.0 / math.pi)
    half = 0.5 * x
    return half + half * jnp.tanh(x * (c + (c * 0.044715) * (x * x)))


BF16_ROWS = 16


def _rows_bf16(row):
    tile = jnp.broadcast_to(row, (BF16_ROWS, LANE)).astype(jnp.bfloat16)
    return jnp.concatenate([tile] * (N_KEYS // BF16_ROWS), axis=0)


def _peer_expert_kernel(x_ref, h2_ref, *refs):
    pu_refs, pvt_refs = refs[:PEER_DMA_PARTS], refs[PEER_DMA_PARTS:2 * PEER_DMA_PARTS]
    lim_ref, rk2_ref, e1_ref, e2_ref, o_ref, acc_ref, a_ref, c_ref, rk2_s, e2_s = refs[2 * PEER_DMA_PARTS:]
    f32 = jnp.float32
    bf16 = jnp.bfloat16
    zero = jnp.zeros((N_KEYS, LANE), bf16)
    e_step = pl.program_id(1)
    tl = h2_ref.shape[0]
    part = pu_refs[0].shape[0]
    n_i1 = PEER_DMA_PARTS * part // N_KEYS
    chunk = 2 * LANE if tl % (2 * LANE) == 0 else LANE

    @pl.when(e_step == 0)
    def _():
        acc_ref[...] = jnp.zeros_like(acc_ref)
        rk2_s[...] = rk2_ref[...]
        e2_s[...] = e2_ref[...]

    for p, pu_ref in enumerate(pu_refs):
        a_ref[p * part:(p + 1) * part, :] = lax.dot_general(pu_ref[...], h2_ref[...], (((1,), (1,)), ((), ())),
                                                            preferred_element_type=f32)
    for c0 in range(0, tl, chunk):
        for l0 in range(c0, c0 + chunk, LANE):
            cols = slice(l0, l0 + LANE)
            for ii in range(n_i1):
                rows = slice(ii * N_KEYS, (ii + 1) * N_KEYS)
                w = jnp.zeros((N_KEYS, LANE), bf16)
                for h in range(PEER_HEADS):
                    sel = rk2_s[h, :, cols] < _rows_bf16(lim_ref[h, ii:ii + 1, cols].astype(f32))
                    w = w + jnp.where(sel, e2_s[h, :, cols], zero) * _rows_bf16(e1_ref[h, ii:ii + 1, cols])
                c_ref[rows, cols] = w * _gelu_tanh(a_ref[rows, cols]).astype(bf16)
        cc = slice(c0, c0 + chunk)
        proj = [jnp.dot(pvt_ref[0], c_ref[p * part:(p + 1) * part, cc], preferred_element_type=f32)
                for p, pvt_ref in enumerate(pvt_refs)]
        acc_ref[:, cc] += functools.reduce(lambda a, b: a + b, proj)

    @pl.when(e_step == pl.num_programs(1) - 1)
    def _():
        o_ref[...] = x_ref[...] + acc_ref[...].T


def _token_tile(T, sizes=(640, 512, 384, 256, 128)):
    for tl in sizes:
        if T % tl == 0:
            return tl
    raise ValueError(f"token count {T} is not a multiple of {LANE}")


def peer_residual(x, g, wqt, sk, pu, pvt, layer):
    T = x.shape[0]
    tl = _token_tile(T)
    nt = T // tl
    n_exp = pu.shape[1]
    head_shape = jax.ShapeDtypeStruct((PEER_HEADS, N_KEYS, T), jnp.int32)
    head_shape_f = jax.ShapeDtypeStruct((PEER_HEADS, N_KEYS, T), jnp.float32)
    head_shape_b = jax.ShapeDtypeStruct((PEER_HEADS, N_KEYS, T), jnp.bfloat16)
    head_spec = pl.BlockSpec((PEER_HEADS, N_KEYS, tl), lambda i: (0, 0, i))
    h2, lim, rk2, e1, e2 = pl.pallas_call(
        _peer_router_kernel,
        grid=(nt,),
        in_specs=[pl.BlockSpec((tl, D_MODEL), lambda i: (i, 0)),
                  pl.BlockSpec((1, D_MODEL), lambda i: (0, 0)),
                  pl.BlockSpec((PEER_HEADS * PEER_QDIM, D_MODEL), lambda i: (0, 0)),
                  pl.BlockSpec((N_HC, N_KEYS, HALF_Q), lambda i: (0, 0, 0))],
        out_specs=[pl.BlockSpec((tl, D_MODEL), lambda i: (i, 0)), head_spec, head_spec, head_spec, head_spec],
        out_shape=[jax.ShapeDtypeStruct((T, D_MODEL), jnp.bfloat16), head_shape, head_shape_b,
                   head_shape_f, head_shape_b],
        scratch_shapes=[pltpu.VMEM((PEER_HEADS * PEER_QDIM, tl), jnp.bfloat16),
                        pltpu.VMEM((N_HC, N_KEYS, tl), jnp.float32),
                        pltpu.VMEM((N_HC, N_KEYS, tl), jnp.int32),
                        pltpu.VMEM((N_HC, PEER_TOPK, tl), jnp.float32)],
        compiler_params=pltpu.CompilerParams(dimension_semantics=("parallel",), vmem_limit_bytes=VMEM_LIMIT),
        name="peer_router",
    )(x, g.reshape(1, D_MODEL), wqt, sk)

    tl = _token_tile(T, (768, 512, 640, 384, 256, 128))
    nt = T // tl
    te = PEER_EXPERT_TILE
    parts = PEER_DMA_PARTS
    part = te // parts
    i2_spec = pl.BlockSpec((PEER_HEADS, N_KEYS, tl), lambda i, e: (0, 0, i))
    i1_spec = pl.BlockSpec((PEER_HEADS, te // N_KEYS, tl), lambda i, e: (0, e, i))
    return pl.pallas_call(
        _peer_expert_kernel,
        grid=(nt, n_exp // te),
        in_specs=[pl.BlockSpec((tl, D_MODEL), lambda i, e: (i, 0)),
                  pl.BlockSpec((tl, D_MODEL), lambda i, e: (i, 0)),
                  *[pl.BlockSpec((None, part, D_MODEL), lambda i, e, p=p: (layer, parts * e + p, 0))
                    for p in range(parts)],
                  *[pl.BlockSpec((None, 1, D_MODEL, part), lambda i, e, p=p: (layer, parts * e + p, 0, 0))
                    for p in range(parts)],
                  i1_spec, i2_spec, i1_spec, i2_spec],
        out_specs=pl.BlockSpec((tl, D_MODEL), lambda i, e: (i, 0)),
        out_shape=jax.ShapeDtypeStruct((T, D_MODEL), jnp.float32),
        scratch_shapes=[pltpu.VMEM((D_MODEL, tl), jnp.float32),
                        pltpu.VMEM((te, tl), jnp.float32),
                        pltpu.VMEM((te, tl), jnp.bfloat16),
                        pltpu.VMEM((PEER_HEADS, N_KEYS, tl), jnp.bfloat16),
                        pltpu.VMEM((PEER_HEADS, N_KEYS, tl), jnp.bfloat16)],
        compiler_params=pltpu.CompilerParams(dimension_semantics=("parallel", "arbitrary"),
                                             vmem_limit_bytes=VMEM_LIMIT),
        name="peer_experts",
    )(x, h2, *([pu] * parts), *([pvt] * parts), lim, rk2, e1, e2)


C_QKVB = 3 * WIDTH_A
C_GB = C_QKVB + 3 * WIDTH_B
C_UC = C_GB + WIDTH_B
C_GATE = C_UC + WIDTH_C
C_BA = C_GATE + 3 * D_MODEL
D_IN2 = C_BA + LANE
S5_BLOCK = 256


def _reorder_w_in(w):
    cols = [w[:, OFF_QA:OFF_BETA], w[:, OFF_UC:OFF_GATE], w[:, OFF_GATE:], w[:, OFF_BETA:OFF_UC],
            jnp.zeros((w.shape[0], LANE - 2 * N_HEADS_B), w.dtype)]
    return jnp.concatenate(cols, axis=1).astype(jnp.bfloat16)


def trunk_layer(x, lp, layer_idx, att, prompt, k_past, v_past, conv_hist, delta0, ssm0_re, ssm0_im):
    f32 = jnp.float32
    B, Lx = x.shape[0], x.shape[1]
    T = B * Lx
    z = norm_matmul(x.reshape(T, D_MODEL), lp['norm1'], lp['w_in2'])

    qn, kn, knb, vb = qk_prep(z, lp['q_norm'], lp['k_norm'], _token_tile(T))
    lam_init = 0.8 - 0.6 * math.exp(-0.3 * layer_idx)
    lqk = lp['lambda_qk']
    lam = jnp.exp(jnp.sum(lqk[0] * lqk[1])) - jnp.exp(jnp.sum(lqk[2] * lqk[3])) + lam_init
    scalars = jnp.stack([lam, jnp.asarray(1.0 - lam_init, f32)]).astype(f32)
    r3 = lambda a: a.reshape(B, Lx, WIDTH_A)
    r4 = lambda a: a.reshape(N_HEADS_A, B, Lx, a.shape[-1])
    if prompt:
        oA = attn_prompt(r4(qn), r4(knb), r4(vb), scalars, att['far'], att['tiles'], lp['subln'])
    else:
        P = k_past.shape[1]
        oA = attn_sample(r4(qn), r4(knb), r4(vb), k_past.reshape(B, P, WIDTH_A), v_past.reshape(B, P, WIDTH_A),
                         scalars, att['bias_past'], att['bias_new'], lp['subln'])

    z3 = z.reshape(B, Lx, D_IN2)
    off = PAD if prompt else 0
    L = Lx - off
    kA = r3(kn)[:, off:].reshape(B, L, N_HEADS_A, 2 * HEAD_DIM_A)
    vA = z3[:, off:, 2 * WIDTH_A:3 * WIDTH_A].reshape(B, L, N_HEADS_A, 2 * HEAD_DIM_A)

    oB, delta, tail = gdn_mixer(z3, (C_QKVB // (3 * WIDTH_B), C_GB // WIDTH_B, C_BA // LANE), conv_hist, delta0,
                                lp['conv_w'], lp['a_log'], lp['dt_bias'], lp['gdn_norm'],
                                GDN_CHUNK if prompt else Lx, off)
    conv_state = tail[:, GDN_TAIL - (CONV_W - 1):]

    oC, ssm_re, ssm_im = s5_glu(z3, C_UC // WIDTH_C, ssm0_re, ssm0_im, lp['s5'], lp['w_glu'],
                                S5_BLOCK if prompt else Lx)

    x2 = merge_residual(x.reshape(T, D_MODEL), oA.reshape(T, WIDTH_A), oB.reshape(T, WIDTH_B),
                        oC.reshape(T, WIDTH_C), z, C_GATE // D_MODEL, lp['wb_a'], lp['wb_b'], lp['wb_c'],
                        lp['w_out'], Lx, off)

    return (x2, kA, vA, conv_state, delta, ssm_re, ssm_im)


def kernel(x_prompt, x_sample, cache_k, cache_v, state_conv, state_delta, state_ssm_re, state_ssm_im,
           meta_tokens, rel_bias, norm1_g, norm2_g, final_norm_g, w_in, q_norm_g, k_norm_g, lambda_qk,
           subln_g, conv_w, gdn_a_log, gdn_dt_bias, gdn_norm_g, s5_a_re, s5_a_im, s5_b_re, s5_b_im,
           s5_c_re, s5_c_im, s5_d, s5_log_dt, w_glu, w_branch_a, w_branch_b, w_branch_c, w_out,
           peer_wq, peer_subkeys, peer_u, peer_v):
    f32 = jnp.float32
    bf16 = jnp.bfloat16
    params = [dict(norm1=norm1_g[l], norm2=norm2_g[l], w_in2=_reorder_w_in(w_in[l]), q_norm=q_norm_g[l],
                   k_norm=k_norm_g[l], lambda_qk=lambda_qk[l], subln=subln_g[l], conv_w=conv_w[l],
                   a_log=gdn_a_log[l], dt_bias=gdn_dt_bias[l], gdn_norm=gdn_norm_g[l],
                   s5=s5_params(s5_a_re[l], s5_a_im[l], s5_b_re[l], s5_b_im[l], s5_c_re[l], s5_c_im[l], s5_d[l],
                                s5_log_dt[l]),
                   w_glu=w_glu[l], wb_a=w_branch_a[l], wb_b=w_branch_b[l], wb_c=w_branch_c[l], w_out=w_out[l],
                   peer_wqt=peer_wq[l].T.astype(bf16),
                   peer_sk=peer_subkeys[l].reshape(N_HC, N_KEYS, HALF_Q).astype(bf16))
              for l in range(DEPTH)]
    part = PEER_EXPERT_TILE // PEER_DMA_PARTS
    pu_all = peer_u.astype(bf16)
    pvt_all = peer_v.astype(bf16).reshape(peer_v.shape[0], -1, part, D_MODEL).transpose(0, 1, 3, 2)

    qpos = jnp.arange(ATT_TQ, dtype=jnp.int32)
    tiles = jnp.stack([t5_bias(qpos, d * ATT_TK + jnp.arange(ATT_TK, dtype=jnp.int32), rel_bias)
                       for d in (-2, -1, 0)], axis=1)
    far = t5_bias(jnp.full((1,), 2 * ATT_TK, jnp.int32), jnp.zeros((1,), jnp.int32), rel_bias).reshape(N_HEADS_A)
    P, Ls = cache_k.shape[2], x_sample.shape[1]
    bias_s = t5_bias(P + jnp.arange(Ls, dtype=jnp.int32), jnp.arange(P + Ls, dtype=jnp.int32), rel_bias)
    att_p = dict(tiles=tiles, far=far)
    att_s = dict(bias_past=bias_s[:, :, :P], bias_new=bias_s[:, :, P:])

    B = x_prompt.shape[0]
    xp = jnp.concatenate([jnp.zeros((B, PAD, D_MODEL), f32),
                          jnp.broadcast_to(meta_tokens[None], (B, N_META, D_MODEL)), x_prompt], axis=1)
    xs = x_sample
    outs_p = [[] for _ in range(6)]
    outs_s = [[] for _ in range(6)]
    for l in range(DEPTH):
        res_p = trunk_layer(
            xp, params[l], l, att_p, True, None, None,
            jnp.zeros((B, CONV_W - 1, 3 * WIDTH_B), f32),
            jnp.zeros((B, N_HEADS_B, HEAD_DIM_B, HEAD_DIM_B), f32),
            jnp.zeros((B, S5_GROUPS, S5_STATE), f32), jnp.zeros((B, S5_GROUPS, S5_STATE), f32))
        res_s = trunk_layer(
            xs, params[l], l, att_s, False, cache_k[l], cache_v[l], state_conv[l],
            state_delta[l], state_ssm_re[l], state_ssm_im[l])
        for acc, r in zip(outs_p, res_p[1:]):
            acc.append(r)
        for acc, r in zip(outs_s, res_s[1:]):
            acc.append(r)
        n_p = res_p[0].shape[0]
        x_all = peer_residual(jnp.concatenate([res_p[0], res_s[0]], axis=0), params[l]['norm2'],
                              params[l]['peer_wqt'], params[l]['peer_sk'], pu_all, pvt_all, l)
        xp = x_all[:n_p].reshape(xp.shape)
        xs = x_all[n_p:].reshape(xs.shape)
    y_prompt = rmsnorm(xp, final_norm_g)[:, FRONT:]
    y_sample = rmsnorm(xs, final_norm_g)

    kp, vp, cp, dp, srp, sip = [jnp.stack(a) for a in outs_p]
    ks_, vs_, cs_, ds_, srs, sis = [jnp.stack(a) for a in outs_s]
    return (y_prompt, y_sample, kp, vp, ks_, vs_, cp, cs_, dp, ds_, srp, sip, srs, sis)
```

```python
import functools
import math

import jax
import jax.numpy as jnp
from jax import lax
from jax.experimental import pallas as pl
from jax.experimental.pallas import tpu as pltpu

D_MODEL = 1024
DEPTH = 4
CHUNK = 64
N_META = 16
EPS = 1e-6
NEG_INF = -1e30
N_HEADS_A = 4
HEAD_DIM_A = 64
NUM_BUCKETS = 32
MAX_DISTANCE = 128
N_HEADS_B = 4
HEAD_DIM_B = 128
CONV_W = 4
S5_GROUP = 16
S5_GROUPS = 32
S5_STATE = 64
PEER_HEADS = 8
PEER_QDIM = 256
N_KEYS = 128
PEER_TOPK = 16

WIDTH_A = N_HEADS_A * 2 * HEAD_DIM_A
WIDTH_B = N_HEADS_B * HEAD_DIM_B
WIDTH_C = S5_GROUPS * S5_GROUP
OFF_QA = 0
OFF_KA = OFF_QA + WIDTH_A
OFF_VA = OFF_KA + WIDTH_A
OFF_QKVB = OFF_VA + WIDTH_A
OFF_GB = OFF_QKVB + 3 * WIDTH_B
OFF_BETA = OFF_GB + WIDTH_B
OFF_ALPHA = OFF_BETA + N_HEADS_B
OFF_UC = OFF_ALPHA + N_HEADS_B
OFF_GATE = OFF_UC + WIDTH_C
D_IN = OFF_GATE + 3 * D_MODEL

LANE = 128
VMEM_LIMIT = 56 * 1024 * 1024
WEIGHT_TILE_BYTES = 6 * 1024 * 1024
HIGHEST = lax.Precision.HIGHEST


def _norm_mm_kernel(x_ref, g_ref, w_ref, o_ref, h_ref):
    @pl.when(pl.program_id(1) == 0)
    def _():
        x = x_ref[...]
        h = x * lax.rsqrt(jnp.mean(x * x, axis=-1, keepdims=True) + EPS) * g_ref[...]
        h_ref[...] = h.astype(jnp.bfloat16)

    o_ref[...] = jnp.dot(h_ref[...], w_ref[...], preferred_element_type=jnp.float32)


def norm_matmul(x, g, wb, tm=512):
    M, K = x.shape
    N = wb.shape[1]
    n_lanes = N // LANE
    tn = LANE * max(d for d in range(1, n_lanes + 1) if n_lanes % d == 0 and d * LANE * K * 2 <= WEIGHT_TILE_BYTES)
    tm = min(tm, M)
    return pl.pallas_call(
        _norm_mm_kernel,
        grid=(pl.cdiv(M, tm), N // tn),
        in_specs=[pl.BlockSpec((tm, K), lambda i, j: (i, 0)),
                  pl.BlockSpec((1, K), lambda i, j: (0, 0)),
                  pl.BlockSpec((K, tn), lambda i, j: (0, j))],
        out_specs=pl.BlockSpec((tm, tn), lambda i, j: (i, j)),
        out_shape=jax.ShapeDtypeStruct((M, N), jnp.float32),
        scratch_shapes=[pltpu.VMEM((tm, K), jnp.bfloat16)],
        compiler_params=pltpu.CompilerParams(dimension_semantics=("parallel", "arbitrary"),
                                             vmem_limit_bytes=VMEM_LIMIT),
        name="norm_proj",
    )(x, g.reshape(1, K), wb)


def rmsnorm(x, g):
    xf = x.astype(jnp.float32)
    y = xf * lax.rsqrt(jnp.mean(xf * xf, axis=-1, keepdims=True) + EPS)
    return (y * g.astype(jnp.float32)).astype(x.dtype)


def t5_bias(q_pos, k_pos, table):
    rel = k_pos[None, :] - q_pos[:, None]
    half = NUM_BUCKETS // 2
    exact = half // 2
    n = jnp.abs(rel)
    nf = jnp.maximum(n, 1).astype(jnp.float32)
    far = exact + (jnp.log(nf / exact) / math.log(MAX_DISTANCE / exact) * (half - exact)).astype(jnp.int32)
    bucket = jnp.where(rel > 0, half, 0) + jnp.where(n < exact, n, jnp.minimum(far, half - 1))
    hit = bucket[None, :, :] == jnp.arange(NUM_BUCKETS, dtype=bucket.dtype)[:, None, None]
    tab = table.astype(jnp.float32)
    return jnp.stack([jnp.sum(jnp.where(hit, tab[:, h, None, None], 0.0), axis=0) for h in range(tab.shape[1])])


FRONT = 256
PAD = FRONT - N_META
ATT_TQ = 256
ATT_TK = 256


def _qk_prep_kernel(q_ref, k_ref, v_ref, gq_ref, gk_ref, seg_ref, qn_ref, kn_ref, knb_ref, vb_ref):
    f32 = jnp.float32
    bf16 = jnp.bfloat16
    seg = seg_ref[...]
    hd = 2 * HEAD_DIM_A

    def norm(x, g):
        ms = jnp.dot(x * x, seg, precision=HIGHEST, preferred_element_type=f32)
        return x * lax.rsqrt(ms + EPS) * g

    qn = (norm(q_ref[...], gq_ref[...]) * (HEAD_DIM_A ** -0.5)).astype(bf16)
    kn = norm(k_ref[...], gk_ref[...])
    kn_ref[...] = kn
    knb = kn.astype(bf16)
    v = v_ref[...].astype(bf16)
    lane = lax.broadcasted_iota(jnp.int32, (v.shape[0], hd), 1)
    ones_col = jnp.where(lane == 0, 1.0, 0.0).astype(bf16)
    for h in range(N_HEADS_A):
        cols = slice(h * hd, (h + 1) * hd)
        qn_ref[h] = qn[:, cols]
        knb_ref[h] = knb[:, cols]
        vb_ref[h] = jnp.concatenate([v[:, cols], ones_col], axis=1)


def qk_prep(z, gq, gk, tm):
    T = z.shape[0]
    hd = 2 * HEAD_DIM_A
    seg = jnp.kron(jnp.eye(WIDTH_A // HEAD_DIM_A, dtype=jnp.float32),
                   jnp.full((HEAD_DIM_A, HEAD_DIM_A), 1.0 / HEAD_DIM_A, jnp.float32))
    row = lambda j: pl.BlockSpec((tm, WIDTH_A), lambda i, j=j: (i, j))
    const = lambda shape: pl.BlockSpec(shape, lambda i: (0,) * len(shape))
    heads = lambda w: pl.BlockSpec((N_HEADS_A, tm, w), lambda i: (0, i, 0))
    hshape = lambda w: jax.ShapeDtypeStruct((N_HEADS_A, T, w), jnp.bfloat16)
    return pl.pallas_call(
        _qk_prep_kernel,
        grid=(T // tm,),
        in_specs=[row(0), row(1), row(2), const((1, WIDTH_A)), const((1, WIDTH_A)), const((WIDTH_A, WIDTH_A))],
        out_specs=[heads(hd), pl.BlockSpec((tm, WIDTH_A), lambda i: (i, 0)), heads(hd), heads(2 * hd)],
        out_shape=[hshape(hd), jax.ShapeDtypeStruct((T, WIDTH_A), jnp.float32), hshape(hd), hshape(2 * hd)],
        compiler_params=pltpu.CompilerParams(dimension_semantics=("parallel",)),
        name="qk_prep",
    )(z, z, z, jnp.tile(gq, WIDTH_A // HEAD_DIM_A).reshape(1, WIDTH_A),
      jnp.tile(gk, WIDTH_A // HEAD_DIM_A).reshape(1, WIDTH_A), seg)


def _split_maps(q):
    lane = lax.broadcasted_iota(jnp.int32, q.shape, 1)
    zero = jnp.zeros_like(q)
    return jnp.where(lane < HEAD_DIM_A, q, zero), jnp.where(lane >= HEAD_DIM_A, q, zero)


def _subln(o, g, scale):
    return o * lax.rsqrt(jnp.mean(o * o, axis=-1, keepdims=True) + EPS) * g * scale


def _attn_prompt_kernel(sc_ref, far_ref, q_ref, k_ref, v_ref, bias_ref, g_ref, o_ref, m_ref, acc_ref,
                        sa_ref, sb_ref):
    f32 = jnp.float32
    h = pl.program_id(1)
    qi = pl.program_id(2)
    tq, tk = ATT_TQ, ATT_TK
    hd = 2 * HEAD_DIM_A
    qs = jnp.concatenate(_split_maps(q_ref[0, 0]), axis=0)
    m_ref[...] = jnp.full(m_ref.shape, NEG_INF, f32)
    acc_ref[...] = jnp.zeros(acc_ref.shape, f32)
    far = far_ref[h]

    def logits(k0, width, general):
        k0 = pl.multiple_of(k0, tk)
        kt = k_ref[0, 0, pl.ds(k0, width), :]
        s = lax.dot_general(qs, kt, (((1,), (1,)), ((), ())), preferred_element_type=f32)
        if general:
            kj = k0 // tk
            bias = jnp.concatenate([bias_ref[0, jnp.clip(kj + j - qi + 2, 0, 2)] for j in range(width // tk)], axis=1)
            qpos = qi * tq + lax.broadcasted_iota(jnp.int32, (tq, width), 0)
            kpos = k0 + lax.broadcasted_iota(jnp.int32, (tq, width), 1)
            qchunk = jnp.where(qpos < FRONT, 0, 1 + jnp.right_shift(qpos - FRONT, 6))
            kchunk = jnp.where(kpos < FRONT, 0, 1 + jnp.right_shift(kpos - FRONT, 6))
            mask = (kpos >= PAD) & (kchunk <= qchunk)
            s = jnp.where(jnp.concatenate([mask, mask], axis=0), s + jnp.concatenate([bias, bias], axis=0), NEG_INF)
        else:
            s = s + far
        return s

    def accumulate(s, k0):
        width = s.shape[1]
        vt = v_ref[0, 0, pl.ds(pl.multiple_of(k0, tk), width), :]
        n_lane_tiles = width // LANE
        smax = s[:, :LANE]
        for j in range(1, n_lane_tiles):
            smax = jnp.maximum(smax, s[:, j * LANE:(j + 1) * LANE])
        m_old = m_ref[...]
        m_new = jnp.maximum(m_old, jnp.broadcast_to(jnp.max(smax, axis=1, keepdims=True), m_old.shape))
        alpha = jnp.exp(m_old - m_new)
        p = jnp.exp(s - jnp.concatenate([m_new] * n_lane_tiles, axis=1))
        acc_ref[...] = (jnp.concatenate([alpha, alpha], axis=1) * acc_ref[...]
                        + jnp.dot(p.astype(jnp.bfloat16), vt, preferred_element_type=f32))
        m_ref[...] = m_new

    def tile(k0, width, general):
        accumulate(logits(k0, width, general), k0)

    tile(0, tk, True)

    wide = 2 * tk
    n_far = jnp.maximum(qi - 2, 0)
    n_single = n_far % 4

    @pl.when(n_single >= 2)
    def _():
        tile(tk, wide, False)

    @pl.when(n_single % 2 == 1)
    def _():
        tile((1 + 2 * (n_single // 2)) * tk, tk, False)

    base = (1 + n_single) * tk
    n_pairs = n_far // 4

    @pl.when(n_pairs > 0)
    def _():
        sa_ref[...] = logits(base, wide, False)

    def pair_body(j, carry):
        k0 = base + 2 * j * wide
        sb_ref[...] = logits(k0 + wide, wide, False)
        accumulate(sa_ref[...], k0)
        nxt = jnp.minimum(k0 + 2 * wide, base + (2 * n_pairs - 1) * wide)
        sa_ref[...] = logits(nxt, wide, False)
        accumulate(sb_ref[...], k0 + wide)
        return carry

    lax.fori_loop(0, n_pairs, pair_body, 0)

    @pl.when(qi >= 2)
    def _():
        tile((qi - 1) * tk, wide, True)

    @pl.when(qi == 1)
    def _():
        tile(tk, tk, True)

    acc = acc_ref[...]
    out = acc[:, :hd] / acc[:, hd:hd + 1]
    o = out[:tq] - sc_ref[0] * out[tq:]
    o_ref[0] = _subln(o, g_ref[...], sc_ref[1])


def attn_prompt(qn, knb, vb, scalars, far, bias_tiles, subln_g):
    _, B, Lp, _ = qn.shape
    hd = 2 * HEAD_DIM_A
    smem = pl.BlockSpec(memory_space=pltpu.SMEM)
    return pl.pallas_call(
        _attn_prompt_kernel,
        grid=(B, N_HEADS_A, Lp // ATT_TQ),
        in_specs=[smem, smem,
                  pl.BlockSpec((1, 1, ATT_TQ, hd), lambda b, h, i: (h, b, i, 0)),
                  pl.BlockSpec((1, 1, Lp, hd), lambda b, h, i: (h, b, 0, 0)),
                  pl.BlockSpec((1, 1, Lp, 2 * hd), lambda b, h, i: (h, b, 0, 0)),
                  pl.BlockSpec((1, 3, ATT_TQ, ATT_TK), lambda b, h, i: (h, 0, 0, 0)),
                  pl.BlockSpec((1, hd), lambda b, h, i: (0, 0))],
        out_specs=pl.BlockSpec((1, ATT_TQ, hd), lambda b, h, i: (b, i, h)),
        out_shape=jax.ShapeDtypeStruct((B, Lp, WIDTH_A), jnp.float32),
        scratch_shapes=[pltpu.VMEM((2 * ATT_TQ, hd), jnp.float32), pltpu.VMEM((2 * ATT_TQ, 2 * hd), jnp.float32),
                        pltpu.VMEM((2 * ATT_TQ, 2 * ATT_TK), jnp.float32),
                        pltpu.VMEM((2 * ATT_TQ, 2 * ATT_TK), jnp.float32)],
        compiler_params=pltpu.CompilerParams(dimension_semantics=("parallel", "parallel", "parallel"),
                                             vmem_limit_bytes=VMEM_LIMIT),
        name="diff_attn_prompt",
    )(scalars, far, qn, knb, vb, bias_tiles, subln_g.reshape(1, hd))


def _attn_sample_kernel(sc_ref, q_ref, kp_ref, vp_ref, kn_ref, vn_ref, bp_ref, bn_ref, g_ref, o_ref):
    f32 = jnp.float32
    bf16 = jnp.bfloat16
    qs = _split_maps(q_ref[0, 0])
    kp = kp_ref[0].astype(bf16)
    vp = vp_ref[0].astype(bf16)
    kn = kn_ref[0, 0]
    vn = vn_ref[0, 0][:, :2 * HEAD_DIM_A]
    dn = (((1,), (1,)), ((), ()))
    outs = []
    for c in range(2):
        sp = lax.dot_general(qs[c], kp, dn, preferred_element_type=f32) + bp_ref[0]
        sn = lax.dot_general(qs[c], kn, dn, preferred_element_type=f32) + bn_ref[0]
        m = jnp.maximum(jnp.max(sp, axis=1, keepdims=True), jnp.max(sn, axis=1, keepdims=True))
        pp = jnp.exp(sp - m)
        pn = jnp.exp(sn - m)
        l = jnp.sum(pp, axis=1, keepdims=True) + jnp.sum(pn, axis=1, keepdims=True)
        acc = (jnp.dot(pp.astype(bf16), vp, preferred_element_type=f32)
               + jnp.dot(pn.astype(bf16), vn, preferred_element_type=f32))
        outs.append(acc / l)
    o_ref[0] = _subln(outs[0] - sc_ref[0] * outs[1], g_ref[...], sc_ref[1])


def attn_sample(qn, knb, vb, k_past, v_past, scalars, bias_past, bias_new, subln_g):
    _, B, L, _ = qn.shape
    P = k_past.shape[1]
    hd = 2 * HEAD_DIM_A
    new = pl.BlockSpec((1, L, hd), lambda b, h: (b, 0, h))
    new_h = lambda w: pl.BlockSpec((1, 1, L, w), lambda b, h: (h, b, 0, 0))
    past = pl.BlockSpec((1, P, hd), lambda b, h: (b, 0, h))
    return pl.pallas_call(
        _attn_sample_kernel,
        grid=(B, N_HEADS_A),
        in_specs=[pl.BlockSpec(memory_space=pltpu.SMEM), new_h(hd), past, past, new_h(hd), new_h(2 * hd),
                  pl.BlockSpec((1, L, P), lambda b, h: (h, 0, 0)),
                  pl.BlockSpec((1, L, L), lambda b, h: (h, 0, 0)),
                  pl.BlockSpec((1, hd), lambda b, h: (0, 0))],
        out_specs=new,
        out_shape=jax.ShapeDtypeStruct((B, L, WIDTH_A), jnp.float32),
        compiler_params=pltpu.CompilerParams(dimension_semantics=("parallel", "parallel")),
        name="diff_attn_sample",
    )(scalars, qn, k_past, v_past, knb, vb, bias_past, bias_new, subln_g.reshape(1, hd))


S5_N = S5_GROUPS * S5_STATE
S5_SLAB = 8
S5_LANES = 512
S5_SPLIT = 2


def _sigmoid(x):
    return 1.0 / (1.0 + jnp.exp(-x))


def _s5_kernel(u_ref, x0_ref, bmat_ref, cmat_ref, lamp_ref, ppow_ref, d_ref, wglu_ref, oc_ref, xf_ref,
               bu_ref, carry_ref):
    f32 = jnp.float32
    bf16 = jnp.bfloat16
    t = pl.program_id(1)
    tb = u_ref.shape[1]

    @pl.when(t == 0)
    def _():
        carry_ref[...] = x0_ref[0]

    u = u_ref[0]
    ub = u.astype(bf16)
    ch, st = WIDTH_C // S5_SPLIT, S5_N // S5_SPLIT
    for s in range(S5_SPLIT):
        for part in range(2):
            cols = slice(part * S5_N + s * st, part * S5_N + (s + 1) * st)
            bu_ref[:, cols] = jnp.dot(ub[:, s * ch:(s + 1) * ch], bmat_ref[s * ch:(s + 1) * ch, cols],
                                      preferred_element_type=f32)

    row = lax.broadcasted_iota(jnp.int32, (S5_SLAB, S5_LANES), 0)
    for c in range(S5_N // S5_LANES):
        re = slice(c * S5_LANES, (c + 1) * S5_LANES)
        im = slice(S5_N + c * S5_LANES, S5_N + (c + 1) * S5_LANES)

        def slab(i, carry):
            cre, cim = carry
            rows = pl.ds(pl.multiple_of(i * S5_SLAB, S5_SLAB), S5_SLAB)
            yre = bu_ref[rows, re]
            yim = bu_ref[rows, im]
            for s in range(3):
                sh = 1 << s
                sre = jnp.where(row >= sh, pltpu.roll(yre, sh, 0), 0.0)
                sim = jnp.where(row >= sh, pltpu.roll(yim, sh, 0), 0.0)
                lr = lamp_ref[s, 0, :, re]
                li = lamp_ref[s, 1, :, re]
                yre, yim = yre + (lr * sre - li * sim), yim + (lr * sim + li * sre)
            pr = ppow_ref[0, :, re]
            pi = ppow_ref[1, :, re]
            yre, yim = yre + (pr * cre - pi * cim), yim + (pr * cim + pi * cre)
            bu_ref[rows, re] = yre
            bu_ref[rows, im] = yim
            last = S5_SLAB - 1
            return (jnp.broadcast_to(yre[last:last + 1, :], yre.shape),
                    jnp.broadcast_to(yim[last:last + 1, :], yim.shape))

        cre, cim = lax.fori_loop(0, tb // S5_SLAB, slab, (carry_ref[:, re], carry_ref[:, im]))
        carry_ref[:, re] = cre
        carry_ref[:, im] = cim

    ys = []
    for s in range(S5_SPLIT):
        out = slice(s * ch, (s + 1) * ch)
        acc = None
        for part in range(2):
            cols = slice(part * S5_N + s * st, part * S5_N + (s + 1) * st)
            term = jnp.dot(bu_ref[:, cols].astype(bf16), cmat_ref[cols, out], preferred_element_type=f32)
            acc = term if acc is None else acc + term
        ys.append(acc)
    y = jnp.concatenate(ys, axis=1) + d_ref[...] * u
    gl = jnp.dot(_gelu_tanh(y).astype(bf16), wglu_ref[...], preferred_element_type=f32)
    oc_ref[0] = gl[:, :WIDTH_C] * _sigmoid(gl[:, WIDTH_C:])

    @pl.when(t == pl.num_programs(1) - 1)
    def _():
        xf_ref[0] = carry_ref[...]


def s5_params(a_re, a_im, b_re, b_im, c_re, c_im, d, log_dt):
    f32 = jnp.float32
    lam = lax.complex(a_re, a_im)
    lam_bar = jnp.exp(lam * jnp.exp(log_dt)[:, None])
    b_bar = ((lam_bar - 1.0) / lam)[..., None] * lax.complex(b_re, b_im)
    eye = jnp.eye(S5_GROUPS, dtype=f32)
    bd_in = lambda m: jnp.einsum('gpi,gh->gihp', m, eye).reshape(WIDTH_C, S5_N)
    bd_out = lambda m: jnp.einsum('gip,gh->gphi', m, eye).reshape(S5_N, WIDTH_C)
    bmat = jnp.concatenate([bd_in(b_bar.real), bd_in(b_bar.imag)], axis=1).astype(jnp.bfloat16)
    cmat = jnp.concatenate([bd_out(c_re), bd_out(-c_im)], axis=0).astype(jnp.bfloat16)
    lb = lam_bar.reshape(S5_N)
    rep = lambda v: jnp.broadcast_to(v[None, :], (S5_SLAB, S5_N))
    pows = [lb, lb * lb, (lb * lb) * (lb * lb)]
    lamp = jnp.stack([jnp.stack([rep(p.real), rep(p.imag)]) for p in pows])
    run = [lb]
    for _ in range(S5_SLAB - 1):
        run.append(run[-1] * lb)
    pp = jnp.stack(run)
    ppow = jnp.stack([pp.real, pp.imag])
    return dict(bmat=bmat, cmat=cmat, lamp=lamp.astype(f32), ppow=ppow.astype(f32), d=d.reshape(1, WIDTH_C))


def s5_glu(z3, col_block, x0_re, x0_im, sp, w_glu, tb):
    B, Lx, _ = z3.shape
    x0 = jnp.concatenate([x0_re.reshape(B, S5_N), x0_im.reshape(B, S5_N)], axis=1)
    x0 = jnp.broadcast_to(x0[:, None, :], (B, S5_SLAB, 2 * S5_N))
    const = lambda a: pl.BlockSpec(a.shape, lambda b, t: (0,) * a.ndim)
    wg = w_glu.astype(jnp.bfloat16)
    oc, xf = pl.pallas_call(
        _s5_kernel,
        grid=(B, Lx // tb),
        in_specs=[pl.BlockSpec((1, tb, WIDTH_C), lambda b, t: (b, t, col_block)),
                  pl.BlockSpec((1, S5_SLAB, 2 * S5_N), lambda b, t: (b, 0, 0)),
                  const(sp['bmat']), const(sp['cmat']), const(sp['lamp']), const(sp['ppow']), const(sp['d']),
                  const(wg)],
        out_specs=[pl.BlockSpec((1, tb, WIDTH_C), lambda b, t: (b, t, 0)),
                   pl.BlockSpec((1, S5_SLAB, 2 * S5_N), lambda b, t: (b, 0, 0))],
        out_shape=[jax.ShapeDtypeStruct((B, Lx, WIDTH_C), jnp.float32),
                   jax.ShapeDtypeStruct((B, S5_SLAB, 2 * S5_N), jnp.float32)],
        scratch_shapes=[pltpu.VMEM((tb, 2 * S5_N), jnp.float32), pltpu.VMEM((S5_SLAB, 2 * S5_N), jnp.float32)],
        compiler_params=pltpu.CompilerParams(dimension_semantics=("parallel", "arbitrary"),
                                             vmem_limit_bytes=VMEM_LIMIT),
        name="s5_glu",
    )(z3, x0, sp['bmat'], sp['cmat'], sp['lamp'], sp['ppow'], sp['d'], wg)
    shape = (B, S5_GROUPS, S5_STATE)
    return oc, xf[:, 0, :S5_N].reshape(shape), xf[:, 0, S5_N:].reshape(shape)


def _gdn_kernel(qkv_ref, gate_ref, ba_ref, hist_ref, s0_ref, cw_ref, av_ref, dt_ref, gn_ref,
                ob_ref, sout_ref, tail_ref, s_ref, prev_ref, *, pad_rows):
    f32 = jnp.float32
    t = pl.program_id(1)
    C = qkv_ref.shape[1]
    hd = HEAD_DIM_B
    dn_last = (((1,), (1,)), ((), ()))
    dn_first = (((0,), (0,)), ((), ()))
    dot = lambda a, b: _dot3(a, b, (((1,), (0,)), ((), ())))

    @pl.when(t == 0)
    def _():
        s_ref[...] = s0_ref[0]
        prev_ref[...] = hist_ref[0]

    cur = qkv_ref[0]
    tail = prev_ref.shape[0]
    ext = jnp.concatenate([prev_ref[...], cur], axis=0)
    w = cw_ref[...]
    conv = cur * w[CONV_W - 1:CONV_W, :]
    for i in range(CONV_W - 1):
        lo = tail - (CONV_W - 1) + i
        conv = conv + ext[lo:lo + C, :] * w[i:i + 1, :]
    prev_ref[...] = cur[C - tail:, :]
    c = conv * _sigmoid(conv)

    ba = ba_ref[0]
    xg = ba + dt_ref[...]
    softplus = jnp.maximum(xg, 0.0) + jnp.log(1.0 + jnp.exp(-jnp.abs(xg)))
    g_all = -jnp.exp(av_ref[...]) * softplus
    if pad_rows:
        grow = t * C + lax.broadcasted_iota(jnp.int32, g_all.shape, 0)
        g_all = jnp.where(grow >= pad_rows, g_all, 0.0)
    ri = lax.broadcasted_iota(jnp.int32, (C, C), 0)
    ci = lax.broadcasted_iota(jnp.int32, (C, C), 1)
    tri = ri >= ci
    strict = ri > ci
    tril = jnp.where(tri, 1.0, 0.0).astype(f32)
    G_col = jnp.dot(tril, g_all, precision=HIGHEST, preferred_element_type=f32)
    G_row = lax.dot_general(g_all, tril, (((0,), (1,)), ((), ())), precision=HIGHEST,
                            preferred_element_type=f32)
    eye = jnp.where(ri == ci, 1.0, 0.0).astype(f32)

    heads = range(N_HEADS_B)
    dot_t = lambda a, b: _dot3(a, b, dn_last)
    Gc = [G_col[:, N_HEADS_B + h:N_HEADS_B + h + 1] for h in heads]
    Gr = [G_row[N_HEADS_B + h:N_HEADS_B + h + 1, :] for h in heads]
    decay = [jnp.where(tri, jnp.exp(jnp.minimum(Gc[h] - Gr[h], 0.0)), 0.0) for h in heads]
    beta = [_sigmoid(ba[:, h:h + 1]) for h in heads]
    unit = lambda a: a * lax.rsqrt(jnp.sum(a * a, axis=-1, keepdims=True) + EPS)
    q = [unit(c[:, h * hd:(h + 1) * hd]) * (hd ** -0.5) for h in heads]
    k = [unit(c[:, WIDTH_B + h * hd:WIDTH_B + (h + 1) * hd]) for h in heads]
    kb = [k[h] * beta[h] for h in heads]
    eG = [jnp.exp(Gc[h]) for h in heads]
    rhs = [jnp.concatenate([c[:, 2 * WIDTH_B + h * hd:2 * WIDTH_B + (h + 1) * hd] * beta[h], kb[h] * eG[h]], axis=1)
           for h in heads]
    A = [jnp.where(strict, dot_t(kb[h], k[h]) * decay[h], 0.0) for h in heads]
    same_block = lambda b: jnp.right_shift(ri, b.bit_length() - 1) == jnp.right_shift(ci, b.bit_length() - 1)
    in_base = same_block(GDN_BASE)
    D = [jnp.where(in_base, A[h], 0.0) for h in heads]
    D2 = [dot(D[h], D[h]) for h in heads]
    D4 = [dot(D2[h], D2[h]) for h in heads]
    tm = [eye - D[h] for h in heads]
    tm = [tm[h] + dot(tm[h], D2[h]) for h in heads]
    tm = [tm[h] + dot(tm[h], D4[h]) for h in heads]
    b = GDN_BASE
    while b < C:
        level = same_block(2 * b) & jnp.logical_not(same_block(b))
        me = [dot(tm[h], jnp.where(level, A[h], 0.0)) for h in heads]
        tm = [tm[h] - dot(me[h], tm[h]) for h in heads]
        b *= 2
    sol = [dot(tm[h], rhs[h]) for h in heads]
    attn = [dot_t(q[h], k[h]) * decay[h] for h in heads]
    S = [s_ref[h] for h in heads]
    v_new = [sol[h][:, :hd] - dot(sol[h][:, hd:], S[h]) for h in heads]
    o = [dot(q[h] * eG[h], S[h]) + dot(attn[h], v_new[h]) for h in heads]
    for h in heads:
        GL = Gc[h][C - 1:C, :]
        s_ref[h] = S[h] * jnp.exp(GL) + _dot3(k[h] * jnp.exp(GL - Gc[h]), v_new[h], dn_first)
        gate = gate_ref[0, :, h * hd:(h + 1) * hd]
        on = o[h] * lax.rsqrt(jnp.mean(o[h] * o[h], axis=-1, keepdims=True) + EPS) * gn_ref[...]
        ob_ref[0, :, h * hd:(h + 1) * hd] = on * (gate * _sigmoid(gate))

    @pl.when(t == pl.num_programs(1) - 1)
    def _():
        sout_ref[0] = s_ref[...]
        tail_ref[0] = cur[C - tail:, :]


GDN_TAIL = 8
GDN_CHUNK = 128
GDN_BASE = 8


def _dot3(a, b, dims):
    f32 = jnp.float32
    bf16 = jnp.bfloat16
    a_hi = a.astype(bf16)
    b_hi = b.astype(bf16)
    a_lo = (a - a_hi.astype(f32)).astype(bf16)
    b_lo = (b - b_hi.astype(f32)).astype(bf16)
    dg = lambda x, y: lax.dot_general(x, y, dims, preferred_element_type=f32)
    return dg(a_hi, b_hi) + (dg(a_hi, b_lo) + dg(a_lo, b_hi))


def gdn_mixer(z3, blocks, conv_hist, delta0, conv_w, a_log, dt_bias, gdn_norm, chunk, pad_rows):
    B, Lx, _ = z3.shape
    f32 = jnp.float32
    wq = 3 * WIDTH_B
    hist = jnp.pad(conv_hist, ((0, 0), (GDN_TAIL - (CONV_W - 1), 0), (0, 0)))
    lane_vec = lambda v: jnp.zeros((1, LANE), f32).at[0, N_HEADS_B:2 * N_HEADS_B].set(v)
    const = lambda shape: pl.BlockSpec(shape, lambda b, t: (0,) * len(shape))
    per_b = lambda shape: pl.BlockSpec((1,) + shape, lambda b, t: (b,) + (0,) * len(shape))
    qb, gb, bb = blocks
    return pl.pallas_call(
        functools.partial(_gdn_kernel, pad_rows=pad_rows),
        grid=(B, Lx // chunk),
        in_specs=[pl.BlockSpec((1, chunk, wq), lambda b, t: (b, t, qb)),
                  pl.BlockSpec((1, chunk, WIDTH_B), lambda b, t: (b, t, gb)),
                  pl.BlockSpec((1, chunk, LANE), lambda b, t: (b, t, bb)),
                  per_b((GDN_TAIL, wq)), per_b((N_HEADS_B, HEAD_DIM_B, HEAD_DIM_B)),
                  const((CONV_W, wq)), const((1, LANE)), const((1, LANE)), const((1, HEAD_DIM_B))],
        out_specs=[pl.BlockSpec((1, chunk, WIDTH_B), lambda b, t: (b, t, 0)),
                   per_b((N_HEADS_B, HEAD_DIM_B, HEAD_DIM_B)), per_b((GDN_TAIL, wq))],
        out_shape=[jax.ShapeDtypeStruct((B, Lx, WIDTH_B), f32),
                   jax.ShapeDtypeStruct((B, N_HEADS_B, HEAD_DIM_B, HEAD_DIM_B), f32),
                   jax.ShapeDtypeStruct((B, GDN_TAIL, wq), f32)],
        scratch_shapes=[pltpu.VMEM((N_HEADS_B, HEAD_DIM_B, HEAD_DIM_B), f32), pltpu.VMEM((GDN_TAIL, wq), f32)],
        compiler_params=pltpu.CompilerParams(dimension_semantics=("parallel", "arbitrary")),
        name="gdn_mixer",
    )(z3, z3, z3, hist, delta0, conv_w, lane_vec(a_log), lane_vec(dt_bias), gdn_norm.reshape(1, HEAD_DIM_B))


def _merge_kernel(x_ref, oa_ref, ob_ref, oc_ref, g0_ref, g1_ref, g2_ref, wa_ref, wb_ref, wc_ref, wo_ref, o_ref,
                  *, rows_per_seq, pad_rows):
    f32 = jnp.float32
    bf16 = jnp.bfloat16
    mm = lambda a, w_ref: jnp.dot(a.astype(bf16), w_ref[...], preferred_element_type=f32)
    merged = (_sigmoid(g0_ref[...]) * mm(oa_ref[...], wa_ref) + _sigmoid(g1_ref[...]) * mm(ob_ref[...], wb_ref)
              + _sigmoid(g2_ref[...]) * mm(oc_ref[...], wc_ref))
    x = x_ref[...] + mm(merged, wo_ref)
    if pad_rows:
        tm = x.shape[0]
        row = pl.program_id(0) * tm + lax.broadcasted_iota(jnp.int32, x.shape, 0)
        x = jnp.where(row % rows_per_seq >= pad_rows, x, 0.0)
    o_ref[...] = x


def merge_residual(x, oA, oB, oC, z, gate_block, wa, wb, wc, wo, rows_per_seq, pad_rows):
    T = x.shape[0]
    tm = _token_tile(T)
    bf16 = jnp.bfloat16
    rows = lambda w, j=0: pl.BlockSpec((tm, w), lambda i, j=j: (i, j))
    const = lambda a: pl.BlockSpec(a.shape, lambda i: (0, 0))
    ws = [w.astype(bf16) for w in (wa, wb, wc, wo)]
    return pl.pallas_call(
        functools.partial(_merge_kernel, rows_per_seq=rows_per_seq, pad_rows=pad_rows),
        grid=(T // tm,),
        in_specs=[rows(D_MODEL), rows(WIDTH_A), rows(WIDTH_B), rows(WIDTH_C),
                  rows(D_MODEL, gate_block), rows(D_MODEL, gate_block + 1), rows(D_MODEL, gate_block + 2)]
                 + [const(w) for w in ws],
        out_specs=rows(D_MODEL),
        out_shape=jax.ShapeDtypeStruct((T, D_MODEL), jnp.float32),
        compiler_params=pltpu.CompilerParams(dimension_semantics=("parallel",), vmem_limit_bytes=VMEM_LIMIT),
        name="merge_residual",
    )(x, oA, oB, oC, z, z, z, *ws)


N_HC = 2 * PEER_HEADS
HALF_Q = PEER_QDIM // 2
PEER_EXPERT_TILE = 8 * N_KEYS
CAND_SUB = 8
PEER_DMA_PARTS = 2


def _extract_top16(s, iota_f):
    n = float(s.shape[0])
    rank = jnp.full(s.shape, PEER_TOPK, jnp.int32)
    vals = []
    for j in range(PEER_TOPK):
        m = jnp.max(s, axis=0, keepdims=True)
        idx = jnp.min(jnp.where(s == m, iota_f, n), axis=0, keepdims=True)
        hit = iota_f == idx
        rank = jnp.where(hit, j, rank)
        s = jnp.where(hit, -jnp.inf, s)
        vals.append(m)
    return vals, rank


def _peer_router_kernel(x_ref, g_ref, wqt_ref, sk_ref, h2_ref, lim_ref, rk2_ref, e1_ref, e2_ref,
                        qt_ref, s_ref, rank_ref, v_ref):
    f32 = jnp.float32
    tl = x_ref.shape[0]
    x = x_ref[...]
    h2 = x * lax.rsqrt(jnp.mean(x * x, axis=-1, keepdims=True) + EPS) * g_ref[...]
    h2b = h2.astype(jnp.bfloat16)
    h2_ref[...] = h2b
    qt_ref[...] = lax.dot_general(wqt_ref[...], h2b, (((1,), (1,)), ((), ())),
                                  preferred_element_type=f32).astype(jnp.bfloat16)

    def score_body(hc, carry):
        r0 = pl.multiple_of(hc * HALF_Q, HALF_Q)
        s_ref[hc] = jnp.dot(sk_ref[hc], qt_ref[pl.ds(r0, HALF_Q), :], preferred_element_type=f32)
        return carry

    lax.fori_loop(0, N_HC, score_body, 0)

    key_iota = lax.broadcasted_iota(jnp.int32, (N_KEYS, LANE), 0).astype(f32)

    def key_body(i, carry):
        hc = i // (tl // LANE)
        c0 = pl.multiple_of((i % (tl // LANE)) * LANE, LANE)
        s = s_ref[hc, :, pl.ds(c0, LANE)]
        vals, rank = _extract_top16(s, key_iota)
        rank_ref[hc, :, pl.ds(c0, LANE)] = rank
        v_ref[hc, :, pl.ds(c0, LANE)] = jnp.concatenate(vals, axis=0)
        return carry

    lax.fori_loop(0, N_HC * (tl // LANE), key_body, 0, unroll=4)

    sub = CAND_SUB
    n_mid = sub - 1
    n_cand = PEER_TOPK + n_mid * sub + (PEER_TOPK - sub)
    cand_iota = lax.broadcasted_iota(jnp.int32, (n_cand, LANE), 0).astype(f32)
    row8 = lax.broadcasted_iota(jnp.int32, (sub, LANE), 0)

    def head_body(i, carry):
        h = i // (tl // LANE)
        c0 = pl.multiple_of((i % (tl // LANE)) * LANE, LANE)
        cols = pl.ds(c0, LANE)
        v1 = v_ref[2 * h, :, cols]
        v2 = v_ref[2 * h + 1, :, cols]
        blocks = [v1[0:1, :] + v2]
        for r1 in range(1, sub):
            blocks.append(jnp.where(row8 < PEER_TOPK // (r1 + 1), v1[r1:r1 + 1, :] + v2[:sub, :], -jnp.inf))
        blocks.append(v1[sub:, :] + v2[0:1, :])
        best, crank = _extract_top16(jnp.concatenate(blocks, axis=0), cand_iota)
        z = jnp.zeros((1, LANE), f32)
        for k in range(PEER_TOPK):
            z = z + jnp.exp(best[k] - best[0])
        sel = jnp.where(crank < PEER_TOPK, 1.0, 0.0)
        rank1 = rank_ref[2 * h, :, cols]
        lim = jnp.zeros((N_KEYS, LANE), jnp.int32)
        for r1 in range(PEER_TOPK):
            if r1 == 0:
                cnt = jnp.sum(sel[:PEER_TOPK, :], axis=0, keepdims=True)
            elif r1 < sub:
                lo = PEER_TOPK + (r1 - 1) * sub
                cnt = jnp.sum(sel[lo:lo + sub, :], axis=0, keepdims=True)
            else:
                lo = PEER_TOPK + n_mid * sub + (r1 - sub)
                cnt = sel[lo:lo + 1, :]
            lim = jnp.where(rank1 == r1, cnt.astype(jnp.int32), lim)
        lim_ref[h, :, cols] = lim
        rk2_ref[h, :, cols] = rank_ref[2 * h + 1, :, cols].astype(f32).astype(jnp.bfloat16)
        e1_ref[h, :, cols] = jnp.exp(s_ref[2 * h, :, cols] - v1[0:1, :])
        e2_ref[h, :, cols] = (jnp.exp(s_ref[2 * h + 1, :, cols] - v2[0:1, :]) / z).astype(jnp.bfloat16)
        return carry

    lax.fori_loop(0, PEER_HEADS * (tl // LANE), head_body, 0, unroll=2)


def _gelu_tanh(x):
    c = math.sqrt(2.0 / math.pi)
    half = 0.5 * x
    return half + half * jnp.tanh(x * (c + (c * 0.044715) * (x * x)))


BF16_ROWS = 16


def _rows_bf16(row):
    tile = jnp.broadcast_to(row, (BF16_ROWS, LANE)).astype(jnp.bfloat16)
    return jnp.concatenate([tile] * (N_KEYS // BF16_ROWS), axis=0)


def _peer_expert_kernel(x_ref, h2_ref, *refs):
    pu_refs, pvt_refs = refs[:PEER_DMA_PARTS], refs[PEER_DMA_PARTS:2 * PEER_DMA_PARTS]
    lim_ref, rk2_ref, e1_ref, e2_ref, o_ref, acc_ref, a_ref, c_ref, rk2_s, e2_s = refs[2 * PEER_DMA_PARTS:]
    f32 = jnp.float32
    bf16 = jnp.bfloat16
    zero = jnp.zeros((N_KEYS, LANE), bf16)
    e_step = pl.program_id(1)
    tl = h2_ref.shape[0]
    part = pu_refs[0].shape[0]
    n_i1 = PEER_DMA_PARTS * part // N_KEYS
    chunk = 2 * LANE if tl % (2 * LANE) == 0 else LANE

    @pl.when(e_step == 0)
    def _():
        acc_ref[...] = jnp.zeros_like(acc_ref)
        rk2_s[...] = rk2_ref[...]
        e2_s[...] = e2_ref[...]

    for p, pu_ref in enumerate(pu_refs):
        a_ref[p * part:(p + 1) * part, :] = lax.dot_general(pu_ref[...], h2_ref[...], (((1,), (1,)), ((), ())),
                                                            preferred_element_type=f32)
    for c0 in range(0, tl, chunk):
        for l0 in range(c0, c0 + chunk, LANE):
            cols = slice(l0, l0 + LANE)
            for ii in range(n_i1):
                rows = slice(ii * N_KEYS, (ii + 1) * N_KEYS)
                w = jnp.zeros((N_KEYS, LANE), bf16)
                for h in range(PEER_HEADS):
                    sel = rk2_s[h, :, cols] < _rows_bf16(lim_ref[h, ii:ii + 1, cols].astype(f32))
                    w = w + jnp.where(sel, e2_s[h, :, cols], zero) * _rows_bf16(e1_ref[h, ii:ii + 1, cols])
                c_ref[rows, cols] = w * _gelu_tanh(a_ref[rows, cols]).astype(bf16)
        cc = slice(c0, c0 + chunk)
        proj = [jnp.dot(pvt_ref[0], c_ref[p * part:(p + 1) * part, cc], preferred_element_type=f32)
                for p, pvt_ref in enumerate(pvt_refs)]
        acc_ref[:, cc] += functools.reduce(lambda a, b: a + b, proj)

    @pl.when(e_step == pl.num_programs(1) - 1)
    def _():
        o_ref[...] = x_ref[...] + acc_ref[...].T


def _token_tile(T, sizes=(640, 512, 384, 256, 128)):
    for tl in sizes:
        if T % tl == 0:
            return tl
    raise ValueError(f"token count {T} is not a multiple of {LANE}")


def peer_residual(x, g, wqt, sk, pu, pvt, layer):
    T = x.shape[0]
    tl = _token_tile(T)
    nt = T // tl
    n_exp = pu.shape[1]
    head_shape = jax.ShapeDtypeStruct((PEER_HEADS, N_KEYS, T), jnp.int32)
    head_shape_f = jax.ShapeDtypeStruct((PEER_HEADS, N_KEYS, T), jnp.float32)
    head_shape_b = jax.ShapeDtypeStruct((PEER_HEADS, N_KEYS, T), jnp.bfloat16)
    head_spec = pl.BlockSpec((PEER_HEADS, N_KEYS, tl), lambda i: (0, 0, i))
    h2, lim, rk2, e1, e2 = pl.pallas_call(
        _peer_router_kernel,
        grid=(nt,),
        in_specs=[pl.BlockSpec((tl, D_MODEL), lambda i: (i, 0)),
                  pl.BlockSpec((1, D_MODEL), lambda i: (0, 0)),
                  pl.BlockSpec((PEER_HEADS * PEER_QDIM, D_MODEL), lambda i: (0, 0)),
                  pl.BlockSpec((N_HC, N_KEYS, HALF_Q), lambda i: (0, 0, 0))],
        out_specs=[pl.BlockSpec((tl, D_MODEL), lambda i: (i, 0)), head_spec, head_spec, head_spec, head_spec],
        out_shape=[jax.ShapeDtypeStruct((T, D_MODEL), jnp.bfloat16), head_shape, head_shape_b,
                   head_shape_f, head_shape_b],
        scratch_shapes=[pltpu.VMEM((PEER_HEADS * PEER_QDIM, tl), jnp.bfloat16),
                        pltpu.VMEM((N_HC, N_KEYS, tl), jnp.float32),
                        pltpu.VMEM((N_HC, N_KEYS, tl), jnp.int32),
                        pltpu.VMEM((N_HC, PEER_TOPK, tl), jnp.float32)],
        compiler_params=pltpu.CompilerParams(dimension_semantics=("parallel",), vmem_limit_bytes=VMEM_LIMIT),
        name="peer_router",
    )(x, g.reshape(1, D_MODEL), wqt, sk)

    tl = _token_tile(T, (1024, 768, 512, 640, 384, 256, 128))
    once = pl.Buffered(1)
    nt = T // tl
    te = PEER_EXPERT_TILE
    parts = PEER_DMA_PARTS
    part = te // parts
    i2_spec = pl.BlockSpec((PEER_HEADS, N_KEYS, tl), lambda i, e: (0, 0, i), pipeline_mode=once)
    i1_spec = pl.BlockSpec((PEER_HEADS, te // N_KEYS, tl), lambda i, e: (0, e, i))
    return pl.pallas_call(
        _peer_expert_kernel,
        grid=(nt, n_exp // te),
        in_specs=[pl.BlockSpec((tl, D_MODEL), lambda i, e: (i, 0), pipeline_mode=once),
                  pl.BlockSpec((tl, D_MODEL), lambda i, e: (i, 0), pipeline_mode=once),
                  *[pl.BlockSpec((None, part, D_MODEL), lambda i, e, p=p: (layer, parts * e + p, 0))
                    for p in range(parts)],
                  *[pl.BlockSpec((None, 1, D_MODEL, part), lambda i, e, p=p: (layer, parts * e + p, 0, 0))
                    for p in range(parts)],
                  i1_spec, i2_spec, i1_spec, i2_spec],
        out_specs=pl.BlockSpec((tl, D_MODEL), lambda i, e: (i, 0)),
        out_shape=jax.ShapeDtypeStruct((T, D_MODEL), jnp.float32),
        scratch_shapes=[pltpu.VMEM((D_MODEL, tl), jnp.float32),
                        pltpu.VMEM((te, tl), jnp.float32),
                        pltpu.VMEM((te, tl), jnp.bfloat16),
                        pltpu.VMEM((PEER_HEADS, N_KEYS, tl), jnp.bfloat16),
                        pltpu.VMEM((PEER_HEADS, N_KEYS, tl), jnp.bfloat16)],
        compiler_params=pltpu.CompilerParams(dimension_semantics=("parallel", "arbitrary"),
                                             vmem_limit_bytes=VMEM_LIMIT),
        name="peer_experts",
    )(x, h2, *([pu] * parts), *([pvt] * parts), lim, rk2, e1, e2)


C_QKVB = 3 * WIDTH_A
C_GB = C_QKVB + 3 * WIDTH_B
C_UC = C_GB + WIDTH_B
C_GATE = C_UC + WIDTH_C
C_BA = C_GATE + 3 * D_MODEL
D_IN2 = C_BA + LANE
S5_BLOCK = 256


def _reorder_w_in(w):
    cols = [w[:, OFF_QA:OFF_BETA], w[:, OFF_UC:OFF_GATE], w[:, OFF_GATE:], w[:, OFF_BETA:OFF_UC],
            jnp.zeros((w.shape[0], LANE - 2 * N_HEADS_B), w.dtype)]
    return jnp.concatenate(cols, axis=1).astype(jnp.bfloat16)


def trunk_layer(x, lp, layer_idx, att, prompt, k_past, v_past, conv_hist, delta0, ssm0_re, ssm0_im):
    f32 = jnp.float32
    B, Lx = x.shape[0], x.shape[1]
    T = B * Lx
    z = norm_matmul(x.reshape(T, D_MODEL), lp['norm1'], lp['w_in2'])

    qn, kn, knb, vb = qk_prep(z, lp['q_norm'], lp['k_norm'], _token_tile(T))
    lam_init = 0.8 - 0.6 * math.exp(-0.3 * layer_idx)
    lqk = lp['lambda_qk']
    lam = jnp.exp(jnp.sum(lqk[0] * lqk[1])) - jnp.exp(jnp.sum(lqk[2] * lqk[3])) + lam_init
    scalars = jnp.stack([lam, jnp.asarray(1.0 - lam_init, f32)]).astype(f32)
    r3 = lambda a: a.reshape(B, Lx, WIDTH_A)
    r4 = lambda a: a.reshape(N_HEADS_A, B, Lx, a.shape[-1])
    if prompt:
        oA = attn_prompt(r4(qn), r4(knb), r4(vb), scalars, att['far'], att['tiles'], lp['subln'])
    else:
        P = k_past.shape[1]
        oA = attn_sample(r4(qn), r4(knb), r4(vb), k_past.reshape(B, P, WIDTH_A), v_past.reshape(B, P, WIDTH_A),
                         scalars, att['bias_past'], att['bias_new'], lp['subln'])

    z3 = z.reshape(B, Lx, D_IN2)
    off = PAD if prompt else 0
    L = Lx - off
    kA = r3(kn)[:, off:].reshape(B, L, N_HEADS_A, 2 * HEAD_DIM_A)
    vA = z3[:, off:, 2 * WIDTH_A:3 * WIDTH_A].reshape(B, L, N_HEADS_A, 2 * HEAD_DIM_A)

    oB, delta, tail = gdn_mixer(z3, (C_QKVB // (3 * WIDTH_B), C_GB // WIDTH_B, C_BA // LANE), conv_hist, delta0,
                                lp['conv_w'], lp['a_log'], lp['dt_bias'], lp['gdn_norm'],
                                GDN_CHUNK if prompt else Lx, off)
    conv_state = tail[:, GDN_TAIL - (CONV_W - 1):]

    oC, ssm_re, ssm_im = s5_glu(z3, C_UC // WIDTH_C, ssm0_re, ssm0_im, lp['s5'], lp['w_glu'],
                                S5_BLOCK if prompt else Lx)

    x2 = merge_residual(x.reshape(T, D_MODEL), oA.reshape(T, WIDTH_A), oB.reshape(T, WIDTH_B),
                        oC.reshape(T, WIDTH_C), z, C_GATE // D_MODEL, lp['wb_a'], lp['wb_b'], lp['wb_c'],
                        lp['w_out'], Lx, off)

    return (x2, kA, vA, conv_state, delta, ssm_re, ssm_im)


def kernel(x_prompt, x_sample, cache_k, cache_v, state_conv, state_delta, state_ssm_re, state_ssm_im,
           meta_tokens, rel_bias, norm1_g, norm2_g, final_norm_g, w_in, q_norm_g, k_norm_g, lambda_qk,
           subln_g, conv_w, gdn_a_log, gdn_dt_bias, gdn_norm_g, s5_a_re, s5_a_im, s5_b_re, s5_b_im,
           s5_c_re, s5_c_im, s5_d, s5_log_dt, w_glu, w_branch_a, w_branch_b, w_branch_c, w_out,
           peer_wq, peer_subkeys, peer_u, peer_v):
    f32 = jnp.float32
    bf16 = jnp.bfloat16
    params = [dict(norm1=norm1_g[l], norm2=norm2_g[l], w_in2=_reorder_w_in(w_in[l]), q_norm=q_norm_g[l],
                   k_norm=k_norm_g[l], lambda_qk=lambda_qk[l], subln=subln_g[l], conv_w=conv_w[l],
                   a_log=gdn_a_log[l], dt_bias=gdn_dt_bias[l], gdn_norm=gdn_norm_g[l],
                   s5=s5_params(s5_a_re[l], s5_a_im[l], s5_b_re[l], s5_b_im[l], s5_c_re[l], s5_c_im[l], s5_d[l],
                                s5_log_dt[l]),
                   w_glu=w_glu[l], wb_a=w_branch_a[l], wb_b=w_branch_b[l], wb_c=w_branch_c[l], w_out=w_out[l],
                   peer_wqt=peer_wq[l].T.astype(bf16),
                   peer_sk=peer_subkeys[l].reshape(N_HC, N_KEYS, HALF_Q).astype(bf16))
              for l in range(DEPTH)]
    part = PEER_EXPERT_TILE // PEER_DMA_PARTS
    pu_all = peer_u.astype(bf16)
    pvt_all = peer_v.astype(bf16).reshape(peer_v.shape[0], -1, part, D_MODEL).transpose(0, 1, 3, 2)

    qpos = jnp.arange(ATT_TQ, dtype=jnp.int32)
    tiles = jnp.stack([t5_bias(qpos, d * ATT_TK + jnp.arange(ATT_TK, dtype=jnp.int32), rel_bias)
                       for d in (-2, -1, 0)], axis=1)
    far = t5_bias(jnp.full((1,), 2 * ATT_TK, jnp.int32), jnp.zeros((1,), jnp.int32), rel_bias).reshape(N_HEADS_A)
    P, Ls = cache_k.shape[2], x_sample.shape[1]
    bias_s = t5_bias(P + jnp.arange(Ls, dtype=jnp.int32), jnp.arange(P + Ls, dtype=jnp.int32), rel_bias)
    att_p = dict(tiles=tiles, far=far)
    att_s = dict(bias_past=bias_s[:, :, :P], bias_new=bias_s[:, :, P:])

    B = x_prompt.shape[0]
    xp = jnp.concatenate([jnp.zeros((B, PAD, D_MODEL), f32),
                          jnp.broadcast_to(meta_tokens[None], (B, N_META, D_MODEL)), x_prompt], axis=1)
    xs = x_sample
    outs_p = [[] for _ in range(6)]
    outs_s = [[] for _ in range(6)]
    for l in range(DEPTH):
        res_p = trunk_layer(
            xp, params[l], l, att_p, True, None, None,
            jnp.zeros((B, CONV_W - 1, 3 * WIDTH_B), f32),
            jnp.zeros((B, N_HEADS_B, HEAD_DIM_B, HEAD_DIM_B), f32),
            jnp.zeros((B, S5_GROUPS, S5_STATE), f32), jnp.zeros((B, S5_GROUPS, S5_STATE), f32))
        res_s = trunk_layer(
            xs, params[l], l, att_s, False, cache_k[l], cache_v[l], state_conv[l],
            state_delta[l], state_ssm_re[l], state_ssm_im[l])
        for acc, r in zip(outs_p, res_p[1:]):
            acc.append(r)
        for acc, r in zip(outs_s, res_s[1:]):
            acc.append(r)
        n_p = res_p[0].shape[0]
        x_all = peer_residual(jnp.concatenate([res_p[0], res_s[0]], axis=0), params[l]['norm2'],
                              params[l]['peer_wqt'], params[l]['peer_sk'], pu_all, pvt_all, l)
        xp = x_all[:n_p].reshape(xp.shape)
        xs = x_all[n_p:].reshape(xs.shape)
    y_prompt = rmsnorm(xp, final_norm_g)[:, FRONT:]
    y_sample = rmsnorm(xs, final_norm_g)

    kp, vp, cp, dp, srp, sip = [jnp.stack(a) for a in outs_p]
    ks_, vs_, cs_, ds_, srs, sis = [jnp.stack(a) for a in outs_s]
    return (y_prompt, y_sample, kp, vp, ks_, vs_, cp, cs_, dp, ds_, srp, sip, srs, sis)
```

```python
import functools
import math

import jax
import jax.numpy as jnp
from jax import lax
from jax.experimental import pallas as pl
from jax.experimental.pallas import tpu as pltpu

D_MODEL = 1024
DEPTH = 4
CHUNK = 64
N_META = 16
EPS = 1e-6
NEG_INF = -1e30
N_HEADS_A = 4
HEAD_DIM_A = 64
NUM_BUCKETS = 32
MAX_DISTANCE = 128
N_HEADS_B = 4
HEAD_DIM_B = 128
CONV_W = 4
S5_GROUP = 16
S5_GROUPS = 32
S5_STATE = 64
PEER_HEADS = 8
PEER_QDIM = 256
N_KEYS = 128
PEER_TOPK = 16

WIDTH_A = N_HEADS_A * 2 * HEAD_DIM_A
WIDTH_B = N_HEADS_B * HEAD_DIM_B
WIDTH_C = S5_GROUPS * S5_GROUP
OFF_QA = 0
OFF_KA = OFF_QA + WIDTH_A
OFF_VA = OFF_KA + WIDTH_A
OFF_QKVB = OFF_VA + WIDTH_A
OFF_GB = OFF_QKVB + 3 * WIDTH_B
OFF_BETA = OFF_GB + WIDTH_B
OFF_ALPHA = OFF_BETA + N_HEADS_B
OFF_UC = OFF_ALPHA + N_HEADS_B
OFF_GATE = OFF_UC + WIDTH_C
D_IN = OFF_GATE + 3 * D_MODEL

LANE = 128
VMEM_LIMIT = 56 * 1024 * 1024
WEIGHT_TILE_BYTES = 6 * 1024 * 1024
HIGHEST = lax.Precision.HIGHEST


def _norm_mm_kernel(x_ref, g_ref, w_ref, o_ref, h_ref):
    @pl.when(pl.program_id(1) == 0)
    def _():
        x = x_ref[...]
        h = x * lax.rsqrt(jnp.mean(x * x, axis=-1, keepdims=True) + EPS) * g_ref[...]
        h_ref[...] = h.astype(jnp.bfloat16)

    o_ref[...] = jnp.dot(h_ref[...], w_ref[...], preferred_element_type=jnp.float32)


def norm_matmul(x, g, wb, tm=512):
    M, K = x.shape
    N = wb.shape[1]
    n_lanes = N // LANE
    tn = LANE * max(d for d in range(1, n_lanes + 1) if n_lanes % d == 0 and d * LANE * K * 2 <= WEIGHT_TILE_BYTES)
    tm = min(tm, M)
    return pl.pallas_call(
        _norm_mm_kernel,
        grid=(pl.cdiv(M, tm), N // tn),
        in_specs=[pl.BlockSpec((tm, K), lambda i, j: (i, 0)),
                  pl.BlockSpec((1, K), lambda i, j: (0, 0)),
                  pl.BlockSpec((K, tn), lambda i, j: (0, j))],
        out_specs=pl.BlockSpec((tm, tn), lambda i, j: (i, j)),
        out_shape=jax.ShapeDtypeStruct((M, N), jnp.float32),
        scratch_shapes=[pltpu.VMEM((tm, K), jnp.bfloat16)],
        compiler_params=pltpu.CompilerParams(dimension_semantics=("parallel", "arbitrary"),
                                             vmem_limit_bytes=VMEM_LIMIT),
        name="norm_proj",
    )(x, g.reshape(1, K), wb)


def rmsnorm(x, g):
    xf = x.astype(jnp.float32)
    y = xf * lax.rsqrt(jnp.mean(xf * xf, axis=-1, keepdims=True) + EPS)
    return (y * g.astype(jnp.float32)).astype(x.dtype)


def t5_bias(q_pos, k_pos, table):
    rel = k_pos[None, :] - q_pos[:, None]
    half = NUM_BUCKETS // 2
    exact = half // 2
    n = jnp.abs(rel)
    nf = jnp.maximum(n, 1).astype(jnp.float32)
    far = exact + (jnp.log(nf / exact) / math.log(MAX_DISTANCE / exact) * (half - exact)).astype(jnp.int32)
    bucket = jnp.where(rel > 0, half, 0) + jnp.where(n < exact, n, jnp.minimum(far, half - 1))
    hit = bucket[None, :, :] == jnp.arange(NUM_BUCKETS, dtype=bucket.dtype)[:, None, None]
    tab = table.astype(jnp.float32)
    return jnp.stack([jnp.sum(jnp.where(hit, tab[:, h, None, None], 0.0), axis=0) for h in range(tab.shape[1])])


FRONT = 256
PAD = FRONT - N_META
ATT_TQ = 256
ATT_TK = 256


def _qk_prep_kernel(q_ref, k_ref, v_ref, gq_ref, gk_ref, seg_ref, qn_ref, kn_ref, knb_ref, vb_ref):
    f32 = jnp.float32
    bf16 = jnp.bfloat16
    seg = seg_ref[...]
    hd = 2 * HEAD_DIM_A

    def norm(x, g):
        ms = jnp.dot(x * x, seg, precision=HIGHEST, preferred_element_type=f32)
        return x * lax.rsqrt(ms + EPS) * g

    qn = (norm(q_ref[...], gq_ref[...]) * (HEAD_DIM_A ** -0.5)).astype(bf16)
    kn = norm(k_ref[...], gk_ref[...])
    kn_ref[...] = kn
    knb = kn.astype(bf16)
    v = v_ref[...].astype(bf16)
    lane = lax.broadcasted_iota(jnp.int32, (v.shape[0], hd), 1)
    ones_col = jnp.where(lane == 0, 1.0, 0.0).astype(bf16)
    for h in range(N_HEADS_A):
        cols = slice(h * hd, (h + 1) * hd)
        qn_ref[h] = qn[:, cols]
        knb_ref[h] = knb[:, cols]
        vb_ref[h] = jnp.concatenate([v[:, cols], ones_col], axis=1)


def qk_prep(z, gq, gk, tm):
    T = z.shape[0]
    hd = 2 * HEAD_DIM_A
    seg = jnp.kron(jnp.eye(WIDTH_A // HEAD_DIM_A, dtype=jnp.float32),
                   jnp.full((HEAD_DIM_A, HEAD_DIM_A), 1.0 / HEAD_DIM_A, jnp.float32))
    row = lambda j: pl.BlockSpec((tm, WIDTH_A), lambda i, j=j: (i, j))
    const = lambda shape: pl.BlockSpec(shape, lambda i: (0,) * len(shape))
    heads = lambda w: pl.BlockSpec((N_HEADS_A, tm, w), lambda i: (0, i, 0))
    hshape = lambda w: jax.ShapeDtypeStruct((N_HEADS_A, T, w), jnp.bfloat16)
    return pl.pallas_call(
        _qk_prep_kernel,
        grid=(T // tm,),
        in_specs=[row(0), row(1), row(2), const((1, WIDTH_A)), const((1, WIDTH_A)), const((WIDTH_A, WIDTH_A))],
        out_specs=[heads(hd), pl.BlockSpec((tm, WIDTH_A), lambda i: (i, 0)), heads(hd), heads(2 * hd)],
        out_shape=[hshape(hd), jax.ShapeDtypeStruct((T, WIDTH_A), jnp.float32), hshape(hd), hshape(2 * hd)],
        compiler_params=pltpu.CompilerParams(dimension_semantics=("parallel",)),
        name="qk_prep",
    )(z, z, z, jnp.tile(gq, WIDTH_A // HEAD_DIM_A).reshape(1, WIDTH_A),
      jnp.tile(gk, WIDTH_A // HEAD_DIM_A).reshape(1, WIDTH_A), seg)


def _split_maps(q):
    lane = lax.broadcasted_iota(jnp.int32, q.shape, 1)
    zero = jnp.zeros_like(q)
    return jnp.where(lane < HEAD_DIM_A, q, zero), jnp.where(lane >= HEAD_DIM_A, q, zero)


def _subln(o, g, scale):
    return o * lax.rsqrt(jnp.mean(o * o, axis=-1, keepdims=True) + EPS) * g * scale


def _attn_prompt_kernel(sc_ref, far_ref, q_ref, k_ref, v_ref, bias_ref, g_ref, o_ref, m_ref, acc_ref,
                        sa_ref, sb_ref):
    f32 = jnp.float32
    h = pl.program_id(1)
    qi = pl.program_id(2)
    tq, tk = ATT_TQ, ATT_TK
    hd = 2 * HEAD_DIM_A
    qs = jnp.concatenate(_split_maps(q_ref[0, 0]), axis=0)
    m_ref[...] = jnp.full(m_ref.shape, NEG_INF, f32)
    acc_ref[...] = jnp.zeros(acc_ref.shape, f32)
    far = far_ref[h]

    def logits(k0, width, general):
        k0 = pl.multiple_of(k0, tk)
        kt = k_ref[0, 0, pl.ds(k0, width), :]
        s = lax.dot_general(qs, kt, (((1,), (1,)), ((), ())), preferred_element_type=f32)
        if general:
            kj = k0 // tk
            bias = jnp.concatenate([bias_ref[0, jnp.clip(kj + j - qi + 2, 0, 2)] for j in range(width // tk)], axis=1)
            qpos = qi * tq + lax.broadcasted_iota(jnp.int32, (tq, width), 0)
            kpos = k0 + lax.broadcasted_iota(jnp.int32, (tq, width), 1)
            qchunk = jnp.where(qpos < FRONT, 0, 1 + jnp.right_shift(qpos - FRONT, 6))
            kchunk = jnp.where(kpos < FRONT, 0, 1 + jnp.right_shift(kpos - FRONT, 6))
            mask = (kpos >= PAD) & (kchunk <= qchunk)
            s = jnp.where(jnp.concatenate([mask, mask], axis=0), s + jnp.concatenate([bias, bias], axis=0), NEG_INF)
        else:
            s = s + far
        return s

    def accumulate(s, k0):
        width = s.shape[1]
        vt = v_ref[0, 0, pl.ds(pl.multiple_of(k0, tk), width), :]
        n_lane_tiles = width // LANE
        smax = s[:, :LANE]
        for j in range(1, n_lane_tiles):
            smax = jnp.maximum(smax, s[:, j * LANE:(j + 1) * LANE])
        m_old = m_ref[...]
        m_new = jnp.maximum(m_old, jnp.broadcast_to(jnp.max(smax, axis=1, keepdims=True), m_old.shape))
        alpha = jnp.exp(m_old - m_new)
        p = jnp.exp(s - jnp.concatenate([m_new] * n_lane_tiles, axis=1))
        acc_ref[...] = (jnp.concatenate([alpha, alpha], axis=1) * acc_ref[...]
                        + jnp.dot(p.astype(jnp.bfloat16), vt, preferred_element_type=f32))
        m_ref[...] = m_new

    def tile(k0, width, general):
        accumulate(logits(k0, width, general), k0)

    tile(0, tk, True)

    wide = 2 * tk
    n_far = jnp.maximum(qi - 2, 0)
    n_single = n_far % 4

    @pl.when(n_single >= 2)
    def _():
        tile(tk, wide, False)

    @pl.when(n_single % 2 == 1)
    def _():
        tile((1 + 2 * (n_single // 2)) * tk, tk, False)

    base = (1 + n_single) * tk
    n_pairs = n_far // 4

    @pl.when(n_pairs > 0)
    def _():
        sa_ref[...] = logits(base, wide, False)

    def pair_body(j, carry):
        k0 = base + 2 * j * wide
        sb_ref[...] = logits(k0 + wide, wide, False)
        accumulate(sa_ref[...], k0)
        nxt = jnp.minimum(k0 + 2 * wide, base + (2 * n_pairs - 1) * wide)
        sa_ref[...] = logits(nxt, wide, False)
        accumulate(sb_ref[...], k0 + wide)
        return carry

    lax.fori_loop(0, n_pairs, pair_body, 0)

    @pl.when(qi >= 2)
    def _():
        tile((qi - 1) * tk, wide, True)

    @pl.when(qi == 1)
    def _():
        tile(tk, tk, True)

    acc = acc_ref[...]
    out = acc[:, :hd] / acc[:, hd:hd + 1]
    o = out[:tq] - sc_ref[0] * out[tq:]
    o_ref[0] = _subln(o, g_ref[...], sc_ref[1])


def attn_prompt(qn, knb, vb, scalars, far, bias_tiles, subln_g):
    _, B, Lp, _ = qn.shape
    hd = 2 * HEAD_DIM_A
    smem = pl.BlockSpec(memory_space=pltpu.SMEM)
    return pl.pallas_call(
        _attn_prompt_kernel,
        grid=(B, N_HEADS_A, Lp // ATT_TQ),
        in_specs=[smem, smem,
                  pl.BlockSpec((1, 1, ATT_TQ, hd), lambda b, h, i: (h, b, i, 0)),
                  pl.BlockSpec((1, 1, Lp, hd), lambda b, h, i: (h, b, 0, 0)),
                  pl.BlockSpec((1, 1, Lp, 2 * hd), lambda b, h, i: (h, b, 0, 0)),
                  pl.BlockSpec((1, 3, ATT_TQ, ATT_TK), lambda b, h, i: (h, 0, 0, 0)),
                  pl.BlockSpec((1, hd), lambda b, h, i: (0, 0))],
        out_specs=pl.BlockSpec((1, ATT_TQ, hd), lambda b, h, i: (b, i, h)),
        out_shape=jax.ShapeDtypeStruct((B, Lp, WIDTH_A), jnp.float32),
        scratch_shapes=[pltpu.VMEM((2 * ATT_TQ, hd), jnp.float32), pltpu.VMEM((2 * ATT_TQ, 2 * hd), jnp.float32),
                        pltpu.VMEM((2 * ATT_TQ, 2 * ATT_TK), jnp.float32),
                        pltpu.VMEM((2 * ATT_TQ, 2 * ATT_TK), jnp.float32)],
        compiler_params=pltpu.CompilerParams(dimension_semantics=("parallel", "parallel", "parallel"),
                                             vmem_limit_bytes=VMEM_LIMIT),
        name="diff_attn_prompt",
    )(scalars, far, qn, knb, vb, bias_tiles, subln_g.reshape(1, hd))


def _attn_sample_kernel(sc_ref, q_ref, kp_ref, vp_ref, kn_ref, vn_ref, bp_ref, bn_ref, g_ref, o_ref):
    f32 = jnp.float32
    bf16 = jnp.bfloat16
    qs = _split_maps(q_ref[0, 0])
    kp = kp_ref[0].astype(bf16)
    vp = vp_ref[0].astype(bf16)
    kn = kn_ref[0, 0]
    vn = vn_ref[0, 0][:, :2 * HEAD_DIM_A]
    dn = (((1,), (1,)), ((), ()))
    outs = []
    for c in range(2):
        sp = lax.dot_general(qs[c], kp, dn, preferred_element_type=f32) + bp_ref[0]
        sn = lax.dot_general(qs[c], kn, dn, preferred_element_type=f32) + bn_ref[0]
        m = jnp.maximum(jnp.max(sp, axis=1, keepdims=True), jnp.max(sn, axis=1, keepdims=True))
        pp = jnp.exp(sp - m)
        pn = jnp.exp(sn - m)
        l = jnp.sum(pp, axis=1, keepdims=True) + jnp.sum(pn, axis=1, keepdims=True)
        acc = (jnp.dot(pp.astype(bf16), vp, preferred_element_type=f32)
               + jnp.dot(pn.astype(bf16), vn, preferred_element_type=f32))
        outs.append(acc / l)
    o_ref[0] = _subln(outs[0] - sc_ref[0] * outs[1], g_ref[...], sc_ref[1])


def attn_sample(qn, knb, vb, k_past, v_past, scalars, bias_past, bias_new, subln_g):
    _, B, L, _ = qn.shape
    P = k_past.shape[1]
    hd = 2 * HEAD_DIM_A
    new = pl.BlockSpec((1, L, hd), lambda b, h: (b, 0, h))
    new_h = lambda w: pl.BlockSpec((1, 1, L, w), lambda b, h: (h, b, 0, 0))
    past = pl.BlockSpec((1, P, hd), lambda b, h: (b, 0, h))
    return pl.pallas_call(
        _attn_sample_kernel,
        grid=(B, N_HEADS_A),
        in_specs=[pl.BlockSpec(memory_space=pltpu.SMEM), new_h(hd), past, past, new_h(hd), new_h(2 * hd),
                  pl.BlockSpec((1, L, P), lambda b, h: (h, 0, 0)),
                  pl.BlockSpec((1, L, L), lambda b, h: (h, 0, 0)),
                  pl.BlockSpec((1, hd), lambda b, h: (0, 0))],
        out_specs=new,
        out_shape=jax.ShapeDtypeStruct((B, L, WIDTH_A), jnp.float32),
        compiler_params=pltpu.CompilerParams(dimension_semantics=("parallel", "parallel")),
        name="diff_attn_sample",
    )(scalars, qn, k_past, v_past, knb, vb, bias_past, bias_new, subln_g.reshape(1, hd))


S5_N = S5_GROUPS * S5_STATE
S5_SLAB = 8
S5_LANES = 512
S5_SPLIT = 2


def _sigmoid(x):
    return 1.0 / (1.0 + jnp.exp(-x))


def _s5_kernel(u_ref, x0_ref, bmat_ref, cmat_ref, lamp_ref, ppow_ref, d_ref, wglu_ref, oc_ref, xf_ref,
               bu_ref, carry_ref):
    f32 = jnp.float32
    bf16 = jnp.bfloat16
    t = pl.program_id(1)
    tb = u_ref.shape[1]

    @pl.when(t == 0)
    def _():
        carry_ref[...] = x0_ref[0]

    u = u_ref[0]
    ub = u.astype(bf16)
    ch, st = WIDTH_C // S5_SPLIT, S5_N // S5_SPLIT
    for s in range(S5_SPLIT):
        for part in range(2):
            cols = slice(part * S5_N + s * st, part * S5_N + (s + 1) * st)
            bu_ref[:, cols] = jnp.dot(ub[:, s * ch:(s + 1) * ch], bmat_ref[s * ch:(s + 1) * ch, cols],
                                      preferred_element_type=f32)

    row = lax.broadcasted_iota(jnp.int32, (S5_SLAB, S5_LANES), 0)
    for c in range(S5_N // S5_LANES):
        re = slice(c * S5_LANES, (c + 1) * S5_LANES)
        im = slice(S5_N + c * S5_LANES, S5_N + (c + 1) * S5_LANES)

        def slab(i, carry):
            cre, cim = carry
            rows = pl.ds(pl.multiple_of(i * S5_SLAB, S5_SLAB), S5_SLAB)
            yre = bu_ref[rows, re]
            yim = bu_ref[rows, im]
            for s in range(3):
                sh = 1 << s
                sre = jnp.where(row >= sh, pltpu.roll(yre, sh, 0), 0.0)
                sim = jnp.where(row >= sh, pltpu.roll(yim, sh, 0), 0.0)
                lr = lamp_ref[s, 0, :, re]
                li = lamp_ref[s, 1, :, re]
                yre, yim = yre + (lr * sre - li * sim), yim + (lr * sim + li * sre)
            pr = ppow_ref[0, :, re]
            pi = ppow_ref[1, :, re]
            yre, yim = yre + (pr * cre - pi * cim), yim + (pr * cim + pi * cre)
            bu_ref[rows, re] = yre
            bu_ref[rows, im] = yim
            last = S5_SLAB - 1
            return (jnp.broadcast_to(yre[last:last + 1, :], yre.shape),
                    jnp.broadcast_to(yim[last:last + 1, :], yim.shape))

        cre, cim = lax.fori_loop(0, tb // S5_SLAB, slab, (carry_ref[:, re], carry_ref[:, im]))
        carry_ref[:, re] = cre
        carry_ref[:, im] = cim

    ys = []
    for s in range(S5_SPLIT):
        out = slice(s * ch, (s + 1) * ch)
        acc = None
        for part in range(2):
            cols = slice(part * S5_N + s * st, part * S5_N + (s + 1) * st)
            term = jnp.dot(bu_ref[:, cols].astype(bf16), cmat_ref[cols, out], preferred_element_type=f32)
            acc = term if acc is None else acc + term
        ys.append(acc)
    y = jnp.concatenate(ys, axis=1) + d_ref[...] * u
    gl = jnp.dot(_gelu_tanh(y).astype(bf16), wglu_ref[...], preferred_element_type=f32)
    oc_ref[0] = gl[:, :WIDTH_C] * _sigmoid(gl[:, WIDTH_C:])

    @pl.when(t == pl.num_programs(1) - 1)
    def _():
        xf_ref[0] = carry_ref[...]


def s5_params(a_re, a_im, b_re, b_im, c_re, c_im, d, log_dt):
    f32 = jnp.float32
    lam = lax.complex(a_re, a_im)
    lam_bar = jnp.exp(lam * jnp.exp(log_dt)[:, None])
    b_bar = ((lam_bar - 1.0) / lam)[..., None] * lax.complex(b_re, b_im)
    eye = jnp.eye(S5_GROUPS, dtype=f32)
    bd_in = lambda m: jnp.einsum('gpi,gh->gihp', m, eye).reshape(WIDTH_C, S5_N)
    bd_out = lambda m: jnp.einsum('gip,gh->gphi', m, eye).reshape(S5_N, WIDTH_C)
    bmat = jnp.concatenate([bd_in(b_bar.real), bd_in(b_bar.imag)], axis=1).astype(jnp.bfloat16)
    cmat = jnp.concatenate([bd_out(c_re), bd_out(-c_im)], axis=0).astype(jnp.bfloat16)
    lb = lam_bar.reshape(S5_N)
    rep = lambda v: jnp.broadcast_to(v[None, :], (S5_SLAB, S5_N))
    pows = [lb, lb * lb, (lb * lb) * (lb * lb)]
    lamp = jnp.stack([jnp.stack([rep(p.real), rep(p.imag)]) for p in pows])
    run = [lb]
    for _ in range(S5_SLAB - 1):
        run.append(run[-1] * lb)
    pp = jnp.stack(run)
    ppow = jnp.stack([pp.real, pp.imag])
    return dict(bmat=bmat, cmat=cmat, lamp=lamp.astype(f32), ppow=ppow.astype(f32), d=d.reshape(1, WIDTH_C))


def s5_glu(z3, col_block, x0_re, x0_im, sp, w_glu, tb):
    B, Lx, _ = z3.shape
    x0 = jnp.concatenate([x0_re.reshape(B, S5_N), x0_im.reshape(B, S5_N)], axis=1)
    x0 = jnp.broadcast_to(x0[:, None, :], (B, S5_SLAB, 2 * S5_N))
    const = lambda a: pl.BlockSpec(a.shape, lambda b, t: (0,) * a.ndim)
    wg = w_glu.astype(jnp.bfloat16)
    oc, xf = pl.pallas_call(
        _s5_kernel,
        grid=(B, Lx // tb),
        in_specs=[pl.BlockSpec((1, tb, WIDTH_C), lambda b, t: (b, t, col_block)),
                  pl.BlockSpec((1, S5_SLAB, 2 * S5_N), lambda b, t: (b, 0, 0)),
                  const(sp['bmat']), const(sp['cmat']), const(sp['lamp']), const(sp['ppow']), const(sp['d']),
                  const(wg)],
        out_specs=[pl.BlockSpec((1, tb, WIDTH_C), lambda b, t: (b, t, 0)),
                   pl.BlockSpec((1, S5_SLAB, 2 * S5_N), lambda b, t: (b, 0, 0))],
        out_shape=[jax.ShapeDtypeStruct((B, Lx, WIDTH_C), jnp.float32),
                   jax.ShapeDtypeStruct((B, S5_SLAB, 2 * S5_N), jnp.float32)],
        scratch_shapes=[pltpu.VMEM((tb, 2 * S5_N), jnp.float32), pltpu.VMEM((S5_SLAB, 2 * S5_N), jnp.float32)],
        compiler_params=pltpu.CompilerParams(dimension_semantics=("parallel", "arbitrary"),
                                             vmem_limit_bytes=VMEM_LIMIT),
        name="s5_glu",
    )(z3, x0, sp['bmat'], sp['cmat'], sp['lamp'], sp['ppow'], sp['d'], wg)
    shape = (B, S5_GROUPS, S5_STATE)
    return oc, xf[:, 0, :S5_N].reshape(shape), xf[:, 0, S5_N:].reshape(shape)


def _gdn_kernel(qkv_ref, gate_ref, ba_ref, hist_ref, s0_ref, cw_ref, av_ref, dt_ref, gn_ref,
                ob_ref, sout_ref, tail_ref, s_ref, prev_ref, *, pad_rows):
    f32 = jnp.float32
    t = pl.program_id(1)
    C = qkv_ref.shape[1]
    hd = HEAD_DIM_B
    dn_last = (((1,), (1,)), ((), ()))
    dn_first = (((0,), (0,)), ((), ()))
    dot = lambda a, b: _dot3(a, b, (((1,), (0,)), ((), ())))

    @pl.when(t == 0)
    def _():
        s_ref[...] = s0_ref[0]
        prev_ref[...] = hist_ref[0]

    cur = qkv_ref[0]
    tail = prev_ref.shape[0]
    ext = jnp.concatenate([prev_ref[...], cur], axis=0)
    w = cw_ref[...]
    conv = cur * w[CONV_W - 1:CONV_W, :]
    for i in range(CONV_W - 1):
        lo = tail - (CONV_W - 1) + i
        conv = conv + ext[lo:lo + C, :] * w[i:i + 1, :]
    prev_ref[...] = cur[C - tail:, :]
    c = conv * _sigmoid(conv)

    ba = ba_ref[0]
    xg = ba + dt_ref[...]
    softplus = jnp.maximum(xg, 0.0) + jnp.log(1.0 + jnp.exp(-jnp.abs(xg)))
    g_all = -jnp.exp(av_ref[...]) * softplus
    if pad_rows:
        grow = t * C + lax.broadcasted_iota(jnp.int32, g_all.shape, 0)
        g_all = jnp.where(grow >= pad_rows, g_all, 0.0)
    ri = lax.broadcasted_iota(jnp.int32, (C, C), 0)
    ci = lax.broadcasted_iota(jnp.int32, (C, C), 1)
    tri = ri >= ci
    strict = ri > ci
    tril = jnp.where(tri, 1.0, 0.0).astype(f32)
    G_col = jnp.dot(tril, g_all, precision=HIGHEST, preferred_element_type=f32)
    G_row = lax.dot_general(g_all, tril, (((0,), (1,)), ((), ())), precision=HIGHEST,
                            preferred_element_type=f32)
    eye = jnp.where(ri == ci, 1.0, 0.0).astype(f32)

    heads = range(N_HEADS_B)
    dot_t = lambda a, b: _dot3(a, b, dn_last)
    Gc = [G_col[:, N_HEADS_B + h:N_HEADS_B + h + 1] for h in heads]
    Gr = [G_row[N_HEADS_B + h:N_HEADS_B + h + 1, :] for h in heads]
    decay = [jnp.where(tri, jnp.exp(jnp.minimum(Gc[h] - Gr[h], 0.0)), 0.0) for h in heads]
    beta = [_sigmoid(ba[:, h:h + 1]) for h in heads]
    unit = lambda a: a * lax.rsqrt(jnp.sum(a * a, axis=-1, keepdims=True) + EPS)
    q = [unit(c[:, h * hd:(h + 1) * hd]) * (hd ** -0.5) for h in heads]
    k = [unit(c[:, WIDTH_B + h * hd:WIDTH_B + (h + 1) * hd]) for h in heads]
    kb = [k[h] * beta[h] for h in heads]
    eG = [jnp.exp(Gc[h]) for h in heads]
    rhs = [jnp.concatenate([c[:, 2 * WIDTH_B + h * hd:2 * WIDTH_B + (h + 1) * hd] * beta[h], kb[h] * eG[h]], axis=1)
           for h in heads]
    A = [jnp.where(strict, dot_t(kb[h], k[h]) * decay[h], 0.0) for h in heads]
    same_block = lambda b: jnp.right_shift(ri, b.bit_length() - 1) == jnp.right_shift(ci, b.bit_length() - 1)
    in_base = same_block(GDN_BASE)
    D = [jnp.where(in_base, A[h], 0.0) for h in heads]
    D2 = [dot(D[h], D[h]) for h in heads]
    D4 = [dot(D2[h], D2[h]) for h in heads]
    tm = [eye - D[h] for h in heads]
    tm = [tm[h] + dot(tm[h], D2[h]) for h in heads]
    tm = [tm[h] + dot(tm[h], D4[h]) for h in heads]
    b = GDN_BASE
    while b < C:
        level = same_block(2 * b) & jnp.logical_not(same_block(b))
        me = [dot(tm[h], jnp.where(level, A[h], 0.0)) for h in heads]
        tm = [tm[h] - dot(me[h], tm[h]) for h in heads]
        b *= 2
    sol = [dot(tm[h], rhs[h]) for h in heads]
    attn = [dot_t(q[h], k[h]) * decay[h] for h in heads]
    S = [s_ref[h] for h in heads]
    v_new = [sol[h][:, :hd] - dot(sol[h][:, hd:], S[h]) for h in heads]
    o = [dot(q[h] * eG[h], S[h]) + dot(attn[h], v_new[h]) for h in heads]
    for h in heads:
        GL = Gc[h][C - 1:C, :]
        s_ref[h] = S[h] * jnp.exp(GL) + _dot3(k[h] * jnp.exp(GL - Gc[h]), v_new[h], dn_first)
        gate = gate_ref[0, :, h * hd:(h + 1) * hd]
        on = o[h] * lax.rsqrt(jnp.mean(o[h] * o[h], axis=-1, keepdims=True) + EPS) * gn_ref[...]
        ob_ref[0, :, h * hd:(h + 1) * hd] = on * (gate * _sigmoid(gate))

    @pl.when(t == pl.num_programs(1) - 1)
    def _():
        sout_ref[0] = s_ref[...]
        tail_ref[0] = cur[C - tail:, :]


GDN_TAIL = 8
GDN_CHUNK = 128
GDN_BASE = 8


def _dot3(a, b, dims):
    f32 = jnp.float32
    bf16 = jnp.bfloat16
    a_hi = a.astype(bf16)
    b_hi = b.astype(bf16)
    a_lo = (a - a_hi.astype(f32)).astype(bf16)
    b_lo = (b - b_hi.astype(f32)).astype(bf16)
    dg = lambda x, y: lax.dot_general(x, y, dims, preferred_element_type=f32)
    return dg(a_hi, b_hi) + (dg(a_hi, b_lo) + dg(a_lo, b_hi))


def gdn_mixer(z3, blocks, conv_hist, delta0, conv_w, a_log, dt_bias, gdn_norm, chunk, pad_rows):
    B, Lx, _ = z3.shape
    f32 = jnp.float32
    wq = 3 * WIDTH_B
    hist = jnp.pad(conv_hist, ((0, 0), (GDN_TAIL - (CONV_W - 1), 0), (0, 0)))
    lane_vec = lambda v: jnp.zeros((1, LANE), f32).at[0, N_HEADS_B:2 * N_HEADS_B].set(v)
    const = lambda shape: pl.BlockSpec(shape, lambda b, t: (0,) * len(shape))
    per_b = lambda shape: pl.BlockSpec((1,) + shape, lambda b, t: (b,) + (0,) * len(shape))
    qb, gb, bb = blocks
    return pl.pallas_call(
        functools.partial(_gdn_kernel, pad_rows=pad_rows),
        grid=(B, Lx // chunk),
        in_specs=[pl.BlockSpec((1, chunk, wq), lambda b, t: (b, t, qb)),
                  pl.BlockSpec((1, chunk, WIDTH_B), lambda b, t: (b, t, gb)),
                  pl.BlockSpec((1, chunk, LANE), lambda b, t: (b, t, bb)),
                  per_b((GDN_TAIL, wq)), per_b((N_HEADS_B, HEAD_DIM_B, HEAD_DIM_B)),
                  const((CONV_W, wq)), const((1, LANE)), const((1, LANE)), const((1, HEAD_DIM_B))],
        out_specs=[pl.BlockSpec((1, chunk, WIDTH_B), lambda b, t: (b, t, 0)),
                   per_b((N_HEADS_B, HEAD_DIM_B, HEAD_DIM_B)), per_b((GDN_TAIL, wq))],
        out_shape=[jax.ShapeDtypeStruct((B, Lx, WIDTH_B), f32),
                   jax.ShapeDtypeStruct((B, N_HEADS_B, HEAD_DIM_B, HEAD_DIM_B), f32),
                   jax.ShapeDtypeStruct((B, GDN_TAIL, wq), f32)],
        scratch_shapes=[pltpu.VMEM((N_HEADS_B, HEAD_DIM_B, HEAD_DIM_B), f32), pltpu.VMEM((GDN_TAIL, wq), f32)],
        compiler_params=pltpu.CompilerParams(dimension_semantics=("parallel", "arbitrary")),
        name="gdn_mixer",
    )(z3, z3, z3, hist, delta0, conv_w, lane_vec(a_log), lane_vec(dt_bias), gdn_norm.reshape(1, HEAD_DIM_B))


def _merge_kernel(x_ref, oa_ref, ob_ref, oc_ref, g0_ref, g1_ref, g2_ref, wa_ref, wb_ref, wc_ref, wo_ref, o_ref,
                  *, rows_per_seq, pad_rows):
    f32 = jnp.float32
    bf16 = jnp.bfloat16
    mm = lambda a, w_ref: jnp.dot(a.astype(bf16), w_ref[...], preferred_element_type=f32)
    merged = (_sigmoid(g0_ref[...]) * mm(oa_ref[...], wa_ref) + _sigmoid(g1_ref[...]) * mm(ob_ref[...], wb_ref)
              + _sigmoid(g2_ref[...]) * mm(oc_ref[...], wc_ref))
    x = x_ref[...] + mm(merged, wo_ref)
    if pad_rows:
        tm = x.shape[0]
        row = pl.program_id(0) * tm + lax.broadcasted_iota(jnp.int32, x.shape, 0)
        x = jnp.where(row % rows_per_seq >= pad_rows, x, 0.0)
    o_ref[...] = x


def merge_residual(x, oA, oB, oC, z, gate_block, wa, wb, wc, wo, rows_per_seq, pad_rows):
    T = x.shape[0]
    tm = _token_tile(T)
    bf16 = jnp.bfloat16
    rows = lambda w, j=0: pl.BlockSpec((tm, w), lambda i, j=j: (i, j))
    const = lambda a: pl.BlockSpec(a.shape, lambda i: (0, 0))
    ws = [w.astype(bf16) for w in (wa, wb, wc, wo)]
    return pl.pallas_call(
        functools.partial(_merge_kernel, rows_per_seq=rows_per_seq, pad_rows=pad_rows),
        grid=(T // tm,),
        in_specs=[rows(D_MODEL), rows(WIDTH_A), rows(WIDTH_B), rows(WIDTH_C),
                  rows(D_MODEL, gate_block), rows(D_MODEL, gate_block + 1), rows(D_MODEL, gate_block + 2)]
                 + [const(w) for w in ws],
        out_specs=rows(D_MODEL),
        out_shape=jax.ShapeDtypeStruct((T, D_MODEL), jnp.float32),
        compiler_params=pltpu.CompilerParams(dimension_semantics=("parallel",), vmem_limit_bytes=VMEM_LIMIT),
        name="merge_residual",
    )(x, oA, oB, oC, z, z, z, *ws)


N_HC = 2 * PEER_HEADS
HALF_Q = PEER_QDIM // 2
PEER_EXPERT_TILE = 8 * N_KEYS
CAND_SUB = 8
PEER_DMA_PARTS = 2


def _extract_top16(s, iota_f):
    n = float(s.shape[0])
    rank = jnp.full(s.shape, PEER_TOPK, jnp.int32)
    vals = []
    for j in range(PEER_TOPK):
        m = jnp.max(s, axis=0, keepdims=True)
        idx = jnp.min(jnp.where(s == m, iota_f, n), axis=0, keepdims=True)
        hit = iota_f == idx
        rank = jnp.where(hit, j, rank)
        s = jnp.where(hit, -jnp.inf, s)
        vals.append(m)
    return vals, rank


def _peer_router_kernel(x_ref, g_ref, wqt_ref, sk_ref, h2_ref, lim_ref, rk2_ref, e1_ref, e2_ref,
                        qt_ref, s_ref, rank_ref, v_ref):
    f32 = jnp.float32
    tl = x_ref.shape[0]
    x = x_ref[...]
    h2 = x * lax.rsqrt(jnp.mean(x * x, axis=-1, keepdims=True) + EPS) * g_ref[...]
    h2b = h2.astype(jnp.bfloat16)
    h2_ref[...] = h2b
    qt_ref[...] = lax.dot_general(wqt_ref[...], h2b, (((1,), (1,)), ((), ())),
                                  preferred_element_type=f32).astype(jnp.bfloat16)

    def score_body(hc, carry):
        r0 = pl.multiple_of(hc * HALF_Q, HALF_Q)
        s_ref[hc] = jnp.dot(sk_ref[hc], qt_ref[pl.ds(r0, HALF_Q), :], preferred_element_type=f32)
        return carry

    lax.fori_loop(0, N_HC, score_body, 0, unroll=4)

    key_iota = lax.broadcasted_iota(jnp.int32, (N_KEYS, LANE), 0).astype(f32)

    def key_body(i, carry):
        hc = i // (tl // LANE)
        c0 = pl.multiple_of((i % (tl // LANE)) * LANE, LANE)
        s = s_ref[hc, :, pl.ds(c0, LANE)]
        vals, rank = _extract_top16(s, key_iota)
        rank_ref[hc, :, pl.ds(c0, LANE)] = rank
        v_ref[hc, :, pl.ds(c0, LANE)] = jnp.concatenate(vals, axis=0)
        return carry

    lax.fori_loop(0, N_HC * (tl // LANE), key_body, 0, unroll=4)

    sub = CAND_SUB
    n_mid = sub - 1
    n_cand = PEER_TOPK + n_mid * sub + (PEER_TOPK - sub)
    cand_iota = lax.broadcasted_iota(jnp.int32, (n_cand, LANE), 0).astype(f32)
    row8 = lax.broadcasted_iota(jnp.int32, (sub, LANE), 0)

    def head_body(i, carry):
        h = i // (tl // LANE)
        c0 = pl.multiple_of((i % (tl // LANE)) * LANE, LANE)
        cols = pl.ds(c0, LANE)
        v1 = v_ref[2 * h, :, cols]
        v2 = v_ref[2 * h + 1, :, cols]
        blocks = [v1[0:1, :] + v2]
        for r1 in range(1, sub):
            blocks.append(jnp.where(row8 < PEER_TOPK // (r1 + 1), v1[r1:r1 + 1, :] + v2[:sub, :], -jnp.inf))
        blocks.append(v1[sub:, :] + v2[0:1, :])
        best, crank = _extract_top16(jnp.concatenate(blocks, axis=0), cand_iota)
        z = jnp.zeros((1, LANE), f32)
        for k in range(PEER_TOPK):
            z = z + jnp.exp(best[k] - best[0])
        sel = jnp.where(crank < PEER_TOPK, 1.0, 0.0)
        rank1 = rank_ref[2 * h, :, cols]
        lim = jnp.zeros((N_KEYS, LANE), jnp.int32)
        for r1 in range(PEER_TOPK):
            if r1 == 0:
                cnt = jnp.sum(sel[:PEER_TOPK, :], axis=0, keepdims=True)
            elif r1 < sub:
                lo = PEER_TOPK + (r1 - 1) * sub
                cnt = jnp.sum(sel[lo:lo + sub, :], axis=0, keepdims=True)
            else:
                lo = PEER_TOPK + n_mid * sub + (r1 - sub)
                cnt = sel[lo:lo + 1, :]
            lim = jnp.where(rank1 == r1, cnt.astype(jnp.int32), lim)
        lim_ref[h, :, cols] = lim
        rk2_ref[h, :, cols] = rank_ref[2 * h + 1, :, cols].astype(f32).astype(jnp.bfloat16)
        e1_ref[h, :, cols] = jnp.exp(s_ref[2 * h, :, cols] - v1[0:1, :])
        e2_ref[h, :, cols] = (jnp.exp(s_ref[2 * h + 1, :, cols] - v2[0:1, :]) / z).astype(jnp.bfloat16)
        return carry

    lax.fori_loop(0, PEER_HEADS * (tl // LANE), head_body, 0, unroll=2)


def _gelu_tanh(x):
    c = math.sqrt(2.0 / math.pi)
    half = 0.5 * x
    return half + half * jnp.tanh(x * (c + (c * 0.044715) * (x * x)))


BF16_ROWS = 16


def _rows_bf16(row):
    tile = jnp.broadcast_to(row, (BF16_ROWS, LANE)).astype(jnp.bfloat16)
    return jnp.concatenate([tile] * (N_KEYS // BF16_ROWS), axis=0)


def _peer_expert_kernel(x_ref, h2_ref, *refs):
    pu_refs, pvt_refs = refs[:PEER_DMA_PARTS], refs[PEER_DMA_PARTS:2 * PEER_DMA_PARTS]
    lim_ref, rk2_ref, e1_ref, e2_ref, o_ref, acc_ref, a_ref, c_ref, rk2_s, e2_s = refs[2 * PEER_DMA_PARTS:]
    f32 = jnp.float32
    bf16 = jnp.bfloat16
    zero = jnp.zeros((N_KEYS, LANE), bf16)
    e_step = pl.program_id(1)
    tl = h2_ref.shape[0]
    part = pu_refs[0].shape[0]
    n_i1 = PEER_DMA_PARTS * part // N_KEYS
    chunk = 2 * LANE if tl % (2 * LANE) == 0 else LANE

    @pl.when(e_step == 0)
    def _():
        acc_ref[...] = jnp.zeros_like(acc_ref)
        rk2_s[...] = rk2_ref[...]
        e2_s[...] = e2_ref[...]

    for p, pu_ref in enumerate(pu_refs):
        a_ref[p * part:(p + 1) * part, :] = lax.dot_general(pu_ref[...], h2_ref[...], (((1,), (1,)), ((), ())),
                                                            preferred_element_type=f32)
    for c0 in range(0, tl, chunk):
        for l0 in range(c0, c0 + chunk, LANE):
            cols = slice(l0, l0 + LANE)
            for ii in range(n_i1):
                rows = slice(ii * N_KEYS, (ii + 1) * N_KEYS)
                w = jnp.zeros((N_KEYS, LANE), bf16)
                for h in range(PEER_HEADS):
                    sel = rk2_s[h, :, cols] < _rows_bf16(lim_ref[h, ii:ii + 1, cols].astype(f32))
                    w = w + jnp.where(sel, e2_s[h, :, cols], zero) * _rows_bf16(e1_ref[h, ii:ii + 1, cols])
                c_ref[rows, cols] = w * _gelu_tanh(a_ref[rows, cols]).astype(bf16)
        cc = slice(c0, c0 + chunk)
        proj = [jnp.dot(pvt_ref[0], c_ref[p * part:(p + 1) * part, cc], preferred_element_type=f32)
                for p, pvt_ref in enumerate(pvt_refs)]
        acc_ref[:, cc] += functools.reduce(lambda a, b: a + b, proj)

    @pl.when(e_step == pl.num_programs(1) - 1)
    def _():
        o_ref[...] = x_ref[...] + acc_ref[...].T


def _token_tile(T, sizes=(640, 512, 384, 256, 128)):
    for tl in sizes:
        if T % tl == 0:
            return tl
    raise ValueError(f"token count {T} is not a multiple of {LANE}")


def peer_residual(x, g, wqt, sk, pu, pvt):
    T = x.shape[0]
    tl = _token_tile(T)
    nt = T // tl
    n_exp = pu.shape[0]
    head_shape = jax.ShapeDtypeStruct((PEER_HEADS, N_KEYS, T), jnp.int32)
    head_shape_f = jax.ShapeDtypeStruct((PEER_HEADS, N_KEYS, T), jnp.float32)
    head_shape_b = jax.ShapeDtypeStruct((PEER_HEADS, N_KEYS, T), jnp.bfloat16)
    head_spec = pl.BlockSpec((PEER_HEADS, N_KEYS, tl), lambda i: (0, 0, i))
    h2, lim, rk2, e1, e2 = pl.pallas_call(
        _peer_router_kernel,
        grid=(nt,),
        in_specs=[pl.BlockSpec((tl, D_MODEL), lambda i: (i, 0)),
                  pl.BlockSpec((1, D_MODEL), lambda i: (0, 0)),
                  pl.BlockSpec((PEER_HEADS * PEER_QDIM, D_MODEL), lambda i: (0, 0)),
                  pl.BlockSpec((N_HC, N_KEYS, HALF_Q), lambda i: (0, 0, 0))],
        out_specs=[pl.BlockSpec((tl, D_MODEL), lambda i: (i, 0)), head_spec, head_spec, head_spec, head_spec],
        out_shape=[jax.ShapeDtypeStruct((T, D_MODEL), jnp.bfloat16), head_shape, head_shape_b,
                   head_shape_f, head_shape_b],
        scratch_shapes=[pltpu.VMEM((PEER_HEADS * PEER_QDIM, tl), jnp.bfloat16),
                        pltpu.VMEM((N_HC, N_KEYS, tl), jnp.float32),
                        pltpu.VMEM((N_HC, N_KEYS, tl), jnp.int32),
                        pltpu.VMEM((N_HC, PEER_TOPK, tl), jnp.float32)],
        compiler_params=pltpu.CompilerParams(dimension_semantics=("parallel",), vmem_limit_bytes=VMEM_LIMIT),
        name="peer_router",
    )(x, g.reshape(1, D_MODEL), wqt, sk)

    tl = _token_tile(T, (768, 512, 640, 384, 256, 128))
    nt = T // tl
    te = PEER_EXPERT_TILE
    parts = PEER_DMA_PARTS
    part = te // parts
    i2_spec = pl.BlockSpec((PEER_HEADS, N_KEYS, tl), lambda i, e: (0, 0, i))
    i1_spec = pl.BlockSpec((PEER_HEADS, te // N_KEYS, tl), lambda i, e: (0, e, i))
    return pl.pallas_call(
        _peer_expert_kernel,
        grid=(nt, n_exp // te),
        in_specs=[pl.BlockSpec((tl, D_MODEL), lambda i, e: (i, 0)),
                  pl.BlockSpec((tl, D_MODEL), lambda i, e: (i, 0)),
                  *[pl.BlockSpec((part, D_MODEL), lambda i, e, p=p: (parts * e + p, 0)) for p in range(parts)],
                  *[pl.BlockSpec((1, D_MODEL, part), lambda i, e, p=p: (parts * e + p, 0, 0)) for p in range(parts)],
                  i1_spec, i2_spec, i1_spec, i2_spec],
        out_specs=pl.BlockSpec((tl, D_MODEL), lambda i, e: (i, 0)),
        out_shape=jax.ShapeDtypeStruct((T, D_MODEL), jnp.float32),
        scratch_shapes=[pltpu.VMEM((D_MODEL, tl), jnp.float32),
                        pltpu.VMEM((te, tl), jnp.float32),
                        pltpu.VMEM((te, tl), jnp.bfloat16),
                        pltpu.VMEM((PEER_HEADS, N_KEYS, tl), jnp.bfloat16),
                        pltpu.VMEM((PEER_HEADS, N_KEYS, tl), jnp.bfloat16)],
        compiler_params=pltpu.CompilerParams(dimension_semantics=("parallel", "arbitrary"),
                                             vmem_limit_bytes=VMEM_LIMIT),
        name="peer_experts",
    )(x, h2, *([pu] * parts), *([pvt] * parts), lim, rk2, e1, e2)


C_QKVB = 3 * WIDTH_A
C_GB = C_QKVB + 3 * WIDTH_B
C_UC = C_GB + WIDTH_B
C_GATE = C_UC + WIDTH_C
C_BA = C_GATE + 3 * D_MODEL
D_IN2 = C_BA + LANE
S5_BLOCK = 256


def _reorder_w_in(w):
    cols = [w[:, OFF_QA:OFF_BETA], w[:, OFF_UC:OFF_GATE], w[:, OFF_GATE:], w[:, OFF_BETA:OFF_UC],
            jnp.zeros((w.shape[0], LANE - 2 * N_HEADS_B), w.dtype)]
    return jnp.concatenate(cols, axis=1).astype(jnp.bfloat16)


def trunk_layer(x, lp, layer_idx, att, prompt, k_past, v_past, conv_hist, delta0, ssm0_re, ssm0_im):
    f32 = jnp.float32
    B, Lx = x.shape[0], x.shape[1]
    T = B * Lx
    z = norm_matmul(x.reshape(T, D_MODEL), lp['norm1'], lp['w_in2'])

    qn, kn, knb, vb = qk_prep(z, lp['q_norm'], lp['k_norm'], _token_tile(T))
    lam_init = 0.8 - 0.6 * math.exp(-0.3 * layer_idx)
    lqk = lp['lambda_qk']
    lam = jnp.exp(jnp.sum(lqk[0] * lqk[1])) - jnp.exp(jnp.sum(lqk[2] * lqk[3])) + lam_init
    scalars = jnp.stack([lam, jnp.asarray(1.0 - lam_init, f32)]).astype(f32)
    r3 = lambda a: a.reshape(B, Lx, WIDTH_A)
    r4 = lambda a: a.reshape(N_HEADS_A, B, Lx, a.shape[-1])
    if prompt:
        oA = attn_prompt(r4(qn), r4(knb), r4(vb), scalars, att['far'], att['tiles'], lp['subln'])
    else:
        P = k_past.shape[1]
        oA = attn_sample(r4(qn), r4(knb), r4(vb), k_past.reshape(B, P, WIDTH_A), v_past.reshape(B, P, WIDTH_A),
                         scalars, att['bias_past'], att['bias_new'], lp['subln'])

    z3 = z.reshape(B, Lx, D_IN2)
    off = PAD if prompt else 0
    L = Lx - off
    kA = r3(kn)[:, off:].reshape(B, L, N_HEADS_A, 2 * HEAD_DIM_A)
    vA = z3[:, off:, 2 * WIDTH_A:3 * WIDTH_A].reshape(B, L, N_HEADS_A, 2 * HEAD_DIM_A)

    oB, delta, tail = gdn_mixer(z3, (C_QKVB // (3 * WIDTH_B), C_GB // WIDTH_B, C_BA // LANE), conv_hist, delta0,
                                lp['conv_w'], lp['a_log'], lp['dt_bias'], lp['gdn_norm'],
                                GDN_CHUNK if prompt else Lx, off)
    conv_state = tail[:, GDN_TAIL - (CONV_W - 1):]

    oC, ssm_re, ssm_im = s5_glu(z3, C_UC // WIDTH_C, ssm0_re, ssm0_im, lp['s5'], lp['w_glu'],
                                S5_BLOCK if prompt else Lx)

    x2 = merge_residual(x.reshape(T, D_MODEL), oA.reshape(T, WIDTH_A), oB.reshape(T, WIDTH_B),
                        oC.reshape(T, WIDTH_C), z, C_GATE // D_MODEL, lp['wb_a'], lp['wb_b'], lp['wb_c'],
                        lp['w_out'], Lx, off)

    x = peer_residual(x2, lp['norm2'], lp['peer_wqt'], lp['peer_sk'], lp['peer_pu'],
                      lp['peer_pvt']).reshape(B, Lx, D_MODEL)
    return (x, kA, vA, conv_state, delta, ssm_re, ssm_im)


def kernel(x_prompt, x_sample, cache_k, cache_v, state_conv, state_delta, state_ssm_re, state_ssm_im,
           meta_tokens, rel_bias, norm1_g, norm2_g, final_norm_g, w_in, q_norm_g, k_norm_g, lambda_qk,
           subln_g, conv_w, gdn_a_log, gdn_dt_bias, gdn_norm_g, s5_a_re, s5_a_im, s5_b_re, s5_b_im,
           s5_c_re, s5_c_im, s5_d, s5_log_dt, w_glu, w_branch_a, w_branch_b, w_branch_c, w_out,
           peer_wq, peer_subkeys, peer_u, peer_v):
    f32 = jnp.float32
    bf16 = jnp.bfloat16
    params = [dict(norm1=norm1_g[l], norm2=norm2_g[l], w_in2=_reorder_w_in(w_in[l]), q_norm=q_norm_g[l],
                   k_norm=k_norm_g[l], lambda_qk=lambda_qk[l], subln=subln_g[l], conv_w=conv_w[l],
                   a_log=gdn_a_log[l], dt_bias=gdn_dt_bias[l], gdn_norm=gdn_norm_g[l],
                   s5=s5_params(s5_a_re[l], s5_a_im[l], s5_b_re[l], s5_b_im[l], s5_c_re[l], s5_c_im[l], s5_d[l],
                                s5_log_dt[l]),
                   w_glu=w_glu[l], wb_a=w_branch_a[l], wb_b=w_branch_b[l], wb_c=w_branch_c[l], w_out=w_out[l],
                   peer_wqt=peer_wq[l].T.astype(bf16),
                   peer_sk=peer_subkeys[l].reshape(N_HC, N_KEYS, HALF_Q).astype(bf16),
                   peer_pu=peer_u[l].astype(bf16), peer_pvt=peer_v[l].astype(bf16).reshape(-1, PEER_EXPERT_TILE // PEER_DMA_PARTS, D_MODEL).transpose(0, 2, 1))
              for l in range(DEPTH)]

    qpos = jnp.arange(ATT_TQ, dtype=jnp.int32)
    tiles = jnp.stack([t5_bias(qpos, d * ATT_TK + jnp.arange(ATT_TK, dtype=jnp.int32), rel_bias)
                       for d in (-2, -1, 0)], axis=1)
    far = t5_bias(jnp.full((1,), 2 * ATT_TK, jnp.int32), jnp.zeros((1,), jnp.int32), rel_bias).reshape(N_HEADS_A)
    P, Ls = cache_k.shape[2], x_sample.shape[1]
    bias_s = t5_bias(P + jnp.arange(Ls, dtype=jnp.int32), jnp.arange(P + Ls, dtype=jnp.int32), rel_bias)
    att_p = dict(tiles=tiles, far=far)
    att_s = dict(bias_past=bias_s[:, :, :P], bias_new=bias_s[:, :, P:])

    B = x_prompt.shape[0]
    xp = jnp.concatenate([jnp.zeros((B, PAD, D_MODEL), f32),
                          jnp.broadcast_to(meta_tokens[None], (B, N_META, D_MODEL)), x_prompt], axis=1)
    outs_p = [[] for _ in range(6)]
    for l in range(DEPTH):
        res = trunk_layer(
            xp, params[l], l, att_p, True, None, None,
            jnp.zeros((B, CONV_W - 1, 3 * WIDTH_B), f32),
            jnp.zeros((B, N_HEADS_B, HEAD_DIM_B, HEAD_DIM_B), f32),
            jnp.zeros((B, S5_GROUPS, S5_STATE), f32), jnp.zeros((B, S5_GROUPS, S5_STATE), f32))
        xp = res[0]
        for acc, r in zip(outs_p, res[1:]):
            acc.append(r)
    y_prompt = rmsnorm(xp, final_norm_g)[:, FRONT:]

    xs = x_sample
    outs_s = [[] for _ in range(6)]
    for l in range(DEPTH):
        res = trunk_layer(
            xs, params[l], l, att_s, False, cache_k[l], cache_v[l], state_conv[l],
            state_delta[l], state_ssm_re[l], state_ssm_im[l])
        xs = res[0]
        for acc, r in zip(outs_s, res[1:]):
            acc.append(r)
    y_sample = rmsnorm(xs, final_norm_g)

    kp, vp, cp, dp, srp, sip = [jnp.stack(a) for a in outs_p]
    ks_, vs_, cs_, ds_, srs, sis = [jnp.stack(a) for a in outs_s]
    return (y_prompt, y_sample, kp, vp, ks_, vs_, cp, cs_, dp, ds_, srp, sip, srs, sis)
```

```python
import functools
import math

import jax
import jax.numpy as jnp
from jax import lax
from jax.experimental import pallas as pl
from jax.experimental.pallas import tpu as pltpu

D_MODEL = 1024
DEPTH = 4
CHUNK = 64
N_META = 16
EPS = 1e-6
NEG_INF = -1e30
N_HEADS_A = 4
HEAD_DIM_A = 64
NUM_BUCKETS = 32
MAX_DISTANCE = 128
N_HEADS_B = 4
HEAD_DIM_B = 128
CONV_W = 4
S5_GROUP = 16
S5_GROUPS = 32
S5_STATE = 64
PEER_HEADS = 8
PEER_QDIM = 256
N_KEYS = 128
PEER_TOPK = 16

WIDTH_A = N_HEADS_A * 2 * HEAD_DIM_A
WIDTH_B = N_HEADS_B * HEAD_DIM_B
WIDTH_C = S5_GROUPS * S5_GROUP
OFF_QA = 0
OFF_KA = OFF_QA + WIDTH_A
OFF_VA = OFF_KA + WIDTH_A
OFF_QKVB = OFF_VA + WIDTH_A
OFF_GB = OFF_QKVB + 3 * WIDTH_B
OFF_BETA = OFF_GB + WIDTH_B
OFF_ALPHA = OFF_BETA + N_HEADS_B
OFF_UC = OFF_ALPHA + N_HEADS_B
OFF_GATE = OFF_UC + WIDTH_C
D_IN = OFF_GATE + 3 * D_MODEL

LANE = 128
VMEM_LIMIT = 56 * 1024 * 1024
WEIGHT_TILE_BYTES = 6 * 1024 * 1024
HIGHEST = lax.Precision.HIGHEST


def _norm_mm_kernel(x_ref, g_ref, w_ref, o_ref, h_ref):
    @pl.when(pl.program_id(1) == 0)
    def _():
        x = x_ref[...]
        h = x * lax.rsqrt(jnp.mean(x * x, axis=-1, keepdims=True) + EPS) * g_ref[...]
        h_ref[...] = h.astype(jnp.bfloat16)

    o_ref[...] = jnp.dot(h_ref[...], w_ref[...], preferred_element_type=jnp.float32)


def norm_matmul(x, g, wb, tm=512):
    M, K = x.shape
    N = wb.shape[1]
    n_lanes = N // LANE
    tn = LANE * max(d for d in range(1, n_lanes + 1) if n_lanes % d == 0 and d * LANE * K * 2 <= WEIGHT_TILE_BYTES)
    tm = min(tm, M)
    return pl.pallas_call(
        _norm_mm_kernel,
        grid=(pl.cdiv(M, tm), N // tn),
        in_specs=[pl.BlockSpec((tm, K), lambda i, j: (i, 0)),
                  pl.BlockSpec((1, K), lambda i, j: (0, 0)),
                  pl.BlockSpec((K, tn), lambda i, j: (0, j))],
        out_specs=pl.BlockSpec((tm, tn), lambda i, j: (i, j)),
        out_shape=jax.ShapeDtypeStruct((M, N), jnp.float32),
        scratch_shapes=[pltpu.VMEM((tm, K), jnp.bfloat16)],
        compiler_params=pltpu.CompilerParams(dimension_semantics=("parallel", "arbitrary"),
                                             vmem_limit_bytes=VMEM_LIMIT),
        name="norm_proj",
    )(x, g.reshape(1, K), wb)


def rmsnorm(x, g):
    xf = x.astype(jnp.float32)
    y = xf * lax.rsqrt(jnp.mean(xf * xf, axis=-1, keepdims=True) + EPS)
    return (y * g.astype(jnp.float32)).astype(x.dtype)


def t5_bias(q_pos, k_pos, table):
    rel = k_pos[None, :] - q_pos[:, None]
    half = NUM_BUCKETS // 2
    exact = half // 2
    n = jnp.abs(rel)
    nf = jnp.maximum(n, 1).astype(jnp.float32)
    far = exact + (jnp.log(nf / exact) / math.log(MAX_DISTANCE / exact) * (half - exact)).astype(jnp.int32)
    bucket = jnp.where(rel > 0, half, 0) + jnp.where(n < exact, n, jnp.minimum(far, half - 1))
    hit = bucket[None, :, :] == jnp.arange(NUM_BUCKETS, dtype=bucket.dtype)[:, None, None]
    tab = table.astype(jnp.float32)
    return jnp.stack([jnp.sum(jnp.where(hit, tab[:, h, None, None], 0.0), axis=0) for h in range(tab.shape[1])])


FRONT = 256
PAD = FRONT - N_META
ATT_TQ = 256
ATT_TK = 256


def _qk_prep_kernel(q_ref, k_ref, v_ref, gq_ref, gk_ref, seg_ref, qn_ref, kn_ref, knb_ref, vb_ref):
    f32 = jnp.float32
    bf16 = jnp.bfloat16
    seg = seg_ref[...]
    hd = 2 * HEAD_DIM_A

    def norm(x, g):
        ms = jnp.dot(x * x, seg, precision=HIGHEST, preferred_element_type=f32)
        return x * lax.rsqrt(ms + EPS) * g

    qn = (norm(q_ref[...], gq_ref[...]) * (HEAD_DIM_A ** -0.5)).astype(bf16)
    kn = norm(k_ref[...], gk_ref[...])
    kn_ref[...] = kn
    knb = kn.astype(bf16)
    v = v_ref[...].astype(bf16)
    lane = lax.broadcasted_iota(jnp.int32, (v.shape[0], hd), 1)
    ones_col = jnp.where(lane == 0, 1.0, 0.0).astype(bf16)
    for h in range(N_HEADS_A):
        cols = slice(h * hd, (h + 1) * hd)
        qn_ref[h] = qn[:, cols]
        knb_ref[h] = knb[:, cols]
        vb_ref[h] = jnp.concatenate([v[:, cols], ones_col], axis=1)


def qk_prep(z, gq, gk, tm):
    T = z.shape[0]
    hd = 2 * HEAD_DIM_A
    seg = jnp.kron(jnp.eye(WIDTH_A // HEAD_DIM_A, dtype=jnp.float32),
                   jnp.full((HEAD_DIM_A, HEAD_DIM_A), 1.0 / HEAD_DIM_A, jnp.float32))
    row = lambda j: pl.BlockSpec((tm, WIDTH_A), lambda i, j=j: (i, j))
    const = lambda shape: pl.BlockSpec(shape, lambda i: (0,) * len(shape))
    heads = lambda w: pl.BlockSpec((N_HEADS_A, tm, w), lambda i: (0, i, 0))
    hshape = lambda w: jax.ShapeDtypeStruct((N_HEADS_A, T, w), jnp.bfloat16)
    return pl.pallas_call(
        _qk_prep_kernel,
        grid=(T // tm,),
        in_specs=[row(0), row(1), row(2), const((1, WIDTH_A)), const((1, WIDTH_A)), const((WIDTH_A, WIDTH_A))],
        out_specs=[heads(hd), pl.BlockSpec((tm, WIDTH_A), lambda i: (i, 0)), heads(hd), heads(2 * hd)],
        out_shape=[hshape(hd), jax.ShapeDtypeStruct((T, WIDTH_A), jnp.float32), hshape(hd), hshape(2 * hd)],
        compiler_params=pltpu.CompilerParams(dimension_semantics=("parallel",)),
        name="qk_prep",
    )(z, z, z, jnp.tile(gq, WIDTH_A // HEAD_DIM_A).reshape(1, WIDTH_A),
      jnp.tile(gk, WIDTH_A // HEAD_DIM_A).reshape(1, WIDTH_A), seg)


def _split_maps(q):
    lane = lax.broadcasted_iota(jnp.int32, q.shape, 1)
    zero = jnp.zeros_like(q)
    return jnp.where(lane < HEAD_DIM_A, q, zero), jnp.where(lane >= HEAD_DIM_A, q, zero)


def _subln(o, g, scale):
    return o * lax.rsqrt(jnp.mean(o * o, axis=-1, keepdims=True) + EPS) * g * scale


def _attn_prompt_kernel(sc_ref, far_ref, q_ref, k_ref, v_ref, bias_ref, g_ref, o_ref, m_ref, acc_ref,
                        sa_ref, sb_ref):
    f32 = jnp.float32
    h = pl.program_id(1)
    qi = pl.program_id(2)
    tq, tk = ATT_TQ, ATT_TK
    hd = 2 * HEAD_DIM_A
    qs = jnp.concatenate(_split_maps(q_ref[0, 0]), axis=0)
    m_ref[...] = jnp.full(m_ref.shape, NEG_INF, f32)
    acc_ref[...] = jnp.zeros(acc_ref.shape, f32)
    far = far_ref[h]

    def logits(k0, width, general):
        k0 = pl.multiple_of(k0, tk)
        kt = k_ref[0, 0, pl.ds(k0, width), :]
        s = lax.dot_general(qs, kt, (((1,), (1,)), ((), ())), preferred_element_type=f32)
        if general:
            kj = k0 // tk
            bias = jnp.concatenate([bias_ref[0, jnp.clip(kj + j - qi + 2, 0, 2)] for j in range(width // tk)], axis=1)
            qpos = qi * tq + lax.broadcasted_iota(jnp.int32, (tq, width), 0)
            kpos = k0 + lax.broadcasted_iota(jnp.int32, (tq, width), 1)
            qchunk = jnp.where(qpos < FRONT, 0, 1 + jnp.right_shift(qpos - FRONT, 6))
            kchunk = jnp.where(kpos < FRONT, 0, 1 + jnp.right_shift(kpos - FRONT, 6))
            mask = (kpos >= PAD) & (kchunk <= qchunk)
            s = jnp.where(jnp.concatenate([mask, mask], axis=0), s + jnp.concatenate([bias, bias], axis=0), NEG_INF)
        else:
            s = s + far
        return s

    def accumulate(s, k0):
        width = s.shape[1]
        vt = v_ref[0, 0, pl.ds(pl.multiple_of(k0, tk), width), :]
        n_lane_tiles = width // LANE
        smax = s[:, :LANE]
        for j in range(1, n_lane_tiles):
            smax = jnp.maximum(smax, s[:, j * LANE:(j + 1) * LANE])
        m_old = m_ref[...]
        m_new = jnp.maximum(m_old, jnp.broadcast_to(jnp.max(smax, axis=1, keepdims=True), m_old.shape))
        alpha = jnp.exp(m_old - m_new)
        p = jnp.exp(s - jnp.concatenate([m_new] * n_lane_tiles, axis=1))
        acc_ref[...] = (jnp.concatenate([alpha, alpha], axis=1) * acc_ref[...]
                        + jnp.dot(p.astype(jnp.bfloat16), vt, preferred_element_type=f32))
        m_ref[...] = m_new

    def tile(k0, width, general):
        accumulate(logits(k0, width, general), k0)

    tile(0, tk, True)

    wide = 2 * tk
    n_far = jnp.maximum(qi - 2, 0)
    n_single = n_far % 4

    @pl.when(n_single >= 2)
    def _():
        tile(tk, wide, False)

    @pl.when(n_single % 2 == 1)
    def _():
        tile((1 + 2 * (n_single // 2)) * tk, tk, False)

    base = (1 + n_single) * tk
    n_pairs = n_far // 4

    @pl.when(n_pairs > 0)
    def _():
        sa_ref[...] = logits(base, wide, False)

    def pair_body(j, carry):
        k0 = base + 2 * j * wide
        sb_ref[...] = logits(k0 + wide, wide, False)
        accumulate(sa_ref[...], k0)
        nxt = jnp.minimum(k0 + 2 * wide, base + (2 * n_pairs - 1) * wide)
        sa_ref[...] = logits(nxt, wide, False)
        accumulate(sb_ref[...], k0 + wide)
        return carry

    lax.fori_loop(0, n_pairs, pair_body, 0)

    @pl.when(qi >= 2)
    def _():
        tile((qi - 1) * tk, wide, True)

    @pl.when(qi == 1)
    def _():
        tile(tk, tk, True)

    acc = acc_ref[...]
    out = acc[:, :hd] / acc[:, hd:hd + 1]
    o = out[:tq] - sc_ref[0] * out[tq:]
    o_ref[0] = _subln(o, g_ref[...], sc_ref[1])


def attn_prompt(qn, knb, vb, scalars, far, bias_tiles, subln_g):
    _, B, Lp, _ = qn.shape
    hd = 2 * HEAD_DIM_A
    smem = pl.BlockSpec(memory_space=pltpu.SMEM)
    return pl.pallas_call(
        _attn_prompt_kernel,
        grid=(B, N_HEADS_A, Lp // ATT_TQ),
        in_specs=[smem, smem,
                  pl.BlockSpec((1, 1, ATT_TQ, hd), lambda b, h, i: (h, b, i, 0)),
                  pl.BlockSpec((1, 1, Lp, hd), lambda b, h, i: (h, b, 0, 0)),
                  pl.BlockSpec((1, 1, Lp, 2 * hd), lambda b, h, i: (h, b, 0, 0)),
                  pl.BlockSpec((1, 3, ATT_TQ, ATT_TK), lambda b, h, i: (h, 0, 0, 0)),
                  pl.BlockSpec((1, hd), lambda b, h, i: (0, 0))],
        out_specs=pl.BlockSpec((1, ATT_TQ, hd), lambda b, h, i: (b, i, h)),
        out_shape=jax.ShapeDtypeStruct((B, Lp, WIDTH_A), jnp.float32),
        scratch_shapes=[pltpu.VMEM((2 * ATT_TQ, hd), jnp.float32), pltpu.VMEM((2 * ATT_TQ, 2 * hd), jnp.float32),
                        pltpu.VMEM((2 * ATT_TQ, 2 * ATT_TK), jnp.float32),
                        pltpu.VMEM((2 * ATT_TQ, 2 * ATT_TK), jnp.float32)],
        compiler_params=pltpu.CompilerParams(dimension_semantics=("parallel", "parallel", "parallel"),
                                             vmem_limit_bytes=VMEM_LIMIT),
        name="diff_attn_prompt",
    )(scalars, far, qn, knb, vb, bias_tiles, subln_g.reshape(1, hd))


def _attn_sample_kernel(sc_ref, q_ref, kp_ref, vp_ref, kn_ref, vn_ref, bp_ref, bn_ref, g_ref, o_ref):
    f32 = jnp.float32
    bf16 = jnp.bfloat16
    qs = _split_maps(q_ref[0, 0])
    kp = kp_ref[0].astype(bf16)
    vp = vp_ref[0].astype(bf16)
    kn = kn_ref[0, 0]
    vn = vn_ref[0, 0][:, :2 * HEAD_DIM_A]
    dn = (((1,), (1,)), ((), ()))
    outs = []
    for c in range(2):
        sp = lax.dot_general(qs[c], kp, dn, preferred_element_type=f32) + bp_ref[0]
        sn = lax.dot_general(qs[c], kn, dn, preferred_element_type=f32) + bn_ref[0]
        m = jnp.maximum(jnp.max(sp, axis=1, keepdims=True), jnp.max(sn, axis=1, keepdims=True))
        pp = jnp.exp(sp - m)
        pn = jnp.exp(sn - m)
        l = jnp.sum(pp, axis=1, keepdims=True) + jnp.sum(pn, axis=1, keepdims=True)
        acc = (jnp.dot(pp.astype(bf16), vp, preferred_element_type=f32)
               + jnp.dot(pn.astype(bf16), vn, preferred_element_type=f32))
        outs.append(acc / l)
    o_ref[0] = _subln(outs[0] - sc_ref[0] * outs[1], g_ref[...], sc_ref[1])


def attn_sample(qn, knb, vb, k_past, v_past, scalars, bias_past, bias_new, subln_g):
    _, B, L, _ = qn.shape
    P = k_past.shape[1]
    hd = 2 * HEAD_DIM_A
    new = pl.BlockSpec((1, L, hd), lambda b, h: (b, 0, h))
    new_h = lambda w: pl.BlockSpec((1, 1, L, w), lambda b, h: (h, b, 0, 0))
    past = pl.BlockSpec((1, P, hd), lambda b, h: (b, 0, h))
    return pl.pallas_call(
        _attn_sample_kernel,
        grid=(B, N_HEADS_A),
        in_specs=[pl.BlockSpec(memory_space=pltpu.SMEM), new_h(hd), past, past, new_h(hd), new_h(2 * hd),
                  pl.BlockSpec((1, L, P), lambda b, h: (h, 0, 0)),
                  pl.BlockSpec((1, L, L), lambda b, h: (h, 0, 0)),
                  pl.BlockSpec((1, hd), lambda b, h: (0, 0))],
        out_specs=new,
        out_shape=jax.ShapeDtypeStruct((B, L, WIDTH_A), jnp.float32),
        compiler_params=pltpu.CompilerParams(dimension_semantics=("parallel", "parallel")),
        name="diff_attn_sample",
    )(scalars, qn, k_past, v_past, knb, vb, bias_past, bias_new, subln_g.reshape(1, hd))


S5_N = S5_GROUPS * S5_STATE
S5_SLAB = 8
S5_LANES = 512
S5_SPLIT = 2


def _sigmoid(x):
    return 1.0 / (1.0 + jnp.exp(-x))


def _s5_kernel(u_ref, x0_ref, bmat_ref, cmat_ref, lamp_ref, ppow_ref, d_ref, wglu_ref, oc_ref, xf_ref,
               bu_ref, carry_ref):
    f32 = jnp.float32
    bf16 = jnp.bfloat16
    t = pl.program_id(1)
    tb = u_ref.shape[1]

    @pl.when(t == 0)
    def _():
        carry_ref[...] = x0_ref[0]

    u = u_ref[0]
    ub = u.astype(bf16)
    ch, st = WIDTH_C // S5_SPLIT, S5_N // S5_SPLIT
    for s in range(S5_SPLIT):
        for part in range(2):
            cols = slice(part * S5_N + s * st, part * S5_N + (s + 1) * st)
            bu_ref[:, cols] = jnp.dot(ub[:, s * ch:(s + 1) * ch], bmat_ref[s * ch:(s + 1) * ch, cols],
                                      preferred_element_type=f32)

    row = lax.broadcasted_iota(jnp.int32, (S5_SLAB, S5_LANES), 0)
    for c in range(S5_N // S5_LANES):
        re = slice(c * S5_LANES, (c + 1) * S5_LANES)
        im = slice(S5_N + c * S5_LANES, S5_N + (c + 1) * S5_LANES)

        def slab(i, carry):
            cre, cim = carry
            rows = pl.ds(pl.multiple_of(i * S5_SLAB, S5_SLAB), S5_SLAB)
            yre = bu_ref[rows, re]
            yim = bu_ref[rows, im]
            for s in range(3):
                sh = 1 << s
                sre = jnp.where(row >= sh, pltpu.roll(yre, sh, 0), 0.0)
                sim = jnp.where(row >= sh, pltpu.roll(yim, sh, 0), 0.0)
                lr = lamp_ref[s, 0, :, re]
                li = lamp_ref[s, 1, :, re]
                yre, yim = yre + (lr * sre - li * sim), yim + (lr * sim + li * sre)
            pr = ppow_ref[0, :, re]
            pi = ppow_ref[1, :, re]
            yre, yim = yre + (pr * cre - pi * cim), yim + (pr * cim + pi * cre)
            bu_ref[rows, re] = yre
            bu_ref[rows, im] = yim
            last = S5_SLAB - 1
            return (jnp.broadcast_to(yre[last:last + 1, :], yre.shape),
                    jnp.broadcast_to(yim[last:last + 1, :], yim.shape))

        cre, cim = lax.fori_loop(0, tb // S5_SLAB, slab, (carry_ref[:, re], carry_ref[:, im]))
        carry_ref[:, re] = cre
        carry_ref[:, im] = cim

    ys = []
    for s in range(S5_SPLIT):
        out = slice(s * ch, (s + 1) * ch)
        acc = None
        for part in range(2):
            cols = slice(part * S5_N + s * st, part * S5_N + (s + 1) * st)
            term = jnp.dot(bu_ref[:, cols].astype(bf16), cmat_ref[cols, out], preferred_element_type=f32)
            acc = term if acc is None else acc + term
        ys.append(acc)
    y = jnp.concatenate(ys, axis=1) + d_ref[...] * u
    gl = jnp.dot(_gelu_tanh(y).astype(bf16), wglu_ref[...], preferred_element_type=f32)
    oc_ref[0] = gl[:, :WIDTH_C] * _sigmoid(gl[:, WIDTH_C:])

    @pl.when(t == pl.num_programs(1) - 1)
    def _():
        xf_ref[0] = carry_ref[...]


def s5_params(a_re, a_im, b_re, b_im, c_re, c_im, d, log_dt):
    f32 = jnp.float32
    lam = lax.complex(a_re, a_im)
    lam_bar = jnp.exp(lam * jnp.exp(log_dt)[:, None])
    b_bar = ((lam_bar - 1.0) / lam)[..., None] * lax.complex(b_re, b_im)
    eye = jnp.eye(S5_GROUPS, dtype=f32)
    bd_in = lambda m: jnp.einsum('gpi,gh->gihp', m, eye).reshape(WIDTH_C, S5_N)
    bd_out = lambda m: jnp.einsum('gip,gh->gphi', m, eye).reshape(S5_N, WIDTH_C)
    bmat = jnp.concatenate([bd_in(b_bar.real), bd_in(b_bar.imag)], axis=1).astype(jnp.bfloat16)
    cmat = jnp.concatenate([bd_out(c_re), bd_out(-c_im)], axis=0).astype(jnp.bfloat16)
    lb = lam_bar.reshape(S5_N)
    rep = lambda v: jnp.broadcast_to(v[None, :], (S5_SLAB, S5_N))
    pows = [lb, lb * lb, (lb * lb) * (lb * lb)]
    lamp = jnp.stack([jnp.stack([rep(p.real), rep(p.imag)]) for p in pows])
    run = [lb]
    for _ in range(S5_SLAB - 1):
        run.append(run[-1] * lb)
    pp = jnp.stack(run)
    ppow = jnp.stack([pp.real, pp.imag])
    return dict(bmat=bmat, cmat=cmat, lamp=lamp.astype(f32), ppow=ppow.astype(f32), d=d.reshape(1, WIDTH_C))


def s5_glu(z3, col_block, x0_re, x0_im, sp, w_glu, tb):
    B, Lx, _ = z3.shape
    x0 = jnp.concatenate([x0_re.reshape(B, S5_N), x0_im.reshape(B, S5_N)], axis=1)
    x0 = jnp.broadcast_to(x0[:, None, :], (B, S5_SLAB, 2 * S5_N))
    const = lambda a: pl.BlockSpec(a.shape, lambda b, t: (0,) * a.ndim)
    wg = w_glu.astype(jnp.bfloat16)
    oc, xf = pl.pallas_call(
        _s5_kernel,
        grid=(B, Lx // tb),
        in_specs=[pl.BlockSpec((1, tb, WIDTH_C), lambda b, t: (b, t, col_block)),
                  pl.BlockSpec((1, S5_SLAB, 2 * S5_N), lambda b, t: (b, 0, 0)),
                  const(sp['bmat']), const(sp['cmat']), const(sp['lamp']), const(sp['ppow']), const(sp['d']),
                  const(wg)],
        out_specs=[pl.BlockSpec((1, tb, WIDTH_C), lambda b, t: (b, t, 0)),
                   pl.BlockSpec((1, S5_SLAB, 2 * S5_N), lambda b, t: (b, 0, 0))],
        out_shape=[jax.ShapeDtypeStruct((B, Lx, WIDTH_C), jnp.float32),
                   jax.ShapeDtypeStruct((B, S5_SLAB, 2 * S5_N), jnp.float32)],
        scratch_shapes=[pltpu.VMEM((tb, 2 * S5_N), jnp.float32), pltpu.VMEM((S5_SLAB, 2 * S5_N), jnp.float32)],
        compiler_params=pltpu.CompilerParams(dimension_semantics=("parallel", "arbitrary"),
                                             vmem_limit_bytes=VMEM_LIMIT),
        name="s5_glu",
    )(z3, x0, sp['bmat'], sp['cmat'], sp['lamp'], sp['ppow'], sp['d'], wg)
    shape = (B, S5_GROUPS, S5_STATE)
    return oc, xf[:, 0, :S5_N].reshape(shape), xf[:, 0, S5_N:].reshape(shape)


def _gdn_kernel(qkv_ref, gate_ref, ba_ref, hist_ref, s0_ref, cw_ref, av_ref, dt_ref, gn_ref,
                ob_ref, sout_ref, tail_ref, s_ref, prev_ref, *, pad_rows):
    f32 = jnp.float32
    t = pl.program_id(1)
    C = qkv_ref.shape[1]
    hd = HEAD_DIM_B
    dn_last = (((1,), (1,)), ((), ()))
    dn_first = (((0,), (0,)), ((), ()))
    dot = lambda a, b: _dot3(a, b, (((1,), (0,)), ((), ())))

    @pl.when(t == 0)
    def _():
        s_ref[...] = s0_ref[0]
        prev_ref[...] = hist_ref[0]

    cur = qkv_ref[0]
    tail = prev_ref.shape[0]
    ext = jnp.concatenate([prev_ref[...], cur], axis=0)
    w = cw_ref[...]
    conv = cur * w[CONV_W - 1:CONV_W, :]
    for i in range(CONV_W - 1):
        lo = tail - (CONV_W - 1) + i
        conv = conv + ext[lo:lo + C, :] * w[i:i + 1, :]
    prev_ref[...] = cur[C - tail:, :]
    c = conv * _sigmoid(conv)

    ba = ba_ref[0]
    xg = ba + dt_ref[...]
    softplus = jnp.maximum(xg, 0.0) + jnp.log(1.0 + jnp.exp(-jnp.abs(xg)))
    g_all = -jnp.exp(av_ref[...]) * softplus
    if pad_rows:
        grow = t * C + lax.broadcasted_iota(jnp.int32, g_all.shape, 0)
        g_all = jnp.where(grow >= pad_rows, g_all, 0.0)
    ri = lax.broadcasted_iota(jnp.int32, (C, C), 0)
    ci = lax.broadcasted_iota(jnp.int32, (C, C), 1)
    tri = ri >= ci
    strict = ri > ci
    tril = jnp.where(tri, 1.0, 0.0).astype(f32)
    G_col = jnp.dot(tril, g_all, precision=HIGHEST, preferred_element_type=f32)
    G_row = lax.dot_general(g_all, tril, (((0,), (1,)), ((), ())), precision=HIGHEST,
                            preferred_element_type=f32)
    eye = jnp.where(ri == ci, 1.0, 0.0).astype(f32)

    heads = range(N_HEADS_B)
    dot_t = lambda a, b: _dot3(a, b, dn_last)
    Gc = [G_col[:, N_HEADS_B + h:N_HEADS_B + h + 1] for h in heads]
    Gr = [G_row[N_HEADS_B + h:N_HEADS_B + h + 1, :] for h in heads]
    decay = [jnp.where(tri, jnp.exp(jnp.minimum(Gc[h] - Gr[h], 0.0)), 0.0) for h in heads]
    beta = [_sigmoid(ba[:, h:h + 1]) for h in heads]
    unit = lambda a: a * lax.rsqrt(jnp.sum(a * a, axis=-1, keepdims=True) + EPS)
    q = [unit(c[:, h * hd:(h + 1) * hd]) * (hd ** -0.5) for h in heads]
    k = [unit(c[:, WIDTH_B + h * hd:WIDTH_B + (h + 1) * hd]) for h in heads]
    kb = [k[h] * beta[h] for h in heads]
    eG = [jnp.exp(Gc[h]) for h in heads]
    rhs = [jnp.concatenate([c[:, 2 * WIDTH_B + h * hd:2 * WIDTH_B + (h + 1) * hd] * beta[h], kb[h] * eG[h]], axis=1)
           for h in heads]
    A = [jnp.where(strict, dot_t(kb[h], k[h]) * decay[h], 0.0) for h in heads]
    same_block = lambda b: jnp.right_shift(ri, b.bit_length() - 1) == jnp.right_shift(ci, b.bit_length() - 1)
    in_base = same_block(GDN_BASE)
    D = [jnp.where(in_base, A[h], 0.0) for h in heads]
    D2 = [dot(D[h], D[h]) for h in heads]
    D4 = [dot(D2[h], D2[h]) for h in heads]
    tm = [eye - D[h] for h in heads]
    tm = [tm[h] + dot(tm[h], D2[h]) for h in heads]
    tm = [tm[h] + dot(tm[h], D4[h]) for h in heads]
    b = GDN_BASE
    while b < C:
        level = same_block(2 * b) & jnp.logical_not(same_block(b))
        me = [dot(tm[h], jnp.where(level, A[h], 0.0)) for h in heads]
        tm = [tm[h] - dot(me[h], tm[h]) for h in heads]
        b *= 2
    sol = [dot(tm[h], rhs[h]) for h in heads]
    attn = [dot_t(q[h], k[h]) * decay[h] for h in heads]
    S = [s_ref[h] for h in heads]
    v_new = [sol[h][:, :hd] - dot(sol[h][:, hd:], S[h]) for h in heads]
    o = [dot(q[h] * eG[h], S[h]) + dot(attn[h], v_new[h]) for h in heads]
    for h in heads:
        GL = Gc[h][C - 1:C, :]
        s_ref[h] = S[h] * jnp.exp(GL) + _dot3(k[h] * jnp.exp(GL - Gc[h]), v_new[h], dn_first)
        gate = gate_ref[0, :, h * hd:(h + 1) * hd]
        on = o[h] * lax.rsqrt(jnp.mean(o[h] * o[h], axis=-1, keepdims=True) + EPS) * gn_ref[...]
        ob_ref[0, :, h * hd:(h + 1) * hd] = on * (gate * _sigmoid(gate))

    @pl.when(t == pl.num_programs(1) - 1)
    def _():
        sout_ref[0] = s_ref[...]
        tail_ref[0] = cur[C - tail:, :]


GDN_TAIL = 8
GDN_CHUNK = 128
GDN_BASE = 8


def _dot3(a, b, dims):
    f32 = jnp.float32
    bf16 = jnp.bfloat16
    a_hi = a.astype(bf16)
    b_hi = b.astype(bf16)
    a_lo = (a - a_hi.astype(f32)).astype(bf16)
    b_lo = (b - b_hi.astype(f32)).astype(bf16)
    dg = lambda x, y: lax.dot_general(x, y, dims, preferred_element_type=f32)
    return dg(a_hi, b_hi) + (dg(a_hi, b_lo) + dg(a_lo, b_hi))


def gdn_mixer(z3, blocks, conv_hist, delta0, conv_w, a_log, dt_bias, gdn_norm, chunk, pad_rows):
    B, Lx, _ = z3.shape
    f32 = jnp.float32
    wq = 3 * WIDTH_B
    hist = jnp.pad(conv_hist, ((0, 0), (GDN_TAIL - (CONV_W - 1), 0), (0, 0)))
    lane_vec = lambda v: jnp.zeros((1, LANE), f32).at[0, N_HEADS_B:2 * N_HEADS_B].set(v)
    const = lambda shape: pl.BlockSpec(shape, lambda b, t: (0,) * len(shape))
    per_b = lambda shape: pl.BlockSpec((1,) + shape, lambda b, t: (b,) + (0,) * len(shape))
    qb, gb, bb = blocks
    return pl.pallas_call(
        functools.partial(_gdn_kernel, pad_rows=pad_rows),
        grid=(B, Lx // chunk),
        in_specs=[pl.BlockSpec((1, chunk, wq), lambda b, t: (b, t, qb)),
                  pl.BlockSpec((1, chunk, WIDTH_B), lambda b, t: (b, t, gb)),
                  pl.BlockSpec((1, chunk, LANE), lambda b, t: (b, t, bb)),
                  per_b((GDN_TAIL, wq)), per_b((N_HEADS_B, HEAD_DIM_B, HEAD_DIM_B)),
                  const((CONV_W, wq)), const((1, LANE)), const((1, LANE)), const((1, HEAD_DIM_B))],
        out_specs=[pl.BlockSpec((1, chunk, WIDTH_B), lambda b, t: (b, t, 0)),
                   per_b((N_HEADS_B, HEAD_DIM_B, HEAD_DIM_B)), per_b((GDN_TAIL, wq))],
        out_shape=[jax.ShapeDtypeStruct((B, Lx, WIDTH_B), f32),
                   jax.ShapeDtypeStruct((B, N_HEADS_B, HEAD_DIM_B, HEAD_DIM_B), f32),
                   jax.ShapeDtypeStruct((B, GDN_TAIL, wq), f32)],
        scratch_shapes=[pltpu.VMEM((N_HEADS_B, HEAD_DIM_B, HEAD_DIM_B), f32), pltpu.VMEM((GDN_TAIL, wq), f32)],
        compiler_params=pltpu.CompilerParams(dimension_semantics=("parallel", "arbitrary")),
        name="gdn_mixer",
    )(z3, z3, z3, hist, delta0, conv_w, lane_vec(a_log), lane_vec(dt_bias), gdn_norm.reshape(1, HEAD_DIM_B))


def _merge_kernel(x_ref, oa_ref, ob_ref, oc_ref, g0_ref, g1_ref, g2_ref, wa_ref, wb_ref, wc_ref, wo_ref, o_ref,
                  *, rows_per_seq, pad_rows):
    f32 = jnp.float32
    bf16 = jnp.bfloat16
    mm = lambda a, w_ref: jnp.dot(a.astype(bf16), w_ref[...], preferred_element_type=f32)
    merged = (_sigmoid(g0_ref[...]) * mm(oa_ref[...], wa_ref) + _sigmoid(g1_ref[...]) * mm(ob_ref[...], wb_ref)
              + _sigmoid(g2_ref[...]) * mm(oc_ref[...], wc_ref))
    x = x_ref[...] + mm(merged, wo_ref)
    if pad_rows:
        tm = x.shape[0]
        row = pl.program_id(0) * tm + lax.broadcasted_iota(jnp.int32, x.shape, 0)
        x = jnp.where(row % rows_per_seq >= pad_rows, x, 0.0)
    o_ref[...] = x


def merge_residual(x, oA, oB, oC, z, gate_block, wa, wb, wc, wo, rows_per_seq, pad_rows):
    T = x.shape[0]
    tm = _token_tile(T)
    bf16 = jnp.bfloat16
    rows = lambda w, j=0: pl.BlockSpec((tm, w), lambda i, j=j: (i, j))
    const = lambda a: pl.BlockSpec(a.shape, lambda i: (0, 0))
    ws = [w.astype(bf16) for w in (wa, wb, wc, wo)]
    return pl.pallas_call(
        functools.partial(_merge_kernel, rows_per_seq=rows_per_seq, pad_rows=pad_rows),
        grid=(T // tm,),
        in_specs=[rows(D_MODEL), rows(WIDTH_A), rows(WIDTH_B), rows(WIDTH_C),
                  rows(D_MODEL, gate_block), rows(D_MODEL, gate_block + 1), rows(D_MODEL, gate_block + 2)]
                 + [const(w) for w in ws],
        out_specs=rows(D_MODEL),
        out_shape=jax.ShapeDtypeStruct((T, D_MODEL), jnp.float32),
        compiler_params=pltpu.CompilerParams(dimension_semantics=("parallel",), vmem_limit_bytes=VMEM_LIMIT),
        name="merge_residual",
    )(x, oA, oB, oC, z, z, z, *ws)


N_HC = 2 * PEER_HEADS
HALF_Q = PEER_QDIM // 2
PEER_EXPERT_TILE = 8 * N_KEYS
CAND_SUB = 8
PEER_DMA_PARTS = 2


def _extract_top16(s, iota_f):
    n = float(s.shape[0])
    rank = jnp.full(s.shape, PEER_TOPK, jnp.int32)
    vals = []
    for j in range(PEER_TOPK):
        m = jnp.max(s, axis=0, keepdims=True)
        idx = jnp.min(jnp.where(s == m, iota_f, n), axis=0, keepdims=True)
        hit = iota_f == idx
        rank = jnp.where(hit, j, rank)
        s = jnp.where(hit, -jnp.inf, s)
        vals.append(m)
    return vals, rank


def _peer_router_kernel(x_ref, g_ref, wqt_ref, sk_ref, h2_ref, lim_ref, rk2_ref, e1_ref, e2_ref,
                        qt_ref, s_ref, rank_ref, v_ref):
    f32 = jnp.float32
    tl = x_ref.shape[0]
    x = x_ref[...]
    h2 = x * lax.rsqrt(jnp.mean(x * x, axis=-1, keepdims=True) + EPS) * g_ref[...]
    h2b = h2.astype(jnp.bfloat16)
    h2_ref[...] = h2b
    qt_ref[...] = lax.dot_general(wqt_ref[...], h2b, (((1,), (1,)), ((), ())),
                                  preferred_element_type=f32).astype(jnp.bfloat16)

    def score_body(hc, carry):
        r0 = pl.multiple_of(hc * HALF_Q, HALF_Q)
        s_ref[hc] = jnp.dot(sk_ref[hc], qt_ref[pl.ds(r0, HALF_Q), :], preferred_element_type=f32)
        return carry

    lax.fori_loop(0, N_HC, score_body, 0, unroll=4)

    key_iota = lax.broadcasted_iota(jnp.int32, (N_KEYS, LANE), 0).astype(f32)

    def key_body(i, carry):
        hc = i // (tl // LANE)
        c0 = pl.multiple_of((i % (tl // LANE)) * LANE, LANE)
        s = s_ref[hc, :, pl.ds(c0, LANE)]
        vals, rank = _extract_top16(s, key_iota)
        rank_ref[hc, :, pl.ds(c0, LANE)] = rank
        v_ref[hc, :, pl.ds(c0, LANE)] = jnp.concatenate(vals, axis=0)
        return carry

    lax.fori_loop(0, N_HC * (tl // LANE), key_body, 0, unroll=8)

    sub = CAND_SUB
    n_mid = sub - 1
    n_cand = PEER_TOPK + n_mid * sub + (PEER_TOPK - sub)
    cand_iota = lax.broadcasted_iota(jnp.int32, (n_cand, LANE), 0).astype(f32)
    row8 = lax.broadcasted_iota(jnp.int32, (sub, LANE), 0)

    def head_body(i, carry):
        h = i // (tl // LANE)
        c0 = pl.multiple_of((i % (tl // LANE)) * LANE, LANE)
        cols = pl.ds(c0, LANE)
        v1 = v_ref[2 * h, :, cols]
        v2 = v_ref[2 * h + 1, :, cols]
        blocks = [v1[0:1, :] + v2]
        for r1 in range(1, sub):
            blocks.append(jnp.where(row8 < PEER_TOPK // (r1 + 1), v1[r1:r1 + 1, :] + v2[:sub, :], -jnp.inf))
        blocks.append(v1[sub:, :] + v2[0:1, :])
        best, crank = _extract_top16(jnp.concatenate(blocks, axis=0), cand_iota)
        z = jnp.zeros((1, LANE), f32)
        for k in range(PEER_TOPK):
            z = z + jnp.exp(best[k] - best[0])
        sel = jnp.where(crank < PEER_TOPK, 1.0, 0.0)
        rank1 = rank_ref[2 * h, :, cols]
        lim = jnp.zeros((N_KEYS, LANE), jnp.int32)
        for r1 in range(PEER_TOPK):
            if r1 == 0:
                cnt = jnp.sum(sel[:PEER_TOPK, :], axis=0, keepdims=True)
            elif r1 < sub:
                lo = PEER_TOPK + (r1 - 1) * sub
                cnt = jnp.sum(sel[lo:lo + sub, :], axis=0, keepdims=True)
            else:
                lo = PEER_TOPK + n_mid * sub + (r1 - sub)
                cnt = sel[lo:lo + 1, :]
            lim = jnp.where(rank1 == r1, cnt.astype(jnp.int32), lim)
        lim_ref[h, :, cols] = lim
        rk2_ref[h, :, cols] = rank_ref[2 * h + 1, :, cols].astype(f32).astype(jnp.bfloat16)
        e1_ref[h, :, cols] = jnp.exp(s_ref[2 * h, :, cols] - v1[0:1, :])
        e2_ref[h, :, cols] = (jnp.exp(s_ref[2 * h + 1, :, cols] - v2[0:1, :]) / z).astype(jnp.bfloat16)
        return carry

    lax.fori_loop(0, PEER_HEADS * (tl // LANE), head_body, 0, unroll=2)


def _gelu_tanh(x):
    c = math.sqrt(2.0 / math.pi)
    half = 0.5 * x
    return half + half * jnp.tanh(x * (c + (c * 0.044715) * (x * x)))


BF16_ROWS = 16


def _rows_bf16(row):
    tile = jnp.broadcast_to(row, (BF16_ROWS, LANE)).astype(jnp.bfloat16)
    return jnp.concatenate([tile] * (N_KEYS // BF16_ROWS), axis=0)


def _peer_expert_kernel(x_ref, h2_ref, *refs):
    pu_refs, pvt_refs = refs[:PEER_DMA_PARTS], refs[PEER_DMA_PARTS:2 * PEER_DMA_PARTS]
    lim_ref, rk2_ref, e1_ref, e2_ref, o_ref, acc_ref, a_ref, c_ref, rk2_s, e2_s = refs[2 * PEER_DMA_PARTS:]
    f32 = jnp.float32
    bf16 = jnp.bfloat16
    zero = jnp.zeros((N_KEYS, LANE), bf16)
    e_step = pl.program_id(1)
    tl = h2_ref.shape[0]
    part = pu_refs[0].shape[0]
    n_i1 = PEER_DMA_PARTS * part // N_KEYS
    chunk = 2 * LANE if tl % (2 * LANE) == 0 else LANE

    @pl.when(e_step == 0)
    def _():
        acc_ref[...] = jnp.zeros_like(acc_ref)
        rk2_s[...] = rk2_ref[...]
        e2_s[...] = e2_ref[...]

    for p, pu_ref in enumerate(pu_refs):
        a_ref[p * part:(p + 1) * part, :] = lax.dot_general(pu_ref[...], h2_ref[...], (((1,), (1,)), ((), ())),
                                                            preferred_element_type=f32)
    for c0 in range(0, tl, chunk):
        for l0 in range(c0, c0 + chunk, LANE):
            cols = slice(l0, l0 + LANE)
            for ii in range(n_i1):
                rows = slice(ii * N_KEYS, (ii + 1) * N_KEYS)
                w = jnp.zeros((N_KEYS, LANE), bf16)
                for h in range(PEER_HEADS):
                    sel = rk2_s[h, :, cols] < _rows_bf16(lim_ref[h, ii:ii + 1, cols].astype(f32))
                    w = w + jnp.where(sel, e2_s[h, :, cols], zero) * _rows_bf16(e1_ref[h, ii:ii + 1, cols])
                c_ref[rows, cols] = w * _gelu_tanh(a_ref[rows, cols]).astype(bf16)
        cc = slice(c0, c0 + chunk)
        proj = [jnp.dot(pvt_ref[0], c_ref[p * part:(p + 1) * part, cc], preferred_element_type=f32)
                for p, pvt_ref in enumerate(pvt_refs)]
        acc_ref[:, cc] += functools.reduce(lambda a, b: a + b, proj)

    @pl.when(e_step == pl.num_programs(1) - 1)
    def _():
        o_ref[...] = x_ref[...] + acc_ref[...].T


def _token_tile(T, sizes=(640, 512, 384, 256, 128)):
    for tl in sizes:
        if T % tl == 0:
            return tl
    raise ValueError(f"token count {T} is not a multiple of {LANE}")


def peer_residual(x, g, wqt, sk, pu, pvt):
    T = x.shape[0]
    tl = _token_tile(T)
    nt = T // tl
    n_exp = pu.shape[0]
    head_shape = jax.ShapeDtypeStruct((PEER_HEADS, N_KEYS, T), jnp.int32)
    head_shape_f = jax.ShapeDtypeStruct((PEER_HEADS, N_KEYS, T), jnp.float32)
    head_shape_b = jax.ShapeDtypeStruct((PEER_HEADS, N_KEYS, T), jnp.bfloat16)
    head_spec = pl.BlockSpec((PEER_HEADS, N_KEYS, tl), lambda i: (0, 0, i))
    h2, lim, rk2, e1, e2 = pl.pallas_call(
        _peer_router_kernel,
        grid=(nt,),
        in_specs=[pl.BlockSpec((tl, D_MODEL), lambda i: (i, 0)),
                  pl.BlockSpec((1, D_MODEL), lambda i: (0, 0)),
                  pl.BlockSpec((PEER_HEADS * PEER_QDIM, D_MODEL), lambda i: (0, 0)),
                  pl.BlockSpec((N_HC, N_KEYS, HALF_Q), lambda i: (0, 0, 0))],
        out_specs=[pl.BlockSpec((tl, D_MODEL), lambda i: (i, 0)), head_spec, head_spec, head_spec, head_spec],
        out_shape=[jax.ShapeDtypeStruct((T, D_MODEL), jnp.bfloat16), head_shape, head_shape_b,
                   head_shape_f, head_shape_b],
        scratch_shapes=[pltpu.VMEM((PEER_HEADS * PEER_QDIM, tl), jnp.bfloat16),
                        pltpu.VMEM((N_HC, N_KEYS, tl), jnp.float32),
                        pltpu.VMEM((N_HC, N_KEYS, tl), jnp.int32),
                        pltpu.VMEM((N_HC, PEER_TOPK, tl), jnp.float32)],
        compiler_params=pltpu.CompilerParams(dimension_semantics=("parallel",), vmem_limit_bytes=VMEM_LIMIT),
        name="peer_router",
    )(x, g.reshape(1, D_MODEL), wqt, sk)

    tl = _token_tile(T, (768, 512, 640, 384, 256, 128))
    nt = T // tl
    te = PEER_EXPERT_TILE
    parts = PEER_DMA_PARTS
    part = te // parts
    i2_spec = pl.BlockSpec((PEER_HEADS, N_KEYS, tl), lambda i, e: (0, 0, i))
    i1_spec = pl.BlockSpec((PEER_HEADS, te // N_KEYS, tl), lambda i, e: (0, e, i))
    return pl.pallas_call(
        _peer_expert_kernel,
        grid=(nt, n_exp // te),
        in_specs=[pl.BlockSpec((tl, D_MODEL), lambda i, e: (i, 0)),
                  pl.BlockSpec((tl, D_MODEL), lambda i, e: (i, 0)),
                  *[pl.BlockSpec((part, D_MODEL), lambda i, e, p=p: (parts * e + p, 0)) for p in range(parts)],
                  *[pl.BlockSpec((1, D_MODEL, part), lambda i, e, p=p: (parts * e + p, 0, 0)) for p in range(parts)],
                  i1_spec, i2_spec, i1_spec, i2_spec],
        out_specs=pl.BlockSpec((tl, D_MODEL), lambda i, e: (i, 0)),
        out_shape=jax.ShapeDtypeStruct((T, D_MODEL), jnp.float32),
        scratch_shapes=[pltpu.VMEM((D_MODEL, tl), jnp.float32),
                        pltpu.VMEM((te, tl), jnp.float32),
                        pltpu.VMEM((te, tl), jnp.bfloat16),
                        pltpu.VMEM((PEER_HEADS, N_KEYS, tl), jnp.bfloat16),
                        pltpu.VMEM((PEER_HEADS, N_KEYS, tl), jnp.bfloat16)],
        compiler_params=pltpu.CompilerParams(dimension_semantics=("parallel", "arbitrary"),
                                             vmem_limit_bytes=VMEM_LIMIT),
        name="peer_experts",
    )(x, h2, *([pu] * parts), *([pvt] * parts), lim, rk2, e1, e2)


C_QKVB = 3 * WIDTH_A
C_GB = C_QKVB + 3 * WIDTH_B
C_UC = C_GB + WIDTH_B
C_GATE = C_UC + WIDTH_C
C_BA = C_GATE + 3 * D_MODEL
D_IN2 = C_BA + LANE
S5_BLOCK = 256


def _reorder_w_in(w):
    cols = [w[:, OFF_QA:OFF_BETA], w[:, OFF_UC:OFF_GATE], w[:, OFF_GATE:], w[:, OFF_BETA:OFF_UC],
            jnp.zeros((w.shape[0], LANE - 2 * N_HEADS_B), w.dtype)]
    return jnp.concatenate(cols, axis=1).astype(jnp.bfloat16)


def trunk_layer(x, lp, layer_idx, att, prompt, k_past, v_past, conv_hist, delta0, ssm0_re, ssm0_im):
    f32 = jnp.float32
    B, Lx = x.shape[0], x.shape[1]
    T = B * Lx
    z = norm_matmul(x.reshape(T, D_MODEL), lp['norm1'], lp['w_in2'])

    qn, kn, knb, vb = qk_prep(z, lp['q_norm'], lp['k_norm'], _token_tile(T))
    lam_init = 0.8 - 0.6 * math.exp(-0.3 * layer_idx)
    lqk = lp['lambda_qk']
    lam = jnp.exp(jnp.sum(lqk[0] * lqk[1])) - jnp.exp(jnp.sum(lqk[2] * lqk[3])) + lam_init
    scalars = jnp.stack([lam, jnp.asarray(1.0 - lam_init, f32)]).astype(f32)
    r3 = lambda a: a.reshape(B, Lx, WIDTH_A)
    r4 = lambda a: a.reshape(N_HEADS_A, B, Lx, a.shape[-1])
    if prompt:
        oA = attn_prompt(r4(qn), r4(knb), r4(vb), scalars, att['far'], att['tiles'], lp['subln'])
    else:
        P = k_past.shape[1]
        oA = attn_sample(r4(qn), r4(knb), r4(vb), k_past.reshape(B, P, WIDTH_A), v_past.reshape(B, P, WIDTH_A),
                         scalars, att['bias_past'], att['bias_new'], lp['subln'])

    z3 = z.reshape(B, Lx, D_IN2)
    off = PAD if prompt else 0
    L = Lx - off
    kA = r3(kn)[:, off:].reshape(B, L, N_HEADS_A, 2 * HEAD_DIM_A)
    vA = z3[:, off:, 2 * WIDTH_A:3 * WIDTH_A].reshape(B, L, N_HEADS_A, 2 * HEAD_DIM_A)

    oB, delta, tail = gdn_mixer(z3, (C_QKVB // (3 * WIDTH_B), C_GB // WIDTH_B, C_BA // LANE), conv_hist, delta0,
                                lp['conv_w'], lp['a_log'], lp['dt_bias'], lp['gdn_norm'],
                                GDN_CHUNK if prompt else Lx, off)
    conv_state = tail[:, GDN_TAIL - (CONV_W - 1):]

    oC, ssm_re, ssm_im = s5_glu(z3, C_UC // WIDTH_C, ssm0_re, ssm0_im, lp['s5'], lp['w_glu'],
                                S5_BLOCK if prompt else Lx)

    x2 = merge_residual(x.reshape(T, D_MODEL), oA.reshape(T, WIDTH_A), oB.reshape(T, WIDTH_B),
                        oC.reshape(T, WIDTH_C), z, C_GATE // D_MODEL, lp['wb_a'], lp['wb_b'], lp['wb_c'],
                        lp['w_out'], Lx, off)

    x = peer_residual(x2, lp['norm2'], lp['peer_wqt'], lp['peer_sk'], lp['peer_pu'],
                      lp['peer_pvt']).reshape(B, Lx, D_MODEL)
    return (x, kA, vA, conv_state, delta, ssm_re, ssm_im)


def kernel(x_prompt, x_sample, cache_k, cache_v, state_conv, state_delta, state_ssm_re, state_ssm_im,
           meta_tokens, rel_bias, norm1_g, norm2_g, final_norm_g, w_in, q_norm_g, k_norm_g, lambda_qk,
           subln_g, conv_w, gdn_a_log, gdn_dt_bias, gdn_norm_g, s5_a_re, s5_a_im, s5_b_re, s5_b_im,
           s5_c_re, s5_c_im, s5_d, s5_log_dt, w_glu, w_branch_a, w_branch_b, w_branch_c, w_out,
           peer_wq, peer_subkeys, peer_u, peer_v):
    f32 = jnp.float32
    bf16 = jnp.bfloat16
    params = [dict(norm1=norm1_g[l], norm2=norm2_g[l], w_in2=_reorder_w_in(w_in[l]), q_norm=q_norm_g[l],
                   k_norm=k_norm_g[l], lambda_qk=lambda_qk[l], subln=subln_g[l], conv_w=conv_w[l],
                   a_log=gdn_a_log[l], dt_bias=gdn_dt_bias[l], gdn_norm=gdn_norm_g[l],
                   s5=s5_params(s5_a_re[l], s5_a_im[l], s5_b_re[l], s5_b_im[l], s5_c_re[l], s5_c_im[l], s5_d[l],
                                s5_log_dt[l]),
                   w_glu=w_glu[l], wb_a=w_branch_a[l], wb_b=w_branch_b[l], wb_c=w_branch_c[l], w_out=w_out[l],
                   peer_wqt=peer_wq[l].T.astype(bf16),
                   peer_sk=peer_subkeys[l].reshape(N_HC, N_KEYS, HALF_Q).astype(bf16),
                   peer_pu=peer_u[l].astype(bf16), peer_pvt=peer_v[l].astype(bf16).reshape(-1, PEER_EXPERT_TILE // PEER_DMA_PARTS, D_MODEL).transpose(0, 2, 1))
              for l in range(DEPTH)]

    qpos = jnp.arange(ATT_TQ, dtype=jnp.int32)
    tiles = jnp.stack([t5_bias(qpos, d * ATT_TK + jnp.arange(ATT_TK, dtype=jnp.int32), rel_bias)
                       for d in (-2, -1, 0)], axis=1)
    far = t5_bias(jnp.full((1,), 2 * ATT_TK, jnp.int32), jnp.zeros((1,), jnp.int32), rel_bias).reshape(N_HEADS_A)
    P, Ls = cache_k.shape[2], x_sample.shape[1]
    bias_s = t5_bias(P + jnp.arange(Ls, dtype=jnp.int32), jnp.arange(P + Ls, dtype=jnp.int32), rel_bias)
    att_p = dict(tiles=tiles, far=far)
    att_s = dict(bias_past=bias_s[:, :, :P], bias_new=bias_s[:, :, P:])

    B = x_prompt.shape[0]
    xp = jnp.concatenate([jnp.zeros((B, PAD, D_MODEL), f32),
                          jnp.broadcast_to(meta_tokens[None], (B, N_META, D_MODEL)), x_prompt], axis=1)
    outs_p = [[] for _ in range(6)]
    for l in range(DEPTH):
        res = trunk_layer(
            xp, params[l], l, att_p, True, None, None,
            jnp.zeros((B, CONV_W - 1, 3 * WIDTH_B), f32),
            jnp.zeros((B, N_HEADS_B, HEAD_DIM_B, HEAD_DIM_B), f32),
            jnp.zeros((B, S5_GROUPS, S5_STATE), f32), jnp.zeros((B, S5_GROUPS, S5_STATE), f32))
        xp = res[0]
        for acc, r in zip(outs_p, res[1:]):
            acc.append(r)
    y_prompt = rmsnorm(xp, final_norm_g)[:, FRONT:]

    xs = x_sample
    outs_s = [[] for _ in range(6)]
    for l in range(DEPTH):
        res = trunk_layer(
            xs, params[l], l, att_s, False, cache_k[l], cache_v[l], state_conv[l],
            state_delta[l], state_ssm_re[l], state_ssm_im[l])
        xs = res[0]
        for acc, r in zip(outs_s, res[1:]):
            acc.append(r)
    y_sample = rmsnorm(xs, final_norm_g)

    kp, vp, cp, dp, srp, sip = [jnp.stack(a) for a in outs_p]
    ks_, vs_, cs_, ds_, srs, sis = [jnp.stack(a) for a in outs_s]
    return (y_prompt, y_sample, kp, vp, ks_, vs_, cp, cs_, dp, ds_, srp, sip, srs, sis)
```
